```python
import jax, jax.numpy as jnp
from jax import lax
import numpy as np

D_MODEL = 2048
BATCH = 2
SEQ = 4096
DEPTH = 1

FOX_HEADS = 8
FOX_HEAD_DIM = 128
FOX_WIDTH = FOX_HEADS * FOX_HEAD_DIM
SWA_Q_HEADS = 16
SWA_KV_HEADS = 2
SWA_HEAD_DIM = 64
SWA_WIDTH = SWA_Q_HEADS * SWA_HEAD_DIM
SWA_KV_WIDTH = SWA_KV_HEADS * SWA_HEAD_DIM
WINDOW = 128
Q_BLOCK = 128
ROPE_THETA = 10000.0
PLE_DIM = 256
NORM_EPS = 1e-6

SPLIT_SIZES = (
    FOX_WIDTH,
    FOX_WIDTH,
    FOX_WIDTH,
    FOX_WIDTH,
    FOX_HEADS,
    SWA_WIDTH,
    SWA_KV_WIDTH,
    SWA_KV_WIDTH,
    SWA_WIDTH,
    D_MODEL,
    D_MODEL,
)
N_IN = sum(SPLIT_SIZES)

kernel_name = "fox_swa_sink_gated_hybrid"


def rms_norm(x, g):
    xf = x.astype(jnp.float32)
    y = xf * lax.rsqrt(jnp.mean(xf * xf, axis=-1, keepdims=True) + NORM_EPS)
    return (y * g.astype(jnp.float32)).astype(x.dtype)


def apply_rope(x, pos):
    half = x.shape[-1] // 2
    inv = ROPE_THETA ** (-jnp.arange(half, dtype=jnp.float32) / half)
    ang = pos.astype(jnp.float32)[..., None] * inv
    cos = jnp.cos(ang)[:, :, None, :]
    sin = jnp.sin(ang)[:, :, None, :]
    xf = x.astype(jnp.float32)
    x1, x2 = xf[..., :half], xf[..., half:]
    out = jnp.concatenate([x1 * cos - x2 * sin, x2 * cos + x1 * sin], axis=-1)
    return out.astype(x.dtype)


def forgetting_attention(q, k, v, log_f):
    B, S, H, D = q.shape
    nb = S // Q_BLOCK
    c = jnp.cumsum(log_f.astype(jnp.float32), axis=1)
    cT = c.transpose(0, 2, 1)
    qb = q.reshape(B, nb, Q_BLOCK, H, D).transpose(1, 0, 2, 3, 4)
    cb = cT.reshape(B, H, nb, Q_BLOCK).transpose(2, 0, 1, 3)
    kpos = jnp.arange(S)
    scale = D ** -0.5

    def one_block(args):
        i, q_i, c_i = args
        s = jnp.einsum('bqhd,bkhd->bhqk', q_i, k, preferred_element_type=jnp.float32) * scale
        s = s + c_i[..., :, None] - cT[..., None, :]
        qpos = i * Q_BLOCK + jnp.arange(Q_BLOCK)
        causal = kpos[None, :] <= qpos[:, None]
        s = jnp.where(causal, s, -jnp.inf)
        w = jax.nn.softmax(s, axis=-1)
        return jnp.einsum('bhqk,bkhd->bqhd', w.astype(v.dtype), v)

    out = lax.map(one_block, (jnp.arange(nb), qb, cb))
    return out.transpose(1, 0, 2, 3, 4).reshape(B, S, H, D)


def sliding_window_sink_attention(q, k, v, sinks):
    B, S, Hq, D = q.shape
    Hkv = k.shape[2]
    G = Hq // Hkv
    nb = S // WINDOW
    qb = q.reshape(B, nb, WINDOW, Hkv, G, D)
    kb = k.reshape(B, nb, WINDOW, Hkv, D)
    vb = v.reshape(B, nb, WINDOW, Hkv, D)
    pad = ((0, 0), (1, 0), (0, 0), (0, 0), (0, 0))
    kk = jnp.concatenate([jnp.pad(kb, pad)[:, :-1], kb], axis=2)
    vv = jnp.concatenate([jnp.pad(vb, pad)[:, :-1], vb], axis=2)
    s = jnp.einsum('bnqhgd,bnkhd->bnhgqk', qb, kk,
                   preferred_element_type=jnp.float32) * (D ** -0.5)
    qi = jnp.arange(WINDOW)[:, None] + WINDOW
    ki = jnp.arange(2 * WINDOW)[None, :]
    rel = qi - ki
    band = (rel >= 0) & (rel < WINDOW)
    has_prev = (jnp.arange(nb) > 0)[:, None, None] | (ki >= WINDOW)[None]
    mask = band[None] & has_prev
    s = jnp.where(mask[None, :, None, None], s, -jnp.inf)
    sink = sinks.astype(jnp.float32).reshape(Hkv, G)[None, None, :, :, None, None]
    m = jnp.maximum(jnp.max(s, axis=-1, keepdims=True), sink)
    e = jnp.exp(s - m)
    w = e / (jnp.sum(e, axis=-1, keepdims=True) + jnp.exp(sink - m))
    out = jnp.einsum('bnhgqk,bnkhd->bnqhgd', w.astype(v.dtype), vv)
    return out.reshape(B, S, Hq, D)


def setup_inputs(seed: int = 0) -> dict:
    key = jax.random.key(seed)
    ks = jax.random.split(key, 16)
    f32 = jnp.float32
    x = jax.random.normal(ks[0], (BATCH, SEQ, D_MODEL), f32)
    p = jax.random.normal(ks[1], (DEPTH, BATCH, SEQ, PLE_DIM), f32)
    offset = jax.random.randint(ks[2], (BATCH, 1), 0, 1024, dtype=jnp.int32)
    positions = (offset + jnp.arange(SEQ, dtype=jnp.int32)[None, :]).astype(jnp.int32)
    pre_norm_g = 1.0 + 0.05 * jax.random.normal(ks[3], (DEPTH, D_MODEL), f32)
    w_in = jax.random.normal(ks[4], (DEPTH, D_MODEL, N_IN), f32) * D_MODEL ** -0.5
    b_forget = jax.random.uniform(ks[5], (DEPTH, FOX_HEADS), f32, 1.0, 4.0)
    sinks = 0.5 * jax.random.normal(ks[6], (DEPTH, SWA_Q_HEADS), f32)
    w_o_fox = jax.random.normal(ks[7], (DEPTH, FOX_WIDTH, D_MODEL), f32) * FOX_WIDTH ** -0.5
    w_o_swa = jax.random.normal(ks[8], (DEPTH, SWA_WIDTH, D_MODEL), f32) * SWA_WIDTH ** -0.5
    w_out = jax.random.normal(ks[9], (DEPTH, D_MODEL, D_MODEL), f32) * D_MODEL ** -0.5
    post_norm_g = 1.0 + 0.05 * jax.random.normal(ks[10], (DEPTH, D_MODEL), f32)
    w_ple = jax.random.normal(ks[11], (DEPTH, PLE_DIM, D_MODEL), f32) * PLE_DIM ** -0.5
    w_ple_gate = jax.random.normal(ks[12], (DEPTH, D_MODEL, D_MODEL), f32) * D_MODEL ** -0.5
    return {"x": x, "p": p, "positions": positions, "pre_norm_g": pre_norm_g, "w_in": w_in,
            "b_forget": b_forget, "sinks": sinks, "w_o_fox": w_o_fox, "w_o_swa": w_o_swa,
            "w_out": w_out, "post_norm_g": post_norm_g, "w_ple": w_ple, "w_ple_gate": w_ple_gate}


def reference(x, p, positions, pre_norm_g, w_in, b_forget, sinks, w_o_fox, w_o_swa,
              w_out, post_norm_g, w_ple, w_ple_gate):
    B, S, _ = x.shape
    cuts = np.cumsum(SPLIT_SIZES)[:-1].tolist()
    for i in range(DEPTH):
        h = rms_norm(x, pre_norm_g[i])
        proj = jnp.einsum('bsd,dn->bsn', h, w_in[i])
        (q_a, k_a, v_a, z_a, f_a, q_b, k_b, v_b, z_b, g_a, g_b) = jnp.split(proj, cuts, axis=-1)

        log_f = jax.nn.log_sigmoid(f_a.astype(jnp.float32) + b_forget[i].astype(jnp.float32))
        y_a = forgetting_attention(q_a.reshape(B, S, FOX_HEADS, FOX_HEAD_DIM),
                                   k_a.reshape(B, S, FOX_HEADS, FOX_HEAD_DIM),
                                   v_a.reshape(B, S, FOX_HEADS, FOX_HEAD_DIM), log_f)
        y_a = y_a.reshape(B, S, FOX_WIDTH) * jax.nn.silu(z_a)
        y_a = jnp.einsum('bsc,cd->bsd', y_a, w_o_fox[i])

        qr = apply_rope(q_b.reshape(B, S, SWA_Q_HEADS, SWA_HEAD_DIM), positions)
        kr = apply_rope(k_b.reshape(B, S, SWA_KV_HEADS, SWA_HEAD_DIM), positions)
        y_b = sliding_window_sink_attention(qr, kr, v_b.reshape(B, S, SWA_KV_HEADS, SWA_HEAD_DIM),
                                            sinks[i])
        y_b = y_b.reshape(B, S, SWA_WIDTH) * jax.nn.silu(z_b)
        y_b = jnp.einsum('bsc,cd->bsd', y_b, w_o_swa[i])

        merged = jax.nn.sigmoid(g_a) * y_a + jax.nn.sigmoid(g_b) * y_b
        out = jnp.einsum('bsd,de->bse', merged, w_out[i])
        x = x + rms_norm(out, post_norm_g[i])

        e = jnp.einsum('bsp,pd->bsd', p[i].astype(x.dtype), w_ple[i])
        gate = jax.nn.sigmoid(jnp.einsum('bsd,de->bse', x, w_ple_gate[i]))
        x = x + gate * e
    return x
```

```python
import functools

import jax
import jax.numpy as jnp
from jax import lax
from jax.experimental import pallas as pl
from jax.experimental.pallas import tpu as pltpu

F32 = jnp.float32
BF16 = jnp.bfloat16

D_MODEL = 2048
FOX_HEADS = 8
FOX_HEAD_DIM = 128
FOX_WIDTH = FOX_HEADS * FOX_HEAD_DIM
SWA_Q_HEADS = 16
SWA_KV_HEADS = 2
SWA_HEAD_DIM = 64
SWA_WIDTH = SWA_Q_HEADS * SWA_HEAD_DIM
SWA_KV_WIDTH = SWA_KV_HEADS * SWA_HEAD_DIM
WINDOW = 128
ROPE_THETA = 10000.0
PLE_DIM = 256
NORM_EPS = 1e-6
LANES = 128

_REF_SPLITS = (FOX_WIDTH, FOX_WIDTH, FOX_WIDTH, FOX_WIDTH, FOX_HEADS, SWA_WIDTH,
               SWA_KV_WIDTH, SWA_KV_WIDTH, SWA_WIDTH, D_MODEL, D_MODEL)
_REF_NAMES = ("q_a", "k_a", "v_a", "z_a", "f_a", "q_b", "k_b", "v_b", "z_b", "g_a", "g_b")
_REF_OFF = {}
_o = 0
for _n, _s in zip(_REF_NAMES, _REF_SPLITS):
    _REF_OFF[_n] = (_o, _s)
    _o += _s

_PROJ_ORDER = ("q_a", "k_a", "v_a", "z_a", "q_b", "z_b", "g_a", "g_b", "k_b", "v_b")
_PROJ_OFF = {}
_o = 0
for _n in _PROJ_ORDER:
    _PROJ_OFF[_n] = _o
    _o += _REF_OFF[_n][1]
PROJ_USED = _o
IN_TN = 512
PROJ_COLS = -(-PROJ_USED // IN_TN) * IN_TN
IN_TM = 1024
NORM_ROWS = 256
F_ROWS = 16

FOX_TQ = 512
FOX_TK = 512
NEG_BIG = -1e30

EPI_TM = 256

_NT = (((1,), (1,)), ((), ()))


def _inproj_kernel(x_ref, g_ref, w_ref, wf_ref, proj_ref, ft_ref, h_ref):
    @pl.when(pl.program_id(1) == 0)
    def _():
        for r in range(IN_TM // NORM_ROWS):
            rows = slice(r * NORM_ROWS, (r + 1) * NORM_ROWS)
            x = x_ref[rows, :]
            ms = jnp.mean(x * x, axis=-1, keepdims=True)
            h = (x * lax.rsqrt(ms + NORM_EPS)) * g_ref[...]
            h_ref[rows, :] = h.astype(BF16)
        ft_ref[...] = lax.dot_general(wf_ref[...], h_ref[...], _NT,
                                      preferred_element_type=F32)

    proj_ref[...] = jnp.dot(h_ref[...], w_ref[...],
                            preferred_element_type=F32).astype(BF16)


def _inproj(x2, g, w_perm, wf_t):
    t = x2.shape[0]
    return pl.pallas_call(
        _inproj_kernel,
        grid=(t // IN_TM, PROJ_COLS // IN_TN),
        in_specs=[
            pl.BlockSpec((IN_TM, D_MODEL), lambda i, j: (i, 0)),
            pl.BlockSpec((1, D_MODEL), lambda i, j: (0, 0)),
            pl.BlockSpec((D_MODEL, IN_TN), lambda i, j: (0, j)),
            pl.BlockSpec((F_ROWS, D_MODEL), lambda i, j: (0, 0)),
        ],
        out_specs=[
            pl.BlockSpec((IN_TM, IN_TN), lambda i, j: (i, j)),
            pl.BlockSpec((F_ROWS, IN_TM), lambda i, j: (0, i)),
        ],
        out_shape=[
            jax.ShapeDtypeStruct((t, PROJ_COLS), BF16),
            jax.ShapeDtypeStruct((F_ROWS, t), F32),
        ],
        scratch_shapes=[pltpu.VMEM((IN_TM, D_MODEL), BF16)],
        compiler_params=pltpu.CompilerParams(
            dimension_semantics=("arbitrary", "arbitrary"),
            vmem_limit_bytes=48 * 1024 * 1024),
        name="inproj",
    )(x2, g, w_perm, wf_t)


def _cumsum_kernel(ft_ref, b_ref, c_ref):
    f = ft_ref[0:FOX_HEADS, :] + b_ref[...]
    lf = jnp.minimum(f, 0.0) - jnp.log1p(jnp.exp(-jnp.abs(f)))
    seq = lf.shape[1]
    lane = lax.broadcasted_iota(jnp.int32, lf.shape, 1)
    c = lf
    shift = 1
    while shift < seq:
        c = c + jnp.where(lane >= shift, pltpu.roll(c, shift, axis=1), 0.0)
        shift *= 2
    c_ref[...] = c


def _forget_cumsum(ft, b_col, batch, seq):
    return pl.pallas_call(
        _cumsum_kernel,
        grid=(batch,),
        in_specs=[
            pl.BlockSpec((F_ROWS, seq), lambda b: (0, b)),
            pl.BlockSpec((FOX_HEADS, 1), lambda b: (0, 0)),
        ],
        out_specs=pl.BlockSpec((FOX_HEADS, seq), lambda b: (0, b)),
        out_shape=jax.ShapeDtypeStruct((FOX_HEADS, batch * seq), F32),
        name="forget_cumsum",
    )(ft, b_col)


def _fox_kernel(q_ref, k_ref, v_ref, cq_ref, ck_ref, o_ref, m_ref, l_ref, acc_ref):
    qi = pl.program_id(2)
    ki = pl.program_id(3)

    @pl.when(ki == 0)
    def _():
        m_ref[...] = jnp.full(m_ref.shape, NEG_BIG, F32)
        l_ref[...] = jnp.zeros(l_ref.shape, F32)
        acc_ref[...] = jnp.zeros(acc_ref.shape, F32)

    def step(masked):
        s = lax.dot_general(q_ref[...], k_ref[...], _NT, preferred_element_type=F32)
        s = s * (FOX_HEAD_DIM ** -0.5) + (cq_ref[0, :, 0:1] - ck_ref[0])
        if masked:
            row = lax.broadcasted_iota(jnp.int32, s.shape, 0)
            col = lax.broadcasted_iota(jnp.int32, s.shape, 1)
            s = jnp.where(row >= col, s, NEG_BIG)
        m_prev = m_ref[...]
        m_new = jnp.maximum(m_prev, jnp.max(s, axis=-1, keepdims=True))
        alpha = jnp.exp(m_prev - m_new)
        p = jnp.exp(s - m_new)
        l_ref[...] = alpha * l_ref[...] + jnp.sum(p, axis=-1, keepdims=True)
        acc_ref[...] = alpha * acc_ref[...] + jnp.dot(
            p.astype(BF16), v_ref[...], preferred_element_type=F32)
        m_ref[...] = m_new

    @pl.when(ki < qi)
    def _():
        step(False)

    @pl.when(ki == qi)
    def _():
        step(True)
        o_ref[...] = (acc_ref[...] / l_ref[...]).astype(o_ref.dtype)


def _fox_attention(proj, c3, batch, seq):
    nq = seq // FOX_TQ
    nk = seq // FOX_TK
    qcol = _PROJ_OFF["q_a"] // FOX_HEAD_DIM
    kcol = _PROJ_OFF["k_a"] // FOX_HEAD_DIM
    vcol = _PROJ_OFF["v_a"] // FOX_HEAD_DIM

    def kv_row(b, qi, ki):
        return b * nk + jnp.minimum(ki, qi)

    return pl.pallas_call(
        _fox_kernel,
        grid=(batch, FOX_HEADS, nq, nk),
        in_specs=[
            pl.BlockSpec((FOX_TQ, FOX_HEAD_DIM), lambda b, h, qi, ki: (b * nq + qi, qcol + h)),
            pl.BlockSpec((FOX_TK, FOX_HEAD_DIM), lambda b, h, qi, ki: (kv_row(b, qi, ki), kcol + h)),
            pl.BlockSpec((FOX_TK, FOX_HEAD_DIM), lambda b, h, qi, ki: (kv_row(b, qi, ki), vcol + h)),
            pl.BlockSpec((1, 1, FOX_TQ), lambda b, h, qi, ki: (h, 0, b * nq + qi)),
            pl.BlockSpec((1, 1, FOX_TK), lambda b, h, qi, ki: (h, 0, kv_row(b, qi, ki))),
        ],
        out_specs=pl.BlockSpec((FOX_TQ, FOX_HEAD_DIM), lambda b, h, qi, ki: (b * nq + qi, h)),
        out_shape=jax.ShapeDtypeStruct((batch * seq, FOX_WIDTH), BF16),
        scratch_shapes=[
            pltpu.VMEM((FOX_TQ, 1), F32),
            pltpu.VMEM((FOX_TQ, 1), F32),
            pltpu.VMEM((FOX_TQ, FOX_HEAD_DIM), F32),
        ],
        compiler_params=pltpu.CompilerParams(
            dimension_semantics=("arbitrary", "arbitrary", "arbitrary", "arbitrary")),
        name="fox_attention",
    )(proj, proj, proj, c3, c3)


def _rope(x, cos, sin_signed):
    width = x.shape[1]
    lane = lax.broadcasted_iota(jnp.int32, x.shape, 1)
    first_half = (lane % SWA_HEAD_DIM) < (SWA_HEAD_DIM // 2)
    half = SWA_HEAD_DIM // 2
    if width == LANES:
        swapped = jnp.where(first_half, pltpu.roll(x, LANES - half, axis=1),
                            pltpu.roll(x, half, axis=1))
    else:
        swapped = jnp.where(first_half, pltpu.roll(x, width - half, axis=1),
                            pltpu.roll(x, half, axis=1))
    reps = width // LANES
    if reps > 1:
        cos = jnp.concatenate([cos] * reps, axis=1)
        sin_signed = jnp.concatenate([sin_signed] * reps, axis=1)
    return x * cos + swapped * sin_signed


def _rope_tables(pos_col, inv_row):
    ang = pos_col.astype(F32) * inv_row
    lane = lax.broadcasted_iota(jnp.int32, ang.shape, 1)
    first_half = (lane % SWA_HEAD_DIM) < (SWA_HEAD_DIM // 2)
    sin = jnp.sin(ang)
    return jnp.cos(ang), jnp.where(first_half, -sin, sin)


def _dup_heads(x):
    lane = lax.broadcasted_iota(jnp.int32, x.shape, 1)
    lo = lane < SWA_HEAD_DIM
    r = pltpu.roll(x, SWA_HEAD_DIM, axis=1)
    return jnp.where(lo, x, r), jnp.where(lo, r, x)


def _swa_kernel(sink_ref, q_ref, kc_ref, kp_ref, vc_ref, vp_ref, posc_ref, posp_ref,
                inv_ref, o_ref):
    n = pl.program_id(1)
    w = WINDOW
    inv = inv_ref[...]
    cos_c, sin_c = _rope_tables(posc_ref[...], inv)
    cos_p, sin_p = _rope_tables(posp_ref[...], inv)

    q = _rope(q_ref[...].astype(F32), cos_c, sin_c)
    k_cur = _rope(kc_ref[...].astype(F32), cos_c, sin_c)
    k_prev = _rope(kp_ref[...].astype(F32), cos_p, sin_p)
    k_all = jnp.concatenate([k_prev, k_cur], axis=0)
    v_all = jnp.concatenate([vp_ref[...], vc_ref[...]], axis=0).astype(F32)
    k_dup = _dup_heads(k_all)
    v_dup = _dup_heads(v_all)

    lane = lax.broadcasted_iota(jnp.int32, (w, LANES), 1)
    lo = lane < SWA_HEAD_DIM
    group = SWA_Q_HEADS // SWA_KV_HEADS
    blocks_per_kv = group * SWA_HEAD_DIM // LANES

    rows = group * w
    qpos = lax.broadcasted_iota(jnp.int32, (rows, 2 * w), 0) % w + w
    kpos = lax.broadcasted_iota(jnp.int32, (rows, 2 * w), 1)
    rel = qpos - kpos
    first_key = jnp.where(n > 0, 0, w)
    mask = (rel >= 0) & (rel < w) & (kpos >= first_key)

    for g in range(SWA_KV_HEADS):
        parts = []
        sink_parts = []
        for j in range(blocks_per_kv):
            col = (g * blocks_per_kv + j) * LANES
            qj = q[:, col:col + LANES]
            parts.append(jnp.where(lo, qj, 0.0))
            parts.append(jnp.where(lo, 0.0, qj))
            for e in range(2):
                head = g * group + 2 * j + e
                sink_parts.append(jnp.full((w, 1), sink_ref[head], F32))
        qs = jnp.concatenate(parts, axis=0).astype(BF16)
        sink = jnp.concatenate(sink_parts, axis=0)
        s = lax.dot_general(qs, k_dup[g].astype(BF16), _NT,
                            preferred_element_type=F32) * (SWA_HEAD_DIM ** -0.5)
        s = jnp.where(mask, s, NEG_BIG)
        m = jnp.maximum(jnp.max(s, axis=-1, keepdims=True), sink)
        e_s = jnp.exp(s - m)
        denom = jnp.sum(e_s, axis=-1, keepdims=True) + jnp.exp(sink - m)
        wgt = (e_s / denom).astype(BF16)
        out = jnp.dot(wgt, v_dup[g].astype(BF16), preferred_element_type=F32)
        for j in range(blocks_per_kv):
            col = (g * blocks_per_kv + j) * LANES
            a = out[(2 * j) * w:(2 * j + 1) * w]
            b = out[(2 * j + 1) * w:(2 * j + 2) * w]
            o_ref[:, col:col + LANES] = jnp.where(lo, a, b).astype(o_ref.dtype)


def _swa_attention(sinks, proj, pos_col, inv_row, batch, seq):
    nb = seq // WINDOW
    qcol = _PROJ_OFF["q_b"] // SWA_WIDTH
    kcol = _PROJ_OFF["k_b"] // LANES
    vcol = _PROJ_OFF["v_b"] // LANES

    def cur(b, n):
        return b * nb + n

    def prev(b, n):
        return b * nb + jnp.maximum(n - 1, 0)

    return pl.pallas_call(
        _swa_kernel,
        grid=(batch, nb),
        in_specs=[
            pl.BlockSpec(memory_space=pltpu.SMEM),
            pl.BlockSpec((WINDOW, SWA_WIDTH), lambda b, n: (cur(b, n), qcol)),
            pl.BlockSpec((WINDOW, LANES), lambda b, n: (cur(b, n), kcol)),
            pl.BlockSpec((WINDOW, LANES), lambda b, n: (prev(b, n), kcol)),
            pl.BlockSpec((WINDOW, LANES), lambda b, n: (cur(b, n), vcol)),
            pl.BlockSpec((WINDOW, LANES), lambda b, n: (prev(b, n), vcol)),
            pl.BlockSpec((WINDOW, 1), lambda b, n: (cur(b, n), 0)),
            pl.BlockSpec((WINDOW, 1), lambda b, n: (prev(b, n), 0)),
            pl.BlockSpec((1, LANES), lambda b, n: (0, 0)),
        ],
        out_specs=pl.BlockSpec((WINDOW, SWA_WIDTH), lambda b, n: (cur(b, n), 0)),
        out_shape=jax.ShapeDtypeStruct((batch * seq, SWA_WIDTH), BF16),
        compiler_params=pltpu.CompilerParams(
            dimension_semantics=("arbitrary", "arbitrary")),
        name="swa_attention",
    )(sinks, proj, proj, proj, proj, proj, pos_col, pos_col, inv_row)


def _silu(z):
    return z * jax.nn.sigmoid(z)


def _epilogue_kernel(ya_ref, za_ref, yb_ref, zb_ref, ga_ref, gb_ref, x_ref, p_ref,
                     wof_ref, wos_ref, wout_ref, gpost_ref, wple_ref, wgate_ref, o_ref):
    ua = (ya_ref[...].astype(F32) * _silu(za_ref[...].astype(F32))).astype(BF16)
    ub = (yb_ref[...].astype(F32) * _silu(zb_ref[...].astype(F32))).astype(BF16)
    oa = jnp.dot(ua, wof_ref[...], preferred_element_type=F32)
    ob = jnp.dot(ub, wos_ref[...], preferred_element_type=F32)
    merged = (jax.nn.sigmoid(ga_ref[...].astype(F32)) * oa
              + jax.nn.sigmoid(gb_ref[...].astype(F32)) * ob)
    out = jnp.dot(merged.astype(BF16), wout_ref[...], preferred_element_type=F32)
    ms = jnp.mean(out * out, axis=-1, keepdims=True)
    x1 = x_ref[...] + (out * lax.rsqrt(ms + NORM_EPS)) * gpost_ref[...]
    e = jnp.dot(p_ref[...].astype(BF16), wple_ref[...], preferred_element_type=F32)
    gate = jax.nn.sigmoid(jnp.dot(x1.astype(BF16), wgate_ref[...],
                                  preferred_element_type=F32))
    o_ref[...] = x1 + gate * e


def _epilogue(ya, yb, proj, x2, p2, wof, wos, wout, gpost, wple, wgate):
    t = x2.shape[0]
    za_col = _PROJ_OFF["z_a"] // FOX_WIDTH
    zb_col = _PROJ_OFF["z_b"] // SWA_WIDTH
    ga_col = _PROJ_OFF["g_a"] // D_MODEL
    gb_col = _PROJ_OFF["g_b"] // D_MODEL
    once = pl.Buffered(1)

    def const(shape):
        return pl.BlockSpec(shape, lambda i: (0, 0), pipeline_mode=once)

    return pl.pallas_call(
        _epilogue_kernel,
        grid=(t // EPI_TM,),
        in_specs=[
            pl.BlockSpec((EPI_TM, FOX_WIDTH), lambda i: (i, 0)),
            pl.BlockSpec((EPI_TM, FOX_WIDTH), lambda i: (i, za_col)),
            pl.BlockSpec((EPI_TM, SWA_WIDTH), lambda i: (i, 0)),
            pl.BlockSpec((EPI_TM, SWA_WIDTH), lambda i: (i, zb_col)),
            pl.BlockSpec((EPI_TM, D_MODEL), lambda i: (i, ga_col)),
            pl.BlockSpec((EPI_TM, D_MODEL), lambda i: (i, gb_col)),
            pl.BlockSpec((EPI_TM, D_MODEL), lambda i: (i, 0)),
            pl.BlockSpec((EPI_TM, PLE_DIM), lambda i: (i, 0)),
            const((FOX_WIDTH, D_MODEL)),
            const((SWA_WIDTH, D_MODEL)),
            const((D_MODEL, D_MODEL)),
            const((1, D_MODEL)),
            const((PLE_DIM, D_MODEL)),
            const((D_MODEL, D_MODEL)),
        ],
        out_specs=pl.BlockSpec((EPI_TM, D_MODEL), lambda i: (i, 0)),
        out_shape=jax.ShapeDtypeStruct((t, D_MODEL), F32),
        compiler_params=pltpu.CompilerParams(
            dimension_semantics=("arbitrary",),
            vmem_limit_bytes=56 * 1024 * 1024),
        name="epilogue",
    )(ya, proj, yb, proj, proj, proj, x2, p2, wof, wos, wout, gpost, wple, wgate)


def _layer(x2, p2, pos_col, batch, seq, pre_g, w_in, b_forget, sinks, w_o_fox, w_o_swa,
           w_out, post_g, w_ple, w_ple_gate):
    segs = [w_in[:, _REF_OFF[n][0]:_REF_OFF[n][0] + _REF_OFF[n][1]] for n in _PROJ_ORDER]
    segs.append(jnp.zeros((D_MODEL, PROJ_COLS - PROJ_USED), w_in.dtype))
    w_perm = jnp.concatenate(segs, axis=1).astype(BF16)
    f0 = _REF_OFF["f_a"][0]
    wf_t = jnp.pad(w_in[:, f0:f0 + FOX_HEADS].T, ((0, F_ROWS - FOX_HEADS), (0, 0))).astype(BF16)

    proj, ft = _inproj(x2, pre_g.reshape(1, D_MODEL), w_perm, wf_t)
    c = _forget_cumsum(ft, b_forget.reshape(FOX_HEADS, 1).astype(F32), batch, seq)
    c3 = c.reshape(FOX_HEADS, 1, batch * seq)
    ya = _fox_attention(proj, c3, batch, seq)

    half = SWA_HEAD_DIM // 2
    inv = ROPE_THETA ** (-jnp.arange(half, dtype=F32) / half)
    inv_row = jnp.tile(inv, LANES // half).reshape(1, LANES)
    yb = _swa_attention(sinks.astype(F32), proj, pos_col, inv_row, batch, seq)

    return _epilogue(ya, yb, proj, x2, p2,
                     w_o_fox.astype(BF16), w_o_swa.astype(BF16), w_out.astype(BF16),
                     post_g.reshape(1, D_MODEL), w_ple.astype(BF16), w_ple_gate.astype(BF16))


def kernel(x, p, positions, pre_norm_g, w_in, b_forget, sinks, w_o_fox, w_o_swa, w_out,
           post_norm_g, w_ple, w_ple_gate):
    batch, seq, _ = x.shape
    depth = p.shape[0]
    x2 = x.reshape(batch * seq, D_MODEL)
    pos_col = positions.reshape(batch * seq, 1)
    for i in range(depth):
        x2 = _layer(x2, p[i].reshape(batch * seq, PLE_DIM), pos_col, batch, seq,
                    pre_norm_g[i], w_in[i], b_forget[i], sinks[i], w_o_fox[i], w_o_swa[i],
                    w_out[i], post_norm_g[i], w_ple[i], w_ple_gate[i])
    return x2.reshape(batch, seq, D_MODEL)
```

```python
import math

import jax
import jax.numpy as jnp
from jax import lax
from jax.experimental import pallas as pl
from jax.experimental.pallas import tpu as pltpu

F32 = jnp.float32
BF16 = jnp.bfloat16

D_MODEL = 2048
FOX_HEADS = 8
FOX_HEAD_DIM = 128
FOX_WIDTH = FOX_HEADS * FOX_HEAD_DIM
SWA_Q_HEADS = 16
SWA_KV_HEADS = 2
SWA_HEAD_DIM = 64
SWA_WIDTH = SWA_Q_HEADS * SWA_HEAD_DIM
SWA_KV_WIDTH = SWA_KV_HEADS * SWA_HEAD_DIM
SWA_GROUP = SWA_Q_HEADS // SWA_KV_HEADS
ROPE_HALF = SWA_HEAD_DIM // 2
WINDOW = 128
ROPE_THETA = 10000.0
PLE_DIM = 256
NORM_EPS = 1e-6
LANES = 128
LOG2E = math.log2(math.e)
NEG_BIG = -1e30

_REF_SPLITS = (FOX_WIDTH, FOX_WIDTH, FOX_WIDTH, FOX_WIDTH, FOX_HEADS, SWA_WIDTH,
               SWA_KV_WIDTH, SWA_KV_WIDTH, SWA_WIDTH, D_MODEL, D_MODEL)
_REF_NAMES = ("q_a", "k_a", "v_a", "z_a", "f_a", "q_b", "k_b", "v_b", "z_b", "g_a", "g_b")
_REF_OFF = {}
_o = 0
for _n, _s in zip(_REF_NAMES, _REF_SPLITS):
    _REF_OFF[_n] = _o
    _o += _s

IN_TM = 1024
IN_TN = 512
NORM_ROWS = 256
F_ROWS = 16

_W_GROUPS = (
    (_REF_OFF["q_a"], 4 * FOX_WIDTH),
    (_REF_OFF["q_b"], SWA_WIDTH),
    (_REF_OFF["z_b"], SWA_WIDTH),
    (_REF_OFF["g_a"], 2 * D_MODEL),
    (_REF_OFF["k_b"], 2 * SWA_KV_WIDTH),
)
_GROUP_TILES = tuple(-(-w // IN_TN) for _, w in _W_GROUPS)
_GROUP_START = tuple(sum(_GROUP_TILES[:i]) for i in range(len(_W_GROUPS)))
PROJ_TILES = sum(_GROUP_TILES)
PROJ_COLS = PROJ_TILES * IN_TN
_PROJ_OFF = {
    "q_a": 0, "k_a": FOX_WIDTH, "v_a": 2 * FOX_WIDTH, "z_a": 3 * FOX_WIDTH,
    "q_b": _GROUP_START[1] * IN_TN,
    "z_b": _GROUP_START[2] * IN_TN,
    "g_a": _GROUP_START[3] * IN_TN, "g_b": _GROUP_START[3] * IN_TN + D_MODEL,
    "k_b": _GROUP_START[4] * IN_TN, "v_b": _GROUP_START[4] * IN_TN + SWA_KV_WIDTH,
}
Q_A_TILES = FOX_WIDTH // IN_TN
Q_A_SCALE = FOX_HEAD_DIM ** -0.5 * LOG2E

FOX_T = 512
FOX_CHUNK = 512

SWA_NW = 4
SWA_QB = SWA_NW * WINDOW

EPI_TM = 256


def _inproj_kernel(x_ref, g_ref, w0_ref, w1_ref, w2_ref, w3_ref, w4_ref, wf_ref,
                   proj_ref, ft_ref, h_ref):
    j = pl.program_id(1)

    @pl.when(j == 0)
    def _():
        for r in range(IN_TM // NORM_ROWS):
            rows = slice(r * NORM_ROWS, (r + 1) * NORM_ROWS)
            x = x_ref[rows, :]
            ms = jnp.mean(x * x, axis=-1, keepdims=True)
            h = (x * lax.rsqrt(ms + NORM_EPS)) * g_ref[...]
            h_ref[rows, :] = h.astype(BF16)
        ft_ref[...] = lax.dot_general(wf_ref[...], h_ref[...], (((1,), (1,)), ((), ())),
                                      preferred_element_type=F32)

    for gi, w_ref in enumerate((w0_ref, w1_ref, w2_ref, w3_ref, w4_ref)):
        start, n = _GROUP_START[gi], _GROUP_TILES[gi]

        @pl.when((j >= start) & (j < start + n))
        def _(w_ref=w_ref, gi=gi):
            acc = jnp.dot(h_ref[...], w_ref[...], preferred_element_type=F32)
            if gi == 0:
                acc = acc * jnp.where(j < Q_A_TILES, Q_A_SCALE, 1.0)
            proj_ref[...] = acc.astype(BF16)


def _inproj(x2, g, w_groups, wf_t):
    t = x2.shape[0]

    def w_spec(gi):
        start, n = _GROUP_START[gi], _GROUP_TILES[gi]
        return pl.BlockSpec((D_MODEL, IN_TN),
                            lambda i, j: (0, jnp.clip(j - start, 0, n - 1)))

    return pl.pallas_call(
        _inproj_kernel,
        grid=(t // IN_TM, PROJ_TILES),
        in_specs=[
            pl.BlockSpec((IN_TM, D_MODEL), lambda i, j: (i, 0)),
            pl.BlockSpec((1, D_MODEL), lambda i, j: (0, 0)),
            *[w_spec(gi) for gi in range(len(_W_GROUPS))],
            pl.BlockSpec((F_ROWS, D_MODEL), lambda i, j: (0, 0)),
        ],
        out_specs=[
            pl.BlockSpec((IN_TM, IN_TN), lambda i, j: (i, j)),
            pl.BlockSpec((F_ROWS, IN_TM), lambda i, j: (0, i)),
        ],
        out_shape=[
            jax.ShapeDtypeStruct((t, PROJ_COLS), BF16),
            jax.ShapeDtypeStruct((F_ROWS, t), F32),
        ],
        scratch_shapes=[pltpu.VMEM((IN_TM, D_MODEL), BF16)],
        compiler_params=pltpu.CompilerParams(
            dimension_semantics=("arbitrary", "arbitrary"),
            vmem_limit_bytes=56 * 1024 * 1024),
        name="inproj",
    )(x2, g, *w_groups, wf_t)


def _cumsum_kernel(ft_ref, b_ref, cb_ref):
    f = ft_ref[0:FOX_HEADS, :] + b_ref[...]
    lf = jnp.minimum(f, 0.0) - jnp.log1p(jnp.exp(-jnp.abs(f)))
    seq = lf.shape[1]
    lane = lax.broadcasted_iota(jnp.int32, lf.shape, 1)
    c = lf
    shift = 1
    while shift < seq:
        c = c + jnp.where(lane >= shift, pltpu.roll(c, shift, axis=1), 0.0)
        shift *= 2
    neg = c * (-LOG2E)
    padded = jnp.concatenate([neg, jnp.zeros((LANES - FOX_HEADS, seq), F32)], axis=0)
    cb_ref[...] = padded.T


def _forget_cumsum(ft, b_col, batch, seq):
    return pl.pallas_call(
        _cumsum_kernel,
        grid=(batch,),
        in_specs=[
            pl.BlockSpec((F_ROWS, seq), lambda b: (0, b)),
            pl.BlockSpec((FOX_HEADS, 1), lambda b: (0, 0)),
        ],
        out_specs=pl.BlockSpec((seq, LANES), lambda b: (b, 0)),
        out_shape=jax.ShapeDtypeStruct((batch * seq, LANES), F32),
        name="forget_cumsum",
    )(ft, b_col)


def _fox_kernel(q_ref, k_ref, v_ref, cb_ref, o_ref, vt_ref, cbb_ref, qt_ref, acc_ref):
    h = pl.program_id(1)
    qi = pl.program_id(2)
    seq = k_ref.shape[0]
    t = FOX_T

    @pl.when(qi == 0)
    def _():
        lane = lax.broadcasted_iota(jnp.int32, (FOX_CHUNK, LANES), 1)
        for r in range(seq // FOX_CHUNK):
            rows = slice(r * FOX_CHUNK, (r + 1) * FOX_CHUNK)
            vt_ref[:, rows] = v_ref[rows, :].astype(F32).T.astype(BF16)
            col = jnp.sum(jnp.where(lane == h, cb_ref[rows, :], 0.0), axis=1, keepdims=True)
            cbb_ref[rows, :] = jnp.broadcast_to(col, (FOX_CHUNK, LANES))

    qt_ref[...] = q_ref[...].astype(F32).T.astype(BF16)
    acc_ref[...] = jnp.zeros(acc_ref.shape, F32)

    def tile(k0, m, l, masked):
        rows = pl.ds(k0, t)
        s = jnp.dot(k_ref[rows, :], qt_ref[...], preferred_element_type=F32)
        s = s + jnp.concatenate([cbb_ref[rows, :]] * (t // LANES), axis=1)
        if masked:
            key = lax.broadcasted_iota(jnp.int32, s.shape, 0)
            qry = lax.broadcasted_iota(jnp.int32, s.shape, 1)
            s = jnp.where(key <= qry, s, NEG_BIG)
        m_new = jnp.maximum(m, jnp.max(s, axis=0, keepdims=True))
        alpha = jnp.exp2(m - m_new)
        p = jnp.exp2(s - m_new)
        l_new = alpha * l + jnp.sum(p, axis=0, keepdims=True)
        acc_ref[...] = alpha * acc_ref[...] + jnp.dot(
            vt_ref[:, rows], p.astype(BF16), preferred_element_type=F32)
        return m_new, l_new

    m0 = jnp.full((1, t), NEG_BIG, F32)
    l0 = jnp.zeros((1, t), F32)
    m, l = lax.fori_loop(
        0, qi, lambda ki, c: tile(pl.multiple_of(ki * t, t), c[0], c[1], False), (m0, l0))
    m, l = tile(pl.multiple_of(qi * t, t), m, l, True)
    o_ref[...] = (acc_ref[...] / l).T.astype(o_ref.dtype)


def _fox_attention(proj, cb_tok, batch, seq):
    nq = seq // FOX_T
    qcol = _PROJ_OFF["q_a"] // FOX_HEAD_DIM
    kcol = _PROJ_OFF["k_a"] // FOX_HEAD_DIM
    vcol = _PROJ_OFF["v_a"] // FOX_HEAD_DIM
    return pl.pallas_call(
        _fox_kernel,
        grid=(batch, FOX_HEADS, nq),
        in_specs=[
            pl.BlockSpec((FOX_T, FOX_HEAD_DIM), lambda b, h, qi: (b * nq + qi, qcol + h)),
            pl.BlockSpec((seq, FOX_HEAD_DIM), lambda b, h, qi: (b, kcol + h)),
            pl.BlockSpec((seq, FOX_HEAD_DIM), lambda b, h, qi: (b, vcol + h)),
            pl.BlockSpec((seq, LANES), lambda b, h, qi: (b, 0)),
        ],
        out_specs=pl.BlockSpec((FOX_T, FOX_HEAD_DIM), lambda b, h, qi: (b * nq + qi, h)),
        out_shape=jax.ShapeDtypeStruct((batch * seq, FOX_WIDTH), BF16),
        scratch_shapes=[
            pltpu.VMEM((FOX_HEAD_DIM, seq), BF16),
            pltpu.VMEM((seq, LANES), F32),
            pltpu.VMEM((FOX_HEAD_DIM, FOX_T), BF16),
            pltpu.VMEM((FOX_HEAD_DIM, FOX_T), F32),
        ],
        compiler_params=pltpu.CompilerParams(
            dimension_semantics=("arbitrary", "arbitrary", "arbitrary"),
            vmem_limit_bytes=40 * 1024 * 1024),
        name="fox_attention",
    )(proj, proj, proj, cb_tok)


def _rope_t(xt, cos, sin):
    out = []
    for hd in range(xt.shape[0] // SWA_HEAD_DIM):
        x1 = xt[hd * SWA_HEAD_DIM: hd * SWA_HEAD_DIM + ROPE_HALF]
        x2 = xt[hd * SWA_HEAD_DIM + ROPE_HALF: (hd + 1) * SWA_HEAD_DIM]
        out.append(x1 * cos - x2 * sin)
        out.append(x2 * cos + x1 * sin)
    return jnp.concatenate(out, axis=0)


def _swa_kernel(sink_ref, q_ref, kc_ref, kp_ref, vc_ref, vp_ref, posc_ref, posp_ref,
                inv_ref, o_ref):
    n = pl.program_id(1)
    w = WINDOW
    inv = inv_ref[...]

    def tables(pos_row):
        ang = inv * pos_row.astype(F32)
        return jnp.cos(ang), jnp.sin(ang)

    def rope_k(k_nat, cos, sin):
        kt = _rope_t(k_nat.astype(F32).T, cos, sin)
        return kt.T.astype(BF16)

    cos_p, sin_p = tables(posp_ref[...])
    k_prev = rope_k(kp_ref[...], cos_p, sin_p)
    v_prev = vp_ref[...]

    key = lax.broadcasted_iota(jnp.int32, (2 * w, SWA_GROUP * w), 0)
    qry = lax.broadcasted_iota(jnp.int32, (2 * w, SWA_GROUP * w), 1) % w
    band = (key <= qry + w) & (key > qry)
    first_key = jnp.where(n > 0, 0, w)
    q_scale = SWA_HEAD_DIM ** -0.5 * LOG2E
    zeros_half = jnp.zeros((SWA_HEAD_DIM, SWA_GROUP * w), F32)

    for wi in range(SWA_NW):
        tok = slice(wi * w, (wi + 1) * w)
        cos, sin = tables(posc_ref[:, tok])
        k_cur = rope_k(kc_ref[tok, :], cos, sin)
        v_cur = vc_ref[tok, :]
        k_all = jnp.concatenate([k_prev, k_cur], axis=0)
        v_all_t = jnp.concatenate([v_prev, v_cur], axis=0).astype(F32).T.astype(BF16)

        qf = q_ref[tok, :].astype(F32)
        qt = jnp.concatenate(
            [qf[:, c * LANES:(c + 1) * LANES].T for c in range(SWA_WIDTH // LANES)], axis=0)
        qt = _rope_t(qt, cos * q_scale, sin * q_scale)

        mask = band & (key >= first_key) if wi == 0 else band

        for g in range(SWA_KV_HEADS):
            heads = range(g * SWA_GROUP, (g + 1) * SWA_GROUP)
            q_g = jnp.concatenate(
                [qt[hd * SWA_HEAD_DIM:(hd + 1) * SWA_HEAD_DIM] for hd in heads], axis=1)
            parts = [zeros_half] * SWA_KV_HEADS
            parts[g] = q_g
            q_z = jnp.concatenate(parts, axis=0).astype(BF16)
            sink = jnp.concatenate(
                [jnp.full((1, w), sink_ref[hd] * LOG2E, F32) for hd in heads], axis=1)

            s = jnp.dot(k_all, q_z, preferred_element_type=F32)
            s = jnp.where(mask, s, NEG_BIG)
            m = jnp.maximum(jnp.max(s, axis=0, keepdims=True), sink)
            e = jnp.exp2(s - m)
            denom = jnp.sum(e, axis=0, keepdims=True) + jnp.exp2(sink - m)
            out_t = jnp.dot(v_all_t, e.astype(BF16), preferred_element_type=F32)
            out_t = out_t[g * SWA_HEAD_DIM:(g + 1) * SWA_HEAD_DIM] / denom
            for jj in range(SWA_GROUP // 2):
                pair = jnp.concatenate(
                    [out_t[:, (2 * jj) * w:(2 * jj + 1) * w],
                     out_t[:, (2 * jj + 1) * w:(2 * jj + 2) * w]], axis=0)
                col = (g * (SWA_GROUP // 2) + jj) * LANES
                o_ref[tok, col:col + LANES] = pair.T.astype(o_ref.dtype)

        k_prev, v_prev = k_cur, v_cur


def _swa_attention(sinks, proj, pos_row, inv_tab, batch, seq):
    nb = seq // SWA_QB
    per = SWA_QB // WINDOW
    qcol = _PROJ_OFF["q_b"] // SWA_WIDTH
    kcol = _PROJ_OFF["k_b"] // LANES
    vcol = _PROJ_OFF["v_b"] // LANES

    def cur(b, n):
        return b * nb + n

    def prev(b, n):
        return jnp.maximum((b * nb + n) * per - 1, 0)

    return pl.pallas_call(
        _swa_kernel,
        grid=(batch, nb),
        in_specs=[
            pl.BlockSpec(memory_space=pltpu.SMEM),
            pl.BlockSpec((SWA_QB, SWA_WIDTH), lambda b, n: (cur(b, n), qcol)),
            pl.BlockSpec((SWA_QB, LANES), lambda b, n: (cur(b, n), kcol)),
            pl.BlockSpec((WINDOW, LANES), lambda b, n: (prev(b, n), kcol)),
            pl.BlockSpec((SWA_QB, LANES), lambda b, n: (cur(b, n), vcol)),
            pl.BlockSpec((WINDOW, LANES), lambda b, n: (prev(b, n), vcol)),
            pl.BlockSpec((1, SWA_QB), lambda b, n: (0, cur(b, n))),
            pl.BlockSpec((1, WINDOW), lambda b, n: (0, prev(b, n))),
            pl.BlockSpec((ROPE_HALF, LANES), lambda b, n: (0, 0)),
        ],
        out_specs=pl.BlockSpec((SWA_QB, SWA_WIDTH), lambda b, n: (cur(b, n), 0)),
        out_shape=jax.ShapeDtypeStruct((batch * seq, SWA_WIDTH), BF16),
        compiler_params=pltpu.CompilerParams(
            dimension_semantics=("arbitrary", "arbitrary")),
        name="swa_attention",
    )(sinks, proj, proj, proj, proj, proj, pos_row, pos_row, inv_tab)


def _silu(z):
    return z * jax.nn.sigmoid(z)


def _epilogue_kernel(ya_ref, za_ref, yb_ref, zb_ref, ga_ref, gb_ref, x_ref, p_ref,
                     wof_ref, wos_ref, wout_ref, gpost_ref, wple_ref, wgate_ref, o_ref):
    ua = (ya_ref[...].astype(F32) * _silu(za_ref[...].astype(F32))).astype(BF16)
    ub = (yb_ref[...].astype(F32) * _silu(zb_ref[...].astype(F32))).astype(BF16)
    oa = jnp.dot(ua, wof_ref[...], preferred_element_type=F32)
    ob = jnp.dot(ub, wos_ref[...], preferred_element_type=F32)
    merged = (jax.nn.sigmoid(ga_ref[...].astype(F32)) * oa
              + jax.nn.sigmoid(gb_ref[...].astype(F32)) * ob)
    out = jnp.dot(merged.astype(BF16), wout_ref[...], preferred_element_type=F32)
    ms = jnp.mean(out * out, axis=-1, keepdims=True)
    x1 = x_ref[...] + (out * lax.rsqrt(ms + NORM_EPS)) * gpost_ref[...]
    e = jnp.dot(p_ref[...].astype(BF16), wple_ref[...], preferred_element_type=F32)
    gate = jax.nn.sigmoid(jnp.dot(x1.astype(BF16), wgate_ref[...],
                                  preferred_element_type=F32))
    o_ref[...] = x1 + gate * e


def _epilogue(ya, yb, proj, x2, p2, wof, wos, wout, gpost, wple, wgate):
    t = x2.shape[0]
    za_col = _PROJ_OFF["z_a"] // FOX_WIDTH
    zb_col = _PROJ_OFF["z_b"] // SWA_WIDTH
    ga_col = _PROJ_OFF["g_a"] // D_MODEL
    gb_col = _PROJ_OFF["g_b"] // D_MODEL
    once = pl.Buffered(1)

    def const(shape):
        return pl.BlockSpec(shape, lambda i: (0, 0), pipeline_mode=once)

    return pl.pallas_call(
        _epilogue_kernel,
        grid=(t // EPI_TM,),
        in_specs=[
            pl.BlockSpec((EPI_TM, FOX_WIDTH), lambda i: (i, 0)),
            pl.BlockSpec((EPI_TM, FOX_WIDTH), lambda i: (i, za_col)),
            pl.BlockSpec((EPI_TM, SWA_WIDTH), lambda i: (i, 0)),
            pl.BlockSpec((EPI_TM, SWA_WIDTH), lambda i: (i, zb_col)),
            pl.BlockSpec((EPI_TM, D_MODEL), lambda i: (i, ga_col)),
            pl.BlockSpec((EPI_TM, D_MODEL), lambda i: (i, gb_col)),
            pl.BlockSpec((EPI_TM, D_MODEL), lambda i: (i, 0)),
            pl.BlockSpec((EPI_TM, PLE_DIM), lambda i: (i, 0)),
            const((FOX_WIDTH, D_MODEL)),
            const((SWA_WIDTH, D_MODEL)),
            const((D_MODEL, D_MODEL)),
            const((1, D_MODEL)),
            const((PLE_DIM, D_MODEL)),
            const((D_MODEL, D_MODEL)),
        ],
        out_specs=pl.BlockSpec((EPI_TM, D_MODEL), lambda i: (i, 0)),
        out_shape=jax.ShapeDtypeStruct((t, D_MODEL), F32),
        compiler_params=pltpu.CompilerParams(
            dimension_semantics=("arbitrary",),
            vmem_limit_bytes=56 * 1024 * 1024),
        name="epilogue",
    )(ya, proj, yb, proj, proj, proj, x2, p2, wof, wos, wout, gpost, wple, wgate)


def _layer(x2, p2, pos_row, batch, seq, pre_g, w_in, b_forget, sinks, w_o_fox, w_o_swa,
           w_out, post_g, w_ple, w_ple_gate):
    w_groups = []
    for (c0, width), tiles in zip(_W_GROUPS, _GROUP_TILES):
        wg = w_in[:, c0:c0 + width].astype(BF16)
        if tiles * IN_TN != width:
            wg = jnp.pad(wg, ((0, 0), (0, tiles * IN_TN - width)))
        w_groups.append(wg)
    f0 = _REF_OFF["f_a"]
    wf_t = jnp.pad(w_in[:, f0:f0 + FOX_HEADS].T, ((0, F_ROWS - FOX_HEADS), (0, 0))).astype(BF16)

    proj, ft = _inproj(x2, pre_g.reshape(1, D_MODEL), w_groups, wf_t)
    cb_tok = _forget_cumsum(ft, b_forget.reshape(FOX_HEADS, 1).astype(F32), batch, seq)
    ya = _fox_attention(proj, cb_tok, batch, seq)

    inv = ROPE_THETA ** (-jnp.arange(ROPE_HALF, dtype=F32) / ROPE_HALF)
    inv_tab = jnp.broadcast_to(inv[:, None], (ROPE_HALF, LANES))
    yb = _swa_attention(sinks.astype(F32), proj, pos_row, inv_tab, batch, seq)

    return _epilogue(ya, yb, proj, x2, p2,
                     w_o_fox.astype(BF16), w_o_swa.astype(BF16), w_out.astype(BF16),
                     post_g.reshape(1, D_MODEL), w_ple.astype(BF16), w_ple_gate.astype(BF16))


def kernel(x, p, positions, pre_norm_g, w_in, b_forget, sinks, w_o_fox, w_o_swa, w_out,
           post_norm_g, w_ple, w_ple_gate):
    batch, seq, _ = x.shape
    depth = p.shape[0]
    x2 = x.reshape(batch * seq, D_MODEL)
    pos_row = positions.reshape(1, batch * seq)
    for i in range(depth):
        x2 = _layer(x2, p[i].reshape(batch * seq, PLE_DIM), pos_row, batch, seq,
                    pre_norm_g[i], w_in[i], b_forget[i], sinks[i], w_o_fox[i], w_o_swa[i],
                    w_out[i], post_norm_g[i], w_ple[i], w_ple_gate[i])
    return x2.reshape(batch, seq, D_MODEL)
```

```python
import math

import jax
import jax.numpy as jnp
from jax import lax
from jax.experimental import pallas as pl
from jax.experimental.pallas import tpu as pltpu

F32 = jnp.float32
BF16 = jnp.bfloat16

D_MODEL = 2048
FOX_HEADS = 8
FOX_HEAD_DIM = 128
FOX_WIDTH = FOX_HEADS * FOX_HEAD_DIM
SWA_Q_HEADS = 16
SWA_KV_HEADS = 2
SWA_HEAD_DIM = 64
SWA_WIDTH = SWA_Q_HEADS * SWA_HEAD_DIM
SWA_KV_WIDTH = SWA_KV_HEADS * SWA_HEAD_DIM
SWA_GROUP = SWA_Q_HEADS // SWA_KV_HEADS
ROPE_HALF = SWA_HEAD_DIM // 2
WINDOW = 128
ROPE_THETA = 10000.0
PLE_DIM = 256
NORM_EPS = 1e-6
LANES = 128
LOG2E = math.log2(math.e)
NEG_BIG = -1e30

_REF_SPLITS = (FOX_WIDTH, FOX_WIDTH, FOX_WIDTH, FOX_WIDTH, FOX_HEADS, SWA_WIDTH,
               SWA_KV_WIDTH, SWA_KV_WIDTH, SWA_WIDTH, D_MODEL, D_MODEL)
_REF_NAMES = ("q_a", "k_a", "v_a", "z_a", "f_a", "q_b", "k_b", "v_b", "z_b", "g_a", "g_b")
_REF_OFF = {}
_o = 0
for _n, _s in zip(_REF_NAMES, _REF_SPLITS):
    _REF_OFF[_n] = _o
    _o += _s

IN_TM = 1024
IN_TN = 1536
NORM_ROWS = 256
F_ROWS = 16

PREP_TN = 512
_W_GROUPS = (
    (_REF_OFF["q_a"], 4 * FOX_WIDTH),
    (_REF_OFF["q_b"], SWA_WIDTH),
    (_REF_OFF["z_b"], SWA_WIDTH),
    (_REF_OFF["g_a"], 2 * D_MODEL),
    (_REF_OFF["k_b"], 2 * SWA_KV_WIDTH),
)
_GROUP_TILES = tuple(-(-w // PREP_TN) for _, w in _W_GROUPS)
_GROUP_START = tuple(sum(_GROUP_TILES[:i]) for i in range(len(_W_GROUPS)))
PREP_TILES = sum(_GROUP_TILES)
PROJ_COLS = PREP_TILES * PREP_TN
_PROJ_OFF = {
    "q_a": 0, "k_a": FOX_WIDTH, "v_a": 2 * FOX_WIDTH, "z_a": 3 * FOX_WIDTH,
    "q_b": _GROUP_START[1] * PREP_TN,
    "z_b": _GROUP_START[2] * PREP_TN,
    "g_a": _GROUP_START[3] * PREP_TN, "g_b": _GROUP_START[3] * PREP_TN + D_MODEL,
    "k_b": _GROUP_START[4] * PREP_TN, "v_b": _GROUP_START[4] * PREP_TN + SWA_KV_WIDTH,
}
PREP_BLOCKS = PREP_TN // LANES + 1
PREP_SHIFT = _REF_OFF["q_b"] % LANES
assert all(c0 % LANES == (PREP_SHIFT if gi else 0) for gi, (c0, _) in enumerate(_W_GROUPS))
Q_A_SCALE = FOX_HEAD_DIM ** -0.5 * LOG2E

FOX_T = 512
FOX_CHUNK = 512
FOX_PAIR = 2
BIAS_TERMS = 3

SWA_NW = 4
SWA_QB = SWA_NW * WINDOW

EPI_TM = 256


def _prep_kernel(*refs):
    blocks, out_ref = refs[:PREP_BLOCKS], refs[PREP_BLOCKS]
    j = pl.program_id(0)
    rows_per = 256
    for r in range(D_MODEL // rows_per):
        rows = slice(r * rows_per, (r + 1) * rows_per)

        @pl.when(j < _GROUP_TILES[0])
        def _(rows=rows):
            for c in range(PREP_TN // LANES):
                out_ref[rows, c * LANES:(c + 1) * LANES] = blocks[c][rows, :].astype(BF16)

        @pl.when(j >= _GROUP_TILES[0])
        def _(rows=rows):
            xs = jnp.concatenate([b[rows, :] for b in blocks], axis=1)
            out_ref[rows, :] = xs[:, PREP_SHIFT:PREP_SHIFT + PREP_TN].astype(BF16)


def _prep_src_block(j):
    base = jnp.int32(0)
    for (c0, _), start in zip(_W_GROUPS, _GROUP_START):
        base = jnp.where(j >= start, c0 // LANES + (j - start) * (PREP_TN // LANES), base)
    return base


def _prep_weights(w_in):
    n_in = w_in.shape[1]
    last_block = (n_in - 1) // LANES

    def spec(t):
        return pl.BlockSpec((D_MODEL, LANES),
                            lambda j: (0, jnp.minimum(_prep_src_block(j) + t, last_block)))

    return pl.pallas_call(
        _prep_kernel,
        grid=(PREP_TILES,),
        in_specs=[spec(t) for t in range(PREP_BLOCKS)],
        out_specs=pl.BlockSpec((D_MODEL, PREP_TN), lambda j: (0, j)),
        out_shape=jax.ShapeDtypeStruct((D_MODEL, PROJ_COLS), BF16),
        compiler_params=pltpu.CompilerParams(dimension_semantics=("arbitrary",)),
        name="prep_weights",
    )(*([w_in] * PREP_BLOCKS))


def _inproj_kernel(x_ref, g_ref, w_ref, scale_ref, wf_ref, proj_ref, ft_ref, h_ref):
    j = pl.program_id(1)

    @pl.when(j == 0)
    def _():
        for r in range(IN_TM // NORM_ROWS):
            rows = slice(r * NORM_ROWS, (r + 1) * NORM_ROWS)
            x = x_ref[rows, :]
            ms = jnp.mean(x * x, axis=-1, keepdims=True)
            h = (x * lax.rsqrt(ms + NORM_EPS)) * g_ref[...]
            h_ref[rows, :] = h.astype(BF16)
        ft_ref[...] = lax.dot_general(wf_ref[...], h_ref[...], (((1,), (1,)), ((), ())),
                                      preferred_element_type=F32)

    acc = jnp.dot(h_ref[...], w_ref[...], preferred_element_type=F32)
    proj_ref[...] = (acc * scale_ref[...]).astype(BF16)


def _inproj(x2, g, w_perm, col_scale, wf_t):
    t = x2.shape[0]
    return pl.pallas_call(
        _inproj_kernel,
        grid=(t // IN_TM, PROJ_COLS // IN_TN),
        in_specs=[
            pl.BlockSpec((IN_TM, D_MODEL), lambda i, j: (i, 0)),
            pl.BlockSpec((1, D_MODEL), lambda i, j: (0, 0)),
            pl.BlockSpec((D_MODEL, IN_TN), lambda i, j: (0, j)),
            pl.BlockSpec((1, IN_TN), lambda i, j: (0, j)),
            pl.BlockSpec((F_ROWS, D_MODEL), lambda i, j: (0, 0)),
        ],
        out_specs=[
            pl.BlockSpec((IN_TM, IN_TN), lambda i, j: (i, j)),
            pl.BlockSpec((F_ROWS, IN_TM), lambda i, j: (0, i)),
        ],
        out_shape=[
            jax.ShapeDtypeStruct((t, PROJ_COLS), BF16),
            jax.ShapeDtypeStruct((F_ROWS, t), F32),
        ],
        scratch_shapes=[pltpu.VMEM((IN_TM, D_MODEL), BF16)],
        compiler_params=pltpu.CompilerParams(
            dimension_semantics=("arbitrary", "arbitrary"),
            vmem_limit_bytes=56 * 1024 * 1024),
        name="inproj",
    )(x2, g, w_perm, col_scale, wf_t)


def _cumsum_kernel(ft_ref, b_ref, cb_ref):
    f = ft_ref[0:FOX_HEADS, :] + b_ref[...]
    lf = jnp.minimum(f, 0.0) - jnp.log1p(jnp.exp(-jnp.abs(f)))
    seq = lf.shape[1]
    lane = lax.broadcasted_iota(jnp.int32, lf.shape, 1)
    c = lf
    shift = 1
    while shift < seq:
        c = c + jnp.where(lane >= shift, pltpu.roll(c, shift, axis=1), 0.0)
        shift *= 2
    neg = c * (-LOG2E)
    padded = jnp.concatenate([neg, jnp.zeros((LANES - FOX_HEADS, seq), F32)], axis=0)
    cb_ref[...] = padded.T


def _forget_cumsum(ft, b_col, batch, seq):
    return pl.pallas_call(
        _cumsum_kernel,
        grid=(batch,),
        in_specs=[
            pl.BlockSpec((F_ROWS, seq), lambda b: (0, b)),
            pl.BlockSpec((FOX_HEADS, 1), lambda b: (0, 0)),
        ],
        out_specs=pl.BlockSpec((seq, LANES), lambda b: (b, 0)),
        out_shape=jax.ShapeDtypeStruct((batch * seq, LANES), F32),
        name="forget_cumsum",
    )(ft, b_col)


def _fox_kernel(q_ref, k_ref, v_ref, cb_ref, o_ref, vt_ref, kb_ref, qt_ref, acc_ref,
                s0_ref, s1_ref):
    hp = pl.program_id(1)
    qi = pl.program_id(2)
    seq = k_ref.shape[0]
    t = FOX_T
    d = FOX_HEAD_DIM

    @pl.when(qi == 0)
    def _():
        lane = lax.broadcasted_iota(jnp.int32, (FOX_CHUNK, LANES), 1)
        row = lax.broadcasted_iota(jnp.int32, (d, t), 0)
        ones_rows = jnp.where(row < BIAS_TERMS, 1.0, 0.0).astype(BF16)
        for hh in range(FOX_PAIR):
            qt_ref[hh, d:2 * d, :] = ones_rows
            for r in range(seq // FOX_CHUNK):
                rows = slice(r * FOX_CHUNK, (r + 1) * FOX_CHUNK)
                vt_ref[hh, :, rows] = v_ref[rows, hh * d:(hh + 1) * d].astype(F32).T.astype(BF16)
                col = jnp.sum(jnp.where(lane == hp * FOX_PAIR + hh, cb_ref[rows, :], 0.0),
                              axis=1, keepdims=True)
                pieces = jnp.zeros((FOX_CHUNK, LANES), F32)
                rest = col
                for i in range(BIAS_TERMS):
                    piece = rest.astype(BF16).astype(F32)
                    pieces = jnp.where(lane == i, piece, pieces)
                    rest = rest - piece
                kb_ref[hh, rows, :] = pieces.astype(BF16)

    for hh in range(FOX_PAIR):
        qt_ref[hh, 0:d, :] = q_ref[:, hh * d:(hh + 1) * d].astype(F32).T.astype(BF16)
        acc_ref[hh] = jnp.zeros((d, t), F32)

    def scores(i, s_ref):
        rows = pl.ds(pl.multiple_of(i * t, t), t)
        for hh in range(FOX_PAIR):
            k_aug = jnp.concatenate([k_ref[rows, hh * d:(hh + 1) * d], kb_ref[hh, rows, :]], axis=1)
            s_ref[hh] = jnp.dot(k_aug, qt_ref[hh], preferred_element_type=F32)

    def absorb(i, s_ref, carry, masked):
        rows = pl.ds(pl.multiple_of(i * t, t), t)
        new = []
        for hh in range(FOX_PAIR):
            m, l = carry[hh]
            s = s_ref[hh]
            if masked:
                key = lax.broadcasted_iota(jnp.int32, s.shape, 0)
                qry = lax.broadcasted_iota(jnp.int32, s.shape, 1)
                s = jnp.where(key <= qry, s, NEG_BIG)
            m_new = jnp.maximum(m, jnp.max(s, axis=0, keepdims=True))
            alpha = jnp.exp2(m - m_new)
            p = jnp.exp2(s - m_new)
            l_new = alpha * l + jnp.sum(p, axis=0, keepdims=True)
            acc_ref[hh] = alpha * acc_ref[hh] + jnp.dot(
                vt_ref[hh, :, rows], p.astype(BF16), preferred_element_type=F32)
            new.append((m_new, l_new))
        return tuple(new)

    def step(i, s_cur, s_next, carry):
        scores(i + 1, s_next)
        return absorb(i, s_cur, carry, False)

    def pair(j, carry):
        carry = step(2 * j, s0_ref, s1_ref, carry)
        return step(2 * j + 1, s1_ref, s0_ref, carry)

    def odd_tail(carry):
        carry = step(qi - 1, s0_ref, s1_ref, carry)
        return absorb(qi, s1_ref, carry, True)

    def even_tail(carry):
        return absorb(qi, s0_ref, carry, True)

    init = tuple((jnp.full((1, t), NEG_BIG, F32), jnp.zeros((1, t), F32))
                 for _ in range(FOX_PAIR))
    scores(0, s0_ref)
    carry = lax.fori_loop(0, qi // 2, pair, init)
    carry = lax.cond(qi % 2 == 1, odd_tail, even_tail, carry)
    for hh in range(FOX_PAIR):
        o_ref[:, hh * d:(hh + 1) * d] = (acc_ref[hh] / carry[hh][1]).T.astype(o_ref.dtype)


def _fox_attention(proj, cb_tok, batch, seq):
    nq = seq // FOX_T
    width = FOX_PAIR * FOX_HEAD_DIM
    qcol = _PROJ_OFF["q_a"] // width
    kcol = _PROJ_OFF["k_a"] // width
    vcol = _PROJ_OFF["v_a"] // width
    return pl.pallas_call(
        _fox_kernel,
        grid=(batch, FOX_HEADS // FOX_PAIR, nq),
        in_specs=[
            pl.BlockSpec((FOX_T, width), lambda b, h, qi: (b * nq + qi, qcol + h)),
            pl.BlockSpec((seq, width), lambda b, h, qi: (b, kcol + h)),
            pl.BlockSpec((seq, width), lambda b, h, qi: (b, vcol + h)),
            pl.BlockSpec((seq, LANES), lambda b, h, qi: (b, 0)),
        ],
        out_specs=pl.BlockSpec((FOX_T, width), lambda b, h, qi: (b * nq + qi, h)),
        out_shape=jax.ShapeDtypeStruct((batch * seq, FOX_WIDTH), BF16),
        scratch_shapes=[
            pltpu.VMEM((FOX_PAIR, FOX_HEAD_DIM, seq), BF16),
            pltpu.VMEM((FOX_PAIR, seq, LANES), BF16),
            pltpu.VMEM((FOX_PAIR, 2 * FOX_HEAD_DIM, FOX_T), BF16),
            pltpu.VMEM((FOX_PAIR, FOX_HEAD_DIM, FOX_T), F32),
            pltpu.VMEM((FOX_PAIR, FOX_T, FOX_T), F32),
            pltpu.VMEM((FOX_PAIR, FOX_T, FOX_T), F32),
        ],
        compiler_params=pltpu.CompilerParams(
            dimension_semantics=("arbitrary", "arbitrary", "arbitrary"),
            vmem_limit_bytes=40 * 1024 * 1024),
        name="fox_attention",
    )(proj, proj, proj, cb_tok)


def _rope_t(xt, cos, sin):
    out = []
    for hd in range(xt.shape[0] // SWA_HEAD_DIM):
        x1 = xt[hd * SWA_HEAD_DIM: hd * SWA_HEAD_DIM + ROPE_HALF]
        x2 = xt[hd * SWA_HEAD_DIM + ROPE_HALF: (hd + 1) * SWA_HEAD_DIM]
        out.append(x1 * cos - x2 * sin)
        out.append(x2 * cos + x1 * sin)
    return jnp.concatenate(out, axis=0)


def _swa_kernel(sink_ref, q_ref, kc_ref, kp_ref, vc_ref, vp_ref, posc_ref, posp_ref,
                inv_ref, o_ref):
    n = pl.program_id(1)
    w = WINDOW
    inv = inv_ref[...]

    def tables(pos_row):
        ang = inv * pos_row.astype(F32)
        return jnp.cos(ang), jnp.sin(ang)

    def rope_k(k_nat, cos, sin):
        kt = _rope_t(k_nat.astype(F32).T, cos, sin)
        return kt.T.astype(BF16)

    cos_p, sin_p = tables(posp_ref[...])
    k_prev = rope_k(kp_ref[...], cos_p, sin_p)
    v_prev = vp_ref[...]

    key = lax.broadcasted_iota(jnp.int32, (2 * w, SWA_GROUP * w), 0)
    qry = lax.broadcasted_iota(jnp.int32, (2 * w, SWA_GROUP * w), 1) % w
    band = (key <= qry + w) & (key > qry)
    first_key = jnp.where(n > 0, 0, w)
    q_scale = SWA_HEAD_DIM ** -0.5 * LOG2E
    zeros_half = jnp.zeros((SWA_HEAD_DIM, SWA_GROUP * w), F32)

    for wi in range(SWA_NW):
        tok = slice(wi * w, (wi + 1) * w)
        cos, sin = tables(posc_ref[:, tok])
        k_cur = rope_k(kc_ref[tok, :], cos, sin)
        v_cur = vc_ref[tok, :]
        k_all = jnp.concatenate([k_prev, k_cur], axis=0)
        v_all_t = jnp.concatenate([v_prev, v_cur], axis=0).astype(F32).T.astype(BF16)

        qf = q_ref[tok, :].astype(F32)
        qt = jnp.concatenate(
            [qf[:, c * LANES:(c + 1) * LANES].T for c in range(SWA_WIDTH // LANES)], axis=0)
        qt = _rope_t(qt, cos * q_scale, sin * q_scale)

        mask = band & (key >= first_key) if wi == 0 else band

        for g in range(SWA_KV_HEADS):
            heads = range(g * SWA_GROUP, (g + 1) * SWA_GROUP)
            q_g = jnp.concatenate(
                [qt[hd * SWA_HEAD_DIM:(hd + 1) * SWA_HEAD_DIM] for hd in heads], axis=1)
            parts = [zeros_half] * SWA_KV_HEADS
            parts[g] = q_g
            q_z = jnp.concatenate(parts, axis=0).astype(BF16)
            sink = jnp.concatenate(
                [jnp.full((1, w), sink_ref[hd] * LOG2E, F32) for hd in heads], axis=1)

            s = jnp.dot(k_all, q_z, preferred_element_type=F32)
            s = jnp.where(mask, s, NEG_BIG)
            m = jnp.maximum(jnp.max(s, axis=0, keepdims=True), sink)
            e = jnp.exp2(s - m)
            denom = jnp.sum(e, axis=0, keepdims=True) + jnp.exp2(sink - m)
            out_t = jnp.dot(v_all_t, e.astype(BF16), preferred_element_type=F32)
            out_t = out_t[g * SWA_HEAD_DIM:(g + 1) * SWA_HEAD_DIM] / denom
            for jj in range(SWA_GROUP // 2):
                pair = jnp.concatenate(
                    [out_t[:, (2 * jj) * w:(2 * jj + 1) * w],
                     out_t[:, (2 * jj + 1) * w:(2 * jj + 2) * w]], axis=0)
                col = (g * (SWA_GROUP // 2) + jj) * LANES
                o_ref[tok, col:col + LANES] = pair.T.astype(o_ref.dtype)

        k_prev, v_prev = k_cur, v_cur


def _swa_attention(sinks, proj, pos_row, inv_tab, batch, seq):
    nb = seq // SWA_QB
    per = SWA_QB // WINDOW
    qcol = _PROJ_OFF["q_b"] // SWA_WIDTH
    kcol = _PROJ_OFF["k_b"] // LANES
    vcol = _PROJ_OFF["v_b"] // LANES

    def cur(b, n):
        return b * nb + n

    def prev(b, n):
        return jnp.maximum((b * nb + n) * per - 1, 0)

    return pl.pallas_call(
        _swa_kernel,
        grid=(batch, nb),
        in_specs=[
            pl.BlockSpec(memory_space=pltpu.SMEM),
            pl.BlockSpec((SWA_QB, SWA_WIDTH), lambda b, n: (cur(b, n), qcol)),
            pl.BlockSpec((SWA_QB, LANES), lambda b, n: (cur(b, n), kcol)),
            pl.BlockSpec((WINDOW, LANES), lambda b, n: (prev(b, n), kcol)),
            pl.BlockSpec((SWA_QB, LANES), lambda b, n: (cur(b, n), vcol)),
            pl.BlockSpec((WINDOW, LANES), lambda b, n: (prev(b, n), vcol)),
            pl.BlockSpec((1, SWA_QB), lambda b, n: (0, cur(b, n))),
            pl.BlockSpec((1, WINDOW), lambda b, n: (0, prev(b, n))),
            pl.BlockSpec((ROPE_HALF, LANES), lambda b, n: (0, 0)),
        ],
        out_specs=pl.BlockSpec((SWA_QB, SWA_WIDTH), lambda b, n: (cur(b, n), 0)),
        out_shape=jax.ShapeDtypeStruct((batch * seq, SWA_WIDTH), BF16),
        compiler_params=pltpu.CompilerParams(
            dimension_semantics=("arbitrary", "arbitrary")),
        name="swa_attention",
    )(sinks, proj, proj, proj, proj, proj, pos_row, pos_row, inv_tab)


def _silu(z):
    return z * jax.nn.sigmoid(z)


def _epilogue_kernel(ya_ref, za_ref, yb_ref, zb_ref, ga_ref, gb_ref, x_ref, p_ref,
                     wof_ref, wos_ref, wout_ref, gpost_ref, wple_ref, wgate_ref, o_ref):
    ua = (ya_ref[...].astype(F32) * _silu(za_ref[...].astype(F32))).astype(BF16)
    ub = (yb_ref[...].astype(F32) * _silu(zb_ref[...].astype(F32))).astype(BF16)
    oa = jnp.dot(ua, wof_ref[...], preferred_element_type=F32)
    ob = jnp.dot(ub, wos_ref[...], preferred_element_type=F32)
    merged = (jax.nn.sigmoid(ga_ref[...].astype(F32)) * oa
              + jax.nn.sigmoid(gb_ref[...].astype(F32)) * ob)
    out = jnp.dot(merged.astype(BF16), wout_ref[...], preferred_element_type=F32)
    ms = jnp.mean(out * out, axis=-1, keepdims=True)
    x1 = x_ref[...] + (out * lax.rsqrt(ms + NORM_EPS)) * gpost_ref[...]
    e = jnp.dot(p_ref[...].astype(BF16), wple_ref[...], preferred_element_type=F32)
    gate = jax.nn.sigmoid(jnp.dot(x1.astype(BF16), wgate_ref[...],
                                  preferred_element_type=F32))
    o_ref[...] = x1 + gate * e


def _epilogue(ya, yb, proj, x2, p2, wof, wos, wout, gpost, wple, wgate):
    t = x2.shape[0]
    za_col = _PROJ_OFF["z_a"] // FOX_WIDTH
    zb_col = _PROJ_OFF["z_b"] // SWA_WIDTH
    ga_col = _PROJ_OFF["g_a"] // D_MODEL
    gb_col = _PROJ_OFF["g_b"] // D_MODEL
    once = pl.Buffered(1)

    def const(shape):
        return pl.BlockSpec(shape, lambda i: (0, 0), pipeline_mode=once)

    return pl.pallas_call(
        _epilogue_kernel,
        grid=(t // EPI_TM,),
        in_specs=[
            pl.BlockSpec((EPI_TM, FOX_WIDTH), lambda i: (i, 0)),
            pl.BlockSpec((EPI_TM, FOX_WIDTH), lambda i: (i, za_col)),
            pl.BlockSpec((EPI_TM, SWA_WIDTH), lambda i: (i, 0)),
            pl.BlockSpec((EPI_TM, SWA_WIDTH), lambda i: (i, zb_col)),
            pl.BlockSpec((EPI_TM, D_MODEL), lambda i: (i, ga_col)),
            pl.BlockSpec((EPI_TM, D_MODEL), lambda i: (i, gb_col)),
            pl.BlockSpec((EPI_TM, D_MODEL), lambda i: (i, 0)),
            pl.BlockSpec((EPI_TM, PLE_DIM), lambda i: (i, 0)),
            const((FOX_WIDTH, D_MODEL)),
            const((SWA_WIDTH, D_MODEL)),
            const((D_MODEL, D_MODEL)),
            const((1, D_MODEL)),
            const((PLE_DIM, D_MODEL)),
            const((D_MODEL, D_MODEL)),
        ],
        out_specs=pl.BlockSpec((EPI_TM, D_MODEL), lambda i: (i, 0)),
        out_shape=jax.ShapeDtypeStruct((t, D_MODEL), F32),
        compiler_params=pltpu.CompilerParams(
            dimension_semantics=("arbitrary",),
            vmem_limit_bytes=56 * 1024 * 1024),
        name="epilogue",
    )(ya, proj, yb, proj, proj, proj, x2, p2, wof, wos, wout, gpost, wple, wgate)


def _layer(x2, p2, pos_row, batch, seq, pre_g, w_in, b_forget, sinks, w_o_fox, w_o_swa,
           w_out, post_g, w_ple, w_ple_gate):
    w_perm = _prep_weights(w_in)
    f0 = _REF_OFF["f_a"]
    wf_t = jnp.pad(w_in[:, f0:f0 + FOX_HEADS].T, ((0, F_ROWS - FOX_HEADS), (0, 0))).astype(BF16)
    col_scale = jnp.where(jnp.arange(PROJ_COLS) < FOX_WIDTH, Q_A_SCALE, 1.0
                          ).astype(F32).reshape(1, PROJ_COLS)

    proj, ft = _inproj(x2, pre_g.reshape(1, D_MODEL), w_perm, col_scale, wf_t)
    cb_tok = _forget_cumsum(ft, b_forget.reshape(FOX_HEADS, 1).astype(F32), batch, seq)
    ya = _fox_attention(proj, cb_tok, batch, seq)

    inv = ROPE_THETA ** (-jnp.arange(ROPE_HALF, dtype=F32) / ROPE_HALF)
    inv_tab = jnp.broadcast_to(inv[:, None], (ROPE_HALF, LANES))
    yb = _swa_attention(sinks.astype(F32), proj, pos_row, inv_tab, batch, seq)

    return _epilogue(ya, yb, proj, x2, p2,
                     w_o_fox.astype(BF16), w_o_swa.astype(BF16), w_out.astype(BF16),
                     post_g.reshape(1, D_MODEL), w_ple.astype(BF16), w_ple_gate.astype(BF16))


def kernel(x, p, positions, pre_norm_g, w_in, b_forget, sinks, w_o_fox, w_o_swa, w_out,
           post_norm_g, w_ple, w_ple_gate):
    batch, seq, _ = x.shape
    depth = p.shape[0]
    x2 = x.reshape(batch * seq, D_MODEL)
    pos_row = positions.reshape(1, batch * seq)
    for i in range(depth):
        x2 = _layer(x2, p[i].reshape(batch * seq, PLE_DIM), pos_row, batch, seq,
                    pre_norm_g[i], w_in[i], b_forget[i], sinks[i], w_o_fox[i], w_o_swa[i],
                    w_out[i], post_norm_g[i], w_ple[i], w_ple_gate[i])
    return x2.reshape(batch, seq, D_MODEL)
```

```python
import math

import jax
import jax.numpy as jnp
from jax import lax
from jax.experimental import pallas as pl
from jax.experimental.pallas import tpu as pltpu

F32 = jnp.float32
BF16 = jnp.bfloat16

D_MODEL = 2048
FOX_HEADS = 8
FOX_HEAD_DIM = 128
FOX_WIDTH = FOX_HEADS * FOX_HEAD_DIM
SWA_Q_HEADS = 16
SWA_KV_HEADS = 2
SWA_HEAD_DIM = 64
SWA_WIDTH = SWA_Q_HEADS * SWA_HEAD_DIM
SWA_KV_WIDTH = SWA_KV_HEADS * SWA_HEAD_DIM
SWA_GROUP = SWA_Q_HEADS // SWA_KV_HEADS
ROPE_HALF = SWA_HEAD_DIM // 2
WINDOW = 128
ROPE_THETA = 10000.0
PLE_DIM = 256
NORM_EPS = 1e-6
LANES = 128
LOG2E = math.log2(math.e)
NEG_BIG = -1e30

_REF_SPLITS = (FOX_WIDTH, FOX_WIDTH, FOX_WIDTH, FOX_WIDTH, FOX_HEADS, SWA_WIDTH,
               SWA_KV_WIDTH, SWA_KV_WIDTH, SWA_WIDTH, D_MODEL, D_MODEL)
_REF_NAMES = ("q_a", "k_a", "v_a", "z_a", "f_a", "q_b", "k_b", "v_b", "z_b", "g_a", "g_b")
_REF_OFF = {}
_o = 0
for _n, _s in zip(_REF_NAMES, _REF_SPLITS):
    _REF_OFF[_n] = _o
    _o += _s

IN_TM = 1024
IN_TN = 1536
NORM_ROWS = 256
F_ROWS = 16

PREP_TN = 512
_W_GROUPS = (
    (_REF_OFF["q_a"], 4 * FOX_WIDTH),
    (_REF_OFF["q_b"], SWA_WIDTH),
    (_REF_OFF["z_b"], SWA_WIDTH),
    (_REF_OFF["g_a"], 2 * D_MODEL),
    (_REF_OFF["k_b"], 2 * SWA_KV_WIDTH),
)
_GROUP_TILES = tuple(-(-w // PREP_TN) for _, w in _W_GROUPS)
_GROUP_START = tuple(sum(_GROUP_TILES[:i]) for i in range(len(_W_GROUPS)))
PREP_TILES = sum(_GROUP_TILES)
PROJ_COLS = PREP_TILES * PREP_TN
_PROJ_OFF = {
    "q_a": 0, "k_a": FOX_WIDTH, "v_a": 2 * FOX_WIDTH, "z_a": 3 * FOX_WIDTH,
    "q_b": _GROUP_START[1] * PREP_TN,
    "z_b": _GROUP_START[2] * PREP_TN,
    "g_a": _GROUP_START[3] * PREP_TN, "g_b": _GROUP_START[3] * PREP_TN + D_MODEL,
    "k_b": _GROUP_START[4] * PREP_TN, "v_b": _GROUP_START[4] * PREP_TN + SWA_KV_WIDTH,
}
SUBLANES = 8
assert all(c0 % SUBLANES == 0 for c0, _ in _W_GROUPS)
Q_A_SCALE = FOX_HEAD_DIM ** -0.5 * LOG2E

FOX_T = 512
FOX_CHUNK = 512
FOX_PAIR = 2
BIAS_TERMS = 3

SWA_NW = 4
SWA_QB = SWA_NW * WINDOW

EPI_TM = 256


def _prep_kernel(wt_ref, out_ref):
    for c in range(PREP_TN // LANES):
        out_ref[:, c * LANES:(c + 1) * LANES] = (
            wt_ref[c * LANES:(c + 1) * LANES, :].T.astype(BF16))


def _prep_src_row(j):
    tile_row = jnp.int32(0)
    for (c0, _), start in zip(_W_GROUPS, _GROUP_START):
        tile_row = jnp.where(j >= start, c0 // SUBLANES + (j - start) * (PREP_TN // SUBLANES),
                             tile_row)
    return tile_row * SUBLANES


def _prep_weights(w_t):
    return pl.pallas_call(
        _prep_kernel,
        grid=(PREP_TILES,),
        in_specs=[pl.BlockSpec((pl.Element(PREP_TN), pl.Element(D_MODEL)),
                               lambda j: (_prep_src_row(j), 0))],
        out_specs=pl.BlockSpec((D_MODEL, PREP_TN), lambda j: (0, j)),
        out_shape=jax.ShapeDtypeStruct((D_MODEL, PROJ_COLS), BF16),
        compiler_params=pltpu.CompilerParams(dimension_semantics=("arbitrary",)),
        name="prep_weights",
    )(w_t)


def _inproj_kernel(x_ref, g_ref, w_ref, scale_ref, wf_ref, proj_ref, ft_ref, h_ref):
    j = pl.program_id(1)

    @pl.when(j == 0)
    def _():
        for r in range(IN_TM // NORM_ROWS):
            rows = slice(r * NORM_ROWS, (r + 1) * NORM_ROWS)
            x = x_ref[rows, :]
            ms = jnp.mean(x * x, axis=-1, keepdims=True)
            h = (x * lax.rsqrt(ms + NORM_EPS)) * g_ref[...]
            h_ref[rows, :] = h.astype(BF16)
        ft_ref[...] = lax.dot_general(wf_ref[...], h_ref[...], (((1,), (1,)), ((), ())),
                                      preferred_element_type=F32)

    acc = jnp.dot(h_ref[...], w_ref[...], preferred_element_type=F32)
    proj_ref[...] = (acc * scale_ref[...]).astype(BF16)


def _inproj(x2, g, w_perm, col_scale, wf_t):
    t = x2.shape[0]
    return pl.pallas_call(
        _inproj_kernel,
        grid=(t // IN_TM, PROJ_COLS // IN_TN),
        in_specs=[
            pl.BlockSpec((IN_TM, D_MODEL), lambda i, j: (i, 0)),
            pl.BlockSpec((1, D_MODEL), lambda i, j: (0, 0)),
            pl.BlockSpec((D_MODEL, IN_TN), lambda i, j: (0, j)),
            pl.BlockSpec((1, IN_TN), lambda i, j: (0, j)),
            pl.BlockSpec((F_ROWS, D_MODEL), lambda i, j: (0, 0)),
        ],
        out_specs=[
            pl.BlockSpec((IN_TM, IN_TN), lambda i, j: (i, j)),
            pl.BlockSpec((F_ROWS, IN_TM), lambda i, j: (0, i)),
        ],
        out_shape=[
            jax.ShapeDtypeStruct((t, PROJ_COLS), BF16),
            jax.ShapeDtypeStruct((F_ROWS, t), F32),
        ],
        scratch_shapes=[pltpu.VMEM((IN_TM, D_MODEL), BF16)],
        compiler_params=pltpu.CompilerParams(
            dimension_semantics=("arbitrary", "arbitrary"),
            vmem_limit_bytes=56 * 1024 * 1024),
        name="inproj",
    )(x2, g, w_perm, col_scale, wf_t)


def _cumsum_kernel(ft_ref, b_ref, cb_ref):
    f = ft_ref[0:FOX_HEADS, :] + b_ref[...]
    lf = jnp.minimum(f, 0.0) - jnp.log1p(jnp.exp(-jnp.abs(f)))
    seq = lf.shape[1]
    lane = lax.broadcasted_iota(jnp.int32, lf.shape, 1)
    c = lf
    shift = 1
    while shift < seq:
        c = c + jnp.where(lane >= shift, pltpu.roll(c, shift, axis=1), 0.0)
        shift *= 2
    neg = c * (-LOG2E)
    padded = jnp.concatenate([neg, jnp.zeros((LANES - FOX_HEADS, seq), F32)], axis=0)
    cb_ref[...] = padded.T


def _forget_cumsum(ft, b_col, batch, seq):
    return pl.pallas_call(
        _cumsum_kernel,
        grid=(batch,),
        in_specs=[
            pl.BlockSpec((F_ROWS, seq), lambda b: (0, b)),
            pl.BlockSpec((FOX_HEADS, 1), lambda b: (0, 0)),
        ],
        out_specs=pl.BlockSpec((seq, LANES), lambda b: (b, 0)),
        out_shape=jax.ShapeDtypeStruct((batch * seq, LANES), F32),
        name="forget_cumsum",
    )(ft, b_col)


def _fox_kernel(q_ref, k_ref, v_ref, cb_ref, o_ref, vt_ref, kb_ref, qt_ref, acc_ref,
                s0_ref, s1_ref):
    hp = pl.program_id(1)
    qi = pl.program_id(2)
    seq = k_ref.shape[0]
    t = FOX_T
    d = FOX_HEAD_DIM

    @pl.when(qi == 0)
    def _():
        lane = lax.broadcasted_iota(jnp.int32, (FOX_CHUNK, LANES), 1)
        row = lax.broadcasted_iota(jnp.int32, (d, t), 0)
        ones_rows = jnp.where(row < BIAS_TERMS, 1.0, 0.0).astype(BF16)
        for hh in range(FOX_PAIR):
            qt_ref[hh, d:2 * d, :] = ones_rows
            for r in range(seq // FOX_CHUNK):
                rows = slice(r * FOX_CHUNK, (r + 1) * FOX_CHUNK)
                vt_ref[hh, :, rows] = v_ref[rows, hh * d:(hh + 1) * d].astype(F32).T.astype(BF16)
                col = jnp.sum(jnp.where(lane == hp * FOX_PAIR + hh, cb_ref[rows, :], 0.0),
                              axis=1, keepdims=True)
                pieces = jnp.zeros((FOX_CHUNK, LANES), F32)
                rest = col
                for i in range(BIAS_TERMS):
                    piece = rest.astype(BF16).astype(F32)
                    pieces = jnp.where(lane == i, piece, pieces)
                    rest = rest - piece
                kb_ref[hh, rows, :] = pieces.astype(BF16)

    for hh in range(FOX_PAIR):
        qt_ref[hh, 0:d, :] = q_ref[:, hh * d:(hh + 1) * d].astype(F32).T.astype(BF16)
        acc_ref[hh] = jnp.zeros((d, t), F32)

    def scores(i, s_ref):
        rows = pl.ds(pl.multiple_of(i * t, t), t)
        for hh in range(FOX_PAIR):
            k_aug = jnp.concatenate([k_ref[rows, hh * d:(hh + 1) * d], kb_ref[hh, rows, :]], axis=1)
            s_ref[hh] = jnp.dot(k_aug, qt_ref[hh], preferred_element_type=F32)

    def absorb(i, s_ref, carry, masked):
        rows = pl.ds(pl.multiple_of(i * t, t), t)
        new = []
        for hh in range(FOX_PAIR):
            m, l = carry[hh]
            s = s_ref[hh]
            if masked:
                key = lax.broadcasted_iota(jnp.int32, s.shape, 0)
                qry = lax.broadcasted_iota(jnp.int32, s.shape, 1)
                s = jnp.where(key <= qry, s, NEG_BIG)
            m_new = jnp.maximum(m, jnp.max(s, axis=0, keepdims=True))
            alpha = jnp.exp2(m - m_new)
            p = jnp.exp2(s - m_new)
            l_new = alpha * l + jnp.sum(p, axis=0, keepdims=True)
            acc_ref[hh] = alpha * acc_ref[hh] + jnp.dot(
                vt_ref[hh, :, rows], p.astype(BF16), preferred_element_type=F32)
            new.append((m_new, l_new))
        return tuple(new)

    def step(i, s_cur, s_next, carry):
        scores(i + 1, s_next)
        return absorb(i, s_cur, carry, False)

    def pair(j, carry):
        carry = step(2 * j, s0_ref, s1_ref, carry)
        return step(2 * j + 1, s1_ref, s0_ref, carry)

    def odd_tail(carry):
        carry = step(qi - 1, s0_ref, s1_ref, carry)
        return absorb(qi, s1_ref, carry, True)

    def even_tail(carry):
        return absorb(qi, s0_ref, carry, True)

    init = tuple((jnp.full((1, t), NEG_BIG, F32), jnp.zeros((1, t), F32))
                 for _ in range(FOX_PAIR))
    scores(0, s0_ref)
    carry = lax.fori_loop(0, qi // 2, pair, init)
    carry = lax.cond(qi % 2 == 1, odd_tail, even_tail, carry)
    for hh in range(FOX_PAIR):
        o_ref[:, hh * d:(hh + 1) * d] = (acc_ref[hh] / carry[hh][1]).T.astype(o_ref.dtype)


def _fox_attention(proj, cb_tok, batch, seq):
    nq = seq // FOX_T
    width = FOX_PAIR * FOX_HEAD_DIM
    qcol = _PROJ_OFF["q_a"] // width
    kcol = _PROJ_OFF["k_a"] // width
    vcol = _PROJ_OFF["v_a"] // width
    return pl.pallas_call(
        _fox_kernel,
        grid=(batch, FOX_HEADS // FOX_PAIR, nq),
        in_specs=[
            pl.BlockSpec((FOX_T, width), lambda b, h, qi: (b * nq + qi, qcol + h)),
            pl.BlockSpec((seq, width), lambda b, h, qi: (b, kcol + h)),
            pl.BlockSpec((seq, width), lambda b, h, qi: (b, vcol + h)),
            pl.BlockSpec((seq, LANES), lambda b, h, qi: (b, 0)),
        ],
        out_specs=pl.BlockSpec((FOX_T, width), lambda b, h, qi: (b * nq + qi, h)),
        out_shape=jax.ShapeDtypeStruct((batch * seq, FOX_WIDTH), BF16),
        scratch_shapes=[
            pltpu.VMEM((FOX_PAIR, FOX_HEAD_DIM, seq), BF16),
            pltpu.VMEM((FOX_PAIR, seq, LANES), BF16),
            pltpu.VMEM((FOX_PAIR, 2 * FOX_HEAD_DIM, FOX_T), BF16),
            pltpu.VMEM((FOX_PAIR, FOX_HEAD_DIM, FOX_T), F32),
            pltpu.VMEM((FOX_PAIR, FOX_T, FOX_T), F32),
            pltpu.VMEM((FOX_PAIR, FOX_T, FOX_T), F32),
        ],
        compiler_params=pltpu.CompilerParams(
            dimension_semantics=("arbitrary", "arbitrary", "arbitrary"),
            vmem_limit_bytes=40 * 1024 * 1024),
        name="fox_attention",
    )(proj, proj, proj, cb_tok)


def _rope_t(xt, cos, sin):
    out = []
    for hd in range(xt.shape[0] // SWA_HEAD_DIM):
        x1 = xt[hd * SWA_HEAD_DIM: hd * SWA_HEAD_DIM + ROPE_HALF]
        x2 = xt[hd * SWA_HEAD_DIM + ROPE_HALF: (hd + 1) * SWA_HEAD_DIM]
        out.append(x1 * cos - x2 * sin)
        out.append(x2 * cos + x1 * sin)
    return jnp.concatenate(out, axis=0)


def _swa_kernel(sink_ref, q_ref, kc_ref, kp_ref, vc_ref, vp_ref, posc_ref, posp_ref,
                inv_ref, o_ref, band_ref):
    n = pl.program_id(1)
    w = WINDOW
    inv = inv_ref[...]

    def tables(pos_row):
        ang = inv * pos_row.astype(F32)
        return jnp.cos(ang), jnp.sin(ang)

    def rope_k(k_nat, cos, sin):
        kt = _rope_t(k_nat.astype(F32).T, cos, sin)
        return kt.T.astype(BF16)

    cos_p, sin_p = tables(posp_ref[...])
    k_prev = rope_k(kp_ref[...], cos_p, sin_p)
    v_prev = vp_ref[...]

    @pl.when((pl.program_id(0) == 0) & (n == 0))
    def _():
        key = lax.broadcasted_iota(jnp.int32, band_ref.shape, 0)
        qry = lax.broadcasted_iota(jnp.int32, band_ref.shape, 1) % w
        band_ref[...] = jnp.where((key <= qry + w) & (key > qry), 0.0, NEG_BIG)

    no_prev = jnp.where(n > 0, 0.0, NEG_BIG)
    q_scale = SWA_HEAD_DIM ** -0.5 * LOG2E
    zeros_half = jnp.zeros((SWA_HEAD_DIM, SWA_GROUP * w), F32)

    sinks = [jnp.concatenate(
        [jnp.full((1, w), sink_ref[hd] * LOG2E, F32)
         for hd in range(g * SWA_GROUP, (g + 1) * SWA_GROUP)], axis=1)
        for g in range(SWA_KV_HEADS)]

    def logits(wi, k_prev):
        tok = slice(wi * w, (wi + 1) * w)
        cos, sin = tables(posc_ref[:, tok])
        k_cur = rope_k(kc_ref[tok, :], cos, sin)
        k_all = jnp.concatenate([k_prev, k_cur], axis=0)
        qf = q_ref[tok, :].astype(F32)
        qt = jnp.concatenate(
            [qf[:, c * LANES:(c + 1) * LANES].T for c in range(SWA_WIDTH // LANES)], axis=0)
        qt = _rope_t(qt, cos * q_scale, sin * q_scale)
        out = []
        for g in range(SWA_KV_HEADS):
            heads = range(g * SWA_GROUP, (g + 1) * SWA_GROUP)
            q_g = jnp.concatenate(
                [qt[hd * SWA_HEAD_DIM:(hd + 1) * SWA_HEAD_DIM] for hd in heads], axis=1)
            parts = [zeros_half] * SWA_KV_HEADS
            parts[g] = q_g
            q_z = jnp.concatenate(parts, axis=0).astype(BF16)
            s = jnp.dot(k_all, q_z, preferred_element_type=F32) + band_ref[...]
            if wi == 0:
                s = jnp.concatenate([s[:w] + no_prev, s[w:]], axis=0)
            out.append(s)
        return out, k_cur

    def finish(wi, scores, v_prev):
        tok = slice(wi * w, (wi + 1) * w)
        v_cur = vc_ref[tok, :]
        v_all_t = jnp.concatenate([v_prev, v_cur], axis=0).astype(F32).T.astype(BF16)
        for g in range(SWA_KV_HEADS):
            s, sink = scores[g], sinks[g]
            m = jnp.maximum(jnp.max(s, axis=0, keepdims=True), sink)
            e = jnp.exp2(s - m)
            denom = jnp.sum(e, axis=0, keepdims=True) + jnp.exp2(sink - m)
            out_t = jnp.dot(v_all_t, e.astype(BF16), preferred_element_type=F32)
            out_t = out_t[g * SWA_HEAD_DIM:(g + 1) * SWA_HEAD_DIM] / denom
            for jj in range(SWA_GROUP // 2):
                pair = jnp.concatenate(
                    [out_t[:, (2 * jj) * w:(2 * jj + 1) * w],
                     out_t[:, (2 * jj + 1) * w:(2 * jj + 2) * w]], axis=0)
                col = (g * (SWA_GROUP // 2) + jj) * LANES
                o_ref[tok, col:col + LANES] = pair.T.astype(o_ref.dtype)
        return v_cur

    pending, k_prev = logits(0, k_prev)
    for wi in range(SWA_NW):
        if wi + 1 < SWA_NW:
            upcoming, k_prev = logits(wi + 1, k_prev)
        v_prev = finish(wi, pending, v_prev)
        if wi + 1 < SWA_NW:
            pending = upcoming


def _swa_attention(sinks, proj, pos_row, inv_tab, batch, seq):
    nb = seq // SWA_QB
    per = SWA_QB // WINDOW
    qcol = _PROJ_OFF["q_b"] // SWA_WIDTH
    kcol = _PROJ_OFF["k_b"] // LANES
    vcol = _PROJ_OFF["v_b"] // LANES

    def cur(b, n):
        return b * nb + n

    def prev(b, n):
        return jnp.maximum((b * nb + n) * per - 1, 0)

    return pl.pallas_call(
        _swa_kernel,
        grid=(batch, nb),
        in_specs=[
            pl.BlockSpec(memory_space=pltpu.SMEM),
            pl.BlockSpec((SWA_QB, SWA_WIDTH), lambda b, n: (cur(b, n), qcol)),
            pl.BlockSpec((SWA_QB, LANES), lambda b, n: (cur(b, n), kcol)),
            pl.BlockSpec((WINDOW, LANES), lambda b, n: (prev(b, n), kcol)),
            pl.BlockSpec((SWA_QB, LANES), lambda b, n: (cur(b, n), vcol)),
            pl.BlockSpec((WINDOW, LANES), lambda b, n: (prev(b, n), vcol)),
            pl.BlockSpec((1, SWA_QB), lambda b, n: (0, cur(b, n))),
            pl.BlockSpec((1, WINDOW), lambda b, n: (0, prev(b, n))),
            pl.BlockSpec((ROPE_HALF, LANES), lambda b, n: (0, 0)),
        ],
        out_specs=pl.BlockSpec((SWA_QB, SWA_WIDTH), lambda b, n: (cur(b, n), 0)),
        out_shape=jax.ShapeDtypeStruct((batch * seq, SWA_WIDTH), BF16),
        scratch_shapes=[pltpu.VMEM((2 * WINDOW, SWA_GROUP * WINDOW), F32)],
        compiler_params=pltpu.CompilerParams(
            dimension_semantics=("arbitrary", "arbitrary")),
        name="swa_attention",
    )(sinks, proj, proj, proj, proj, proj, pos_row, pos_row, inv_tab)


def _silu(z):
    return z * jax.nn.sigmoid(z)


def _epilogue_kernel(ya_ref, za_ref, yb_ref, zb_ref, ga_ref, gb_ref, x_ref, p_ref,
                     wof_ref, wos_ref, wout_ref, gpost_ref, wple_ref, wgate_ref, o_ref):
    ua = (ya_ref[...].astype(F32) * _silu(za_ref[...].astype(F32))).astype(BF16)
    ub = (yb_ref[...].astype(F32) * _silu(zb_ref[...].astype(F32))).astype(BF16)
    oa = jnp.dot(ua, wof_ref[...], preferred_element_type=F32)
    ob = jnp.dot(ub, wos_ref[...], preferred_element_type=F32)
    merged = (jax.nn.sigmoid(ga_ref[...].astype(F32)) * oa
              + jax.nn.sigmoid(gb_ref[...].astype(F32)) * ob)
    out = jnp.dot(merged.astype(BF16), wout_ref[...], preferred_element_type=F32)
    ms = jnp.mean(out * out, axis=-1, keepdims=True)
    x1 = x_ref[...] + (out * lax.rsqrt(ms + NORM_EPS)) * gpost_ref[...]
    e = jnp.dot(p_ref[...].astype(BF16), wple_ref[...], preferred_element_type=F32)
    gate = jax.nn.sigmoid(jnp.dot(x1.astype(BF16), wgate_ref[...],
                                  preferred_element_type=F32))
    o_ref[...] = x1 + gate * e


def _epilogue(ya, yb, proj, x2, p2, wof, wos, wout, gpost, wple, wgate):
    t = x2.shape[0]
    za_col = _PROJ_OFF["z_a"] // FOX_WIDTH
    zb_col = _PROJ_OFF["z_b"] // SWA_WIDTH
    ga_col = _PROJ_OFF["g_a"] // D_MODEL
    gb_col = _PROJ_OFF["g_b"] // D_MODEL
    once = pl.Buffered(1)

    def const(shape):
        return pl.BlockSpec(shape, lambda i: (0, 0), pipeline_mode=once)

    return pl.pallas_call(
        _epilogue_kernel,
        grid=(t // EPI_TM,),
        in_specs=[
            pl.BlockSpec((EPI_TM, FOX_WIDTH), lambda i: (i, 0)),
            pl.BlockSpec((EPI_TM, FOX_WIDTH), lambda i: (i, za_col)),
            pl.BlockSpec((EPI_TM, SWA_WIDTH), lambda i: (i, 0)),
            pl.BlockSpec((EPI_TM, SWA_WIDTH), lambda i: (i, zb_col)),
            pl.BlockSpec((EPI_TM, D_MODEL), lambda i: (i, ga_col)),
            pl.BlockSpec((EPI_TM, D_MODEL), lambda i: (i, gb_col)),
            pl.BlockSpec((EPI_TM, D_MODEL), lambda i: (i, 0)),
            pl.BlockSpec((EPI_TM, PLE_DIM), lambda i: (i, 0)),
            const((FOX_WIDTH, D_MODEL)),
            const((SWA_WIDTH, D_MODEL)),
            const((D_MODEL, D_MODEL)),
            const((1, D_MODEL)),
            const((PLE_DIM, D_MODEL)),
            const((D_MODEL, D_MODEL)),
        ],
        out_specs=pl.BlockSpec((EPI_TM, D_MODEL), lambda i: (i, 0)),
        out_shape=jax.ShapeDtypeStruct((t, D_MODEL), F32),
        compiler_params=pltpu.CompilerParams(
            dimension_semantics=("arbitrary",),
            vmem_limit_bytes=56 * 1024 * 1024),
        name="epilogue",
    )(ya, proj, yb, proj, proj, proj, x2, p2, wof, wos, wout, gpost, wple, wgate)


def _layer(x2, p2, pos_row, batch, seq, pre_g, w_in, b_forget, sinks, w_o_fox, w_o_swa,
           w_out, post_g, w_ple, w_ple_gate):
    w_t = w_in.T
    w_perm = _prep_weights(w_t)
    f0 = _REF_OFF["f_a"]
    wf_t = jnp.pad(w_t[f0:f0 + FOX_HEADS], ((0, F_ROWS - FOX_HEADS), (0, 0))).astype(BF16)
    col_scale = jnp.where(jnp.arange(PROJ_COLS) < FOX_WIDTH, Q_A_SCALE, 1.0
                          ).astype(F32).reshape(1, PROJ_COLS)

    proj, ft = _inproj(x2, pre_g.reshape(1, D_MODEL), w_perm, col_scale, wf_t)
    cb_tok = _forget_cumsum(ft, b_forget.reshape(FOX_HEADS, 1).astype(F32), batch, seq)
    ya = _fox_attention(proj, cb_tok, batch, seq)

    inv = ROPE_THETA ** (-jnp.arange(ROPE_HALF, dtype=F32) / ROPE_HALF)
    inv_tab = jnp.broadcast_to(inv[:, None], (ROPE_HALF, LANES))
    yb = _swa_attention(sinks.astype(F32), proj, pos_row, inv_tab, batch, seq)

    return _epilogue(ya, yb, proj, x2, p2,
                     w_o_fox.astype(BF16), w_o_swa.astype(BF16), w_out.astype(BF16),
                     post_g.reshape(1, D_MODEL), w_ple.astype(BF16), w_ple_gate.astype(BF16))


def kernel(x, p, positions, pre_norm_g, w_in, b_forget, sinks, w_o_fox, w_o_swa, w_out,
           post_norm_g, w_ple, w_ple_gate):
    batch, seq, _ = x.shape
    depth = p.shape[0]
    x2 = x.reshape(batch * seq, D_MODEL)
    pos_row = positions.reshape(1, batch * seq)
    for i in range(depth):
        x2 = _layer(x2, p[i].reshape(batch * seq, PLE_DIM), pos_row, batch, seq,
                    pre_norm_g[i], w_in[i], b_forget[i], sinks[i], w_o_fox[i], w_o_swa[i],
                    w_out[i], post_norm_g[i], w_ple[i], w_ple_gate[i])
    return x2.reshape(batch, seq, D_MODEL)
```

```python
import math

import jax
import jax.numpy as jnp
from jax import lax
from jax.experimental import pallas as pl
from jax.experimental.pallas import tpu as pltpu

F32 = jnp.float32
BF16 = jnp.bfloat16

D_MODEL = 2048
FOX_HEADS = 8
FOX_HEAD_DIM = 128
FOX_WIDTH = FOX_HEADS * FOX_HEAD_DIM
SWA_Q_HEADS = 16
SWA_KV_HEADS = 2
SWA_HEAD_DIM = 64
SWA_WIDTH = SWA_Q_HEADS * SWA_HEAD_DIM
SWA_KV_WIDTH = SWA_KV_HEADS * SWA_HEAD_DIM
SWA_GROUP = SWA_Q_HEADS // SWA_KV_HEADS
ROPE_HALF = SWA_HEAD_DIM // 2
WINDOW = 128
ROPE_THETA = 10000.0
PLE_DIM = 256
NORM_EPS = 1e-6
LANES = 128
LOG2E = math.log2(math.e)
NEG_BIG = -1e30

_REF_SPLITS = (FOX_WIDTH, FOX_WIDTH, FOX_WIDTH, FOX_WIDTH, FOX_HEADS, SWA_WIDTH,
               SWA_KV_WIDTH, SWA_KV_WIDTH, SWA_WIDTH, D_MODEL, D_MODEL)
_REF_NAMES = ("q_a", "k_a", "v_a", "z_a", "f_a", "q_b", "k_b", "v_b", "z_b", "g_a", "g_b")
_REF_OFF = {}
_o = 0
for _n, _s in zip(_REF_NAMES, _REF_SPLITS):
    _REF_OFF[_n] = _o
    _o += _s

IN_TM = 1024
IN_TN = 1536
NORM_ROWS = 256
F_ROWS = 16

PREP_TN = 512
_W_GROUPS = (
    (_REF_OFF["q_a"], 4 * FOX_WIDTH),
    (_REF_OFF["q_b"], SWA_WIDTH),
    (_REF_OFF["z_b"], SWA_WIDTH),
    (_REF_OFF["g_a"], 2 * D_MODEL),
    (_REF_OFF["k_b"], 2 * SWA_KV_WIDTH),
)
_GROUP_TILES = tuple(-(-w // PREP_TN) for _, w in _W_GROUPS)
_GROUP_START = tuple(sum(_GROUP_TILES[:i]) for i in range(len(_W_GROUPS)))
PREP_TILES = sum(_GROUP_TILES)
PROJ_COLS = PREP_TILES * PREP_TN
_PROJ_OFF = {
    "q_a": 0, "k_a": FOX_WIDTH, "v_a": 2 * FOX_WIDTH, "z_a": 3 * FOX_WIDTH,
    "q_b": _GROUP_START[1] * PREP_TN,
    "z_b": _GROUP_START[2] * PREP_TN,
    "g_a": _GROUP_START[3] * PREP_TN, "g_b": _GROUP_START[3] * PREP_TN + D_MODEL,
    "k_b": _GROUP_START[4] * PREP_TN, "v_b": _GROUP_START[4] * PREP_TN + SWA_KV_WIDTH,
}
SUBLANES = 8
assert all(c0 % SUBLANES == 0 for c0, _ in _W_GROUPS)
Q_A_SCALE = FOX_HEAD_DIM ** -0.5 * LOG2E

FOX_T = 512
FOX_CHUNK = 512
FOX_PAIR = 4
BIAS_TERMS = 3

SWA_NW = 4
SWA_QB = SWA_NW * WINDOW

EPI_TM = 256


def _prep_kernel(wt_ref, out_ref):
    for c in range(PREP_TN // LANES):
        out_ref[:, c * LANES:(c + 1) * LANES] = (
            wt_ref[c * LANES:(c + 1) * LANES, :].T.astype(BF16))


def _prep_src_row(j):
    tile_row = jnp.int32(0)
    for (c0, _), start in zip(_W_GROUPS, _GROUP_START):
        tile_row = jnp.where(j >= start, c0 // SUBLANES + (j - start) * (PREP_TN // SUBLANES),
                             tile_row)
    return tile_row * SUBLANES


def _prep_weights(w_t):
    return pl.pallas_call(
        _prep_kernel,
        grid=(PREP_TILES,),
        in_specs=[pl.BlockSpec((pl.Element(PREP_TN), pl.Element(D_MODEL)),
                               lambda j: (_prep_src_row(j), 0))],
        out_specs=pl.BlockSpec((D_MODEL, PREP_TN), lambda j: (0, j)),
        out_shape=jax.ShapeDtypeStruct((D_MODEL, PROJ_COLS), BF16),
        compiler_params=pltpu.CompilerParams(dimension_semantics=("arbitrary",)),
        name="prep_weights",
    )(w_t)


def _inproj_kernel(x_ref, g_ref, w_ref, scale_ref, wf_ref, proj_ref, ft_ref, h_ref):
    j = pl.program_id(1)

    @pl.when(j == 0)
    def _():
        for r in range(IN_TM // NORM_ROWS):
            rows = slice(r * NORM_ROWS, (r + 1) * NORM_ROWS)
            x = x_ref[rows, :]
            ms = jnp.mean(x * x, axis=-1, keepdims=True)
            h = (x * lax.rsqrt(ms + NORM_EPS)) * g_ref[...]
            h_ref[rows, :] = h.astype(BF16)
        ft_ref[...] = lax.dot_general(wf_ref[...], h_ref[...], (((1,), (1,)), ((), ())),
                                      preferred_element_type=F32)

    acc = jnp.dot(h_ref[...], w_ref[...], preferred_element_type=F32)
    proj_ref[...] = (acc * scale_ref[...]).astype(BF16)


def _inproj(x2, g, w_perm, col_scale, wf_t):
    t = x2.shape[0]
    return pl.pallas_call(
        _inproj_kernel,
        grid=(t // IN_TM, PROJ_COLS // IN_TN),
        in_specs=[
            pl.BlockSpec((IN_TM, D_MODEL), lambda i, j: (i, 0)),
            pl.BlockSpec((1, D_MODEL), lambda i, j: (0, 0)),
            pl.BlockSpec((D_MODEL, IN_TN), lambda i, j: (0, j)),
            pl.BlockSpec((1, IN_TN), lambda i, j: (0, j)),
            pl.BlockSpec((F_ROWS, D_MODEL), lambda i, j: (0, 0)),
        ],
        out_specs=[
            pl.BlockSpec((IN_TM, IN_TN), lambda i, j: (i, j)),
            pl.BlockSpec((F_ROWS, IN_TM), lambda i, j: (0, i)),
        ],
        out_shape=[
            jax.ShapeDtypeStruct((t, PROJ_COLS), BF16),
            jax.ShapeDtypeStruct((F_ROWS, t), F32),
        ],
        scratch_shapes=[pltpu.VMEM((IN_TM, D_MODEL), BF16)],
        compiler_params=pltpu.CompilerParams(
            dimension_semantics=("arbitrary", "arbitrary"),
            vmem_limit_bytes=56 * 1024 * 1024),
        name="inproj",
    )(x2, g, w_perm, col_scale, wf_t)


def _cumsum_kernel(ft_ref, b_ref, kb_ref):
    f = ft_ref[0:FOX_HEADS, :] + b_ref[...]
    lf = jnp.minimum(f, 0.0) - jnp.log1p(jnp.exp(-jnp.abs(f)))
    seq = lf.shape[1]
    lane = lax.broadcasted_iota(jnp.int32, lf.shape, 1)
    c = lf
    shift = 1
    while shift < seq:
        c = c + jnp.where(lane >= shift, pltpu.roll(c, shift, axis=1), 0.0)
        shift *= 2
    rest = c * (-LOG2E)
    pieces = []
    for _ in range(BIAS_TERMS):
        piece = rest.astype(BF16).astype(F32)
        pieces.append(piece)
        rest = rest - piece
    pieces.append(jnp.zeros((LANES - BIAS_TERMS * FOX_HEADS, seq), F32))
    kb_ref[...] = jnp.concatenate(pieces, axis=0).T.astype(BF16)


def _forget_cumsum(ft, b_col, batch, seq):
    return pl.pallas_call(
        _cumsum_kernel,
        grid=(batch,),
        in_specs=[
            pl.BlockSpec((F_ROWS, seq), lambda b: (0, b)),
            pl.BlockSpec((FOX_HEADS, 1), lambda b: (0, 0)),
        ],
        out_specs=pl.BlockSpec((seq, LANES), lambda b: (b, 0)),
        out_shape=jax.ShapeDtypeStruct((batch * seq, LANES), BF16),
        name="forget_cumsum",
    )(ft, b_col)


def _fox_kernel(q_ref, k_ref, v_ref, kb_ref, o_ref, vt_ref, qt_ref, acc_ref, s0_ref, s1_ref):
    hp = pl.program_id(1)
    qi = pl.program_id(2)
    seq = k_ref.shape[0]
    t = FOX_T
    d = FOX_HEAD_DIM

    @pl.when(qi == 0)
    def _():
        row = lax.broadcasted_iota(jnp.int32, (d, t), 0)
        for hh in range(FOX_PAIR):
            mine = (row < BIAS_TERMS * FOX_HEADS) & (row % FOX_HEADS == hp * FOX_PAIR + hh)
            qt_ref[hh, d:2 * d, :] = jnp.where(mine, 1.0, 0.0).astype(BF16)
            for r in range(seq // FOX_CHUNK):
                rows = slice(r * FOX_CHUNK, (r + 1) * FOX_CHUNK)
                vt_ref[hh, :, rows] = v_ref[rows, hh * d:(hh + 1) * d].astype(F32).T.astype(BF16)

    for hh in range(FOX_PAIR):
        qt_ref[hh, 0:d, :] = q_ref[:, hh * d:(hh + 1) * d].astype(F32).T.astype(BF16)
        acc_ref[hh] = jnp.zeros((d, t), F32)

    def scores(i, s_ref):
        rows = pl.ds(pl.multiple_of(i * t, t), t)
        for hh in range(FOX_PAIR):
            k_aug = jnp.concatenate([k_ref[rows, hh * d:(hh + 1) * d], kb_ref[rows, :]], axis=1)
            s_ref[hh] = jnp.dot(k_aug, qt_ref[hh], preferred_element_type=F32)

    def absorb(i, s_ref, carry, masked):
        rows = pl.ds(pl.multiple_of(i * t, t), t)
        new = []
        for hh in range(FOX_PAIR):
            m, l = carry[hh]
            s = s_ref[hh]
            if masked:
                key = lax.broadcasted_iota(jnp.int32, s.shape, 0)
                qry = lax.broadcasted_iota(jnp.int32, s.shape, 1)
                s = jnp.where(key <= qry, s, NEG_BIG)
            m_new = jnp.maximum(m, jnp.max(s, axis=0, keepdims=True))
            alpha = jnp.exp2(m - m_new)
            p = jnp.exp2(s - m_new)
            l_new = alpha * l + jnp.sum(p, axis=0, keepdims=True)
            acc_ref[hh] = alpha * acc_ref[hh] + jnp.dot(
                vt_ref[hh, :, rows], p.astype(BF16), preferred_element_type=F32)
            new.append((m_new, l_new))
        return tuple(new)

    def step(i, s_cur, s_next, carry):
        scores(i + 1, s_next)
        return absorb(i, s_cur, carry, False)

    def pair(j, carry):
        carry = step(2 * j, s0_ref, s1_ref, carry)
        return step(2 * j + 1, s1_ref, s0_ref, carry)

    def odd_tail(carry):
        carry = step(qi - 1, s0_ref, s1_ref, carry)
        return absorb(qi, s1_ref, carry, True)

    def even_tail(carry):
        return absorb(qi, s0_ref, carry, True)

    init = tuple((jnp.full((1, t), NEG_BIG, F32), jnp.zeros((1, t), F32))
                 for _ in range(FOX_PAIR))
    scores(0, s0_ref)
    carry = lax.fori_loop(0, qi // 2, pair, init)
    carry = lax.cond(qi % 2 == 1, odd_tail, even_tail, carry)
    for hh in range(FOX_PAIR):
        o_ref[:, hh * d:(hh + 1) * d] = (acc_ref[hh] / carry[hh][1]).T.astype(o_ref.dtype)


def _fox_attention(proj, kb_tok, batch, seq):
    nq = seq // FOX_T
    width = FOX_PAIR * FOX_HEAD_DIM
    qcol = _PROJ_OFF["q_a"] // width
    kcol = _PROJ_OFF["k_a"] // width
    vcol = _PROJ_OFF["v_a"] // width
    return pl.pallas_call(
        _fox_kernel,
        grid=(batch, FOX_HEADS // FOX_PAIR, nq),
        in_specs=[
            pl.BlockSpec((FOX_T, width), lambda b, h, qi: (b * nq + qi, qcol + h)),
            pl.BlockSpec((seq, width), lambda b, h, qi: (b, kcol + h)),
            pl.BlockSpec((seq, width), lambda b, h, qi: (b, vcol + h)),
            pl.BlockSpec((seq, LANES), lambda b, h, qi: (b, 0)),
        ],
        out_specs=pl.BlockSpec((FOX_T, width), lambda b, h, qi: (b * nq + qi, h)),
        out_shape=jax.ShapeDtypeStruct((batch * seq, FOX_WIDTH), BF16),
        scratch_shapes=[
            pltpu.VMEM((FOX_PAIR, FOX_HEAD_DIM, seq), BF16),
            pltpu.VMEM((FOX_PAIR, 2 * FOX_HEAD_DIM, FOX_T), BF16),
            pltpu.VMEM((FOX_PAIR, FOX_HEAD_DIM, FOX_T), F32),
            pltpu.VMEM((FOX_PAIR, FOX_T, FOX_T), F32),
            pltpu.VMEM((FOX_PAIR, FOX_T, FOX_T), F32),
        ],
        compiler_params=pltpu.CompilerParams(
            dimension_semantics=("arbitrary", "arbitrary", "arbitrary"),
            vmem_limit_bytes=40 * 1024 * 1024),
        name="fox_attention",
    )(proj, proj, proj, kb_tok)


def _rope_t(xt, cos, sin):
    out = []
    for hd in range(xt.shape[0] // SWA_HEAD_DIM):
        x1 = xt[hd * SWA_HEAD_DIM: hd * SWA_HEAD_DIM + ROPE_HALF]
        x2 = xt[hd * SWA_HEAD_DIM + ROPE_HALF: (hd + 1) * SWA_HEAD_DIM]
        out.append(x1 * cos - x2 * sin)
        out.append(x2 * cos + x1 * sin)
    return jnp.concatenate(out, axis=0)


def _swa_kernel(sink_ref, q_ref, kc_ref, kp_ref, vc_ref, vp_ref, posc_ref, posp_ref,
                inv_ref, o_ref, band_ref):
    n = pl.program_id(1)
    w = WINDOW
    inv = inv_ref[...]

    def tables(pos_row):
        ang = inv * pos_row.astype(F32)
        return jnp.cos(ang), jnp.sin(ang)

    def rope_k(k_nat, cos, sin):
        kt = _rope_t(k_nat.astype(F32).T, cos, sin)
        return kt.T.astype(BF16)

    cos_p, sin_p = tables(posp_ref[...])
    k_prev = rope_k(kp_ref[...], cos_p, sin_p)
    v_prev = vp_ref[...]

    @pl.when((pl.program_id(0) == 0) & (n == 0))
    def _():
        key = lax.broadcasted_iota(jnp.int32, band_ref.shape, 0)
        qry = lax.broadcasted_iota(jnp.int32, band_ref.shape, 1) % w
        band_ref[...] = jnp.where((key <= qry + w) & (key > qry), 0.0, NEG_BIG)

    no_prev = jnp.where(n > 0, 0.0, NEG_BIG)
    q_scale = SWA_HEAD_DIM ** -0.5 * LOG2E
    zeros_half = jnp.zeros((SWA_HEAD_DIM, SWA_GROUP * w), F32)

    sinks = [jnp.concatenate(
        [jnp.full((1, w), sink_ref[hd] * LOG2E, F32)
         for hd in range(g * SWA_GROUP, (g + 1) * SWA_GROUP)], axis=1)
        for g in range(SWA_KV_HEADS)]

    def logits(wi, k_prev):
        tok = slice(wi * w, (wi + 1) * w)
        cos, sin = tables(posc_ref[:, tok])
        k_cur = rope_k(kc_ref[tok, :], cos, sin)
        k_all = jnp.concatenate([k_prev, k_cur], axis=0)
        qf = q_ref[tok, :].astype(F32)
        qt = jnp.concatenate(
            [qf[:, c * LANES:(c + 1) * LANES].T for c in range(SWA_WIDTH // LANES)], axis=0)
        qt = _rope_t(qt, cos * q_scale, sin * q_scale)
        out = []
        for g in range(SWA_KV_HEADS):
            heads = range(g * SWA_GROUP, (g + 1) * SWA_GROUP)
            q_g = jnp.concatenate(
                [qt[hd * SWA_HEAD_DIM:(hd + 1) * SWA_HEAD_DIM] for hd in heads], axis=1)
            parts = [zeros_half] * SWA_KV_HEADS
            parts[g] = q_g
            q_z = jnp.concatenate(parts, axis=0).astype(BF16)
            s = jnp.dot(k_all, q_z, preferred_element_type=F32) + band_ref[...]
            if wi == 0:
                s = jnp.concatenate([s[:w] + no_prev, s[w:]], axis=0)
            out.append(s)
        return out, k_cur

    def finish(wi, scores, v_prev):
        tok = slice(wi * w, (wi + 1) * w)
        v_cur = vc_ref[tok, :]
        v_all_t = jnp.concatenate([v_prev, v_cur], axis=0).astype(F32).T.astype(BF16)
        for g in range(SWA_KV_HEADS):
            s, sink = scores[g], sinks[g]
            m = jnp.maximum(jnp.max(s, axis=0, keepdims=True), sink)
            e = jnp.exp2(s - m)
            denom = jnp.sum(e, axis=0, keepdims=True) + jnp.exp2(sink - m)
            out_t = jnp.dot(v_all_t, e.astype(BF16), preferred_element_type=F32)
            out_t = out_t[g * SWA_HEAD_DIM:(g + 1) * SWA_HEAD_DIM] / denom
            for jj in range(SWA_GROUP // 2):
                pair = jnp.concatenate(
                    [out_t[:, (2 * jj) * w:(2 * jj + 1) * w],
                     out_t[:, (2 * jj + 1) * w:(2 * jj + 2) * w]], axis=0)
                col = (g * (SWA_GROUP // 2) + jj) * LANES
                o_ref[tok, col:col + LANES] = pair.T.astype(o_ref.dtype)
        return v_cur

    pending, k_prev = logits(0, k_prev)
    for wi in range(SWA_NW):
        if wi + 1 < SWA_NW:
            upcoming, k_prev = logits(wi + 1, k_prev)
        v_prev = finish(wi, pending, v_prev)
        if wi + 1 < SWA_NW:
            pending = upcoming


def _swa_attention(sinks, proj, pos_row, inv_tab, batch, seq):
    nb = seq // SWA_QB
    per = SWA_QB // WINDOW
    qcol = _PROJ_OFF["q_b"] // SWA_WIDTH
    kcol = _PROJ_OFF["k_b"] // LANES
    vcol = _PROJ_OFF["v_b"] // LANES

    def cur(b, n):
        return b * nb + n

    def prev(b, n):
        return jnp.maximum((b * nb + n) * per - 1, 0)

    return pl.pallas_call(
        _swa_kernel,
        grid=(batch, nb),
        in_specs=[
            pl.BlockSpec(memory_space=pltpu.SMEM),
            pl.BlockSpec((SWA_QB, SWA_WIDTH), lambda b, n: (cur(b, n), qcol)),
            pl.BlockSpec((SWA_QB, LANES), lambda b, n: (cur(b, n), kcol)),
            pl.BlockSpec((WINDOW, LANES), lambda b, n: (prev(b, n), kcol)),
            pl.BlockSpec((SWA_QB, LANES), lambda b, n: (cur(b, n), vcol)),
            pl.BlockSpec((WINDOW, LANES), lambda b, n: (prev(b, n), vcol)),
            pl.BlockSpec((1, SWA_QB), lambda b, n: (0, cur(b, n))),
            pl.BlockSpec((1, WINDOW), lambda b, n: (0, prev(b, n))),
            pl.BlockSpec((ROPE_HALF, LANES), lambda b, n: (0, 0)),
        ],
        out_specs=pl.BlockSpec((SWA_QB, SWA_WIDTH), lambda b, n: (cur(b, n), 0)),
        out_shape=jax.ShapeDtypeStruct((batch * seq, SWA_WIDTH), BF16),
        scratch_shapes=[pltpu.VMEM((2 * WINDOW, SWA_GROUP * WINDOW), F32)],
        compiler_params=pltpu.CompilerParams(
            dimension_semantics=("arbitrary", "arbitrary")),
        name="swa_attention",
    )(sinks, proj, proj, proj, proj, proj, pos_row, pos_row, inv_tab)


def _silu(z):
    return z * jax.nn.sigmoid(z)


def _epilogue_kernel(ya_ref, za_ref, yb_ref, zb_ref, ga_ref, gb_ref, x_ref, p_ref,
                     wof_ref, wos_ref, wout_ref, gpost_ref, wple_ref, wgate_ref, o_ref):
    ua = (ya_ref[...].astype(F32) * _silu(za_ref[...].astype(F32))).astype(BF16)
    ub = (yb_ref[...].astype(F32) * _silu(zb_ref[...].astype(F32))).astype(BF16)
    oa = jnp.dot(ua, wof_ref[...], preferred_element_type=F32)
    ob = jnp.dot(ub, wos_ref[...], preferred_element_type=F32)
    merged = (jax.nn.sigmoid(ga_ref[...].astype(F32)) * oa
              + jax.nn.sigmoid(gb_ref[...].astype(F32)) * ob)
    out = jnp.dot(merged.astype(BF16), wout_ref[...], preferred_element_type=F32)
    ms = jnp.mean(out * out, axis=-1, keepdims=True)
    x1 = x_ref[...] + (out * lax.rsqrt(ms + NORM_EPS)) * gpost_ref[...]
    e = jnp.dot(p_ref[...].astype(BF16), wple_ref[...], preferred_element_type=F32)
    gate = jax.nn.sigmoid(jnp.dot(x1.astype(BF16), wgate_ref[...],
                                  preferred_element_type=F32))
    o_ref[...] = x1 + gate * e


def _epilogue(ya, yb, proj, x2, p2, wof, wos, wout, gpost, wple, wgate):
    t = x2.shape[0]
    za_col = _PROJ_OFF["z_a"] // FOX_WIDTH
    zb_col = _PROJ_OFF["z_b"] // SWA_WIDTH
    ga_col = _PROJ_OFF["g_a"] // D_MODEL
    gb_col = _PROJ_OFF["g_b"] // D_MODEL
    once = pl.Buffered(1)

    def const(shape):
        return pl.BlockSpec(shape, lambda i: (0, 0), pipeline_mode=once)

    return pl.pallas_call(
        _epilogue_kernel,
        grid=(t // EPI_TM,),
        in_specs=[
            pl.BlockSpec((EPI_TM, FOX_WIDTH), lambda i: (i, 0)),
            pl.BlockSpec((EPI_TM, FOX_WIDTH), lambda i: (i, za_col)),
            pl.BlockSpec((EPI_TM, SWA_WIDTH), lambda i: (i, 0)),
            pl.BlockSpec((EPI_TM, SWA_WIDTH), lambda i: (i, zb_col)),
            pl.BlockSpec((EPI_TM, D_MODEL), lambda i: (i, ga_col)),
            pl.BlockSpec((EPI_TM, D_MODEL), lambda i: (i, gb_col)),
            pl.BlockSpec((EPI_TM, D_MODEL), lambda i: (i, 0)),
            pl.BlockSpec((EPI_TM, PLE_DIM), lambda i: (i, 0)),
            const((FOX_WIDTH, D_MODEL)),
            const((SWA_WIDTH, D_MODEL)),
            const((D_MODEL, D_MODEL)),
            const((1, D_MODEL)),
            const((PLE_DIM, D_MODEL)),
            const((D_MODEL, D_MODEL)),
        ],
        out_specs=pl.BlockSpec((EPI_TM, D_MODEL), lambda i: (i, 0)),
        out_shape=jax.ShapeDtypeStruct((t, D_MODEL), F32),
        compiler_params=pltpu.CompilerParams(
            dimension_semantics=("arbitrary",),
            vmem_limit_bytes=56 * 1024 * 1024),
        name="epilogue",
    )(ya, proj, yb, proj, proj, proj, x2, p2, wof, wos, wout, gpost, wple, wgate)


def _layer(x2, p2, pos_row, batch, seq, pre_g, w_in, b_forget, sinks, w_o_fox, w_o_swa,
           w_out, post_g, w_ple, w_ple_gate):
    w_t = w_in.T
    w_perm = _prep_weights(w_t)
    f0 = _REF_OFF["f_a"]
    wf_t = jnp.pad(w_t[f0:f0 + FOX_HEADS], ((0, F_ROWS - FOX_HEADS), (0, 0))).astype(BF16)
    col_scale = jnp.where(jnp.arange(PROJ_COLS) < FOX_WIDTH, Q_A_SCALE, 1.0
                          ).astype(F32).reshape(1, PROJ_COLS)

    proj, ft = _inproj(x2, pre_g.reshape(1, D_MODEL), w_perm, col_scale, wf_t)
    kb_tok = _forget_cumsum(ft, b_forget.reshape(FOX_HEADS, 1).astype(F32), batch, seq)
    ya = _fox_attention(proj, kb_tok, batch, seq)

    inv = ROPE_THETA ** (-jnp.arange(ROPE_HALF, dtype=F32) / ROPE_HALF)
    inv_tab = jnp.broadcast_to(inv[:, None], (ROPE_HALF, LANES))
    yb = _swa_attention(sinks.astype(F32), proj, pos_row, inv_tab, batch, seq)

    return _epilogue(ya, yb, proj, x2, p2,
                     w_o_fox.astype(BF16), w_o_swa.astype(BF16), w_out.astype(BF16),
                     post_g.reshape(1, D_MODEL), w_ple.astype(BF16), w_ple_gate.astype(BF16))


def kernel(x, p, positions, pre_norm_g, w_in, b_forget, sinks, w_o_fox, w_o_swa, w_out,
           post_norm_g, w_ple, w_ple_gate):
    batch, seq, _ = x.shape
    depth = p.shape[0]
    x2 = x.reshape(batch * seq, D_MODEL)
    pos_row = positions.reshape(1, batch * seq)
    for i in range(depth):
        x2 = _layer(x2, p[i].reshape(batch * seq, PLE_DIM), pos_row, batch, seq,
                    pre_norm_g[i], w_in[i], b_forget[i], sinks[i], w_o_fox[i], w_o_swa[i],
                    w_out[i], post_norm_g[i], w_ple[i], w_ple_gate[i])
    return x2.reshape(batch, seq, D_MODEL)
```

```python
import math

import jax
import jax.numpy as jnp
from jax import lax
from jax.experimental import pallas as pl
from jax.experimental.pallas import tpu as pltpu

F32 = jnp.float32
BF16 = jnp.bfloat16

D_MODEL = 2048
FOX_HEADS = 8
FOX_HEAD_DIM = 128
FOX_WIDTH = FOX_HEADS * FOX_HEAD_DIM
SWA_Q_HEADS = 16
SWA_KV_HEADS = 2
SWA_HEAD_DIM = 64
SWA_WIDTH = SWA_Q_HEADS * SWA_HEAD_DIM
SWA_KV_WIDTH = SWA_KV_HEADS * SWA_HEAD_DIM
SWA_GROUP = SWA_Q_HEADS // SWA_KV_HEADS
ROPE_HALF = SWA_HEAD_DIM // 2
WINDOW = 128
ROPE_THETA = 10000.0
PLE_DIM = 256
NORM_EPS = 1e-6
LANES = 128
LOG2E = math.log2(math.e)
NEG_BIG = -1e30

_REF_SPLITS = (FOX_WIDTH, FOX_WIDTH, FOX_WIDTH, FOX_WIDTH, FOX_HEADS, SWA_WIDTH,
               SWA_KV_WIDTH, SWA_KV_WIDTH, SWA_WIDTH, D_MODEL, D_MODEL)
_REF_NAMES = ("q_a", "k_a", "v_a", "z_a", "f_a", "q_b", "k_b", "v_b", "z_b", "g_a", "g_b")
_REF_OFF = {}
_o = 0
for _n, _s in zip(_REF_NAMES, _REF_SPLITS):
    _REF_OFF[_n] = _o
    _o += _s

IN_TM = 1024
IN_TN = 1536
NORM_ROWS = 512
F_ROWS = 16

PREP_TN = 512
_W_GROUPS = (
    (_REF_OFF["q_a"], 4 * FOX_WIDTH),
    (_REF_OFF["q_b"], SWA_WIDTH),
    (_REF_OFF["z_b"], SWA_WIDTH),
    (_REF_OFF["g_a"], 2 * D_MODEL),
    (_REF_OFF["k_b"], 2 * SWA_KV_WIDTH),
)
_GROUP_TILES = tuple(-(-w // PREP_TN) for _, w in _W_GROUPS)
_GROUP_START = tuple(sum(_GROUP_TILES[:i]) for i in range(len(_W_GROUPS)))
PREP_TILES = sum(_GROUP_TILES)
PROJ_COLS = PREP_TILES * PREP_TN
_PROJ_OFF = {
    "q_a": 0, "k_a": FOX_WIDTH, "v_a": 2 * FOX_WIDTH, "z_a": 3 * FOX_WIDTH,
    "q_b": _GROUP_START[1] * PREP_TN,
    "z_b": _GROUP_START[2] * PREP_TN,
    "g_a": _GROUP_START[3] * PREP_TN, "g_b": _GROUP_START[3] * PREP_TN + D_MODEL,
    "k_b": _GROUP_START[4] * PREP_TN, "v_b": _GROUP_START[4] * PREP_TN + SWA_KV_WIDTH,
}
SUBLANES = 8
assert all(c0 % SUBLANES == 0 for c0, _ in _W_GROUPS)
W_PER_TILE = IN_TN // PREP_TN
_NT = (((1,), (1,)), ((), ()))
Q_A_SCALE = FOX_HEAD_DIM ** -0.5 * LOG2E

FOX_T = 512
FOX_CHUNK = 512
FOX_PAIR = 4
BIAS_TERMS = 3

SWA_NW = 4
SWA_QB = SWA_NW * WINDOW

EPI_TM = 256


def _prenorm_kernel(x_ref, g_ref, wf_ref, h_ref, ft_ref):
    x = x_ref[...]
    ms = jnp.mean(x * x, axis=-1, keepdims=True)
    h = ((x * lax.rsqrt(ms + NORM_EPS)) * g_ref[...]).astype(BF16)
    h_ref[...] = h
    ft_ref[...] = lax.dot_general(wf_ref[...], h, _NT, preferred_element_type=F32)


def _prenorm(x2, g, wf_t):
    t = x2.shape[0]
    return pl.pallas_call(
        _prenorm_kernel,
        grid=(t // NORM_ROWS,),
        in_specs=[
            pl.BlockSpec((NORM_ROWS, D_MODEL), lambda i: (i, 0)),
            pl.BlockSpec((1, D_MODEL), lambda i: (0, 0)),
            pl.BlockSpec((F_ROWS, D_MODEL), lambda i: (0, 0)),
        ],
        out_specs=[
            pl.BlockSpec((NORM_ROWS, D_MODEL), lambda i: (i, 0)),
            pl.BlockSpec((F_ROWS, NORM_ROWS), lambda i: (0, i)),
        ],
        out_shape=[
            jax.ShapeDtypeStruct((t, D_MODEL), BF16),
            jax.ShapeDtypeStruct((F_ROWS, t), F32),
        ],
        compiler_params=pltpu.CompilerParams(dimension_semantics=("arbitrary",)),
        name="prenorm",
    )(x2, g, wf_t)


def _w_src_row(tile):
    tile_row = jnp.int32(0)
    for (c0, _), start in zip(_W_GROUPS, _GROUP_START):
        tile_row = jnp.where(tile >= start,
                             c0 // SUBLANES + (tile - start) * (PREP_TN // SUBLANES), tile_row)
    return tile_row * SUBLANES


def _inproj_kernel(h_ref, *refs):
    w_refs = refs[:W_PER_TILE]
    scale_ref, proj_ref, wb_ref = refs[W_PER_TILE:]

    @pl.when(pl.program_id(1) == 0)
    def _():
        for k, w_ref in enumerate(w_refs):
            for r in range(PREP_TN // LANES):
                rows = slice(r * LANES, (r + 1) * LANES)
                wb_ref[k * PREP_TN + r * LANES:k * PREP_TN + (r + 1) * LANES, :] = (
                    w_ref[rows, :].astype(BF16))

    acc = lax.dot_general(h_ref[...], wb_ref[...], _NT, preferred_element_type=F32)
    proj_ref[...] = (acc * scale_ref[...]).astype(BF16)


def _inproj(h, w_t, col_scale):
    t = h.shape[0]

    def w_spec(k):
        return pl.BlockSpec((pl.Element(PREP_TN), pl.Element(D_MODEL)),
                            lambda j, i: (_w_src_row(j * W_PER_TILE + k), 0))

    return pl.pallas_call(
        _inproj_kernel,
        grid=(PROJ_COLS // IN_TN, t // IN_TM),
        in_specs=[
            pl.BlockSpec((IN_TM, D_MODEL), lambda j, i: (i, 0)),
            *[w_spec(k) for k in range(W_PER_TILE)],
            pl.BlockSpec((1, IN_TN), lambda j, i: (0, j)),
        ],
        out_specs=pl.BlockSpec((IN_TM, IN_TN), lambda j, i: (i, j)),
        out_shape=jax.ShapeDtypeStruct((t, PROJ_COLS), BF16),
        scratch_shapes=[pltpu.VMEM((IN_TN, D_MODEL), BF16)],
        compiler_params=pltpu.CompilerParams(
            dimension_semantics=("arbitrary", "arbitrary"),
            vmem_limit_bytes=56 * 1024 * 1024),
        name="inproj",
    )(h, *([w_t] * W_PER_TILE), col_scale)


def _cumsum_kernel(ft_ref, b_ref, kb_ref):
    f = ft_ref[0:FOX_HEADS, :] + b_ref[...]
    lf = jnp.minimum(f, 0.0) - jnp.log1p(jnp.exp(-jnp.abs(f)))
    seq = lf.shape[1]
    lane = lax.broadcasted_iota(jnp.int32, lf.shape, 1)
    c = lf
    shift = 1
    while shift < seq:
        c = c + jnp.where(lane >= shift, pltpu.roll(c, shift, axis=1), 0.0)
        shift *= 2
    rest = c * (-LOG2E)
    pieces = []
    for _ in range(BIAS_TERMS):
        piece = rest.astype(BF16).astype(F32)
        pieces.append(piece)
        rest = rest - piece
    pieces.append(jnp.zeros((LANES - BIAS_TERMS * FOX_HEADS, seq), F32))
    kb_ref[...] = jnp.concatenate(pieces, axis=0).T.astype(BF16)


def _forget_cumsum(ft, b_col, batch, seq):
    return pl.pallas_call(
        _cumsum_kernel,
        grid=(batch,),
        in_specs=[
            pl.BlockSpec((F_ROWS, seq), lambda b: (0, b)),
            pl.BlockSpec((FOX_HEADS, 1), lambda b: (0, 0)),
        ],
        out_specs=pl.BlockSpec((seq, LANES), lambda b: (b, 0)),
        out_shape=jax.ShapeDtypeStruct((batch * seq, LANES), BF16),
        name="forget_cumsum",
    )(ft, b_col)


def _fox_kernel(q_ref, k_ref, v_ref, kb_ref, o_ref, vt_ref, qt_ref, acc_ref, s0_ref, s1_ref):
    hp = pl.program_id(1)
    qi = pl.program_id(2)
    seq = k_ref.shape[0]
    t = FOX_T
    d = FOX_HEAD_DIM

    @pl.when(qi == 0)
    def _():
        row = lax.broadcasted_iota(jnp.int32, (d, t), 0)
        for hh in range(FOX_PAIR):
            mine = (row < BIAS_TERMS * FOX_HEADS) & (row % FOX_HEADS == hp * FOX_PAIR + hh)
            qt_ref[hh, d:2 * d, :] = jnp.where(mine, 1.0, 0.0).astype(BF16)
            for r in range(seq // FOX_CHUNK):
                rows = slice(r * FOX_CHUNK, (r + 1) * FOX_CHUNK)
                vt_ref[hh, :, rows] = v_ref[rows, hh * d:(hh + 1) * d].astype(F32).T.astype(BF16)

    for hh in range(FOX_PAIR):
        qt_ref[hh, 0:d, :] = q_ref[:, hh * d:(hh + 1) * d].astype(F32).T.astype(BF16)
        acc_ref[hh] = jnp.zeros((d, t), F32)

    def scores(i, s_ref):
        rows = pl.ds(pl.multiple_of(i * t, t), t)
        for hh in range(FOX_PAIR):
            k_aug = jnp.concatenate([k_ref[rows, hh * d:(hh + 1) * d], kb_ref[rows, :]], axis=1)
            s_ref[hh] = jnp.dot(k_aug, qt_ref[hh], preferred_element_type=F32)

    def absorb(i, s_ref, carry, masked):
        rows = pl.ds(pl.multiple_of(i * t, t), t)
        new = []
        for hh in range(FOX_PAIR):
            m, l = carry[hh]
            s = s_ref[hh]
            if masked:
                key = lax.broadcasted_iota(jnp.int32, s.shape, 0)
                qry = lax.broadcasted_iota(jnp.int32, s.shape, 1)
                s = jnp.where(key <= qry, s, NEG_BIG)
            m_new = jnp.maximum(m, jnp.max(s, axis=0, keepdims=True))
            alpha = jnp.exp2(m - m_new)
            p = jnp.exp2(s - m_new)
            l_new = alpha * l + jnp.sum(p, axis=0, keepdims=True)
            acc_ref[hh] = alpha * acc_ref[hh] + jnp.dot(
                vt_ref[hh, :, rows], p.astype(BF16), preferred_element_type=F32)
            new.append((m_new, l_new))
        return tuple(new)

    def step(i, s_cur, s_next, carry):
        scores(i + 1, s_next)
        return absorb(i, s_cur, carry, False)

    def pair(j, carry):
        carry = step(2 * j, s0_ref, s1_ref, carry)
        return step(2 * j + 1, s1_ref, s0_ref, carry)

    def odd_tail(carry):
        carry = step(qi - 1, s0_ref, s1_ref, carry)
        return absorb(qi, s1_ref, carry, True)

    def even_tail(carry):
        return absorb(qi, s0_ref, carry, True)

    init = tuple((jnp.full((1, t), NEG_BIG, F32), jnp.zeros((1, t), F32))
                 for _ in range(FOX_PAIR))
    scores(0, s0_ref)
    carry = lax.fori_loop(0, qi // 2, pair, init)
    carry = lax.cond(qi % 2 == 1, odd_tail, even_tail, carry)
    for hh in range(FOX_PAIR):
        o_ref[:, hh * d:(hh + 1) * d] = (acc_ref[hh] / carry[hh][1]).T.astype(o_ref.dtype)


def _fox_attention(proj, kb_tok, batch, seq):
    nq = seq // FOX_T
    width = FOX_PAIR * FOX_HEAD_DIM
    qcol = _PROJ_OFF["q_a"] // width
    kcol = _PROJ_OFF["k_a"] // width
    vcol = _PROJ_OFF["v_a"] // width
    return pl.pallas_call(
        _fox_kernel,
        grid=(batch, FOX_HEADS // FOX_PAIR, nq),
        in_specs=[
            pl.BlockSpec((FOX_T, width), lambda b, h, qi: (b * nq + qi, qcol + h)),
            pl.BlockSpec((seq, width), lambda b, h, qi: (b, kcol + h)),
            pl.BlockSpec((seq, width), lambda b, h, qi: (b, vcol + h)),
            pl.BlockSpec((seq, LANES), lambda b, h, qi: (b, 0)),
        ],
        out_specs=pl.BlockSpec((FOX_T, width), lambda b, h, qi: (b * nq + qi, h)),
        out_shape=jax.ShapeDtypeStruct((batch * seq, FOX_WIDTH), BF16),
        scratch_shapes=[
            pltpu.VMEM((FOX_PAIR, FOX_HEAD_DIM, seq), BF16),
            pltpu.VMEM((FOX_PAIR, 2 * FOX_HEAD_DIM, FOX_T), BF16),
            pltpu.VMEM((FOX_PAIR, FOX_HEAD_DIM, FOX_T), F32),
            pltpu.VMEM((FOX_PAIR, FOX_T, FOX_T), F32),
            pltpu.VMEM((FOX_PAIR, FOX_T, FOX_T), F32),
        ],
        compiler_params=pltpu.CompilerParams(
            dimension_semantics=("arbitrary", "arbitrary", "arbitrary"),
            vmem_limit_bytes=40 * 1024 * 1024),
        name="fox_attention",
    )(proj, proj, proj, kb_tok)


def _rope_t(xt, cos, sin):
    out = []
    for hd in range(xt.shape[0] // SWA_HEAD_DIM):
        x1 = xt[hd * SWA_HEAD_DIM: hd * SWA_HEAD_DIM + ROPE_HALF]
        x2 = xt[hd * SWA_HEAD_DIM + ROPE_HALF: (hd + 1) * SWA_HEAD_DIM]
        out.append(x1 * cos - x2 * sin)
        out.append(x2 * cos + x1 * sin)
    return jnp.concatenate(out, axis=0)


def _swa_kernel(sink_ref, q_ref, kc_ref, kp_ref, vc_ref, vp_ref, posc_ref, posp_ref,
                inv_ref, o_ref, band_ref):
    n = pl.program_id(1)
    w = WINDOW
    inv = inv_ref[...]

    def tables(pos_row):
        ang = inv * pos_row.astype(F32)
        return jnp.cos(ang), jnp.sin(ang)

    def rope_k(k_nat, cos, sin):
        kt = _rope_t(k_nat.astype(F32).T, cos, sin)
        return kt.T.astype(BF16)

    cos_p, sin_p = tables(posp_ref[...])
    k_prev = rope_k(kp_ref[...], cos_p, sin_p)
    v_prev = vp_ref[...]

    @pl.when((pl.program_id(0) == 0) & (n == 0))
    def _():
        key = lax.broadcasted_iota(jnp.int32, band_ref.shape, 0)
        qry = lax.broadcasted_iota(jnp.int32, band_ref.shape, 1) % w
        band_ref[...] = jnp.where((key <= qry + w) & (key > qry), 0.0, NEG_BIG)

    no_prev = jnp.where(n > 0, 0.0, NEG_BIG)
    q_scale = SWA_HEAD_DIM ** -0.5 * LOG2E
    zeros_half = jnp.zeros((SWA_HEAD_DIM, SWA_GROUP * w), F32)

    sinks = [jnp.concatenate(
        [jnp.full((1, w), sink_ref[hd] * LOG2E, F32)
         for hd in range(g * SWA_GROUP, (g + 1) * SWA_GROUP)], axis=1)
        for g in range(SWA_KV_HEADS)]

    def logits(wi, k_prev):
        tok = slice(wi * w, (wi + 1) * w)
        cos, sin = tables(posc_ref[:, tok])
        k_cur = rope_k(kc_ref[tok, :], cos, sin)
        k_all = jnp.concatenate([k_prev, k_cur], axis=0)
        qf = q_ref[tok, :].astype(F32)
        qt = jnp.concatenate(
            [qf[:, c * LANES:(c + 1) * LANES].T for c in range(SWA_WIDTH // LANES)], axis=0)
        qt = _rope_t(qt, cos * q_scale, sin * q_scale)
        out = []
        for g in range(SWA_KV_HEADS):
            heads = range(g * SWA_GROUP, (g + 1) * SWA_GROUP)
            q_g = jnp.concatenate(
                [qt[hd * SWA_HEAD_DIM:(hd + 1) * SWA_HEAD_DIM] for hd in heads], axis=1)
            parts = [zeros_half] * SWA_KV_HEADS
            parts[g] = q_g
            q_z = jnp.concatenate(parts, axis=0).astype(BF16)
            s = jnp.dot(k_all, q_z, preferred_element_type=F32) + band_ref[...]
            if wi == 0:
                s = jnp.concatenate([s[:w] + no_prev, s[w:]], axis=0)
            out.append(s)
        return out, k_cur

    def finish(wi, scores, v_prev):
        tok = slice(wi * w, (wi + 1) * w)
        v_cur = vc_ref[tok, :]
        v_all_t = jnp.concatenate([v_prev, v_cur], axis=0).astype(F32).T.astype(BF16)
        for g in range(SWA_KV_HEADS):
            s, sink = scores[g], sinks[g]
            m = jnp.maximum(jnp.max(s, axis=0, keepdims=True), sink)
            e = jnp.exp2(s - m)
            denom = jnp.sum(e, axis=0, keepdims=True) + jnp.exp2(sink - m)
            out_t = jnp.dot(v_all_t, e.astype(BF16), preferred_element_type=F32)
            out_t = out_t[g * SWA_HEAD_DIM:(g + 1) * SWA_HEAD_DIM] / denom
            for jj in range(SWA_GROUP // 2):
                pair = jnp.concatenate(
                    [out_t[:, (2 * jj) * w:(2 * jj + 1) * w],
                     out_t[:, (2 * jj + 1) * w:(2 * jj + 2) * w]], axis=0)
                col = (g * (SWA_GROUP // 2) + jj) * LANES
                o_ref[tok, col:col + LANES] = pair.T.astype(o_ref.dtype)
        return v_cur

    pending, k_prev = logits(0, k_prev)
    for wi in range(SWA_NW):
        if wi + 1 < SWA_NW:
            upcoming, k_prev = logits(wi + 1, k_prev)
        v_prev = finish(wi, pending, v_prev)
        if wi + 1 < SWA_NW:
            pending = upcoming


def _swa_attention(sinks, proj, pos_row, inv_tab, batch, seq):
    nb = seq // SWA_QB
    per = SWA_QB // WINDOW
    qcol = _PROJ_OFF["q_b"] // SWA_WIDTH
    kcol = _PROJ_OFF["k_b"] // LANES
    vcol = _PROJ_OFF["v_b"] // LANES

    def cur(b, n):
        return b * nb + n

    def prev(b, n):
        return jnp.maximum((b * nb + n) * per - 1, 0)

    return pl.pallas_call(
        _swa_kernel,
        grid=(batch, nb),
        in_specs=[
            pl.BlockSpec(memory_space=pltpu.SMEM),
            pl.BlockSpec((SWA_QB, SWA_WIDTH), lambda b, n: (cur(b, n), qcol)),
            pl.BlockSpec((SWA_QB, LANES), lambda b, n: (cur(b, n), kcol)),
            pl.BlockSpec((WINDOW, LANES), lambda b, n: (prev(b, n), kcol)),
            pl.BlockSpec((SWA_QB, LANES), lambda b, n: (cur(b, n), vcol)),
            pl.BlockSpec((WINDOW, LANES), lambda b, n: (prev(b, n), vcol)),
            pl.BlockSpec((1, SWA_QB), lambda b, n: (0, cur(b, n))),
            pl.BlockSpec((1, WINDOW), lambda b, n: (0, prev(b, n))),
            pl.BlockSpec((ROPE_HALF, LANES), lambda b, n: (0, 0)),
        ],
        out_specs=pl.BlockSpec((SWA_QB, SWA_WIDTH), lambda b, n: (cur(b, n), 0)),
        out_shape=jax.ShapeDtypeStruct((batch * seq, SWA_WIDTH), BF16),
        scratch_shapes=[pltpu.VMEM((2 * WINDOW, SWA_GROUP * WINDOW), F32)],
        compiler_params=pltpu.CompilerParams(
            dimension_semantics=("arbitrary", "arbitrary")),
        name="swa_attention",
    )(sinks, proj, proj, proj, proj, proj, pos_row, pos_row, inv_tab)


def _silu(z):
    return z * jax.nn.sigmoid(z)


def _epilogue_kernel(ya_ref, za_ref, yb_ref, zb_ref, ga_ref, gb_ref, x_ref, p_ref,
                     wof_ref, wos_ref, wout_ref, gpost_ref, wple_ref, wgate_ref, o_ref):
    ua = (ya_ref[...].astype(F32) * _silu(za_ref[...].astype(F32))).astype(BF16)
    ub = (yb_ref[...].astype(F32) * _silu(zb_ref[...].astype(F32))).astype(BF16)
    oa = jnp.dot(ua, wof_ref[...], preferred_element_type=F32)
    ob = jnp.dot(ub, wos_ref[...], preferred_element_type=F32)
    merged = (jax.nn.sigmoid(ga_ref[...].astype(F32)) * oa
              + jax.nn.sigmoid(gb_ref[...].astype(F32)) * ob)
    out = jnp.dot(merged.astype(BF16), wout_ref[...], preferred_element_type=F32)
    ms = jnp.mean(out * out, axis=-1, keepdims=True)
    x1 = x_ref[...] + (out * lax.rsqrt(ms + NORM_EPS)) * gpost_ref[...]
    e = jnp.dot(p_ref[...].astype(BF16), wple_ref[...], preferred_element_type=F32)
    gate = jax.nn.sigmoid(jnp.dot(x1.astype(BF16), wgate_ref[...],
                                  preferred_element_type=F32))
    o_ref[...] = x1 + gate * e


def _epilogue(ya, yb, proj, x2, p2, wof, wos, wout, gpost, wple, wgate):
    t = x2.shape[0]
    za_col = _PROJ_OFF["z_a"] // FOX_WIDTH
    zb_col = _PROJ_OFF["z_b"] // SWA_WIDTH
    ga_col = _PROJ_OFF["g_a"] // D_MODEL
    gb_col = _PROJ_OFF["g_b"] // D_MODEL
    once = pl.Buffered(1)

    def const(shape):
        return pl.BlockSpec(shape, lambda i: (0, 0), pipeline_mode=once)

    return pl.pallas_call(
        _epilogue_kernel,
        grid=(t // EPI_TM,),
        in_specs=[
            pl.BlockSpec((EPI_TM, FOX_WIDTH), lambda i: (i, 0)),
            pl.BlockSpec((EPI_TM, FOX_WIDTH), lambda i: (i, za_col)),
            pl.BlockSpec((EPI_TM, SWA_WIDTH), lambda i: (i, 0)),
            pl.BlockSpec((EPI_TM, SWA_WIDTH), lambda i: (i, zb_col)),
            pl.BlockSpec((EPI_TM, D_MODEL), lambda i: (i, ga_col)),
            pl.BlockSpec((EPI_TM, D_MODEL), lambda i: (i, gb_col)),
            pl.BlockSpec((EPI_TM, D_MODEL), lambda i: (i, 0)),
            pl.BlockSpec((EPI_TM, PLE_DIM), lambda i: (i, 0)),
            const((FOX_WIDTH, D_MODEL)),
            const((SWA_WIDTH, D_MODEL)),
            const((D_MODEL, D_MODEL)),
            const((1, D_MODEL)),
            const((PLE_DIM, D_MODEL)),
            const((D_MODEL, D_MODEL)),
        ],
        out_specs=pl.BlockSpec((EPI_TM, D_MODEL), lambda i: (i, 0)),
        out_shape=jax.ShapeDtypeStruct((t, D_MODEL), F32),
        compiler_params=pltpu.CompilerParams(
            dimension_semantics=("arbitrary",),
            vmem_limit_bytes=56 * 1024 * 1024),
        name="epilogue",
    )(ya, proj, yb, proj, proj, proj, x2, p2, wof, wos, wout, gpost, wple, wgate)


def _layer(x2, p2, pos_row, batch, seq, pre_g, w_in, b_forget, sinks, w_o_fox, w_o_swa,
           w_out, post_g, w_ple, w_ple_gate):
    w_t = w_in.T
    f0 = _REF_OFF["f_a"]
    wf_t = jnp.pad(w_t[f0:f0 + FOX_HEADS], ((0, F_ROWS - FOX_HEADS), (0, 0))).astype(BF16)
    col_scale = jnp.where(jnp.arange(PROJ_COLS) < FOX_WIDTH, Q_A_SCALE, 1.0
                          ).astype(F32).reshape(1, PROJ_COLS)

    h, ft = _prenorm(x2, pre_g.reshape(1, D_MODEL), wf_t)
    proj = _inproj(h, w_t, col_scale)
    kb_tok = _forget_cumsum(ft, b_forget.reshape(FOX_HEADS, 1).astype(F32), batch, seq)
    ya = _fox_attention(proj, kb_tok, batch, seq)

    inv = ROPE_THETA ** (-jnp.arange(ROPE_HALF, dtype=F32) / ROPE_HALF)
    inv_tab = jnp.broadcast_to(inv[:, None], (ROPE_HALF, LANES))
    yb = _swa_attention(sinks.astype(F32), proj, pos_row, inv_tab, batch, seq)

    return _epilogue(ya, yb, proj, x2, p2,
                     w_o_fox.astype(BF16), w_o_swa.astype(BF16), w_out.astype(BF16),
                     post_g.reshape(1, D_MODEL), w_ple.astype(BF16), w_ple_gate.astype(BF16))


def kernel(x, p, positions, pre_norm_g, w_in, b_forget, sinks, w_o_fox, w_o_swa, w_out,
           post_norm_g, w_ple, w_ple_gate):
    batch, seq, _ = x.shape
    depth = p.shape[0]
    x2 = x.reshape(batch * seq, D_MODEL)
    pos_row = positions.reshape(1, batch * seq)
    for i in range(depth):
        x2 = _layer(x2, p[i].reshape(batch * seq, PLE_DIM), pos_row, batch, seq,
                    pre_norm_g[i], w_in[i], b_forget[i], sinks[i], w_o_fox[i], w_o_swa[i],
                    w_out[i], post_norm_g[i], w_ple[i], w_ple_gate[i])
    return x2.reshape(batch, seq, D_MODEL)
```

```python
import math

import jax
import jax.numpy as jnp
from jax import lax
from jax.experimental import pallas as pl
from jax.experimental.pallas import tpu as pltpu

F32 = jnp.float32
BF16 = jnp.bfloat16

D_MODEL = 2048
FOX_HEADS = 8
FOX_HEAD_DIM = 128
FOX_WIDTH = FOX_HEADS * FOX_HEAD_DIM
SWA_Q_HEADS = 16
SWA_KV_HEADS = 2
SWA_HEAD_DIM = 64
SWA_WIDTH = SWA_Q_HEADS * SWA_HEAD_DIM
SWA_KV_WIDTH = SWA_KV_HEADS * SWA_HEAD_DIM
SWA_GROUP = SWA_Q_HEADS // SWA_KV_HEADS
ROPE_HALF = SWA_HEAD_DIM // 2
WINDOW = 128
ROPE_THETA = 10000.0
PLE_DIM = 256
NORM_EPS = 1e-6
LANES = 128
LOG2E = math.log2(math.e)
NEG_BIG = -1e30

_REF_SPLITS = (FOX_WIDTH, FOX_WIDTH, FOX_WIDTH, FOX_WIDTH, FOX_HEADS, SWA_WIDTH,
               SWA_KV_WIDTH, SWA_KV_WIDTH, SWA_WIDTH, D_MODEL, D_MODEL)
_REF_NAMES = ("q_a", "k_a", "v_a", "z_a", "f_a", "q_b", "k_b", "v_b", "z_b", "g_a", "g_b")
_REF_OFF = {}
_o = 0
for _n, _s in zip(_REF_NAMES, _REF_SPLITS):
    _REF_OFF[_n] = _o
    _o += _s

IN_TM = 1024
IN_TN = 1536
NORM_ROWS = 512
F_ROWS = 16

PREP_TN = 512
_W_GROUPS = (
    (_REF_OFF["q_a"], 4 * FOX_WIDTH),
    (_REF_OFF["q_b"], SWA_WIDTH),
    (_REF_OFF["z_b"], SWA_WIDTH),
    (_REF_OFF["g_a"], 2 * D_MODEL),
    (_REF_OFF["k_b"], 2 * SWA_KV_WIDTH),
)
_GROUP_TILES = tuple(-(-w // PREP_TN) for _, w in _W_GROUPS)
_GROUP_START = tuple(sum(_GROUP_TILES[:i]) for i in range(len(_W_GROUPS)))
PREP_TILES = sum(_GROUP_TILES)
PROJ_COLS = PREP_TILES * PREP_TN
_PROJ_OFF = {
    "q_a": 0, "k_a": FOX_WIDTH, "v_a": 2 * FOX_WIDTH, "z_a": 3 * FOX_WIDTH,
    "q_b": _GROUP_START[1] * PREP_TN,
    "z_b": _GROUP_START[2] * PREP_TN,
    "g_a": _GROUP_START[3] * PREP_TN, "g_b": _GROUP_START[3] * PREP_TN + D_MODEL,
    "k_b": _GROUP_START[4] * PREP_TN, "v_b": _GROUP_START[4] * PREP_TN + SWA_KV_WIDTH,
}
SUBLANES = 8
assert all(c0 % SUBLANES == 0 for c0, _ in _W_GROUPS)
W_PER_TILE = IN_TN // PREP_TN
_NT = (((1,), (1,)), ((), ()))
Q_A_SCALE = FOX_HEAD_DIM ** -0.5 * LOG2E

FOX_T = 512
FOX_CHUNK = 512
FOX_PAIR = 4
BIAS_TERMS = 3

SWA_NW = 4
SWA_QB = SWA_NW * WINDOW

EPI_TM = 256


def _prenorm_kernel(x_ref, g_ref, wf_ref, h_ref, ft_ref):
    x = x_ref[...]
    ms = jnp.mean(x * x, axis=-1, keepdims=True)
    h = ((x * lax.rsqrt(ms + NORM_EPS)) * g_ref[...]).astype(BF16)
    h_ref[...] = h
    ft_ref[...] = lax.dot_general(wf_ref[...], h, _NT, preferred_element_type=F32)


def _prenorm(x2, g, wf_t):
    t = x2.shape[0]
    return pl.pallas_call(
        _prenorm_kernel,
        grid=(t // NORM_ROWS,),
        in_specs=[
            pl.BlockSpec((NORM_ROWS, D_MODEL), lambda i: (i, 0)),
            pl.BlockSpec((1, D_MODEL), lambda i: (0, 0)),
            pl.BlockSpec((F_ROWS, D_MODEL), lambda i: (0, 0)),
        ],
        out_specs=[
            pl.BlockSpec((NORM_ROWS, D_MODEL), lambda i: (i, 0)),
            pl.BlockSpec((F_ROWS, NORM_ROWS), lambda i: (0, i)),
        ],
        out_shape=[
            jax.ShapeDtypeStruct((t, D_MODEL), BF16),
            jax.ShapeDtypeStruct((F_ROWS, t), F32),
        ],
        compiler_params=pltpu.CompilerParams(dimension_semantics=("arbitrary",)),
        name="prenorm",
    )(x2, g, wf_t)


def _w_src_row(tile):
    tile_row = jnp.int32(0)
    for (c0, _), start in zip(_W_GROUPS, _GROUP_START):
        tile_row = jnp.where(tile >= start,
                             c0 // SUBLANES + (tile - start) * (PREP_TN // SUBLANES), tile_row)
    return tile_row * SUBLANES


def _inproj_kernel(h_ref, *refs):
    w_refs = refs[:W_PER_TILE]
    scale_ref, proj_ref, wb_ref = refs[W_PER_TILE:]

    @pl.when(pl.program_id(1) == 0)
    def _():
        for k, w_ref in enumerate(w_refs):
            for r in range(PREP_TN // LANES):
                rows = slice(r * LANES, (r + 1) * LANES)
                wb_ref[k * PREP_TN + r * LANES:k * PREP_TN + (r + 1) * LANES, :] = (
                    w_ref[rows, :].astype(BF16))

    acc = lax.dot_general(h_ref[...], wb_ref[...], _NT, preferred_element_type=F32)
    proj_ref[...] = (acc * scale_ref[...]).astype(BF16)


def _inproj(h, w_t, col_scale):
    t = h.shape[0]

    def w_spec(k):
        return pl.BlockSpec((pl.Element(PREP_TN), pl.Element(D_MODEL)),
                            lambda j, i: (_w_src_row(j * W_PER_TILE + k), 0))

    return pl.pallas_call(
        _inproj_kernel,
        grid=(PROJ_COLS // IN_TN, t // IN_TM),
        in_specs=[
            pl.BlockSpec((IN_TM, D_MODEL), lambda j, i: (i, 0)),
            *[w_spec(k) for k in range(W_PER_TILE)],
            pl.BlockSpec((1, IN_TN), lambda j, i: (0, j)),
        ],
        out_specs=pl.BlockSpec((IN_TM, IN_TN), lambda j, i: (i, j)),
        out_shape=jax.ShapeDtypeStruct((t, PROJ_COLS), BF16),
        scratch_shapes=[pltpu.VMEM((IN_TN, D_MODEL), BF16)],
        compiler_params=pltpu.CompilerParams(
            dimension_semantics=("arbitrary", "arbitrary"),
            vmem_limit_bytes=56 * 1024 * 1024),
        name="inproj",
    )(h, *([w_t] * W_PER_TILE), col_scale)


def _cumsum_kernel(ft_ref, b_ref, kb_ref):
    f = ft_ref[0:FOX_HEADS, :] + b_ref[...]
    lf = jnp.minimum(f, 0.0) - jnp.log1p(jnp.exp(-jnp.abs(f)))
    seq = lf.shape[1]
    lane = lax.broadcasted_iota(jnp.int32, lf.shape, 1)
    c = lf
    shift = 1
    while shift < seq:
        c = c + jnp.where(lane >= shift, pltpu.roll(c, shift, axis=1), 0.0)
        shift *= 2
    rest = c * (-LOG2E)
    pieces = []
    for _ in range(BIAS_TERMS):
        piece = rest.astype(BF16).astype(F32)
        pieces.append(piece)
        rest = rest - piece
    pieces.append(jnp.zeros((LANES - BIAS_TERMS * FOX_HEADS, seq), F32))
    kb_ref[...] = jnp.concatenate(pieces, axis=0).T.astype(BF16)


def _forget_cumsum(ft, b_col, batch, seq):
    return pl.pallas_call(
        _cumsum_kernel,
        grid=(batch,),
        in_specs=[
            pl.BlockSpec((F_ROWS, seq), lambda b: (0, b)),
            pl.BlockSpec((FOX_HEADS, 1), lambda b: (0, 0)),
        ],
        out_specs=pl.BlockSpec((seq, LANES), lambda b: (b, 0)),
        out_shape=jax.ShapeDtypeStruct((batch * seq, LANES), BF16),
        name="forget_cumsum",
    )(ft, b_col)


def _fox_kernel(q_ref, k_ref, v_ref, kb_ref, o_ref, vt_ref, qt_ref, acc_ref, s0_ref, s1_ref):
    hp = pl.program_id(1)
    seq = k_ref.shape[0]
    t = FOX_T
    d = FOX_HEAD_DIM

    row = lax.broadcasted_iota(jnp.int32, (d, t), 0)
    for hh in range(FOX_PAIR):
        mine = (row < BIAS_TERMS * FOX_HEADS) & (row % FOX_HEADS == hp * FOX_PAIR + hh)
        qt_ref[hh, d:2 * d, :] = jnp.where(mine, 1.0, 0.0).astype(BF16)
        for r in range(seq // FOX_CHUNK):
            rows = slice(r * FOX_CHUNK, (r + 1) * FOX_CHUNK)
            vt_ref[hh, :, rows] = v_ref[rows, hh * d:(hh + 1) * d].astype(F32).T.astype(BF16)

    def scores(i, s_ref):
        rows = pl.ds(pl.multiple_of(i * t, t), t)
        for hh in range(FOX_PAIR):
            k_aug = jnp.concatenate([k_ref[rows, hh * d:(hh + 1) * d], kb_ref[rows, :]], axis=1)
            s_ref[hh] = jnp.dot(k_aug, qt_ref[hh], preferred_element_type=F32)

    def absorb(i, s_ref, carry, masked):
        rows = pl.ds(pl.multiple_of(i * t, t), t)
        new = []
        for hh in range(FOX_PAIR):
            m, l = carry[hh]
            s = s_ref[hh]
            if masked:
                key = lax.broadcasted_iota(jnp.int32, s.shape, 0)
                qry = lax.broadcasted_iota(jnp.int32, s.shape, 1)
                s = jnp.where(key <= qry, s, NEG_BIG)
            m_new = jnp.maximum(m, jnp.max(s, axis=0, keepdims=True))
            alpha = jnp.exp2(m - m_new)
            p = jnp.exp2(s - m_new)
            l_new = alpha * l + jnp.sum(p, axis=0, keepdims=True)
            acc_ref[hh] = alpha * acc_ref[hh] + jnp.dot(
                vt_ref[hh, :, rows], p.astype(BF16), preferred_element_type=F32)
            new.append((m_new, l_new))
        return tuple(new)

    def step(i, s_cur, s_next, carry):
        scores(i + 1, s_next)
        return absorb(i, s_cur, carry, False)

    def pair(j, carry):
        carry = step(2 * j, s0_ref, s1_ref, carry)
        return step(2 * j + 1, s1_ref, s0_ref, carry)

    def q_tile(qi, _):
        q_rows = pl.ds(pl.multiple_of(qi * t, t), t)
        for hh in range(FOX_PAIR):
            qt_ref[hh, 0:d, :] = q_ref[q_rows, hh * d:(hh + 1) * d].astype(F32).T.astype(BF16)
            acc_ref[hh] = jnp.zeros((d, t), F32)

        def odd_tail(carry):
            carry = step(qi - 1, s0_ref, s1_ref, carry)
            return absorb(qi, s1_ref, carry, True)

        def even_tail(carry):
            return absorb(qi, s0_ref, carry, True)

        init = tuple((jnp.full((1, t), NEG_BIG, F32), jnp.zeros((1, t), F32))
                     for _ in range(FOX_PAIR))
        scores(0, s0_ref)
        carry = lax.fori_loop(0, qi // 2, pair, init)
        carry = lax.cond(qi % 2 == 1, odd_tail, even_tail, carry)
        for hh in range(FOX_PAIR):
            o_ref[q_rows, hh * d:(hh + 1) * d] = (
                acc_ref[hh] / carry[hh][1]).T.astype(o_ref.dtype)
        return 0

    lax.fori_loop(0, seq // t, q_tile, 0)


def _fox_attention(proj, kb_tok, batch, seq):
    width = FOX_PAIR * FOX_HEAD_DIM
    qcol = _PROJ_OFF["q_a"] // width
    kcol = _PROJ_OFF["k_a"] // width
    vcol = _PROJ_OFF["v_a"] // width
    return pl.pallas_call(
        _fox_kernel,
        grid=(batch, FOX_HEADS // FOX_PAIR),
        in_specs=[
            pl.BlockSpec((seq, width), lambda b, h: (b, qcol + h)),
            pl.BlockSpec((seq, width), lambda b, h: (b, kcol + h)),
            pl.BlockSpec((seq, width), lambda b, h: (b, vcol + h)),
            pl.BlockSpec((seq, LANES), lambda b, h: (b, 0)),
        ],
        out_specs=pl.BlockSpec((seq, width), lambda b, h: (b, h)),
        out_shape=jax.ShapeDtypeStruct((batch * seq, FOX_WIDTH), BF16),
        scratch_shapes=[
            pltpu.VMEM((FOX_PAIR, FOX_HEAD_DIM, seq), BF16),
            pltpu.VMEM((FOX_PAIR, 2 * FOX_HEAD_DIM, FOX_T), BF16),
            pltpu.VMEM((FOX_PAIR, FOX_HEAD_DIM, FOX_T), F32),
            pltpu.VMEM((FOX_PAIR, FOX_T, FOX_T), F32),
            pltpu.VMEM((FOX_PAIR, FOX_T, FOX_T), F32),
        ],
        compiler_params=pltpu.CompilerParams(
            dimension_semantics=("arbitrary", "arbitrary"),
            vmem_limit_bytes=56 * 1024 * 1024),
        name="fox_attention",
    )(proj, proj, proj, kb_tok)


def _rope_t(xt, cos, sin):
    out = []
    for hd in range(xt.shape[0] // SWA_HEAD_DIM):
        x1 = xt[hd * SWA_HEAD_DIM: hd * SWA_HEAD_DIM + ROPE_HALF]
        x2 = xt[hd * SWA_HEAD_DIM + ROPE_HALF: (hd + 1) * SWA_HEAD_DIM]
        out.append(x1 * cos - x2 * sin)
        out.append(x2 * cos + x1 * sin)
    return jnp.concatenate(out, axis=0)


def _swa_kernel(sink_ref, q_ref, kc_ref, kp_ref, vc_ref, vp_ref, posc_ref, posp_ref,
                inv_ref, o_ref, band_ref):
    n = pl.program_id(1)
    w = WINDOW
    inv = inv_ref[...]

    def tables(pos_row):
        ang = inv * pos_row.astype(F32)
        return jnp.cos(ang), jnp.sin(ang)

    def rope_k(k_nat, cos, sin):
        kt = _rope_t(k_nat.astype(F32).T, cos, sin)
        return kt.T.astype(BF16)

    cos_p, sin_p = tables(posp_ref[...])
    k_prev = rope_k(kp_ref[...], cos_p, sin_p)
    v_prev = vp_ref[...]

    @pl.when((pl.program_id(0) == 0) & (n == 0))
    def _():
        key = lax.broadcasted_iota(jnp.int32, band_ref.shape, 0)
        qry = lax.broadcasted_iota(jnp.int32, band_ref.shape, 1) % w
        band_ref[...] = jnp.where((key <= qry + w) & (key > qry), 0.0, NEG_BIG)

    no_prev = jnp.where(n > 0, 0.0, NEG_BIG)
    q_scale = SWA_HEAD_DIM ** -0.5 * LOG2E
    zeros_half = jnp.zeros((SWA_HEAD_DIM, SWA_GROUP * w), F32)

    sinks = [jnp.concatenate(
        [jnp.full((1, w), sink_ref[hd] * LOG2E, F32)
         for hd in range(g * SWA_GROUP, (g + 1) * SWA_GROUP)], axis=1)
        for g in range(SWA_KV_HEADS)]

    def logits(wi, k_prev):
        tok = slice(wi * w, (wi + 1) * w)
        cos, sin = tables(posc_ref[:, tok])
        k_cur = rope_k(kc_ref[tok, :], cos, sin)
        k_all = jnp.concatenate([k_prev, k_cur], axis=0)
        qf = q_ref[tok, :].astype(F32)
        qt = jnp.concatenate(
            [qf[:, c * LANES:(c + 1) * LANES].T for c in range(SWA_WIDTH // LANES)], axis=0)
        qt = _rope_t(qt, cos * q_scale, sin * q_scale)
        out = []
        for g in range(SWA_KV_HEADS):
            heads = range(g * SWA_GROUP, (g + 1) * SWA_GROUP)
            q_g = jnp.concatenate(
                [qt[hd * SWA_HEAD_DIM:(hd + 1) * SWA_HEAD_DIM] for hd in heads], axis=1)
            parts = [zeros_half] * SWA_KV_HEADS
            parts[g] = q_g
            q_z = jnp.concatenate(parts, axis=0).astype(BF16)
            s = jnp.dot(k_all, q_z, preferred_element_type=F32) + band_ref[...]
            if wi == 0:
                s = jnp.concatenate([s[:w] + no_prev, s[w:]], axis=0)
            out.append(s)
        return out, k_cur

    def finish(wi, scores, v_prev):
        tok = slice(wi * w, (wi + 1) * w)
        v_cur = vc_ref[tok, :]
        v_all_t = jnp.concatenate([v_prev, v_cur], axis=0).astype(F32).T.astype(BF16)
        for g in range(SWA_KV_HEADS):
            s, sink = scores[g], sinks[g]
            m = jnp.maximum(jnp.max(s, axis=0, keepdims=True), sink)
            e = jnp.exp2(s - m)
            denom = jnp.sum(e, axis=0, keepdims=True) + jnp.exp2(sink - m)
            out_t = jnp.dot(v_all_t, e.astype(BF16), preferred_element_type=F32)
            out_t = out_t[g * SWA_HEAD_DIM:(g + 1) * SWA_HEAD_DIM] / denom
            for jj in range(SWA_GROUP // 2):
                pair = jnp.concatenate(
                    [out_t[:, (2 * jj) * w:(2 * jj + 1) * w],
                     out_t[:, (2 * jj + 1) * w:(2 * jj + 2) * w]], axis=0)
                col = (g * (SWA_GROUP // 2) + jj) * LANES
                o_ref[tok, col:col + LANES] = pair.T.astype(o_ref.dtype)
        return v_cur

    pending, k_prev = logits(0, k_prev)
    for wi in range(SWA_NW):
        if wi + 1 < SWA_NW:
            upcoming, k_prev = logits(wi + 1, k_prev)
        v_prev = finish(wi, pending, v_prev)
        if wi + 1 < SWA_NW:
            pending = upcoming


def _swa_attention(sinks, proj, pos_row, inv_tab, batch, seq):
    nb = seq // SWA_QB
    per = SWA_QB // WINDOW
    qcol = _PROJ_OFF["q_b"] // SWA_WIDTH
    kcol = _PROJ_OFF["k_b"] // LANES
    vcol = _PROJ_OFF["v_b"] // LANES

    def cur(b, n):
        return b * nb + n

    def prev(b, n):
        return jnp.maximum((b * nb + n) * per - 1, 0)

    return pl.pallas_call(
        _swa_kernel,
        grid=(batch, nb),
        in_specs=[
            pl.BlockSpec(memory_space=pltpu.SMEM),
            pl.BlockSpec((SWA_QB, SWA_WIDTH), lambda b, n: (cur(b, n), qcol)),
            pl.BlockSpec((SWA_QB, LANES), lambda b, n: (cur(b, n), kcol)),
            pl.BlockSpec((WINDOW, LANES), lambda b, n: (prev(b, n), kcol)),
            pl.BlockSpec((SWA_QB, LANES), lambda b, n: (cur(b, n), vcol)),
            pl.BlockSpec((WINDOW, LANES), lambda b, n: (prev(b, n), vcol)),
            pl.BlockSpec((1, SWA_QB), lambda b, n: (0, cur(b, n))),
            pl.BlockSpec((1, WINDOW), lambda b, n: (0, prev(b, n))),
            pl.BlockSpec((ROPE_HALF, LANES), lambda b, n: (0, 0)),
        ],
        out_specs=pl.BlockSpec((SWA_QB, SWA_WIDTH), lambda b, n: (cur(b, n), 0)),
        out_shape=jax.ShapeDtypeStruct((batch * seq, SWA_WIDTH), BF16),
        scratch_shapes=[pltpu.VMEM((2 * WINDOW, SWA_GROUP * WINDOW), F32)],
        compiler_params=pltpu.CompilerParams(
            dimension_semantics=("arbitrary", "arbitrary")),
        name="swa_attention",
    )(sinks, proj, proj, proj, proj, proj, pos_row, pos_row, inv_tab)


def _silu(z):
    return z * jax.nn.sigmoid(z)


def _epilogue_kernel(ya_ref, za_ref, yb_ref, zb_ref, ga_ref, gb_ref, x_ref, p_ref,
                     wof_ref, wos_ref, wout_ref, gpost_ref, wple_ref, wgate_ref, o_ref):
    ua = (ya_ref[...].astype(F32) * _silu(za_ref[...].astype(F32))).astype(BF16)
    ub = (yb_ref[...].astype(F32) * _silu(zb_ref[...].astype(F32))).astype(BF16)
    oa = jnp.dot(ua, wof_ref[...], preferred_element_type=F32)
    ob = jnp.dot(ub, wos_ref[...], preferred_element_type=F32)
    merged = (jax.nn.sigmoid(ga_ref[...].astype(F32)) * oa
              + jax.nn.sigmoid(gb_ref[...].astype(F32)) * ob)
    out = jnp.dot(merged.astype(BF16), wout_ref[...], preferred_element_type=F32)
    ms = jnp.mean(out * out, axis=-1, keepdims=True)
    x1 = x_ref[...] + (out * lax.rsqrt(ms + NORM_EPS)) * gpost_ref[...]
    e = jnp.dot(p_ref[...].astype(BF16), wple_ref[...], preferred_element_type=F32)
    gate = jax.nn.sigmoid(jnp.dot(x1.astype(BF16), wgate_ref[...],
                                  preferred_element_type=F32))
    o_ref[...] = x1 + gate * e


def _epilogue(ya, yb, proj, x2, p2, wof, wos, wout, gpost, wple, wgate):
    t = x2.shape[0]
    za_col = _PROJ_OFF["z_a"] // FOX_WIDTH
    zb_col = _PROJ_OFF["z_b"] // SWA_WIDTH
    ga_col = _PROJ_OFF["g_a"] // D_MODEL
    gb_col = _PROJ_OFF["g_b"] // D_MODEL
    once = pl.Buffered(1)

    def const(shape):
        return pl.BlockSpec(shape, lambda i: (0, 0), pipeline_mode=once)

    return pl.pallas_call(
        _epilogue_kernel,
        grid=(t // EPI_TM,),
        in_specs=[
            pl.BlockSpec((EPI_TM, FOX_WIDTH), lambda i: (i, 0)),
            pl.BlockSpec((EPI_TM, FOX_WIDTH), lambda i: (i, za_col)),
            pl.BlockSpec((EPI_TM, SWA_WIDTH), lambda i: (i, 0)),
            pl.BlockSpec((EPI_TM, SWA_WIDTH), lambda i: (i, zb_col)),
            pl.BlockSpec((EPI_TM, D_MODEL), lambda i: (i, ga_col)),
            pl.BlockSpec((EPI_TM, D_MODEL), lambda i: (i, gb_col)),
            pl.BlockSpec((EPI_TM, D_MODEL), lambda i: (i, 0)),
            pl.BlockSpec((EPI_TM, PLE_DIM), lambda i: (i, 0)),
            const((FOX_WIDTH, D_MODEL)),
            const((SWA_WIDTH, D_MODEL)),
            const((D_MODEL, D_MODEL)),
            const((1, D_MODEL)),
            const((PLE_DIM, D_MODEL)),
            const((D_MODEL, D_MODEL)),
        ],
        out_specs=pl.BlockSpec((EPI_TM, D_MODEL), lambda i: (i, 0)),
        out_shape=jax.ShapeDtypeStruct((t, D_MODEL), F32),
        compiler_params=pltpu.CompilerParams(
            dimension_semantics=("arbitrary",),
            vmem_limit_bytes=56 * 1024 * 1024),
        name="epilogue",
    )(ya, proj, yb, proj, proj, proj, x2, p2, wof, wos, wout, gpost, wple, wgate)


def _layer(x2, p2, pos_row, batch, seq, pre_g, w_in, b_forget, sinks, w_o_fox, w_o_swa,
           w_out, post_g, w_ple, w_ple_gate):
    w_t = w_in.T
    f0 = _REF_OFF["f_a"]
    wf_t = jnp.pad(w_t[f0:f0 + FOX_HEADS], ((0, F_ROWS - FOX_HEADS), (0, 0))).astype(BF16)
    col_scale = jnp.where(jnp.arange(PROJ_COLS) < FOX_WIDTH, Q_A_SCALE, 1.0
                          ).astype(F32).reshape(1, PROJ_COLS)

    h, ft = _prenorm(x2, pre_g.reshape(1, D_MODEL), wf_t)
    proj = _inproj(h, w_t, col_scale)
    kb_tok = _forget_cumsum(ft, b_forget.reshape(FOX_HEADS, 1).astype(F32), batch, seq)
    ya = _fox_attention(proj, kb_tok, batch, seq)

    inv = ROPE_THETA ** (-jnp.arange(ROPE_HALF, dtype=F32) / ROPE_HALF)
    inv_tab = jnp.broadcast_to(inv[:, None], (ROPE_HALF, LANES))
    yb = _swa_attention(sinks.astype(F32), proj, pos_row, inv_tab, batch, seq)

    return _epilogue(ya, yb, proj, x2, p2,
                     w_o_fox.astype(BF16), w_o_swa.astype(BF16), w_out.astype(BF16),
                     post_g.reshape(1, D_MODEL), w_ple.astype(BF16), w_ple_gate.astype(BF16))


def kernel(x, p, positions, pre_norm_g, w_in, b_forget, sinks, w_o_fox, w_o_swa, w_out,
           post_norm_g, w_ple, w_ple_gate):
    batch, seq, _ = x.shape
    depth = p.shape[0]
    x2 = x.reshape(batch * seq, D_MODEL)
    pos_row = positions.reshape(1, batch * seq)
    for i in range(depth):
        x2 = _layer(x2, p[i].reshape(batch * seq, PLE_DIM), pos_row, batch, seq,
                    pre_norm_g[i], w_in[i], b_forget[i], sinks[i], w_o_fox[i], w_o_swa[i],
                    w_out[i], post_norm_g[i], w_ple[i], w_ple_gate[i])
    return x2.reshape(batch, seq, D_MODEL)
```

```python
import math

import jax
import jax.numpy as jnp
from jax import lax
from jax.experimental import pallas as pl
from jax.experimental.pallas import tpu as pltpu

F32 = jnp.float32
BF16 = jnp.bfloat16

D_MODEL = 2048
FOX_HEADS = 8
FOX_HEAD_DIM = 128
FOX_WIDTH = FOX_HEADS * FOX_HEAD_DIM
SWA_Q_HEADS = 16
SWA_KV_HEADS = 2
SWA_HEAD_DIM = 64
SWA_WIDTH = SWA_Q_HEADS * SWA_HEAD_DIM
SWA_KV_WIDTH = SWA_KV_HEADS * SWA_HEAD_DIM
SWA_GROUP = SWA_Q_HEADS // SWA_KV_HEADS
ROPE_HALF = SWA_HEAD_DIM // 2
WINDOW = 128
ROPE_THETA = 10000.0
PLE_DIM = 256
NORM_EPS = 1e-6
LANES = 128
LOG2E = math.log2(math.e)
NEG_BIG = -1e30

_REF_SPLITS = (FOX_WIDTH, FOX_WIDTH, FOX_WIDTH, FOX_WIDTH, FOX_HEADS, SWA_WIDTH,
               SWA_KV_WIDTH, SWA_KV_WIDTH, SWA_WIDTH, D_MODEL, D_MODEL)
_REF_NAMES = ("q_a", "k_a", "v_a", "z_a", "f_a", "q_b", "k_b", "v_b", "z_b", "g_a", "g_b")
_REF_OFF = {}
_o = 0
for _n, _s in zip(_REF_NAMES, _REF_SPLITS):
    _REF_OFF[_n] = _o
    _o += _s

IN_TM = 1024
IN_TN = 1536
NORM_ROWS = 512
F_ROWS = 16

PREP_TN = 512
_W_GROUPS = (
    (_REF_OFF["q_a"], 4 * FOX_WIDTH),
    (_REF_OFF["q_b"], SWA_WIDTH),
    (_REF_OFF["z_b"], SWA_WIDTH),
    (_REF_OFF["g_a"], 2 * D_MODEL),
    (_REF_OFF["k_b"], 2 * SWA_KV_WIDTH),
)
_GROUP_TILES = tuple(-(-w // PREP_TN) for _, w in _W_GROUPS)
_GROUP_START = tuple(sum(_GROUP_TILES[:i]) for i in range(len(_W_GROUPS)))
PREP_TILES = sum(_GROUP_TILES)
PROJ_COLS = PREP_TILES * PREP_TN
_PROJ_OFF = {
    "q_a": 0, "k_a": FOX_WIDTH, "v_a": 2 * FOX_WIDTH, "z_a": 3 * FOX_WIDTH,
    "q_b": _GROUP_START[1] * PREP_TN,
    "z_b": _GROUP_START[2] * PREP_TN,
    "g_a": _GROUP_START[3] * PREP_TN, "g_b": _GROUP_START[3] * PREP_TN + D_MODEL,
    "k_b": _GROUP_START[4] * PREP_TN, "v_b": _GROUP_START[4] * PREP_TN + SWA_KV_WIDTH,
}
SUBLANES = 8
assert all(c0 % SUBLANES == 0 for c0, _ in _W_GROUPS)
W_PER_TILE = IN_TN // PREP_TN
_NT = (((1,), (1,)), ((), ()))
Q_A_SCALE = FOX_HEAD_DIM ** -0.5 * LOG2E

FOX_T = 512
FOX_CHUNK = 512
FOX_PAIR = 4
BIAS_TERMS = 3

SWA_NW = 4
SWA_QB = SWA_NW * WINDOW

EPI_TM = 256


def _prenorm_kernel(x_ref, g_ref, wf_ref, h_ref, ft_ref):
    x = x_ref[...]
    ms = jnp.mean(x * x, axis=-1, keepdims=True)
    h = ((x * lax.rsqrt(ms + NORM_EPS)) * g_ref[...]).astype(BF16)
    h_ref[...] = h
    ft_ref[...] = lax.dot_general(wf_ref[...], h, _NT, preferred_element_type=F32)


def _prenorm(x2, g, wf_t):
    t = x2.shape[0]
    return pl.pallas_call(
        _prenorm_kernel,
        grid=(t // NORM_ROWS,),
        in_specs=[
            pl.BlockSpec((NORM_ROWS, D_MODEL), lambda i: (i, 0)),
            pl.BlockSpec((1, D_MODEL), lambda i: (0, 0)),
            pl.BlockSpec((F_ROWS, D_MODEL), lambda i: (0, 0)),
        ],
        out_specs=[
            pl.BlockSpec((NORM_ROWS, D_MODEL), lambda i: (i, 0)),
            pl.BlockSpec((F_ROWS, NORM_ROWS), lambda i: (0, i)),
        ],
        out_shape=[
            jax.ShapeDtypeStruct((t, D_MODEL), BF16),
            jax.ShapeDtypeStruct((F_ROWS, t), F32),
        ],
        compiler_params=pltpu.CompilerParams(dimension_semantics=("arbitrary",)),
        name="prenorm",
    )(x2, g, wf_t)


def _w_src_row(tile):
    tile_row = jnp.int32(0)
    for (c0, _), start in zip(_W_GROUPS, _GROUP_START):
        tile_row = jnp.where(tile >= start,
                             c0 // SUBLANES + (tile - start) * (PREP_TN // SUBLANES), tile_row)
    return tile_row * SUBLANES


def _inproj_kernel(h_ref, *refs):
    w_refs = refs[:W_PER_TILE]
    scale_ref, proj_ref, wb_ref = refs[W_PER_TILE:]

    @pl.when(pl.program_id(1) == 0)
    def _():
        for k, w_ref in enumerate(w_refs):
            for r in range(PREP_TN // LANES):
                rows = slice(r * LANES, (r + 1) * LANES)
                wb_ref[k * PREP_TN + r * LANES:k * PREP_TN + (r + 1) * LANES, :] = (
                    w_ref[rows, :].astype(BF16))

    acc = lax.dot_general(h_ref[...], wb_ref[...], _NT, preferred_element_type=F32)
    proj_ref[...] = (acc * scale_ref[...]).astype(BF16)


def _inproj(h, w_t, col_scale):
    t = h.shape[0]

    def w_spec(k):
        return pl.BlockSpec((pl.Element(PREP_TN), pl.Element(D_MODEL)),
                            lambda j, i: (_w_src_row(j * W_PER_TILE + k), 0))

    return pl.pallas_call(
        _inproj_kernel,
        grid=(PROJ_COLS // IN_TN, t // IN_TM),
        in_specs=[
            pl.BlockSpec((IN_TM, D_MODEL), lambda j, i: (i, 0)),
            *[w_spec(k) for k in range(W_PER_TILE)],
            pl.BlockSpec((1, IN_TN), lambda j, i: (0, j)),
        ],
        out_specs=pl.BlockSpec((IN_TM, IN_TN), lambda j, i: (i, j)),
        out_shape=jax.ShapeDtypeStruct((t, PROJ_COLS), BF16),
        scratch_shapes=[pltpu.VMEM((IN_TN, D_MODEL), BF16)],
        compiler_params=pltpu.CompilerParams(
            dimension_semantics=("arbitrary", "arbitrary"),
            vmem_limit_bytes=56 * 1024 * 1024),
        name="inproj",
    )(h, *([w_t] * W_PER_TILE), col_scale)


def _cumsum_kernel(ft_ref, b_ref, kb_ref):
    f = ft_ref[0:FOX_HEADS, :] + b_ref[...]
    lf = jnp.minimum(f, 0.0) - jnp.log1p(jnp.exp(-jnp.abs(f)))
    seq = lf.shape[1]
    lane = lax.broadcasted_iota(jnp.int32, lf.shape, 1)
    c = lf
    shift = 1
    while shift < seq:
        c = c + jnp.where(lane >= shift, pltpu.roll(c, shift, axis=1), 0.0)
        shift *= 2
    rest = c * (-LOG2E)
    pieces = []
    for _ in range(BIAS_TERMS):
        piece = rest.astype(BF16).astype(F32)
        pieces.append(piece)
        rest = rest - piece
    pieces.append(jnp.zeros((LANES - BIAS_TERMS * FOX_HEADS, seq), F32))
    kb_ref[...] = jnp.concatenate(pieces, axis=0).T.astype(BF16)


def _forget_cumsum(ft, b_col, batch, seq):
    return pl.pallas_call(
        _cumsum_kernel,
        grid=(batch,),
        in_specs=[
            pl.BlockSpec((F_ROWS, seq), lambda b: (0, b)),
            pl.BlockSpec((FOX_HEADS, 1), lambda b: (0, 0)),
        ],
        out_specs=pl.BlockSpec((seq, LANES), lambda b: (b, 0)),
        out_shape=jax.ShapeDtypeStruct((batch * seq, LANES), BF16),
        name="forget_cumsum",
    )(ft, b_col)


def _fox_kernel(q_ref, k_ref, v_ref, kb_ref, o_ref, vt_ref, qt_ref, acc_ref, s0_ref, s1_ref):
    hp = pl.program_id(1)
    seq = k_ref.shape[0]
    t = FOX_T
    d = FOX_HEAD_DIM

    row = lax.broadcasted_iota(jnp.int32, (d, t), 0)
    for hh in range(FOX_PAIR):
        mine = (row < BIAS_TERMS * FOX_HEADS) & (row % FOX_HEADS == hp * FOX_PAIR + hh)
        qt_ref[hh, d:2 * d, :] = jnp.where(mine, 1.0, 0.0).astype(BF16)
        for r in range(seq // FOX_CHUNK):
            rows = slice(r * FOX_CHUNK, (r + 1) * FOX_CHUNK)
            vt_ref[hh, :, rows] = v_ref[rows, hh * d:(hh + 1) * d].astype(F32).T.astype(BF16)

    def scores(i, s_ref):
        rows = pl.ds(pl.multiple_of(i * t, t), t)
        for hh in range(FOX_PAIR):
            k_aug = jnp.concatenate([k_ref[rows, hh * d:(hh + 1) * d], kb_ref[rows, :]], axis=1)
            s_ref[hh] = jnp.dot(k_aug, qt_ref[hh], preferred_element_type=F32)

    def absorb(i, s_ref, carry, masked):
        rows = pl.ds(pl.multiple_of(i * t, t), t)
        new = []
        for hh in range(FOX_PAIR):
            m, l = carry[hh]
            s = s_ref[hh]
            if masked:
                key = lax.broadcasted_iota(jnp.int32, s.shape, 0)
                qry = lax.broadcasted_iota(jnp.int32, s.shape, 1)
                s = jnp.where(key <= qry, s, NEG_BIG)
            m_new = jnp.maximum(m, jnp.max(s, axis=0, keepdims=True))
            alpha = jnp.exp2(m - m_new)
            p = jnp.exp2(s - m_new)
            l_new = alpha * l + jnp.sum(p, axis=0, keepdims=True)
            acc_ref[hh] = alpha * acc_ref[hh] + jnp.dot(
                vt_ref[hh, :, rows], p.astype(BF16), preferred_element_type=F32)
            new.append((m_new, l_new))
        return tuple(new)

    def step(i, s_cur, s_next, carry):
        scores(i + 1, s_next)
        return absorb(i, s_cur, carry, False)

    def pair(j, carry):
        carry = step(2 * j, s0_ref, s1_ref, carry)
        return step(2 * j + 1, s1_ref, s0_ref, carry)

    def q_tile(qi, _):
        q_rows = pl.ds(pl.multiple_of(qi * t, t), t)
        for hh in range(FOX_PAIR):
            qt_ref[hh, 0:d, :] = q_ref[q_rows, hh * d:(hh + 1) * d].astype(F32).T.astype(BF16)
            acc_ref[hh] = jnp.zeros((d, t), F32)

        def odd_tail(carry):
            carry = step(qi - 1, s0_ref, s1_ref, carry)
            return absorb(qi, s1_ref, carry, True)

        def even_tail(carry):
            return absorb(qi, s0_ref, carry, True)

        init = tuple((jnp.full((1, t), NEG_BIG, F32), jnp.zeros((1, t), F32))
                     for _ in range(FOX_PAIR))
        scores(0, s0_ref)
        carry = lax.fori_loop(0, qi // 2, pair, init)
        carry = lax.cond(qi % 2 == 1, odd_tail, even_tail, carry)
        for hh in range(FOX_PAIR):
            o_ref[q_rows, hh * d:(hh + 1) * d] = (
                acc_ref[hh] / carry[hh][1]).T.astype(o_ref.dtype)
        return 0

    lax.fori_loop(0, seq // t, q_tile, 0)


def _fox_attention(proj, kb_tok, batch, seq):
    width = FOX_PAIR * FOX_HEAD_DIM
    qcol = _PROJ_OFF["q_a"] // width
    kcol = _PROJ_OFF["k_a"] // width
    vcol = _PROJ_OFF["v_a"] // width
    return pl.pallas_call(
        _fox_kernel,
        grid=(batch, FOX_HEADS // FOX_PAIR),
        in_specs=[
            pl.BlockSpec((seq, width), lambda b, h: (b, qcol + h)),
            pl.BlockSpec((seq, width), lambda b, h: (b, kcol + h)),
            pl.BlockSpec((seq, width), lambda b, h: (b, vcol + h)),
            pl.BlockSpec((seq, LANES), lambda b, h: (b, 0)),
        ],
        out_specs=pl.BlockSpec((seq, width), lambda b, h: (b, h)),
        out_shape=jax.ShapeDtypeStruct((batch * seq, FOX_WIDTH), BF16),
        scratch_shapes=[
            pltpu.VMEM((FOX_PAIR, FOX_HEAD_DIM, seq), BF16),
            pltpu.VMEM((FOX_PAIR, 2 * FOX_HEAD_DIM, FOX_T), BF16),
            pltpu.VMEM((FOX_PAIR, FOX_HEAD_DIM, FOX_T), F32),
            pltpu.VMEM((FOX_PAIR, FOX_T, FOX_T), F32),
            pltpu.VMEM((FOX_PAIR, FOX_T, FOX_T), F32),
        ],
        compiler_params=pltpu.CompilerParams(
            dimension_semantics=("arbitrary", "arbitrary"),
            vmem_limit_bytes=56 * 1024 * 1024),
        name="fox_attention",
    )(proj, proj, proj, kb_tok)


def _rope_t(xt, cos, sin):
    out = []
    for hd in range(xt.shape[0] // SWA_HEAD_DIM):
        x1 = xt[hd * SWA_HEAD_DIM: hd * SWA_HEAD_DIM + ROPE_HALF]
        x2 = xt[hd * SWA_HEAD_DIM + ROPE_HALF: (hd + 1) * SWA_HEAD_DIM]
        out.append(x1 * cos - x2 * sin)
        out.append(x2 * cos + x1 * sin)
    return jnp.concatenate(out, axis=0)


def _swa_kernel(sink_ref, q_ref, kc_ref, kp_ref, vc_ref, vp_ref, posc_ref, posp_ref,
                inv_ref, *rest):
    n_cast = (len(rest) - 2) // 2
    cast_in, o_ref = rest[:n_cast], rest[n_cast]
    cast_out, band_ref = rest[n_cast + 1:2 * n_cast + 1], rest[-1]
    for src_ref, dst_ref in zip(cast_in, cast_out):
        dst_ref[...] = src_ref[...].astype(dst_ref.dtype)

    n = pl.program_id(1)
    w = WINDOW
    inv = inv_ref[...]

    def tables(pos_row):
        ang = inv * pos_row.astype(F32)
        return jnp.cos(ang), jnp.sin(ang)

    def rope_k(k_nat, cos, sin):
        kt = _rope_t(k_nat.astype(F32).T, cos, sin)
        return kt.T.astype(BF16)

    cos_p, sin_p = tables(posp_ref[...])
    k_prev = rope_k(kp_ref[...], cos_p, sin_p)
    v_prev = vp_ref[...]

    @pl.when((pl.program_id(0) == 0) & (n == 0))
    def _():
        key = lax.broadcasted_iota(jnp.int32, band_ref.shape, 0)
        qry = lax.broadcasted_iota(jnp.int32, band_ref.shape, 1) % w
        band_ref[...] = jnp.where((key <= qry + w) & (key > qry), 0.0, NEG_BIG)

    no_prev = jnp.where(n > 0, 0.0, NEG_BIG)
    q_scale = SWA_HEAD_DIM ** -0.5 * LOG2E
    zeros_half = jnp.zeros((SWA_HEAD_DIM, SWA_GROUP * w), F32)

    sinks = [jnp.concatenate(
        [jnp.full((1, w), sink_ref[hd] * LOG2E, F32)
         for hd in range(g * SWA_GROUP, (g + 1) * SWA_GROUP)], axis=1)
        for g in range(SWA_KV_HEADS)]

    def logits(wi, k_prev):
        tok = slice(wi * w, (wi + 1) * w)
        cos, sin = tables(posc_ref[:, tok])
        k_cur = rope_k(kc_ref[tok, :], cos, sin)
        k_all = jnp.concatenate([k_prev, k_cur], axis=0)
        qf = q_ref[tok, :].astype(F32)
        qt = jnp.concatenate(
            [qf[:, c * LANES:(c + 1) * LANES].T for c in range(SWA_WIDTH // LANES)], axis=0)
        qt = _rope_t(qt, cos * q_scale, sin * q_scale)
        out = []
        for g in range(SWA_KV_HEADS):
            heads = range(g * SWA_GROUP, (g + 1) * SWA_GROUP)
            q_g = jnp.concatenate(
                [qt[hd * SWA_HEAD_DIM:(hd + 1) * SWA_HEAD_DIM] for hd in heads], axis=1)
            parts = [zeros_half] * SWA_KV_HEADS
            parts[g] = q_g
            q_z = jnp.concatenate(parts, axis=0).astype(BF16)
            s = jnp.dot(k_all, q_z, preferred_element_type=F32) + band_ref[...]
            if wi == 0:
                s = jnp.concatenate([s[:w] + no_prev, s[w:]], axis=0)
            out.append(s)
        return out, k_cur

    def finish(wi, scores, v_prev):
        tok = slice(wi * w, (wi + 1) * w)
        v_cur = vc_ref[tok, :]
        v_all_t = jnp.concatenate([v_prev, v_cur], axis=0).astype(F32).T.astype(BF16)
        for g in range(SWA_KV_HEADS):
            s, sink = scores[g], sinks[g]
            m = jnp.maximum(jnp.max(s, axis=0, keepdims=True), sink)
            e = jnp.exp2(s - m)
            denom = jnp.sum(e, axis=0, keepdims=True) + jnp.exp2(sink - m)
            out_t = jnp.dot(v_all_t, e.astype(BF16), preferred_element_type=F32)
            out_t = out_t[g * SWA_HEAD_DIM:(g + 1) * SWA_HEAD_DIM] / denom
            for jj in range(SWA_GROUP // 2):
                pair = jnp.concatenate(
                    [out_t[:, (2 * jj) * w:(2 * jj + 1) * w],
                     out_t[:, (2 * jj + 1) * w:(2 * jj + 2) * w]], axis=0)
                col = (g * (SWA_GROUP // 2) + jj) * LANES
                o_ref[tok, col:col + LANES] = pair.T.astype(o_ref.dtype)
        return v_cur

    pending, k_prev = logits(0, k_prev)
    for wi in range(SWA_NW):
        if wi + 1 < SWA_NW:
            upcoming, k_prev = logits(wi + 1, k_prev)
        v_prev = finish(wi, pending, v_prev)
        if wi + 1 < SWA_NW:
            pending = upcoming


def _swa_attention(sinks, proj, pos_row, inv_tab, batch, seq, weights_f32):
    nb = seq // SWA_QB
    per = SWA_QB // WINDOW
    qcol = _PROJ_OFF["q_b"] // SWA_WIDTH
    kcol = _PROJ_OFF["k_b"] // LANES
    vcol = _PROJ_OFF["v_b"] // LANES
    steps = batch * nb

    def cur(b, n):
        return b * nb + n

    def prev(b, n):
        return jnp.maximum((b * nb + n) * per - 1, 0)

    def chunk_spec(wgt):
        rows, cols = wgt.shape
        assert rows % (steps * 2 * SUBLANES) == 0
        return pl.BlockSpec((rows // steps, cols), lambda b, n: (cur(b, n), 0))

    cast_specs = [chunk_spec(wgt) for wgt in weights_f32]
    return pl.pallas_call(
        _swa_kernel,
        grid=(batch, nb),
        in_specs=[
            pl.BlockSpec(memory_space=pltpu.SMEM),
            pl.BlockSpec((SWA_QB, SWA_WIDTH), lambda b, n: (cur(b, n), qcol)),
            pl.BlockSpec((SWA_QB, LANES), lambda b, n: (cur(b, n), kcol)),
            pl.BlockSpec((WINDOW, LANES), lambda b, n: (prev(b, n), kcol)),
            pl.BlockSpec((SWA_QB, LANES), lambda b, n: (cur(b, n), vcol)),
            pl.BlockSpec((WINDOW, LANES), lambda b, n: (prev(b, n), vcol)),
            pl.BlockSpec((1, SWA_QB), lambda b, n: (0, cur(b, n))),
            pl.BlockSpec((1, WINDOW), lambda b, n: (0, prev(b, n))),
            pl.BlockSpec((ROPE_HALF, LANES), lambda b, n: (0, 0)),
            *cast_specs,
        ],
        out_specs=[pl.BlockSpec((SWA_QB, SWA_WIDTH), lambda b, n: (cur(b, n), 0)), *cast_specs],
        out_shape=[jax.ShapeDtypeStruct((batch * seq, SWA_WIDTH), BF16),
                   *[jax.ShapeDtypeStruct(wgt.shape, BF16) for wgt in weights_f32]],
        scratch_shapes=[pltpu.VMEM((2 * WINDOW, SWA_GROUP * WINDOW), F32)],
        compiler_params=pltpu.CompilerParams(
            dimension_semantics=("arbitrary", "arbitrary")),
        name="swa_attention",
    )(sinks, proj, proj, proj, proj, proj, pos_row, pos_row, inv_tab, *weights_f32)


def _silu(z):
    return z * jax.nn.sigmoid(z)


def _epilogue_kernel(ya_ref, za_ref, yb_ref, zb_ref, ga_ref, gb_ref, x_ref, p_ref,
                     wof_ref, wos_ref, wout_ref, gpost_ref, wple_ref, wgate_ref, o_ref):
    ua = (ya_ref[...].astype(F32) * _silu(za_ref[...].astype(F32))).astype(BF16)
    ub = (yb_ref[...].astype(F32) * _silu(zb_ref[...].astype(F32))).astype(BF16)
    oa = jnp.dot(ua, wof_ref[...], preferred_element_type=F32)
    ob = jnp.dot(ub, wos_ref[...], preferred_element_type=F32)
    merged = (jax.nn.sigmoid(ga_ref[...].astype(F32)) * oa
              + jax.nn.sigmoid(gb_ref[...].astype(F32)) * ob)
    out = jnp.dot(merged.astype(BF16), wout_ref[...], preferred_element_type=F32)
    ms = jnp.mean(out * out, axis=-1, keepdims=True)
    x1 = x_ref[...] + (out * lax.rsqrt(ms + NORM_EPS)) * gpost_ref[...]
    e = jnp.dot(p_ref[...].astype(BF16), wple_ref[...], preferred_element_type=F32)
    gate = jax.nn.sigmoid(jnp.dot(x1.astype(BF16), wgate_ref[...],
                                  preferred_element_type=F32))
    o_ref[...] = x1 + gate * e


def _epilogue(ya, yb, proj, x2, p2, wof, wos, wout, gpost, wple, wgate):
    t = x2.shape[0]
    za_col = _PROJ_OFF["z_a"] // FOX_WIDTH
    zb_col = _PROJ_OFF["z_b"] // SWA_WIDTH
    ga_col = _PROJ_OFF["g_a"] // D_MODEL
    gb_col = _PROJ_OFF["g_b"] // D_MODEL
    once = pl.Buffered(1)

    def const(shape):
        return pl.BlockSpec(shape, lambda i: (0, 0), pipeline_mode=once)

    return pl.pallas_call(
        _epilogue_kernel,
        grid=(t // EPI_TM,),
        in_specs=[
            pl.BlockSpec((EPI_TM, FOX_WIDTH), lambda i: (i, 0)),
            pl.BlockSpec((EPI_TM, FOX_WIDTH), lambda i: (i, za_col)),
            pl.BlockSpec((EPI_TM, SWA_WIDTH), lambda i: (i, 0)),
            pl.BlockSpec((EPI_TM, SWA_WIDTH), lambda i: (i, zb_col)),
            pl.BlockSpec((EPI_TM, D_MODEL), lambda i: (i, ga_col)),
            pl.BlockSpec((EPI_TM, D_MODEL), lambda i: (i, gb_col)),
            pl.BlockSpec((EPI_TM, D_MODEL), lambda i: (i, 0)),
            pl.BlockSpec((EPI_TM, PLE_DIM), lambda i: (i, 0)),
            const((FOX_WIDTH, D_MODEL)),
            const((SWA_WIDTH, D_MODEL)),
            const((D_MODEL, D_MODEL)),
            const((1, D_MODEL)),
            const((PLE_DIM, D_MODEL)),
            const((D_MODEL, D_MODEL)),
        ],
        out_specs=pl.BlockSpec((EPI_TM, D_MODEL), lambda i: (i, 0)),
        out_shape=jax.ShapeDtypeStruct((t, D_MODEL), F32),
        compiler_params=pltpu.CompilerParams(
            dimension_semantics=("arbitrary",),
            vmem_limit_bytes=56 * 1024 * 1024),
        name="epilogue",
    )(ya, proj, yb, proj, proj, proj, x2, p2, wof, wos, wout, gpost, wple, wgate)


def _layer(x2, p2, pos_row, batch, seq, pre_g, w_in, b_forget, sinks, w_o_fox, w_o_swa,
           w_out, post_g, w_ple, w_ple_gate):
    w_t = w_in.T
    f0 = _REF_OFF["f_a"]
    wf_t = jnp.pad(w_t[f0:f0 + FOX_HEADS], ((0, F_ROWS - FOX_HEADS), (0, 0))).astype(BF16)
    col_scale = jnp.where(jnp.arange(PROJ_COLS) < FOX_WIDTH, Q_A_SCALE, 1.0
                          ).astype(F32).reshape(1, PROJ_COLS)

    h, ft = _prenorm(x2, pre_g.reshape(1, D_MODEL), wf_t)
    proj = _inproj(h, w_t, col_scale)
    kb_tok = _forget_cumsum(ft, b_forget.reshape(FOX_HEADS, 1).astype(F32), batch, seq)
    ya = _fox_attention(proj, kb_tok, batch, seq)

    inv = ROPE_THETA ** (-jnp.arange(ROPE_HALF, dtype=F32) / ROPE_HALF)
    inv_tab = jnp.broadcast_to(inv[:, None], (ROPE_HALF, LANES))
    yb, wof, wos, wout, wple, wgate = _swa_attention(
        sinks.astype(F32), proj, pos_row, inv_tab, batch, seq,
        (w_o_fox, w_o_swa, w_out, w_ple, w_ple_gate))

    return _epilogue(ya, yb, proj, x2, p2, wof, wos, wout,
                     post_g.reshape(1, D_MODEL), wple, wgate)


def kernel(x, p, positions, pre_norm_g, w_in, b_forget, sinks, w_o_fox, w_o_swa, w_out,
           post_norm_g, w_ple, w_ple_gate):
    batch, seq, _ = x.shape
    depth = p.shape[0]
    x2 = x.reshape(batch * seq, D_MODEL)
    pos_row = positions.reshape(1, batch * seq)
    for i in range(depth):
        x2 = _layer(x2, p[i].reshape(batch * seq, PLE_DIM), pos_row, batch, seq,
                    pre_norm_g[i], w_in[i], b_forget[i], sinks[i], w_o_fox[i], w_o_swa[i],
                    w_out[i], post_norm_g[i], w_ple[i], w_ple_gate[i])
    return x2.reshape(batch, seq, D_MODEL)
```

```python
import math

import jax
import jax.numpy as jnp
from jax import lax
from jax.experimental import pallas as pl
from jax.experimental.pallas import tpu as pltpu

F32 = jnp.float32
BF16 = jnp.bfloat16

D_MODEL = 2048
FOX_HEADS = 8
FOX_HEAD_DIM = 128
FOX_WIDTH = FOX_HEADS * FOX_HEAD_DIM
SWA_Q_HEADS = 16
SWA_KV_HEADS = 2
SWA_HEAD_DIM = 64
SWA_WIDTH = SWA_Q_HEADS * SWA_HEAD_DIM
SWA_KV_WIDTH = SWA_KV_HEADS * SWA_HEAD_DIM
SWA_GROUP = SWA_Q_HEADS // SWA_KV_HEADS
ROPE_HALF = SWA_HEAD_DIM // 2
WINDOW = 128
ROPE_THETA = 10000.0
PLE_DIM = 256
NORM_EPS = 1e-6
LANES = 128
LOG2E = math.log2(math.e)
NEG_BIG = -1e30

_REF_SPLITS = (FOX_WIDTH, FOX_WIDTH, FOX_WIDTH, FOX_WIDTH, FOX_HEADS, SWA_WIDTH,
               SWA_KV_WIDTH, SWA_KV_WIDTH, SWA_WIDTH, D_MODEL, D_MODEL)
_REF_NAMES = ("q_a", "k_a", "v_a", "z_a", "f_a", "q_b", "k_b", "v_b", "z_b", "g_a", "g_b")
_REF_OFF = {}
_o = 0
for _n, _s in zip(_REF_NAMES, _REF_SPLITS):
    _REF_OFF[_n] = _o
    _o += _s

IN_TM = 1024
IN_TN = 1536
NORM_ROWS = 512
F_ROWS = 16

PREP_TN = 512
_W_GROUPS = (
    (_REF_OFF["q_a"], 4 * FOX_WIDTH),
    (_REF_OFF["q_b"], SWA_WIDTH),
    (_REF_OFF["z_b"], SWA_WIDTH),
    (_REF_OFF["g_a"], 2 * D_MODEL),
    (_REF_OFF["k_b"], 2 * SWA_KV_WIDTH),
)
_GROUP_TILES = tuple(-(-w // PREP_TN) for _, w in _W_GROUPS)
_GROUP_START = tuple(sum(_GROUP_TILES[:i]) for i in range(len(_W_GROUPS)))
PREP_TILES = sum(_GROUP_TILES)
PROJ_COLS = PREP_TILES * PREP_TN
_PROJ_OFF = {
    "q_a": 0, "k_a": FOX_WIDTH, "v_a": 2 * FOX_WIDTH, "z_a": 3 * FOX_WIDTH,
    "q_b": _GROUP_START[1] * PREP_TN,
    "z_b": _GROUP_START[2] * PREP_TN,
    "g_a": _GROUP_START[3] * PREP_TN, "g_b": _GROUP_START[3] * PREP_TN + D_MODEL,
    "k_b": _GROUP_START[4] * PREP_TN, "v_b": _GROUP_START[4] * PREP_TN + SWA_KV_WIDTH,
}
SUBLANES = 8
assert all(c0 % SUBLANES == 0 for c0, _ in _W_GROUPS)
W_PER_TILE = IN_TN // PREP_TN
LAST_TILE = PROJ_COLS // IN_TN - 1
_NT = (((1,), (1,)), ((), ()))
Q_A_SCALE = FOX_HEAD_DIM ** -0.5 * LOG2E

FOX_T = 512
FOX_CHUNK = 512
FOX_PAIR = 4
BIAS_TERMS = 3

SWA_NW = 8
SWA_QB = SWA_NW * WINDOW

EPI_TM = 256


def _cast_windows(w_refs, wb_ref):
    for k, w_ref in enumerate(w_refs):
        for r in range(PREP_TN // LANES):
            rows = slice(r * LANES, (r + 1) * LANES)
            wb_ref[k * PREP_TN + r * LANES:k * PREP_TN + (r + 1) * LANES, :] = (
                w_ref[rows, :].astype(BF16))


def _prenorm_kernel(x_ref, g_ref, wf_ref, *refs):
    w_refs = refs[:W_PER_TILE]
    h_ref, ft_ref, proj_ref, wb_ref = refs[W_PER_TILE:]

    @pl.when(pl.program_id(0) == 0)
    def _():
        _cast_windows(w_refs, wb_ref)

    x = x_ref[...]
    ms = jnp.mean(x * x, axis=-1, keepdims=True)
    h = ((x * lax.rsqrt(ms + NORM_EPS)) * g_ref[...]).astype(BF16)
    h_ref[...] = h
    ft_ref[...] = lax.dot_general(wf_ref[...], h, _NT, preferred_element_type=F32)
    proj_ref[...] = lax.dot_general(h, wb_ref[...], _NT,
                                    preferred_element_type=F32).astype(BF16)


def _prenorm(x2, g, wf_t, w_t):
    t = x2.shape[0]
    once = pl.Buffered(1)

    def w_spec(k):
        return pl.BlockSpec((pl.Element(PREP_TN), pl.Element(D_MODEL)),
                            lambda i: (_w_src_row(LAST_TILE * W_PER_TILE + k), 0),
                            pipeline_mode=once)

    return pl.pallas_call(
        _prenorm_kernel,
        grid=(t // NORM_ROWS,),
        in_specs=[
            pl.BlockSpec((NORM_ROWS, D_MODEL), lambda i: (i, 0)),
            pl.BlockSpec((1, D_MODEL), lambda i: (0, 0)),
            pl.BlockSpec((F_ROWS, D_MODEL), lambda i: (0, 0)),
            *[w_spec(k) for k in range(W_PER_TILE)],
        ],
        out_specs=[
            pl.BlockSpec((NORM_ROWS, D_MODEL), lambda i: (i, 0)),
            pl.BlockSpec((F_ROWS, NORM_ROWS), lambda i: (0, i)),
            pl.BlockSpec((NORM_ROWS, IN_TN), lambda i: (i, LAST_TILE)),
        ],
        out_shape=[
            jax.ShapeDtypeStruct((t, D_MODEL), BF16),
            jax.ShapeDtypeStruct((F_ROWS, t), F32),
            jax.ShapeDtypeStruct((t, PROJ_COLS), BF16),
        ],
        scratch_shapes=[pltpu.VMEM((IN_TN, D_MODEL), BF16)],
        compiler_params=pltpu.CompilerParams(
            dimension_semantics=("arbitrary",),
            vmem_limit_bytes=56 * 1024 * 1024),
        name="prenorm",
    )(x2, g, wf_t, *([w_t] * W_PER_TILE))


def _w_src_row(tile):
    tile_row = jnp.int32(0)
    for (c0, _), start in zip(_W_GROUPS, _GROUP_START):
        tile_row = jnp.where(tile >= start,
                             c0 // SUBLANES + (tile - start) * (PREP_TN // SUBLANES), tile_row)
    return tile_row * SUBLANES


def _inproj_kernel(h_ref, *refs):
    w_refs = refs[:W_PER_TILE]
    scale_ref, _, proj_ref, wb_ref = refs[W_PER_TILE:]

    @pl.when(pl.program_id(1) == 0)
    def _():
        _cast_windows(w_refs, wb_ref)

    acc = lax.dot_general(h_ref[...], wb_ref[...], _NT, preferred_element_type=F32)
    proj_ref[...] = (acc * scale_ref[...]).astype(BF16)


def _inproj(h, w_t, col_scale, proj_partial):
    t = h.shape[0]

    def w_spec(k):
        return pl.BlockSpec((pl.Element(PREP_TN), pl.Element(D_MODEL)),
                            lambda j, i: (_w_src_row(j * W_PER_TILE + k), 0))

    n_in = 1 + W_PER_TILE + 1
    return pl.pallas_call(
        _inproj_kernel,
        grid=(LAST_TILE, t // IN_TM),
        in_specs=[
            pl.BlockSpec((IN_TM, D_MODEL), lambda j, i: (i, 0)),
            *[w_spec(k) for k in range(W_PER_TILE)],
            pl.BlockSpec((1, IN_TN), lambda j, i: (0, j)),
            pl.BlockSpec(memory_space=pl.ANY),
        ],
        out_specs=pl.BlockSpec((IN_TM, IN_TN), lambda j, i: (i, j)),
        out_shape=jax.ShapeDtypeStruct((t, PROJ_COLS), BF16),
        input_output_aliases={n_in: 0},
        scratch_shapes=[pltpu.VMEM((IN_TN, D_MODEL), BF16)],
        compiler_params=pltpu.CompilerParams(
            dimension_semantics=("arbitrary", "arbitrary"),
            vmem_limit_bytes=56 * 1024 * 1024),
        name="inproj",
    )(h, *([w_t] * W_PER_TILE), col_scale, proj_partial)


def _cumsum_kernel(ft_ref, b_ref, kb_ref):
    f = ft_ref[0:FOX_HEADS, :] + b_ref[...]
    lf = jnp.minimum(f, 0.0) - jnp.log1p(jnp.exp(-jnp.abs(f)))
    seq = lf.shape[1]
    lane = lax.broadcasted_iota(jnp.int32, lf.shape, 1)
    c = lf
    shift = 1
    while shift < seq:
        c = c + jnp.where(lane >= shift, pltpu.roll(c, shift, axis=1), 0.0)
        shift *= 2
    rest = c * (-LOG2E)
    pieces = []
    for _ in range(BIAS_TERMS):
        piece = rest.astype(BF16).astype(F32)
        pieces.append(piece)
        rest = rest - piece
    pieces.append(jnp.zeros((LANES - BIAS_TERMS * FOX_HEADS, seq), F32))
    kb_ref[...] = jnp.concatenate(pieces, axis=0).T.astype(BF16)


def _forget_cumsum(ft, b_col, batch, seq):
    return pl.pallas_call(
        _cumsum_kernel,
        grid=(batch,),
        in_specs=[
            pl.BlockSpec((F_ROWS, seq), lambda b: (0, b)),
            pl.BlockSpec((FOX_HEADS, 1), lambda b: (0, 0)),
        ],
        out_specs=pl.BlockSpec((seq, LANES), lambda b: (b, 0)),
        out_shape=jax.ShapeDtypeStruct((batch * seq, LANES), BF16),
        name="forget_cumsum",
    )(ft, b_col)


def _fox_kernel(q_ref, k_ref, v_ref, kb_ref, o_ref, vt_ref, qt_ref, acc_ref,
                sa_ref, ma_ref, sb_ref, mb_ref):
    s0_ref, s1_ref = (sa_ref, ma_ref), (sb_ref, mb_ref)
    hp = pl.program_id(1)
    seq = k_ref.shape[0]
    t = FOX_T
    d = FOX_HEAD_DIM

    row = lax.broadcasted_iota(jnp.int32, (d, t), 0)
    for hh in range(FOX_PAIR):
        mine = (row < BIAS_TERMS * FOX_HEADS) & (row % FOX_HEADS == hp * FOX_PAIR + hh)
        qt_ref[hh, d:2 * d, :] = jnp.where(mine, 1.0, 0.0).astype(BF16)
        for r in range(seq // FOX_CHUNK):
            rows = slice(r * FOX_CHUNK, (r + 1) * FOX_CHUNK)
            vt_ref[hh, :, rows] = v_ref[rows, hh * d:(hh + 1) * d].astype(F32).T.astype(BF16)

    def scores(i, buf):
        s_ref, max_ref = buf
        rows = pl.ds(pl.multiple_of(i * t, t), t)
        for hh in range(FOX_PAIR):
            k_aug = jnp.concatenate([k_ref[rows, hh * d:(hh + 1) * d], kb_ref[rows, :]], axis=1)
            s = jnp.dot(k_aug, qt_ref[hh], preferred_element_type=F32)
            s_ref[hh] = s
            max_ref[hh] = jnp.max(s, axis=0, keepdims=True)

    def absorb(i, buf, carry, masked):
        s_ref, max_ref = buf
        rows = pl.ds(pl.multiple_of(i * t, t), t)
        new = []
        for hh in range(FOX_PAIR):
            m, l = carry[hh]
            s = s_ref[hh]
            if masked:
                key = lax.broadcasted_iota(jnp.int32, s.shape, 0)
                qry = lax.broadcasted_iota(jnp.int32, s.shape, 1)
                s = jnp.where(key <= qry, s, NEG_BIG)
                tile_max = jnp.max(s, axis=0, keepdims=True)
            else:
                tile_max = max_ref[hh]
            m_new = jnp.maximum(m, tile_max)
            alpha = jnp.exp2(m - m_new)
            p = jnp.exp2(s - m_new)
            l_new = alpha * l + jnp.sum(p, axis=0, keepdims=True)
            acc_ref[hh] = alpha * acc_ref[hh] + jnp.dot(
                vt_ref[hh, :, rows], p.astype(BF16), preferred_element_type=F32)
            new.append((m_new, l_new))
        return tuple(new)

    def step(i, s_cur, s_next, carry):
        scores(i + 1, s_next)
        return absorb(i, s_cur, carry, False)

    def pair(j, carry):
        carry = step(2 * j, s0_ref, s1_ref, carry)
        return step(2 * j + 1, s1_ref, s0_ref, carry)

    def q_tile(qi, _):
        q_rows = pl.ds(pl.multiple_of(qi * t, t), t)
        for hh in range(FOX_PAIR):
            qt_ref[hh, 0:d, :] = q_ref[q_rows, hh * d:(hh + 1) * d].astype(F32).T.astype(BF16)
            acc_ref[hh] = jnp.zeros((d, t), F32)

        def odd_tail(carry):
            carry = step(qi - 1, s0_ref, s1_ref, carry)
            return absorb(qi, s1_ref, carry, True)

        def even_tail(carry):
            return absorb(qi, s0_ref, carry, True)

        init = tuple((jnp.full((1, t), NEG_BIG, F32), jnp.zeros((1, t), F32))
                     for _ in range(FOX_PAIR))
        scores(0, s0_ref)
        carry = lax.fori_loop(0, qi // 2, pair, init)
        carry = lax.cond(qi % 2 == 1, odd_tail, even_tail, carry)
        for hh in range(FOX_PAIR):
            o_ref[q_rows, hh * d:(hh + 1) * d] = (
                acc_ref[hh] / carry[hh][1]).T.astype(o_ref.dtype)
        return 0

    lax.fori_loop(0, seq // t, q_tile, 0)


def _fox_attention(proj, kb_tok, batch, seq):
    width = FOX_PAIR * FOX_HEAD_DIM
    qcol = _PROJ_OFF["q_a"] // width
    kcol = _PROJ_OFF["k_a"] // width
    vcol = _PROJ_OFF["v_a"] // width
    return pl.pallas_call(
        _fox_kernel,
        grid=(batch, FOX_HEADS // FOX_PAIR),
        in_specs=[
            pl.BlockSpec((seq, width), lambda b, h: (b, qcol + h)),
            pl.BlockSpec((seq, width), lambda b, h: (b, kcol + h)),
            pl.BlockSpec((seq, width), lambda b, h: (b, vcol + h)),
            pl.BlockSpec((seq, LANES), lambda b, h: (b, 0)),
        ],
        out_specs=pl.BlockSpec((seq, width), lambda b, h: (b, h)),
        out_shape=jax.ShapeDtypeStruct((batch * seq, FOX_WIDTH), BF16),
        scratch_shapes=[
            pltpu.VMEM((FOX_PAIR, FOX_HEAD_DIM, seq), BF16),
            pltpu.VMEM((FOX_PAIR, 2 * FOX_HEAD_DIM, FOX_T), BF16),
            pltpu.VMEM((FOX_PAIR, FOX_HEAD_DIM, FOX_T), F32),
            pltpu.VMEM((FOX_PAIR, FOX_T, FOX_T), F32),
            pltpu.VMEM((FOX_PAIR, 1, FOX_T), F32),
            pltpu.VMEM((FOX_PAIR, FOX_T, FOX_T), F32),
            pltpu.VMEM((FOX_PAIR, 1, FOX_T), F32),
        ],
        compiler_params=pltpu.CompilerParams(
            dimension_semantics=("arbitrary", "arbitrary"),
            vmem_limit_bytes=56 * 1024 * 1024),
        name="fox_attention",
    )(proj, proj, proj, kb_tok)


def _rope_t(xt, cos, sin):
    out = []
    for hd in range(xt.shape[0] // SWA_HEAD_DIM):
        x1 = xt[hd * SWA_HEAD_DIM: hd * SWA_HEAD_DIM + ROPE_HALF]
        x2 = xt[hd * SWA_HEAD_DIM + ROPE_HALF: (hd + 1) * SWA_HEAD_DIM]
        out.append(x1 * cos - x2 * sin)
        out.append(x2 * cos + x1 * sin)
    return jnp.concatenate(out, axis=0)


def _swa_kernel(sink_ref, q_ref, kc_ref, kp_ref, vc_ref, vp_ref, posc_ref, posp_ref,
                inv_ref, *rest):
    n_cast = (len(rest) - 2) // 2
    cast_in, o_ref = rest[:n_cast], rest[n_cast]
    cast_out, band_ref = rest[n_cast + 1:2 * n_cast + 1], rest[-1]
    for src_ref, dst_ref in zip(cast_in, cast_out):
        dst_ref[...] = src_ref[...].astype(dst_ref.dtype)

    n = pl.program_id(1)
    w = WINDOW
    inv = inv_ref[...]

    def tables(pos_row):
        ang = inv * pos_row.astype(F32)
        return jnp.cos(ang), jnp.sin(ang)

    def rope_k(k_nat, cos, sin):
        kt = _rope_t(k_nat.astype(F32).T, cos, sin)
        return kt.T.astype(BF16)

    cos_p, sin_p = tables(posp_ref[...])
    k_prev = rope_k(kp_ref[...], cos_p, sin_p)
    v_prev = vp_ref[...]

    @pl.when((pl.program_id(0) == 0) & (n == 0))
    def _():
        key = lax.broadcasted_iota(jnp.int32, band_ref.shape, 0)
        qry = lax.broadcasted_iota(jnp.int32, band_ref.shape, 1) % w
        band_ref[...] = jnp.where((key <= qry + w) & (key > qry), 0.0, NEG_BIG)

    no_prev = jnp.where(n > 0, 0.0, NEG_BIG)
    q_scale = SWA_HEAD_DIM ** -0.5 * LOG2E
    zeros_half = jnp.zeros((SWA_HEAD_DIM, SWA_GROUP * w), F32)

    sinks = [jnp.concatenate(
        [jnp.full((1, w), sink_ref[hd] * LOG2E, F32)
         for hd in range(g * SWA_GROUP, (g + 1) * SWA_GROUP)], axis=1)
        for g in range(SWA_KV_HEADS)]

    def logits(wi, k_prev):
        tok = slice(wi * w, (wi + 1) * w)
        cos, sin = tables(posc_ref[:, tok])
        k_cur = rope_k(kc_ref[tok, :], cos, sin)
        k_all = jnp.concatenate([k_prev, k_cur], axis=0)
        qf = q_ref[tok, :].astype(F32)
        qt = jnp.concatenate(
            [qf[:, c * LANES:(c + 1) * LANES].T for c in range(SWA_WIDTH // LANES)], axis=0)
        qt = _rope_t(qt, cos * q_scale, sin * q_scale)
        out = []
        for g in range(SWA_KV_HEADS):
            heads = range(g * SWA_GROUP, (g + 1) * SWA_GROUP)
            q_g = jnp.concatenate(
                [qt[hd * SWA_HEAD_DIM:(hd + 1) * SWA_HEAD_DIM] for hd in heads], axis=1)
            parts = [zeros_half] * SWA_KV_HEADS
            parts[g] = q_g
            q_z = jnp.concatenate(parts, axis=0).astype(BF16)
            s = jnp.dot(k_all, q_z, preferred_element_type=F32) + band_ref[...]
            if wi == 0:
                s = jnp.concatenate([s[:w] + no_prev, s[w:]], axis=0)
            out.append(s)
        return out, k_cur

    def finish(wi, scores, v_prev):
        tok = slice(wi * w, (wi + 1) * w)
        v_cur = vc_ref[tok, :]
        v_all_t = jnp.concatenate([v_prev, v_cur], axis=0).astype(F32).T.astype(BF16)
        for g in range(SWA_KV_HEADS):
            s, sink = scores[g], sinks[g]
            m = jnp.maximum(jnp.max(s, axis=0, keepdims=True), sink)
            e = jnp.exp2(s - m)
            denom = jnp.sum(e, axis=0, keepdims=True) + jnp.exp2(sink - m)
            out_t = jnp.dot(v_all_t, e.astype(BF16), preferred_element_type=F32)
            out_t = out_t[g * SWA_HEAD_DIM:(g + 1) * SWA_HEAD_DIM] / denom
            for jj in range(SWA_GROUP // 2):
                pair = jnp.concatenate(
                    [out_t[:, (2 * jj) * w:(2 * jj + 1) * w],
                     out_t[:, (2 * jj + 1) * w:(2 * jj + 2) * w]], axis=0)
                col = (g * (SWA_GROUP // 2) + jj) * LANES
                o_ref[tok, col:col + LANES] = pair.T.astype(o_ref.dtype)
        return v_cur

    pending, k_prev = logits(0, k_prev)
    for wi in range(SWA_NW):
        if wi + 1 < SWA_NW:
            upcoming, k_prev = logits(wi + 1, k_prev)
        v_prev = finish(wi, pending, v_prev)
        if wi + 1 < SWA_NW:
            pending = upcoming


def _swa_attention(sinks, proj, pos_row, inv_tab, batch, seq, weights_f32):
    nb = seq // SWA_QB
    per = SWA_QB // WINDOW
    qcol = _PROJ_OFF["q_b"] // SWA_WIDTH
    kcol = _PROJ_OFF["k_b"] // LANES
    vcol = _PROJ_OFF["v_b"] // LANES
    steps = batch * nb

    def cur(b, n):
        return b * nb + n

    def prev(b, n):
        return jnp.maximum((b * nb + n) * per - 1, 0)

    def chunk_spec(wgt):
        rows, cols = wgt.shape
        assert rows % (steps * 2 * SUBLANES) == 0
        return pl.BlockSpec((rows // steps, cols), lambda b, n: (cur(b, n), 0))

    cast_specs = [chunk_spec(wgt) for wgt in weights_f32]
    return pl.pallas_call(
        _swa_kernel,
        grid=(batch, nb),
        in_specs=[
            pl.BlockSpec(memory_space=pltpu.SMEM),
            pl.BlockSpec((SWA_QB, SWA_WIDTH), lambda b, n: (cur(b, n), qcol)),
            pl.BlockSpec((SWA_QB, LANES), lambda b, n: (cur(b, n), kcol)),
            pl.BlockSpec((WINDOW, LANES), lambda b, n: (prev(b, n), kcol)),
            pl.BlockSpec((SWA_QB, LANES), lambda b, n: (cur(b, n), vcol)),
            pl.BlockSpec((WINDOW, LANES), lambda b, n: (prev(b, n), vcol)),
            pl.BlockSpec((1, SWA_QB), lambda b, n: (0, cur(b, n))),
            pl.BlockSpec((1, WINDOW), lambda b, n: (0, prev(b, n))),
            pl.BlockSpec((ROPE_HALF, LANES), lambda b, n: (0, 0)),
            *cast_specs,
        ],
        out_specs=[pl.BlockSpec((SWA_QB, SWA_WIDTH), lambda b, n: (cur(b, n), 0)), *cast_specs],
        out_shape=[jax.ShapeDtypeStruct((batch * seq, SWA_WIDTH), BF16),
                   *[jax.ShapeDtypeStruct(wgt.shape, BF16) for wgt in weights_f32]],
        scratch_shapes=[pltpu.VMEM((2 * WINDOW, SWA_GROUP * WINDOW), F32)],
        compiler_params=pltpu.CompilerParams(
            dimension_semantics=("arbitrary", "arbitrary")),
        name="swa_attention",
    )(sinks, proj, proj, proj, proj, proj, pos_row, pos_row, inv_tab, *weights_f32)


def _silu(z):
    return z * jax.nn.sigmoid(z)


def _epilogue_kernel(ya_ref, za_ref, yb_ref, zb_ref, ga_ref, gb_ref, x_ref, p_ref,
                     wof_ref, wos_ref, wout_ref, gpost_ref, wple_ref, wgate_ref, o_ref):
    ua = (ya_ref[...].astype(F32) * _silu(za_ref[...].astype(F32))).astype(BF16)
    ub = (yb_ref[...].astype(F32) * _silu(zb_ref[...].astype(F32))).astype(BF16)
    oa = jnp.dot(ua, wof_ref[...], preferred_element_type=F32)
    ob = jnp.dot(ub, wos_ref[...], preferred_element_type=F32)
    merged = (jax.nn.sigmoid(ga_ref[...].astype(F32)) * oa
              + jax.nn.sigmoid(gb_ref[...].astype(F32)) * ob)
    out = jnp.dot(merged.astype(BF16), wout_ref[...], preferred_element_type=F32)
    ms = jnp.mean(out * out, axis=-1, keepdims=True)
    x1 = x_ref[...] + (out * lax.rsqrt(ms + NORM_EPS)) * gpost_ref[...]
    e = jnp.dot(p_ref[...].astype(BF16), wple_ref[...], preferred_element_type=F32)
    gate = jax.nn.sigmoid(jnp.dot(x1.astype(BF16), wgate_ref[...],
                                  preferred_element_type=F32))
    o_ref[...] = x1 + gate * e


def _epilogue(ya, yb, proj, x2, p2, wof, wos, wout, gpost, wple, wgate):
    t = x2.shape[0]
    za_col = _PROJ_OFF["z_a"] // FOX_WIDTH
    zb_col = _PROJ_OFF["z_b"] // SWA_WIDTH
    ga_col = _PROJ_OFF["g_a"] // D_MODEL
    gb_col = _PROJ_OFF["g_b"] // D_MODEL
    once = pl.Buffered(1)

    def const(shape):
        return pl.BlockSpec(shape, lambda i: (0, 0), pipeline_mode=once)

    return pl.pallas_call(
        _epilogue_kernel,
        grid=(t // EPI_TM,),
        in_specs=[
            pl.BlockSpec((EPI_TM, FOX_WIDTH), lambda i: (i, 0)),
            pl.BlockSpec((EPI_TM, FOX_WIDTH), lambda i: (i, za_col)),
            pl.BlockSpec((EPI_TM, SWA_WIDTH), lambda i: (i, 0)),
            pl.BlockSpec((EPI_TM, SWA_WIDTH), lambda i: (i, zb_col)),
            pl.BlockSpec((EPI_TM, D_MODEL), lambda i: (i, ga_col)),
            pl.BlockSpec((EPI_TM, D_MODEL), lambda i: (i, gb_col)),
            pl.BlockSpec((EPI_TM, D_MODEL), lambda i: (i, 0)),
            pl.BlockSpec((EPI_TM, PLE_DIM), lambda i: (i, 0)),
            const((FOX_WIDTH, D_MODEL)),
            const((SWA_WIDTH, D_MODEL)),
            const((D_MODEL, D_MODEL)),
            const((1, D_MODEL)),
            const((PLE_DIM, D_MODEL)),
            const((D_MODEL, D_MODEL)),
        ],
        out_specs=pl.BlockSpec((EPI_TM, D_MODEL), lambda i: (i, 0)),
        out_shape=jax.ShapeDtypeStruct((t, D_MODEL), F32),
        compiler_params=pltpu.CompilerParams(
            dimension_semantics=("arbitrary",),
            vmem_limit_bytes=56 * 1024 * 1024),
        name="epilogue",
    )(ya, proj, yb, proj, proj, proj, x2, p2, wof, wos, wout, gpost, wple, wgate)


def _layer(x2, p2, pos_row, batch, seq, pre_g, w_in, b_forget, sinks, w_o_fox, w_o_swa,
           w_out, post_g, w_ple, w_ple_gate):
    w_t = w_in.T
    f0 = _REF_OFF["f_a"]
    wf_t = jnp.pad(w_t[f0:f0 + FOX_HEADS], ((0, F_ROWS - FOX_HEADS), (0, 0))).astype(BF16)
    col_scale = jnp.where(jnp.arange(PROJ_COLS) < FOX_WIDTH, Q_A_SCALE, 1.0
                          ).astype(F32).reshape(1, PROJ_COLS)

    h, ft, proj_partial = _prenorm(x2, pre_g.reshape(1, D_MODEL), wf_t, w_t)
    proj = _inproj(h, w_t, col_scale, proj_partial)
    kb_tok = _forget_cumsum(ft, b_forget.reshape(FOX_HEADS, 1).astype(F32), batch, seq)
    ya = _fox_attention(proj, kb_tok, batch, seq)

    inv = ROPE_THETA ** (-jnp.arange(ROPE_HALF, dtype=F32) / ROPE_HALF)
    inv_tab = jnp.broadcast_to(inv[:, None], (ROPE_HALF, LANES))
    yb, wof, wos, wout, wple, wgate = _swa_attention(
        sinks.astype(F32), proj, pos_row, inv_tab, batch, seq,
        (w_o_fox, w_o_swa, w_out, w_ple, w_ple_gate))

    return _epilogue(ya, yb, proj, x2, p2, wof, wos, wout,
                     post_g.reshape(1, D_MODEL), wple, wgate)


def kernel(x, p, positions, pre_norm_g, w_in, b_forget, sinks, w_o_fox, w_o_swa, w_out,
           post_norm_g, w_ple, w_ple_gate):
    batch, seq, _ = x.shape
    depth = p.shape[0]
    x2 = x.reshape(batch * seq, D_MODEL)
    pos_row = positions.reshape(1, batch * seq)
    for i in range(depth):
        x2 = _layer(x2, p[i].reshape(batch * seq, PLE_DIM), pos_row, batch, seq,
                    pre_norm_g[i], w_in[i], b_forget[i], sinks[i], w_o_fox[i], w_o_swa[i],
                    w_out[i], post_norm_g[i], w_ple[i], w_ple_gate[i])
    return x2.reshape(batch, seq, D_MODEL)
```

```python
import math

import jax
import jax.numpy as jnp
from jax import lax
from jax.experimental import pallas as pl
from jax.experimental.pallas import tpu as pltpu

F32 = jnp.float32
BF16 = jnp.bfloat16

D_MODEL = 2048
FOX_HEADS = 8
FOX_HEAD_DIM = 128
FOX_WIDTH = FOX_HEADS * FOX_HEAD_DIM
SWA_Q_HEADS = 16
SWA_KV_HEADS = 2
SWA_HEAD_DIM = 64
SWA_WIDTH = SWA_Q_HEADS * SWA_HEAD_DIM
SWA_KV_WIDTH = SWA_KV_HEADS * SWA_HEAD_DIM
SWA_GROUP = SWA_Q_HEADS // SWA_KV_HEADS
ROPE_HALF = SWA_HEAD_DIM // 2
WINDOW = 128
ROPE_THETA = 10000.0
PLE_DIM = 256
NORM_EPS = 1e-6
LANES = 128
LOG2E = math.log2(math.e)
NEG_BIG = -1e30

_REF_SPLITS = (FOX_WIDTH, FOX_WIDTH, FOX_WIDTH, FOX_WIDTH, FOX_HEADS, SWA_WIDTH,
               SWA_KV_WIDTH, SWA_KV_WIDTH, SWA_WIDTH, D_MODEL, D_MODEL)
_REF_NAMES = ("q_a", "k_a", "v_a", "z_a", "f_a", "q_b", "k_b", "v_b", "z_b", "g_a", "g_b")
_REF_OFF = {}
_o = 0
for _n, _s in zip(_REF_NAMES, _REF_SPLITS):
    _REF_OFF[_n] = _o
    _o += _s

IN_TM = 1024
IN_TN = 1536
NORM_ROWS = 512
F_ROWS = 16

PREP_TN = 512
_W_GROUPS = (
    (_REF_OFF["q_a"], 4 * FOX_WIDTH),
    (_REF_OFF["q_b"], SWA_WIDTH),
    (_REF_OFF["z_b"], SWA_WIDTH),
    (_REF_OFF["g_a"], 2 * D_MODEL),
    (_REF_OFF["k_b"], 2 * SWA_KV_WIDTH),
)
_GROUP_TILES = tuple(-(-w // PREP_TN) for _, w in _W_GROUPS)
_GROUP_START = tuple(sum(_GROUP_TILES[:i]) for i in range(len(_W_GROUPS)))
PREP_TILES = sum(_GROUP_TILES)
PROJ_COLS = PREP_TILES * PREP_TN
_PROJ_OFF = {
    "q_a": 0, "k_a": FOX_WIDTH, "v_a": 2 * FOX_WIDTH, "z_a": 3 * FOX_WIDTH,
    "q_b": _GROUP_START[1] * PREP_TN,
    "z_b": _GROUP_START[2] * PREP_TN,
    "g_a": _GROUP_START[3] * PREP_TN, "g_b": _GROUP_START[3] * PREP_TN + D_MODEL,
    "k_b": _GROUP_START[4] * PREP_TN, "v_b": _GROUP_START[4] * PREP_TN + SWA_KV_WIDTH,
}
SUBLANES = 8
assert all(c0 % SUBLANES == 0 for c0, _ in _W_GROUPS)
W_PER_TILE = IN_TN // PREP_TN
LAST_TILE = PROJ_COLS // IN_TN - 1
_NT = (((1,), (1,)), ((), ()))
Q_A_SCALE = FOX_HEAD_DIM ** -0.5 * LOG2E

FOX_T = 512
FOX_CHUNK = 512
FOX_PAIR = 4
BIAS_TERMS = 3

SWA_NW = 8
SWA_QB = SWA_NW * WINDOW

EPI_TM = 256


def _cast_windows(w_refs, wb_ref):
    for k, w_ref in enumerate(w_refs):
        for r in range(PREP_TN // LANES):
            rows = slice(r * LANES, (r + 1) * LANES)
            wb_ref[k * PREP_TN + r * LANES:k * PREP_TN + (r + 1) * LANES, :] = (
                w_ref[rows, :].astype(BF16))


def _prenorm_kernel(x_ref, g_ref, wf_ref, *refs):
    w_refs = refs[:W_PER_TILE]
    h_ref, ft_ref, proj_ref, wb_ref = refs[W_PER_TILE:]

    @pl.when(pl.program_id(0) == 0)
    def _():
        _cast_windows(w_refs, wb_ref)

    x = x_ref[...]
    ms = jnp.mean(x * x, axis=-1, keepdims=True)
    h = ((x * lax.rsqrt(ms + NORM_EPS)) * g_ref[...]).astype(BF16)
    h_ref[...] = h
    ft_ref[...] = lax.dot_general(wf_ref[...], h, _NT, preferred_element_type=F32)
    proj_ref[...] = lax.dot_general(h, wb_ref[...], _NT,
                                    preferred_element_type=F32).astype(BF16)


def _prenorm(x2, g, wf_t, w_t):
    t = x2.shape[0]
    once = pl.Buffered(1)

    def w_spec(k):
        return pl.BlockSpec((pl.Element(PREP_TN), pl.Element(D_MODEL)),
                            lambda i: (_w_src_row(LAST_TILE * W_PER_TILE + k), 0),
                            pipeline_mode=once)

    return pl.pallas_call(
        _prenorm_kernel,
        grid=(t // NORM_ROWS,),
        in_specs=[
            pl.BlockSpec((NORM_ROWS, D_MODEL), lambda i: (i, 0)),
            pl.BlockSpec((1, D_MODEL), lambda i: (0, 0)),
            pl.BlockSpec((F_ROWS, D_MODEL), lambda i: (0, 0)),
            *[w_spec(k) for k in range(W_PER_TILE)],
        ],
        out_specs=[
            pl.BlockSpec((NORM_ROWS, D_MODEL), lambda i: (i, 0)),
            pl.BlockSpec((F_ROWS, NORM_ROWS), lambda i: (0, i)),
            pl.BlockSpec((NORM_ROWS, IN_TN), lambda i: (i, LAST_TILE)),
        ],
        out_shape=[
            jax.ShapeDtypeStruct((t, D_MODEL), BF16),
            jax.ShapeDtypeStruct((F_ROWS, t), F32),
            jax.ShapeDtypeStruct((t, PROJ_COLS), BF16),
        ],
        scratch_shapes=[pltpu.VMEM((IN_TN, D_MODEL), BF16)],
        compiler_params=pltpu.CompilerParams(
            dimension_semantics=("arbitrary",),
            vmem_limit_bytes=56 * 1024 * 1024),
        name="prenorm",
    )(x2, g, wf_t, *([w_t] * W_PER_TILE))


def _w_src_row(tile):
    tile_row = jnp.int32(0)
    for (c0, _), start in zip(_W_GROUPS, _GROUP_START):
        tile_row = jnp.where(tile >= start,
                             c0 // SUBLANES + (tile - start) * (PREP_TN // SUBLANES), tile_row)
    return tile_row * SUBLANES


def _inproj_kernel(h_ref, *refs):
    w_refs = refs[:W_PER_TILE]
    scale_ref, _, proj_ref, wb_ref = refs[W_PER_TILE:]

    @pl.when(pl.program_id(1) == 0)
    def _():
        _cast_windows(w_refs, wb_ref)

    acc = lax.dot_general(h_ref[...], wb_ref[...], _NT, preferred_element_type=F32)
    proj_ref[...] = (acc * scale_ref[...]).astype(BF16)


def _inproj(h, w_t, col_scale, proj_partial):
    t = h.shape[0]

    def w_spec(k):
        return pl.BlockSpec((pl.Element(PREP_TN), pl.Element(D_MODEL)),
                            lambda j, i: (_w_src_row(j * W_PER_TILE + k), 0))

    n_in = 1 + W_PER_TILE + 1
    return pl.pallas_call(
        _inproj_kernel,
        grid=(LAST_TILE, t // IN_TM),
        in_specs=[
            pl.BlockSpec((IN_TM, D_MODEL), lambda j, i: (i, 0)),
            *[w_spec(k) for k in range(W_PER_TILE)],
            pl.BlockSpec((1, IN_TN), lambda j, i: (0, j)),
            pl.BlockSpec(memory_space=pl.ANY),
        ],
        out_specs=pl.BlockSpec((IN_TM, IN_TN), lambda j, i: (i, j)),
        out_shape=jax.ShapeDtypeStruct((t, PROJ_COLS), BF16),
        input_output_aliases={n_in: 0},
        scratch_shapes=[pltpu.VMEM((IN_TN, D_MODEL), BF16)],
        compiler_params=pltpu.CompilerParams(
            dimension_semantics=("arbitrary", "arbitrary"),
            vmem_limit_bytes=56 * 1024 * 1024),
        name="inproj",
    )(h, *([w_t] * W_PER_TILE), col_scale, proj_partial)


def _cumsum_kernel(ft_ref, b_ref, kb_ref):
    f = ft_ref[0:FOX_HEADS, :] + b_ref[...]
    lf = jnp.minimum(f, 0.0) - jnp.log1p(jnp.exp(-jnp.abs(f)))
    seq = lf.shape[1]
    lane = lax.broadcasted_iota(jnp.int32, lf.shape, 1)
    c = lf
    shift = 1
    while shift < seq:
        c = c + jnp.where(lane >= shift, pltpu.roll(c, shift, axis=1), 0.0)
        shift *= 2
    rest = c * (-LOG2E)
    pieces = []
    for _ in range(BIAS_TERMS):
        piece = rest.astype(BF16).astype(F32)
        pieces.append(piece)
        rest = rest - piece
    pieces.append(jnp.zeros((LANES - BIAS_TERMS * FOX_HEADS, seq), F32))
    kb_ref[...] = jnp.concatenate(pieces, axis=0).T.astype(BF16)


def _forget_cumsum(ft, b_col, batch, seq):
    return pl.pallas_call(
        _cumsum_kernel,
        grid=(batch,),
        in_specs=[
            pl.BlockSpec((F_ROWS, seq), lambda b: (0, b)),
            pl.BlockSpec((FOX_HEADS, 1), lambda b: (0, 0)),
        ],
        out_specs=pl.BlockSpec((seq, LANES), lambda b: (b, 0)),
        out_shape=jax.ShapeDtypeStruct((batch * seq, LANES), BF16),
        name="forget_cumsum",
    )(ft, b_col)


def _fox_kernel(q_ref, k_ref, v_ref, kb_ref, o_ref, vt_ref, qt_ref, acc_ref,
                sa_ref, ma_ref, sb_ref, mb_ref):
    s0_ref, s1_ref = (sa_ref, ma_ref), (sb_ref, mb_ref)
    hp = pl.program_id(1)
    seq = k_ref.shape[0]
    t = FOX_T
    d = FOX_HEAD_DIM

    def transposed(x):
        return x.astype(F32).T.astype(BF16)

    row = lax.broadcasted_iota(jnp.int32, (d, t), 0)
    for hh in range(FOX_PAIR):
        mine = (row < BIAS_TERMS * FOX_HEADS) & (row % FOX_HEADS == hp * FOX_PAIR + hh)
        qt_ref[hh, d:2 * d, :] = jnp.where(mine, 1.0, 0.0).astype(BF16)
        for r in range(seq // FOX_CHUNK):
            rows = slice(r * FOX_CHUNK, (r + 1) * FOX_CHUNK)
            vt_ref[hh, :, rows] = transposed(v_ref[rows, hh * d:(hh + 1) * d])

    def scores(i, buf):
        s_ref, max_ref = buf
        rows = pl.ds(pl.multiple_of(i * t, t), t)
        for hh in range(FOX_PAIR):
            k_aug = jnp.concatenate([k_ref[rows, hh * d:(hh + 1) * d], kb_ref[rows, :]], axis=1)
            s = jnp.dot(k_aug, qt_ref[hh], preferred_element_type=F32)
            s_ref[hh] = s
            max_ref[hh] = jnp.max(s, axis=0, keepdims=True)

    def absorb(i, buf, carry, masked):
        s_ref, max_ref = buf
        rows = pl.ds(pl.multiple_of(i * t, t), t)
        new = []
        for hh in range(FOX_PAIR):
            m, l = carry[hh]
            s = s_ref[hh]
            if masked:
                key = lax.broadcasted_iota(jnp.int32, s.shape, 0)
                qry = lax.broadcasted_iota(jnp.int32, s.shape, 1)
                s = jnp.where(key <= qry, s, NEG_BIG)
                tile_max = jnp.max(s, axis=0, keepdims=True)
            else:
                tile_max = max_ref[hh]
            m_new = jnp.maximum(m, tile_max)
            alpha = jnp.exp2(m - m_new)
            p = jnp.exp2(s - m_new)
            l_new = alpha * l + jnp.sum(p, axis=0, keepdims=True)
            acc_ref[hh] = alpha * acc_ref[hh] + jnp.dot(
                vt_ref[hh, :, rows], p.astype(BF16), preferred_element_type=F32)
            new.append((m_new, l_new))
        return tuple(new)

    def step(i, s_cur, s_next, carry):
        scores(i + 1, s_next)
        return absorb(i, s_cur, carry, False)

    def pair(j, carry):
        carry = step(2 * j, s0_ref, s1_ref, carry)
        return step(2 * j + 1, s1_ref, s0_ref, carry)

    def q_tile(qi, _):
        q_rows = pl.ds(pl.multiple_of(qi * t, t), t)
        for hh in range(FOX_PAIR):
            qt_ref[hh, 0:d, :] = transposed(q_ref[q_rows, hh * d:(hh + 1) * d])
            acc_ref[hh] = jnp.zeros((d, t), F32)

        def odd_tail(carry):
            carry = step(qi - 1, s0_ref, s1_ref, carry)
            return absorb(qi, s1_ref, carry, True)

        def even_tail(carry):
            return absorb(qi, s0_ref, carry, True)

        init = tuple((jnp.full((1, t), NEG_BIG, F32), jnp.zeros((1, t), F32))
                     for _ in range(FOX_PAIR))
        scores(0, s0_ref)
        carry = lax.fori_loop(0, qi // 2, pair, init)
        carry = lax.cond(qi % 2 == 1, odd_tail, even_tail, carry)
        for hh in range(FOX_PAIR):
            o_ref[q_rows, hh * d:(hh + 1) * d] = (
                acc_ref[hh] / carry[hh][1]).T.astype(o_ref.dtype)
        return 0

    lax.fori_loop(0, seq // t, q_tile, 0)


def _fox_attention(proj, kb_tok, batch, seq):
    width = FOX_PAIR * FOX_HEAD_DIM
    qcol = _PROJ_OFF["q_a"] // width
    kcol = _PROJ_OFF["k_a"] // width
    vcol = _PROJ_OFF["v_a"] // width
    return pl.pallas_call(
        _fox_kernel,
        grid=(batch, FOX_HEADS // FOX_PAIR),
        in_specs=[
            pl.BlockSpec((seq, width), lambda b, h: (b, qcol + h)),
            pl.BlockSpec((seq, width), lambda b, h: (b, kcol + h)),
            pl.BlockSpec((seq, width), lambda b, h: (b, vcol + h)),
            pl.BlockSpec((seq, LANES), lambda b, h: (b, 0)),
        ],
        out_specs=pl.BlockSpec((seq, width), lambda b, h: (b, h)),
        out_shape=jax.ShapeDtypeStruct((batch * seq, FOX_WIDTH), BF16),
        scratch_shapes=[
            pltpu.VMEM((FOX_PAIR, FOX_HEAD_DIM, seq), BF16),
            pltpu.VMEM((FOX_PAIR, 2 * FOX_HEAD_DIM, FOX_T), BF16),
            pltpu.VMEM((FOX_PAIR, FOX_HEAD_DIM, FOX_T), F32),
            pltpu.VMEM((FOX_PAIR, FOX_T, FOX_T), F32),
            pltpu.VMEM((FOX_PAIR, 1, FOX_T), F32),
            pltpu.VMEM((FOX_PAIR, FOX_T, FOX_T), F32),
            pltpu.VMEM((FOX_PAIR, 1, FOX_T), F32),
        ],
        compiler_params=pltpu.CompilerParams(
            dimension_semantics=("arbitrary", "arbitrary"),
            vmem_limit_bytes=56 * 1024 * 1024),
        name="fox_attention",
    )(proj, proj, proj, kb_tok)


def _rope_t(xt, cos, sin):
    out = []
    for hd in range(xt.shape[0] // SWA_HEAD_DIM):
        x1 = xt[hd * SWA_HEAD_DIM: hd * SWA_HEAD_DIM + ROPE_HALF]
        x2 = xt[hd * SWA_HEAD_DIM + ROPE_HALF: (hd + 1) * SWA_HEAD_DIM]
        out.append(x1 * cos - x2 * sin)
        out.append(x2 * cos + x1 * sin)
    return jnp.concatenate(out, axis=0)


def _swa_kernel(sink_ref, q_ref, kc_ref, kp_ref, vc_ref, vp_ref, posc_ref, posp_ref,
                inv_ref, *rest):
    n_cast = (len(rest) - 3) // 2
    cast_in, o_ref = rest[:n_cast], rest[n_cast]
    cast_out, (band_ref, eye_ref) = rest[n_cast + 1:2 * n_cast + 1], rest[-2:]
    for src_ref, dst_ref in zip(cast_in, cast_out):
        dst_ref[...] = src_ref[...].astype(dst_ref.dtype)

    n = pl.program_id(1)
    w = WINDOW
    inv = inv_ref[...]

    def tables(pos_row):
        ang = inv * pos_row.astype(F32)
        return jnp.cos(ang), jnp.sin(ang)

    def rope_k(k_nat, cos, sin):
        kt = _rope_t(k_nat.astype(F32).T, cos, sin)
        return kt.T.astype(BF16)

    cos_p, sin_p = tables(posp_ref[...])
    k_prev = rope_k(kp_ref[...], cos_p, sin_p)
    v_prev = vp_ref[...]

    @pl.when((pl.program_id(0) == 0) & (n == 0))
    def _():
        key = lax.broadcasted_iota(jnp.int32, band_ref.shape, 0)
        qry = lax.broadcasted_iota(jnp.int32, band_ref.shape, 1)
        band_ref[...] = jnp.where((key <= qry + w) & (key > qry), 0.0, NEG_BIG).astype(BF16)
        src = lax.broadcasted_iota(jnp.int32, eye_ref.shape, 0)
        dst = lax.broadcasted_iota(jnp.int32, eye_ref.shape, 1) % w
        eye_ref[...] = jnp.where(src == dst, 1.0, 0.0).astype(BF16)

    no_prev = jnp.where(n > 0, 0.0, NEG_BIG)
    q_scale = SWA_HEAD_DIM ** -0.5 * LOG2E
    zeros_half = jnp.zeros((SWA_HEAD_DIM, SWA_GROUP * w), F32)
    ones_rows = jnp.ones((2 * SUBLANES, 2 * w), BF16)

    sinks = [jnp.concatenate(
        [jnp.full((1, w), sink_ref[hd] * LOG2E, F32)
         for hd in range(g * SWA_GROUP, (g + 1) * SWA_GROUP)], axis=1)
        for g in range(SWA_KV_HEADS)]

    def logits(wi, k_prev):
        tok = slice(wi * w, (wi + 1) * w)
        cos, sin = tables(posc_ref[:, tok])
        k_cur = rope_k(kc_ref[tok, :], cos, sin)
        k_all = jnp.concatenate([k_prev, k_cur], axis=0)
        k_aug = jnp.concatenate([k_all, band_ref[...]], axis=1)
        qf = q_ref[tok, :].astype(F32)
        qt = jnp.concatenate(
            [qf[:, c * LANES:(c + 1) * LANES].T for c in range(SWA_WIDTH // LANES)], axis=0)
        qt = _rope_t(qt, cos * q_scale, sin * q_scale)
        out = []
        for g in range(SWA_KV_HEADS):
            heads = range(g * SWA_GROUP, (g + 1) * SWA_GROUP)
            q_g = jnp.concatenate(
                [qt[hd * SWA_HEAD_DIM:(hd + 1) * SWA_HEAD_DIM] for hd in heads], axis=1)
            parts = [zeros_half] * SWA_KV_HEADS
            parts[g] = q_g
            q_z = jnp.concatenate(parts, axis=0).astype(BF16)
            q_aug = jnp.concatenate([q_z, eye_ref[...]], axis=0)
            s = jnp.dot(k_aug, q_aug, preferred_element_type=F32)
            if wi == 0:
                s = jnp.concatenate([s[:w] + no_prev, s[w:]], axis=0)
            out.append(s)
        return out, k_cur

    def finish(wi, scores, v_prev):
        tok = slice(wi * w, (wi + 1) * w)
        v_cur = vc_ref[tok, :]
        v_all_t = jnp.concatenate([v_prev, v_cur], axis=0).astype(F32).T.astype(BF16)
        for g in range(SWA_KV_HEADS):
            s, sink = scores[g], sinks[g]
            m = jnp.maximum(jnp.max(s, axis=0, keepdims=True), sink)
            e = jnp.exp2(s - m).astype(BF16)
            v_aug = jnp.concatenate(
                [v_all_t[g * SWA_HEAD_DIM:(g + 1) * SWA_HEAD_DIM], ones_rows], axis=0)
            pv = jnp.dot(v_aug, e, preferred_element_type=F32)
            denom = pv[SWA_HEAD_DIM:SWA_HEAD_DIM + 1] + jnp.exp2(sink - m)
            out_t = pv[:SWA_HEAD_DIM] / denom
            for jj in range(SWA_GROUP // 2):
                pair = jnp.concatenate(
                    [out_t[:, (2 * jj) * w:(2 * jj + 1) * w],
                     out_t[:, (2 * jj + 1) * w:(2 * jj + 2) * w]], axis=0)
                col = (g * (SWA_GROUP // 2) + jj) * LANES
                o_ref[tok, col:col + LANES] = pair.T.astype(o_ref.dtype)
        return v_cur

    pending, k_prev = logits(0, k_prev)
    for wi in range(SWA_NW):
        if wi + 1 < SWA_NW:
            upcoming, k_prev = logits(wi + 1, k_prev)
        v_prev = finish(wi, pending, v_prev)
        if wi + 1 < SWA_NW:
            pending = upcoming


def _swa_attention(sinks, proj, pos_row, inv_tab, batch, seq, weights_f32):
    nb = seq // SWA_QB
    per = SWA_QB // WINDOW
    qcol = _PROJ_OFF["q_b"] // SWA_WIDTH
    kcol = _PROJ_OFF["k_b"] // LANES
    vcol = _PROJ_OFF["v_b"] // LANES
    steps = batch * nb

    def cur(b, n):
        return b * nb + n

    def prev(b, n):
        return jnp.maximum((b * nb + n) * per - 1, 0)

    def chunk_spec(wgt):
        rows, cols = wgt.shape
        assert rows % (steps * 2 * SUBLANES) == 0
        return pl.BlockSpec((rows // steps, cols), lambda b, n: (cur(b, n), 0))

    cast_specs = [chunk_spec(wgt) for wgt in weights_f32]
    return pl.pallas_call(
        _swa_kernel,
        grid=(batch, nb),
        in_specs=[
            pl.BlockSpec(memory_space=pltpu.SMEM),
            pl.BlockSpec((SWA_QB, SWA_WIDTH), lambda b, n: (cur(b, n), qcol)),
            pl.BlockSpec((SWA_QB, LANES), lambda b, n: (cur(b, n), kcol)),
            pl.BlockSpec((WINDOW, LANES), lambda b, n: (prev(b, n), kcol)),
            pl.BlockSpec((SWA_QB, LANES), lambda b, n: (cur(b, n), vcol)),
            pl.BlockSpec((WINDOW, LANES), lambda b, n: (prev(b, n), vcol)),
            pl.BlockSpec((1, SWA_QB), lambda b, n: (0, cur(b, n))),
            pl.BlockSpec((1, WINDOW), lambda b, n: (0, prev(b, n))),
            pl.BlockSpec((ROPE_HALF, LANES), lambda b, n: (0, 0)),
            *cast_specs,
        ],
        out_specs=[pl.BlockSpec((SWA_QB, SWA_WIDTH), lambda b, n: (cur(b, n), 0)), *cast_specs],
        out_shape=[jax.ShapeDtypeStruct((batch * seq, SWA_WIDTH), BF16),
                   *[jax.ShapeDtypeStruct(wgt.shape, BF16) for wgt in weights_f32]],
        scratch_shapes=[
            pltpu.VMEM((2 * WINDOW, WINDOW), BF16),
            pltpu.VMEM((WINDOW, SWA_GROUP * WINDOW), BF16),
        ],
        compiler_params=pltpu.CompilerParams(
            dimension_semantics=("arbitrary", "arbitrary")),
        name="swa_attention",
    )(sinks, proj, proj, proj, proj, proj, pos_row, pos_row, inv_tab, *weights_f32)


def _silu(z):
    return z * jax.nn.sigmoid(z)


def _epilogue_kernel(ya_ref, za_ref, yb_ref, zb_ref, ga_ref, gb_ref, x_ref, p_ref,
                     wof_ref, wos_ref, wout_ref, gpost_ref, wple_ref, wgate_ref, o_ref):
    ua = (ya_ref[...].astype(F32) * _silu(za_ref[...].astype(F32))).astype(BF16)
    ub = (yb_ref[...].astype(F32) * _silu(zb_ref[...].astype(F32))).astype(BF16)
    oa = jnp.dot(ua, wof_ref[...], preferred_element_type=F32)
    ob = jnp.dot(ub, wos_ref[...], preferred_element_type=F32)
    merged = (jax.nn.sigmoid(ga_ref[...].astype(F32)) * oa
              + jax.nn.sigmoid(gb_ref[...].astype(F32)) * ob)
    out = jnp.dot(merged.astype(BF16), wout_ref[...], preferred_element_type=F32)
    ms = jnp.mean(out * out, axis=-1, keepdims=True)
    x1 = x_ref[...] + (out * lax.rsqrt(ms + NORM_EPS)) * gpost_ref[...]
    e = jnp.dot(p_ref[...].astype(BF16), wple_ref[...], preferred_element_type=F32)
    gate = jax.nn.sigmoid(jnp.dot(x1.astype(BF16), wgate_ref[...],
                                  preferred_element_type=F32))
    o_ref[...] = x1 + gate * e


def _epilogue(ya, yb, proj, x2, p2, wof, wos, wout, gpost, wple, wgate):
    t = x2.shape[0]
    za_col = _PROJ_OFF["z_a"] // FOX_WIDTH
    zb_col = _PROJ_OFF["z_b"] // SWA_WIDTH
    ga_col = _PROJ_OFF["g_a"] // D_MODEL
    gb_col = _PROJ_OFF["g_b"] // D_MODEL
    once = pl.Buffered(1)

    def const(shape):
        return pl.BlockSpec(shape, lambda i: (0, 0), pipeline_mode=once)

    return pl.pallas_call(
        _epilogue_kernel,
        grid=(t // EPI_TM,),
        in_specs=[
            pl.BlockSpec((EPI_TM, FOX_WIDTH), lambda i: (i, 0)),
            pl.BlockSpec((EPI_TM, FOX_WIDTH), lambda i: (i, za_col)),
            pl.BlockSpec((EPI_TM, SWA_WIDTH), lambda i: (i, 0)),
            pl.BlockSpec((EPI_TM, SWA_WIDTH), lambda i: (i, zb_col)),
            pl.BlockSpec((EPI_TM, D_MODEL), lambda i: (i, ga_col)),
            pl.BlockSpec((EPI_TM, D_MODEL), lambda i: (i, gb_col)),
            pl.BlockSpec((EPI_TM, D_MODEL), lambda i: (i, 0)),
            pl.BlockSpec((EPI_TM, PLE_DIM), lambda i: (i, 0)),
            const((FOX_WIDTH, D_MODEL)),
            const((SWA_WIDTH, D_MODEL)),
            const((D_MODEL, D_MODEL)),
            const((1, D_MODEL)),
            const((PLE_DIM, D_MODEL)),
            const((D_MODEL, D_MODEL)),
        ],
        out_specs=pl.BlockSpec((EPI_TM, D_MODEL), lambda i: (i, 0)),
        out_shape=jax.ShapeDtypeStruct((t, D_MODEL), F32),
        compiler_params=pltpu.CompilerParams(
            dimension_semantics=("arbitrary",),
            vmem_limit_bytes=56 * 1024 * 1024),
        name="epilogue",
    )(ya, proj, yb, proj, proj, proj, x2, p2, wof, wos, wout, gpost, wple, wgate)


def _layer(x2, p2, pos_row, batch, seq, pre_g, w_in, b_forget, sinks, w_o_fox, w_o_swa,
           w_out, post_g, w_ple, w_ple_gate):
    w_t = w_in.T
    f0 = _REF_OFF["f_a"]
    wf_t = jnp.pad(w_t[f0:f0 + FOX_HEADS], ((0, F_ROWS - FOX_HEADS), (0, 0))).astype(BF16)
    col_scale = jnp.where(jnp.arange(PROJ_COLS) < FOX_WIDTH, Q_A_SCALE, 1.0
                          ).astype(F32).reshape(1, PROJ_COLS)

    h, ft, proj_partial = _prenorm(x2, pre_g.reshape(1, D_MODEL), wf_t, w_t)
    proj = _inproj(h, w_t, col_scale, proj_partial)
    kb_tok = _forget_cumsum(ft, b_forget.reshape(FOX_HEADS, 1).astype(F32), batch, seq)
    ya = _fox_attention(proj, kb_tok, batch, seq)

    inv = ROPE_THETA ** (-jnp.arange(ROPE_HALF, dtype=F32) / ROPE_HALF)
    inv_tab = jnp.broadcast_to(inv[:, None], (ROPE_HALF, LANES))
    yb, wof, wos, wout, wple, wgate = _swa_attention(
        sinks.astype(F32), proj, pos_row, inv_tab, batch, seq,
        (w_o_fox, w_o_swa, w_out, w_ple, w_ple_gate))

    return _epilogue(ya, yb, proj, x2, p2, wof, wos, wout,
                     post_g.reshape(1, D_MODEL), wple, wgate)


def kernel(x, p, positions, pre_norm_g, w_in, b_forget, sinks, w_o_fox, w_o_swa, w_out,
           post_norm_g, w_ple, w_ple_gate):
    batch, seq, _ = x.shape
    depth = p.shape[0]
    x2 = x.reshape(batch * seq, D_MODEL)
    pos_row = positions.reshape(1, batch * seq)
    for i in range(depth):
        x2 = _layer(x2, p[i].reshape(batch * seq, PLE_DIM), pos_row, batch, seq,
                    pre_norm_g[i], w_in[i], b_forget[i], sinks[i], w_o_fox[i], w_o_swa[i],
                    w_out[i], post_norm_g[i], w_ple[i], w_ple_gate[i])
    return x2.reshape(batch, seq, D_MODEL)
```

```python
import math

import jax
import jax.numpy as jnp
from jax import lax
from jax.experimental import pallas as pl
from jax.experimental.pallas import tpu as pltpu

F32 = jnp.float32
BF16 = jnp.bfloat16

D_MODEL = 2048
FOX_HEADS = 8
FOX_HEAD_DIM = 128
FOX_WIDTH = FOX_HEADS * FOX_HEAD_DIM
SWA_Q_HEADS = 16
SWA_KV_HEADS = 2
SWA_HEAD_DIM = 64
SWA_WIDTH = SWA_Q_HEADS * SWA_HEAD_DIM
SWA_KV_WIDTH = SWA_KV_HEADS * SWA_HEAD_DIM
SWA_GROUP = SWA_Q_HEADS // SWA_KV_HEADS
ROPE_HALF = SWA_HEAD_DIM // 2
WINDOW = 128
ROPE_THETA = 10000.0
PLE_DIM = 256
NORM_EPS = 1e-6
LANES = 128
LOG2E = math.log2(math.e)
NEG_BIG = -1e30

_REF_SPLITS = (FOX_WIDTH, FOX_WIDTH, FOX_WIDTH, FOX_WIDTH, FOX_HEADS, SWA_WIDTH,
               SWA_KV_WIDTH, SWA_KV_WIDTH, SWA_WIDTH, D_MODEL, D_MODEL)
_REF_NAMES = ("q_a", "k_a", "v_a", "z_a", "f_a", "q_b", "k_b", "v_b", "z_b", "g_a", "g_b")
_REF_OFF = {}
_o = 0
for _n, _s in zip(_REF_NAMES, _REF_SPLITS):
    _REF_OFF[_n] = _o
    _o += _s

IN_TM = 1024
IN_TN = 1536
NORM_ROWS = 512
F_ROWS = 16

PREP_TN = 512
_W_GROUPS = (
    (_REF_OFF["q_a"], 4 * FOX_WIDTH),
    (_REF_OFF["q_b"], SWA_WIDTH),
    (_REF_OFF["z_b"], SWA_WIDTH),
    (_REF_OFF["g_a"], 2 * D_MODEL),
    (_REF_OFF["k_b"], 2 * SWA_KV_WIDTH),
)
_GROUP_TILES = tuple(-(-w // PREP_TN) for _, w in _W_GROUPS)
_GROUP_START = tuple(sum(_GROUP_TILES[:i]) for i in range(len(_W_GROUPS)))
PREP_TILES = sum(_GROUP_TILES)
PROJ_COLS = PREP_TILES * PREP_TN
_PROJ_OFF = {
    "q_a": 0, "k_a": FOX_WIDTH, "v_a": 2 * FOX_WIDTH, "z_a": 3 * FOX_WIDTH,
    "q_b": _GROUP_START[1] * PREP_TN,
    "z_b": _GROUP_START[2] * PREP_TN,
    "g_a": _GROUP_START[3] * PREP_TN, "g_b": _GROUP_START[3] * PREP_TN + D_MODEL,
    "k_b": _GROUP_START[4] * PREP_TN, "v_b": _GROUP_START[4] * PREP_TN + SWA_KV_WIDTH,
}
SUBLANES = 8
assert all(c0 % SUBLANES == 0 for c0, _ in _W_GROUPS)
W_PER_TILE = IN_TN // PREP_TN
LAST_TILE = PROJ_COLS // IN_TN - 1
_NT = (((1,), (1,)), ((), ()))
Q_A_SCALE = FOX_HEAD_DIM ** -0.5 * LOG2E

FOX_T = 512
FOX_CHUNK = 512
FOX_PAIR = 4
BIAS_TERMS = 3
ONES_ROWS = 16

SWA_NW = 8
SWA_QB = SWA_NW * WINDOW

EPI_TM = 256


def _cast_windows(w_refs, wb_ref):
    for k, w_ref in enumerate(w_refs):
        for r in range(PREP_TN // LANES):
            rows = slice(r * LANES, (r + 1) * LANES)
            wb_ref[k * PREP_TN + r * LANES:k * PREP_TN + (r + 1) * LANES, :] = (
                w_ref[rows, :].astype(BF16))


def _prenorm_kernel(x_ref, g_ref, wf_ref, *refs):
    w_refs = refs[:W_PER_TILE]
    h_ref, ft_ref, proj_ref, wb_ref = refs[W_PER_TILE:]

    @pl.when(pl.program_id(0) == 0)
    def _():
        _cast_windows(w_refs, wb_ref)

    x = x_ref[...]
    ms = jnp.mean(x * x, axis=-1, keepdims=True)
    h = ((x * lax.rsqrt(ms + NORM_EPS)) * g_ref[...]).astype(BF16)
    h_ref[...] = h
    ft_ref[...] = lax.dot_general(wf_ref[...], h, _NT, preferred_element_type=F32)
    proj_ref[...] = lax.dot_general(h, wb_ref[...], _NT,
                                    preferred_element_type=F32).astype(BF16)


def _prenorm(x2, g, wf_t, w_t):
    t = x2.shape[0]
    once = pl.Buffered(1)

    def w_spec(k):
        return pl.BlockSpec((pl.Element(PREP_TN), pl.Element(D_MODEL)),
                            lambda i: (_w_src_row(LAST_TILE * W_PER_TILE + k), 0),
                            pipeline_mode=once)

    return pl.pallas_call(
        _prenorm_kernel,
        grid=(t // NORM_ROWS,),
        in_specs=[
            pl.BlockSpec((NORM_ROWS, D_MODEL), lambda i: (i, 0)),
            pl.BlockSpec((1, D_MODEL), lambda i: (0, 0)),
            pl.BlockSpec((F_ROWS, D_MODEL), lambda i: (0, 0)),
            *[w_spec(k) for k in range(W_PER_TILE)],
        ],
        out_specs=[
            pl.BlockSpec((NORM_ROWS, D_MODEL), lambda i: (i, 0)),
            pl.BlockSpec((F_ROWS, NORM_ROWS), lambda i: (0, i)),
            pl.BlockSpec((NORM_ROWS, IN_TN), lambda i: (i, LAST_TILE)),
        ],
        out_shape=[
            jax.ShapeDtypeStruct((t, D_MODEL), BF16),
            jax.ShapeDtypeStruct((F_ROWS, t), F32),
            jax.ShapeDtypeStruct((t, PROJ_COLS), BF16),
        ],
        scratch_shapes=[pltpu.VMEM((IN_TN, D_MODEL), BF16)],
        compiler_params=pltpu.CompilerParams(
            dimension_semantics=("arbitrary",),
            vmem_limit_bytes=56 * 1024 * 1024),
        name="prenorm",
    )(x2, g, wf_t, *([w_t] * W_PER_TILE))


def _w_src_row(tile):
    tile_row = jnp.int32(0)
    for (c0, _), start in zip(_W_GROUPS, _GROUP_START):
        tile_row = jnp.where(tile >= start,
                             c0 // SUBLANES + (tile - start) * (PREP_TN // SUBLANES), tile_row)
    return tile_row * SUBLANES


def _inproj_kernel(h_ref, *refs):
    w_refs = refs[:W_PER_TILE]
    scale_ref, _, proj_ref, wb_ref = refs[W_PER_TILE:]

    @pl.when(pl.program_id(1) == 0)
    def _():
        _cast_windows(w_refs, wb_ref)

    acc = lax.dot_general(h_ref[...], wb_ref[...], _NT, preferred_element_type=F32)
    proj_ref[...] = (acc * scale_ref[...]).astype(BF16)


def _inproj(h, w_t, col_scale, proj_partial):
    t = h.shape[0]

    def w_spec(k):
        return pl.BlockSpec((pl.Element(PREP_TN), pl.Element(D_MODEL)),
                            lambda j, i: (_w_src_row(j * W_PER_TILE + k), 0))

    n_in = 1 + W_PER_TILE + 1
    return pl.pallas_call(
        _inproj_kernel,
        grid=(LAST_TILE, t // IN_TM),
        in_specs=[
            pl.BlockSpec((IN_TM, D_MODEL), lambda j, i: (i, 0)),
            *[w_spec(k) for k in range(W_PER_TILE)],
            pl.BlockSpec((1, IN_TN), lambda j, i: (0, j)),
            pl.BlockSpec(memory_space=pl.ANY),
        ],
        out_specs=pl.BlockSpec((IN_TM, IN_TN), lambda j, i: (i, j)),
        out_shape=jax.ShapeDtypeStruct((t, PROJ_COLS), BF16),
        input_output_aliases={n_in: 0},
        scratch_shapes=[pltpu.VMEM((IN_TN, D_MODEL), BF16)],
        compiler_params=pltpu.CompilerParams(
            dimension_semantics=("arbitrary", "arbitrary"),
            vmem_limit_bytes=56 * 1024 * 1024),
        name="inproj",
    )(h, *([w_t] * W_PER_TILE), col_scale, proj_partial)


def _cumsum_kernel(ft_ref, b_ref, kb_ref):
    f = ft_ref[0:FOX_HEADS, :] + b_ref[...]
    lf = jnp.minimum(f, 0.0) - jnp.log1p(jnp.exp(-jnp.abs(f)))
    seq = lf.shape[1]
    lane = lax.broadcasted_iota(jnp.int32, lf.shape, 1)
    c = lf
    shift = 1
    while shift < seq:
        c = c + jnp.where(lane >= shift, pltpu.roll(c, shift, axis=1), 0.0)
        shift *= 2
    rest = c * (-LOG2E)
    pieces = []
    for _ in range(BIAS_TERMS):
        piece = rest.astype(BF16).astype(F32)
        pieces.append(piece)
        rest = rest - piece
    pieces.append(jnp.zeros((LANES - BIAS_TERMS * FOX_HEADS, seq), F32))
    kb_ref[...] = jnp.concatenate(pieces, axis=0).T.astype(BF16)


def _forget_cumsum(ft, b_col, batch, seq):
    return pl.pallas_call(
        _cumsum_kernel,
        grid=(batch,),
        in_specs=[
            pl.BlockSpec((F_ROWS, seq), lambda b: (0, b)),
            pl.BlockSpec((FOX_HEADS, 1), lambda b: (0, 0)),
        ],
        out_specs=pl.BlockSpec((seq, LANES), lambda b: (b, 0)),
        out_shape=jax.ShapeDtypeStruct((batch * seq, LANES), BF16),
        name="forget_cumsum",
    )(ft, b_col)


def _fox_kernel(q_ref, k_ref, v_ref, kb_ref, o_ref, vt_ref, qt_ref, acc_ref,
                sa_ref, ma_ref, sb_ref, mb_ref):
    s0_ref, s1_ref = (sa_ref, ma_ref), (sb_ref, mb_ref)
    hp = pl.program_id(1)
    seq = k_ref.shape[0]
    t = FOX_T
    d = FOX_HEAD_DIM

    def transposed(x):
        return x.astype(F32).T.astype(BF16)

    row = lax.broadcasted_iota(jnp.int32, (d, t), 0)
    for hh in range(FOX_PAIR):
        mine = (row < BIAS_TERMS * FOX_HEADS) & (row % FOX_HEADS == hp * FOX_PAIR + hh)
        qt_ref[hh, d:2 * d, :] = jnp.where(mine, 1.0, 0.0).astype(BF16)
        vt_ref[hh, d:, :] = jnp.ones((ONES_ROWS, seq), BF16)
        for r in range(seq // FOX_CHUNK):
            rows = slice(r * FOX_CHUNK, (r + 1) * FOX_CHUNK)
            vt_ref[hh, 0:d, rows] = transposed(v_ref[rows, hh * d:(hh + 1) * d])

    def scores(i, buf):
        s_ref, max_ref = buf
        rows = pl.ds(pl.multiple_of(i * t, t), t)
        for hh in range(FOX_PAIR):
            k_aug = jnp.concatenate([k_ref[rows, hh * d:(hh + 1) * d], kb_ref[rows, :]], axis=1)
            s = jnp.dot(k_aug, qt_ref[hh], preferred_element_type=F32)
            s_ref[hh] = s
            max_ref[hh] = jnp.max(s, axis=0, keepdims=True)

    def absorb(i, buf, carry, masked):
        s_ref, max_ref = buf
        rows = pl.ds(pl.multiple_of(i * t, t), t)
        new = []
        for hh in range(FOX_PAIR):
            m = carry[hh]
            s = s_ref[hh]
            if masked:
                key = lax.broadcasted_iota(jnp.int32, s.shape, 0)
                qry = lax.broadcasted_iota(jnp.int32, s.shape, 1)
                s = jnp.where(key <= qry, s, NEG_BIG)
                tile_max = jnp.max(s, axis=0, keepdims=True)
            else:
                tile_max = max_ref[hh]
            m_new = jnp.maximum(m, tile_max)
            alpha = jnp.exp2(m - m_new)
            p = jnp.exp2(s - m_new)
            acc_ref[hh] = alpha * acc_ref[hh] + jnp.dot(
                vt_ref[hh, :, rows], p.astype(BF16), preferred_element_type=F32)
            new.append(m_new)
        return tuple(new)

    def step(i, s_cur, s_next, carry):
        scores(i + 1, s_next)
        return absorb(i, s_cur, carry, False)

    def pair(j, carry):
        carry = step(2 * j, s0_ref, s1_ref, carry)
        return step(2 * j + 1, s1_ref, s0_ref, carry)

    def q_tile(qi, _):
        q_rows = pl.ds(pl.multiple_of(qi * t, t), t)
        for hh in range(FOX_PAIR):
            qt_ref[hh, 0:d, :] = transposed(q_ref[q_rows, hh * d:(hh + 1) * d])
            acc_ref[hh] = jnp.zeros((d + ONES_ROWS, t), F32)

        def odd_tail(carry):
            carry = step(qi - 1, s0_ref, s1_ref, carry)
            return absorb(qi, s1_ref, carry, True)

        def even_tail(carry):
            return absorb(qi, s0_ref, carry, True)

        init = tuple(jnp.full((1, t), NEG_BIG, F32) for _ in range(FOX_PAIR))
        scores(0, s0_ref)
        carry = lax.fori_loop(0, qi // 2, pair, init)
        carry = lax.cond(qi % 2 == 1, odd_tail, even_tail, carry)
        for hh in range(FOX_PAIR):
            o_ref[q_rows, hh * d:(hh + 1) * d] = (
                acc_ref[hh, 0:d, :] / acc_ref[hh, d:d + 1, :]).T.astype(o_ref.dtype)
        return 0

    lax.fori_loop(0, seq // t, q_tile, 0)


def _fox_attention(proj, kb_tok, batch, seq):
    width = FOX_PAIR * FOX_HEAD_DIM
    qcol = _PROJ_OFF["q_a"] // width
    kcol = _PROJ_OFF["k_a"] // width
    vcol = _PROJ_OFF["v_a"] // width
    return pl.pallas_call(
        _fox_kernel,
        grid=(batch, FOX_HEADS // FOX_PAIR),
        in_specs=[
            pl.BlockSpec((seq, width), lambda b, h: (b, qcol + h)),
            pl.BlockSpec((seq, width), lambda b, h: (b, kcol + h)),
            pl.BlockSpec((seq, width), lambda b, h: (b, vcol + h)),
            pl.BlockSpec((seq, LANES), lambda b, h: (b, 0)),
        ],
        out_specs=pl.BlockSpec((seq, width), lambda b, h: (b, h)),
        out_shape=jax.ShapeDtypeStruct((batch * seq, FOX_WIDTH), BF16),
        scratch_shapes=[
            pltpu.VMEM((FOX_PAIR, FOX_HEAD_DIM + ONES_ROWS, seq), BF16),
            pltpu.VMEM((FOX_PAIR, 2 * FOX_HEAD_DIM, FOX_T), BF16),
            pltpu.VMEM((FOX_PAIR, FOX_HEAD_DIM + ONES_ROWS, FOX_T), F32),
            pltpu.VMEM((FOX_PAIR, FOX_T, FOX_T), F32),
            pltpu.VMEM((FOX_PAIR, 1, FOX_T), F32),
            pltpu.VMEM((FOX_PAIR, FOX_T, FOX_T), F32),
            pltpu.VMEM((FOX_PAIR, 1, FOX_T), F32),
        ],
        compiler_params=pltpu.CompilerParams(
            dimension_semantics=("arbitrary", "arbitrary"),
            vmem_limit_bytes=56 * 1024 * 1024),
        name="fox_attention",
    )(proj, proj, proj, kb_tok)


def _rope_t(xt, cos, sin):
    out = []
    for hd in range(xt.shape[0] // SWA_HEAD_DIM):
        x1 = xt[hd * SWA_HEAD_DIM: hd * SWA_HEAD_DIM + ROPE_HALF]
        x2 = xt[hd * SWA_HEAD_DIM + ROPE_HALF: (hd + 1) * SWA_HEAD_DIM]
        out.append(x1 * cos - x2 * sin)
        out.append(x2 * cos + x1 * sin)
    return jnp.concatenate(out, axis=0)


def _swa_kernel(sink_ref, q_ref, kc_ref, kp_ref, vc_ref, vp_ref, posc_ref, posp_ref,
                inv_ref, *rest):
    n_cast = (len(rest) - 3) // 2
    cast_in, o_ref = rest[:n_cast], rest[n_cast]
    cast_out, (band_ref, eye_ref) = rest[n_cast + 1:2 * n_cast + 1], rest[-2:]
    for src_ref, dst_ref in zip(cast_in, cast_out):
        dst_ref[...] = src_ref[...].astype(dst_ref.dtype)

    n = pl.program_id(1)
    w = WINDOW
    inv = inv_ref[...]

    def tables(pos_row):
        ang = inv * pos_row.astype(F32)
        return jnp.cos(ang), jnp.sin(ang)

    def rope_k(k_nat, cos, sin):
        kt = _rope_t(k_nat.astype(F32).T, cos, sin)
        return kt.T.astype(BF16)

    cos_p, sin_p = tables(posp_ref[...])
    k_prev = rope_k(kp_ref[...], cos_p, sin_p)
    v_prev = vp_ref[...]

    @pl.when((pl.program_id(0) == 0) & (n == 0))
    def _():
        key = lax.broadcasted_iota(jnp.int32, band_ref.shape, 0)
        qry = lax.broadcasted_iota(jnp.int32, band_ref.shape, 1)
        band_ref[...] = jnp.where((key <= qry + w) & (key > qry), 0.0, NEG_BIG).astype(BF16)
        src = lax.broadcasted_iota(jnp.int32, eye_ref.shape, 0)
        dst = lax.broadcasted_iota(jnp.int32, eye_ref.shape, 1) % w
        eye_ref[...] = jnp.where(src == dst, 1.0, 0.0).astype(BF16)

    no_prev = jnp.where(n > 0, 0.0, NEG_BIG)
    q_scale = SWA_HEAD_DIM ** -0.5 * LOG2E
    zeros_half = jnp.zeros((SWA_HEAD_DIM, SWA_GROUP * w), F32)
    ones_rows = jnp.ones((2 * SUBLANES, 2 * w), BF16)

    sinks = [jnp.concatenate(
        [jnp.full((1, w), sink_ref[hd] * LOG2E, F32)
         for hd in range(g * SWA_GROUP, (g + 1) * SWA_GROUP)], axis=1)
        for g in range(SWA_KV_HEADS)]

    def logits(wi, k_prev):
        tok = slice(wi * w, (wi + 1) * w)
        cos, sin = tables(posc_ref[:, tok])
        k_cur = rope_k(kc_ref[tok, :], cos, sin)
        k_all = jnp.concatenate([k_prev, k_cur], axis=0)
        k_aug = jnp.concatenate([k_all, band_ref[...]], axis=1)
        qf = q_ref[tok, :].astype(F32)
        qt = jnp.concatenate(
            [qf[:, c * LANES:(c + 1) * LANES].T for c in range(SWA_WIDTH // LANES)], axis=0)
        qt = _rope_t(qt, cos * q_scale, sin * q_scale)
        out = []
        for g in range(SWA_KV_HEADS):
            heads = range(g * SWA_GROUP, (g + 1) * SWA_GROUP)
            q_g = jnp.concatenate(
                [qt[hd * SWA_HEAD_DIM:(hd + 1) * SWA_HEAD_DIM] for hd in heads], axis=1)
            parts = [zeros_half] * SWA_KV_HEADS
            parts[g] = q_g
            q_z = jnp.concatenate(parts, axis=0).astype(BF16)
            q_aug = jnp.concatenate([q_z, eye_ref[...]], axis=0)
            s = jnp.dot(k_aug, q_aug, preferred_element_type=F32)
            if wi == 0:
                s = jnp.concatenate([s[:w] + no_prev, s[w:]], axis=0)
            out.append(s)
        return out, k_cur

    def finish(wi, scores, v_prev):
        tok = slice(wi * w, (wi + 1) * w)
        v_cur = vc_ref[tok, :]
        v_all_t = jnp.concatenate([v_prev, v_cur], axis=0).astype(F32).T.astype(BF16)
        for g in range(SWA_KV_HEADS):
            s, sink = scores[g], sinks[g]
            m = jnp.maximum(jnp.max(s, axis=0, keepdims=True), sink)
            e = jnp.exp2(s - m).astype(BF16)
            v_aug = jnp.concatenate(
                [v_all_t[g * SWA_HEAD_DIM:(g + 1) * SWA_HEAD_DIM], ones_rows], axis=0)
            pv = jnp.dot(v_aug, e, preferred_element_type=F32)
            denom = pv[SWA_HEAD_DIM:SWA_HEAD_DIM + 1] + jnp.exp2(sink - m)
            out_t = pv[:SWA_HEAD_DIM] / denom
            for jj in range(SWA_GROUP // 2):
                pair = jnp.concatenate(
                    [out_t[:, (2 * jj) * w:(2 * jj + 1) * w],
                     out_t[:, (2 * jj + 1) * w:(2 * jj + 2) * w]], axis=0)
                col = (g * (SWA_GROUP // 2) + jj) * LANES
                o_ref[tok, col:col + LANES] = pair.T.astype(o_ref.dtype)
        return v_cur

    pending, k_prev = logits(0, k_prev)
    for wi in range(SWA_NW):
        if wi + 1 < SWA_NW:
            upcoming, k_prev = logits(wi + 1, k_prev)
        v_prev = finish(wi, pending, v_prev)
        if wi + 1 < SWA_NW:
            pending = upcoming


def _swa_attention(sinks, proj, pos_row, inv_tab, batch, seq, weights_f32):
    nb = seq // SWA_QB
    per = SWA_QB // WINDOW
    qcol = _PROJ_OFF["q_b"] // SWA_WIDTH
    kcol = _PROJ_OFF["k_b"] // LANES
    vcol = _PROJ_OFF["v_b"] // LANES
    steps = batch * nb

    def cur(b, n):
        return b * nb + n

    def prev(b, n):
        return jnp.maximum((b * nb + n) * per - 1, 0)

    def chunk_spec(wgt):
        rows, cols = wgt.shape
        assert rows % (steps * 2 * SUBLANES) == 0
        return pl.BlockSpec((rows // steps, cols), lambda b, n: (cur(b, n), 0))

    cast_specs = [chunk_spec(wgt) for wgt in weights_f32]
    return pl.pallas_call(
        _swa_kernel,
        grid=(batch, nb),
        in_specs=[
            pl.BlockSpec(memory_space=pltpu.SMEM),
            pl.BlockSpec((SWA_QB, SWA_WIDTH), lambda b, n: (cur(b, n), qcol)),
            pl.BlockSpec((SWA_QB, LANES), lambda b, n: (cur(b, n), kcol)),
            pl.BlockSpec((WINDOW, LANES), lambda b, n: (prev(b, n), kcol)),
            pl.BlockSpec((SWA_QB, LANES), lambda b, n: (cur(b, n), vcol)),
            pl.BlockSpec((WINDOW, LANES), lambda b, n: (prev(b, n), vcol)),
            pl.BlockSpec((1, SWA_QB), lambda b, n: (0, cur(b, n))),
            pl.BlockSpec((1, WINDOW), lambda b, n: (0, prev(b, n))),
            pl.BlockSpec((ROPE_HALF, LANES), lambda b, n: (0, 0)),
            *cast_specs,
        ],
        out_specs=[pl.BlockSpec((SWA_QB, SWA_WIDTH), lambda b, n: (cur(b, n), 0)), *cast_specs],
        out_shape=[jax.ShapeDtypeStruct((batch * seq, SWA_WIDTH), BF16),
                   *[jax.ShapeDtypeStruct(wgt.shape, BF16) for wgt in weights_f32]],
        scratch_shapes=[
            pltpu.VMEM((2 * WINDOW, WINDOW), BF16),
            pltpu.VMEM((WINDOW, SWA_GROUP * WINDOW), BF16),
        ],
        compiler_params=pltpu.CompilerParams(
            dimension_semantics=("arbitrary", "arbitrary")),
        name="swa_attention",
    )(sinks, proj, proj, proj, proj, proj, pos_row, pos_row, inv_tab, *weights_f32)


def _silu(z):
    return z * jax.nn.sigmoid(z)


def _epilogue_kernel(ya_ref, za_ref, yb_ref, zb_ref, ga_ref, gb_ref, x_ref, p_ref,
                     wof_ref, wos_ref, wout_ref, gpost_ref, wple_ref, wgate_ref, o_ref):
    ua = (ya_ref[...].astype(F32) * _silu(za_ref[...].astype(F32))).astype(BF16)
    ub = (yb_ref[...].astype(F32) * _silu(zb_ref[...].astype(F32))).astype(BF16)
    oa = jnp.dot(ua, wof_ref[...], preferred_element_type=F32)
    ob = jnp.dot(ub, wos_ref[...], preferred_element_type=F32)
    merged = (jax.nn.sigmoid(ga_ref[...].astype(F32)) * oa
              + jax.nn.sigmoid(gb_ref[...].astype(F32)) * ob)
    out = jnp.dot(merged.astype(BF16), wout_ref[...], preferred_element_type=F32)
    ms = jnp.mean(out * out, axis=-1, keepdims=True)
    x1 = x_ref[...] + (out * lax.rsqrt(ms + NORM_EPS)) * gpost_ref[...]
    e = jnp.dot(p_ref[...].astype(BF16), wple_ref[...], preferred_element_type=F32)
    gate = jax.nn.sigmoid(jnp.dot(x1.astype(BF16), wgate_ref[...],
                                  preferred_element_type=F32))
    o_ref[...] = x1 + gate * e


def _epilogue(ya, yb, proj, x2, p2, wof, wos, wout, gpost, wple, wgate):
    t = x2.shape[0]
    za_col = _PROJ_OFF["z_a"] // FOX_WIDTH
    zb_col = _PROJ_OFF["z_b"] // SWA_WIDTH
    ga_col = _PROJ_OFF["g_a"] // D_MODEL
    gb_col = _PROJ_OFF["g_b"] // D_MODEL
    once = pl.Buffered(1)

    def const(shape):
        return pl.BlockSpec(shape, lambda i: (0, 0), pipeline_mode=once)

    return pl.pallas_call(
        _epilogue_kernel,
        grid=(t // EPI_TM,),
        in_specs=[
            pl.BlockSpec((EPI_TM, FOX_WIDTH), lambda i: (i, 0)),
            pl.BlockSpec((EPI_TM, FOX_WIDTH), lambda i: (i, za_col)),
            pl.BlockSpec((EPI_TM, SWA_WIDTH), lambda i: (i, 0)),
            pl.BlockSpec((EPI_TM, SWA_WIDTH), lambda i: (i, zb_col)),
            pl.BlockSpec((EPI_TM, D_MODEL), lambda i: (i, ga_col)),
            pl.BlockSpec((EPI_TM, D_MODEL), lambda i: (i, gb_col)),
            pl.BlockSpec((EPI_TM, D_MODEL), lambda i: (i, 0)),
            pl.BlockSpec((EPI_TM, PLE_DIM), lambda i: (i, 0)),
            const((FOX_WIDTH, D_MODEL)),
            const((SWA_WIDTH, D_MODEL)),
            const((D_MODEL, D_MODEL)),
            const((1, D_MODEL)),
            const((PLE_DIM, D_MODEL)),
            const((D_MODEL, D_MODEL)),
        ],
        out_specs=pl.BlockSpec((EPI_TM, D_MODEL), lambda i: (i, 0)),
        out_shape=jax.ShapeDtypeStruct((t, D_MODEL), F32),
        compiler_params=pltpu.CompilerParams(
            dimension_semantics=("arbitrary",),
            vmem_limit_bytes=56 * 1024 * 1024),
        name="epilogue",
    )(ya, proj, yb, proj, proj, proj, x2, p2, wof, wos, wout, gpost, wple, wgate)


def _layer(x2, p2, pos_row, batch, seq, pre_g, w_in, b_forget, sinks, w_o_fox, w_o_swa,
           w_out, post_g, w_ple, w_ple_gate):
    w_t = w_in.T
    f0 = _REF_OFF["f_a"]
    wf_t = jnp.pad(w_t[f0:f0 + FOX_HEADS], ((0, F_ROWS - FOX_HEADS), (0, 0))).astype(BF16)
    col_scale = jnp.where(jnp.arange(PROJ_COLS) < FOX_WIDTH, Q_A_SCALE, 1.0
                          ).astype(F32).reshape(1, PROJ_COLS)

    h, ft, proj_partial = _prenorm(x2, pre_g.reshape(1, D_MODEL), wf_t, w_t)
    proj = _inproj(h, w_t, col_scale, proj_partial)
    kb_tok = _forget_cumsum(ft, b_forget.reshape(FOX_HEADS, 1).astype(F32), batch, seq)
    ya = _fox_attention(proj, kb_tok, batch, seq)

    inv = ROPE_THETA ** (-jnp.arange(ROPE_HALF, dtype=F32) / ROPE_HALF)
    inv_tab = jnp.broadcast_to(inv[:, None], (ROPE_HALF, LANES))
    yb, wof, wos, wout, wple, wgate = _swa_attention(
        sinks.astype(F32), proj, pos_row, inv_tab, batch, seq,
        (w_o_fox, w_o_swa, w_out, w_ple, w_ple_gate))

    return _epilogue(ya, yb, proj, x2, p2, wof, wos, wout,
                     post_g.reshape(1, D_MODEL), wple, wgate)


def kernel(x, p, positions, pre_norm_g, w_in, b_forget, sinks, w_o_fox, w_o_swa, w_out,
           post_norm_g, w_ple, w_ple_gate):
    batch, seq, _ = x.shape
    depth = p.shape[0]
    x2 = x.reshape(batch * seq, D_MODEL)
    pos_row = positions.reshape(1, batch * seq)
    for i in range(depth):
        x2 = _layer(x2, p[i].reshape(batch * seq, PLE_DIM), pos_row, batch, seq,
                    pre_norm_g[i], w_in[i], b_forget[i], sinks[i], w_o_fox[i], w_o_swa[i],
                    w_out[i], post_norm_g[i], w_ple[i], w_ple_gate[i])
    return x2.reshape(batch, seq, D_MODEL)
```

```python
import math

import jax
import jax.numpy as jnp
from jax import lax
from jax.experimental import pallas as pl
from jax.experimental.pallas import tpu as pltpu

F32 = jnp.float32
BF16 = jnp.bfloat16

D_MODEL = 2048
FOX_HEADS = 8
FOX_HEAD_DIM = 128
FOX_WIDTH = FOX_HEADS * FOX_HEAD_DIM
SWA_Q_HEADS = 16
SWA_KV_HEADS = 2
SWA_HEAD_DIM = 64
SWA_WIDTH = SWA_Q_HEADS * SWA_HEAD_DIM
SWA_KV_WIDTH = SWA_KV_HEADS * SWA_HEAD_DIM
SWA_GROUP = SWA_Q_HEADS // SWA_KV_HEADS
ROPE_HALF = SWA_HEAD_DIM // 2
WINDOW = 128
ROPE_THETA = 10000.0
PLE_DIM = 256
NORM_EPS = 1e-6
LANES = 128
LOG2E = math.log2(math.e)
NEG_BIG = -1e30

_REF_SPLITS = (FOX_WIDTH, FOX_WIDTH, FOX_WIDTH, FOX_WIDTH, FOX_HEADS, SWA_WIDTH,
               SWA_KV_WIDTH, SWA_KV_WIDTH, SWA_WIDTH, D_MODEL, D_MODEL)
_REF_NAMES = ("q_a", "k_a", "v_a", "z_a", "f_a", "q_b", "k_b", "v_b", "z_b", "g_a", "g_b")
_REF_OFF = {}
_o = 0
for _n, _s in zip(_REF_NAMES, _REF_SPLITS):
    _REF_OFF[_n] = _o
    _o += _s

IN_TM = 1024
IN_TN = 1536
NORM_ROWS = 512
F_ROWS = 16

PREP_TN = 512
_W_GROUPS = (
    (_REF_OFF["q_a"], 4 * FOX_WIDTH),
    (_REF_OFF["q_b"], SWA_WIDTH),
    (_REF_OFF["z_b"], SWA_WIDTH),
    (_REF_OFF["g_a"], 2 * D_MODEL),
    (_REF_OFF["k_b"], 2 * SWA_KV_WIDTH),
)
_GROUP_TILES = tuple(-(-w // PREP_TN) for _, w in _W_GROUPS)
_GROUP_START = tuple(sum(_GROUP_TILES[:i]) for i in range(len(_W_GROUPS)))
PREP_TILES = sum(_GROUP_TILES)
PROJ_COLS = PREP_TILES * PREP_TN
_PROJ_OFF = {
    "q_a": 0, "k_a": FOX_WIDTH, "v_a": 2 * FOX_WIDTH, "z_a": 3 * FOX_WIDTH,
    "q_b": _GROUP_START[1] * PREP_TN,
    "z_b": _GROUP_START[2] * PREP_TN,
    "g_a": _GROUP_START[3] * PREP_TN, "g_b": _GROUP_START[3] * PREP_TN + D_MODEL,
    "k_b": _GROUP_START[4] * PREP_TN, "v_b": _GROUP_START[4] * PREP_TN + SWA_KV_WIDTH,
}
SUBLANES = 8
assert all(c0 % SUBLANES == 0 for c0, _ in _W_GROUPS)
W_PER_TILE = IN_TN // PREP_TN
LAST_TILE = PROJ_COLS // IN_TN - 1
_NT = (((1,), (1,)), ((), ()))
Q_A_SCALE = FOX_HEAD_DIM ** -0.5 * LOG2E

FOX_T = 512
FOX_CHUNK = 512
FOX_PAIR = 2
BIAS_TERMS = 3
ONES_ROWS = 16

SWA_NW = 8
SWA_QB = SWA_NW * WINDOW

EPI_TM = 256


def _cast_windows(w_refs, wb_ref):
    for k, w_ref in enumerate(w_refs):
        for r in range(PREP_TN // LANES):
            rows = slice(r * LANES, (r + 1) * LANES)
            wb_ref[k * PREP_TN + r * LANES:k * PREP_TN + (r + 1) * LANES, :] = (
                w_ref[rows, :].astype(BF16))


def _prenorm_kernel(x_ref, g_ref, wf_ref, *refs):
    w_refs = refs[:W_PER_TILE]
    h_ref, ft_ref, proj_ref, wb_ref = refs[W_PER_TILE:]

    @pl.when(pl.program_id(0) == 0)
    def _():
        _cast_windows(w_refs, wb_ref)

    x = x_ref[...]
    ms = jnp.mean(x * x, axis=-1, keepdims=True)
    h = ((x * lax.rsqrt(ms + NORM_EPS)) * g_ref[...]).astype(BF16)
    h_ref[...] = h
    ft_ref[...] = lax.dot_general(wf_ref[...], h, _NT, preferred_element_type=F32)
    proj_ref[...] = lax.dot_general(h, wb_ref[...], _NT,
                                    preferred_element_type=F32).astype(BF16)


def _prenorm(x2, g, wf_t, w_t):
    t = x2.shape[0]
    once = pl.Buffered(1)

    def w_spec(k):
        return pl.BlockSpec((pl.Element(PREP_TN), pl.Element(D_MODEL)),
                            lambda i: (_w_src_row(LAST_TILE * W_PER_TILE + k), 0),
                            pipeline_mode=once)

    return pl.pallas_call(
        _prenorm_kernel,
        grid=(t // NORM_ROWS,),
        in_specs=[
            pl.BlockSpec((NORM_ROWS, D_MODEL), lambda i: (i, 0)),
            pl.BlockSpec((1, D_MODEL), lambda i: (0, 0)),
            pl.BlockSpec((F_ROWS, D_MODEL), lambda i: (0, 0)),
            *[w_spec(k) for k in range(W_PER_TILE)],
        ],
        out_specs=[
            pl.BlockSpec((NORM_ROWS, D_MODEL), lambda i: (i, 0)),
            pl.BlockSpec((F_ROWS, NORM_ROWS), lambda i: (0, i)),
            pl.BlockSpec((NORM_ROWS, IN_TN), lambda i: (i, LAST_TILE)),
        ],
        out_shape=[
            jax.ShapeDtypeStruct((t, D_MODEL), BF16),
            jax.ShapeDtypeStruct((F_ROWS, t), F32),
            jax.ShapeDtypeStruct((t, PROJ_COLS), BF16),
        ],
        scratch_shapes=[pltpu.VMEM((IN_TN, D_MODEL), BF16)],
        compiler_params=pltpu.CompilerParams(
            dimension_semantics=("arbitrary",),
            vmem_limit_bytes=56 * 1024 * 1024),
        name="prenorm",
    )(x2, g, wf_t, *([w_t] * W_PER_TILE))


def _w_src_row(tile):
    tile_row = jnp.int32(0)
    for (c0, _), start in zip(_W_GROUPS, _GROUP_START):
        tile_row = jnp.where(tile >= start,
                             c0 // SUBLANES + (tile - start) * (PREP_TN // SUBLANES), tile_row)
    return tile_row * SUBLANES


def _inproj_kernel(h_ref, *refs):
    w_refs = refs[:W_PER_TILE]
    scale_ref, _, proj_ref, wb_ref = refs[W_PER_TILE:]

    @pl.when(pl.program_id(1) == 0)
    def _():
        _cast_windows(w_refs, wb_ref)

    acc = lax.dot_general(h_ref[...], wb_ref[...], _NT, preferred_element_type=F32)
    proj_ref[...] = (acc * scale_ref[...]).astype(BF16)


def _inproj(h, w_t, col_scale, proj_partial):
    t = h.shape[0]

    def w_spec(k):
        return pl.BlockSpec((pl.Element(PREP_TN), pl.Element(D_MODEL)),
                            lambda j, i: (_w_src_row(j * W_PER_TILE + k), 0))

    n_in = 1 + W_PER_TILE + 1
    return pl.pallas_call(
        _inproj_kernel,
        grid=(LAST_TILE, t // IN_TM),
        in_specs=[
            pl.BlockSpec((IN_TM, D_MODEL), lambda j, i: (i, 0)),
            *[w_spec(k) for k in range(W_PER_TILE)],
            pl.BlockSpec((1, IN_TN), lambda j, i: (0, j)),
            pl.BlockSpec(memory_space=pl.ANY),
        ],
        out_specs=pl.BlockSpec((IN_TM, IN_TN), lambda j, i: (i, j)),
        out_shape=jax.ShapeDtypeStruct((t, PROJ_COLS), BF16),
        input_output_aliases={n_in: 0},
        scratch_shapes=[pltpu.VMEM((IN_TN, D_MODEL), BF16)],
        compiler_params=pltpu.CompilerParams(
            dimension_semantics=("arbitrary", "arbitrary"),
            vmem_limit_bytes=56 * 1024 * 1024),
        name="inproj",
    )(h, *([w_t] * W_PER_TILE), col_scale, proj_partial)


def _cumsum_kernel(ft_ref, b_ref, kb_ref):
    f = ft_ref[0:FOX_HEADS, :] + b_ref[...]
    lf = jnp.minimum(f, 0.0) - jnp.log1p(jnp.exp(-jnp.abs(f)))
    seq = lf.shape[1]
    lane = lax.broadcasted_iota(jnp.int32, lf.shape, 1)
    c = lf
    shift = 1
    while shift < seq:
        c = c + jnp.where(lane >= shift, pltpu.roll(c, shift, axis=1), 0.0)
        shift *= 2
    rest = c * (-LOG2E)
    pieces = []
    for _ in range(BIAS_TERMS):
        piece = rest.astype(BF16).astype(F32)
        pieces.append(piece)
        rest = rest - piece
    pieces.append(jnp.zeros((LANES - BIAS_TERMS * FOX_HEADS, seq), F32))
    kb_ref[...] = jnp.concatenate(pieces, axis=0).T.astype(BF16)


def _forget_cumsum(ft, b_col, batch, seq):
    return pl.pallas_call(
        _cumsum_kernel,
        grid=(batch,),
        in_specs=[
            pl.BlockSpec((F_ROWS, seq), lambda b: (0, b)),
            pl.BlockSpec((FOX_HEADS, 1), lambda b: (0, 0)),
        ],
        out_specs=pl.BlockSpec((seq, LANES), lambda b: (b, 0)),
        out_shape=jax.ShapeDtypeStruct((batch * seq, LANES), BF16),
        name="forget_cumsum",
    )(ft, b_col)


def _fox_kernel(q_ref, k_ref, v_ref, kb_ref, o_ref, vt_ref, qt_ref, acc_ref,
                sa_ref, ma_ref, sb_ref, mb_ref):
    s0_ref, s1_ref = (sa_ref, ma_ref), (sb_ref, mb_ref)
    hp = pl.program_id(1)
    seq = k_ref.shape[0]
    t = FOX_T
    d = FOX_HEAD_DIM

    def transposed(x):
        return x.astype(F32).T.astype(BF16)

    row = lax.broadcasted_iota(jnp.int32, (d, t), 0)
    for hh in range(FOX_PAIR):
        mine = (row < BIAS_TERMS * FOX_HEADS) & (row % FOX_HEADS == hp * FOX_PAIR + hh)
        qt_ref[hh, d:2 * d, :] = jnp.where(mine, 1.0, 0.0).astype(BF16)
        vt_ref[hh, d:, :] = jnp.ones((ONES_ROWS, seq), BF16)
        for r in range(seq // FOX_CHUNK):
            rows = slice(r * FOX_CHUNK, (r + 1) * FOX_CHUNK)
            vt_ref[hh, 0:d, rows] = transposed(v_ref[rows, hh * d:(hh + 1) * d])

    def scores(i, buf):
        s_ref, max_ref = buf
        rows = pl.ds(pl.multiple_of(i * t, t), t)
        for hh in range(FOX_PAIR):
            k_aug = jnp.concatenate([k_ref[rows, hh * d:(hh + 1) * d], kb_ref[rows, :]], axis=1)
            s = jnp.dot(k_aug, qt_ref[hh], preferred_element_type=F32)
            s_ref[hh] = s
            max_ref[hh] = jnp.max(s, axis=0, keepdims=True)

    def absorb(i, buf, carry, masked):
        s_ref, max_ref = buf
        rows = pl.ds(pl.multiple_of(i * t, t), t)
        new = []
        for hh in range(FOX_PAIR):
            m = carry[hh]
            s = s_ref[hh]
            if masked:
                key = lax.broadcasted_iota(jnp.int32, s.shape, 0)
                qry = lax.broadcasted_iota(jnp.int32, s.shape, 1)
                s = jnp.where(key <= qry, s, NEG_BIG)
                tile_max = jnp.max(s, axis=0, keepdims=True)
            else:
                tile_max = max_ref[hh]
            m_new = jnp.maximum(m, tile_max)
            alpha = jnp.exp2(m - m_new)
            p = jnp.exp2(s - m_new)
            acc_ref[hh] = alpha * acc_ref[hh] + jnp.dot(
                vt_ref[hh, :, rows], p.astype(BF16), preferred_element_type=F32)
            new.append(m_new)
        return tuple(new)

    def step(i, s_cur, s_next, carry):
        scores(i + 1, s_next)
        return absorb(i, s_cur, carry, False)

    def pair(j, carry):
        carry = step(2 * j, s0_ref, s1_ref, carry)
        return step(2 * j + 1, s1_ref, s0_ref, carry)

    def q_tile(qi, _):
        q_rows = pl.ds(pl.multiple_of(qi * t, t), t)
        for hh in range(FOX_PAIR):
            qt_ref[hh, 0:d, :] = transposed(q_ref[q_rows, hh * d:(hh + 1) * d])
            acc_ref[hh] = jnp.zeros((d + ONES_ROWS, t), F32)

        def odd_tail(carry):
            carry = step(qi - 1, s0_ref, s1_ref, carry)
            return absorb(qi, s1_ref, carry, True)

        def even_tail(carry):
            return absorb(qi, s0_ref, carry, True)

        init = tuple(jnp.full((1, t), NEG_BIG, F32) for _ in range(FOX_PAIR))
        scores(0, s0_ref)
        carry = lax.fori_loop(0, qi // 2, pair, init)
        carry = lax.cond(qi % 2 == 1, odd_tail, even_tail, carry)
        for hh in range(FOX_PAIR):
            o_ref[q_rows, hh * d:(hh + 1) * d] = (
                acc_ref[hh, 0:d, :] / acc_ref[hh, d:d + 1, :]).T.astype(o_ref.dtype)
        return 0

    lax.fori_loop(0, seq // t, q_tile, 0)


def _fox_attention(proj, kb_tok, batch, seq):
    width = FOX_PAIR * FOX_HEAD_DIM
    qcol = _PROJ_OFF["q_a"] // width
    kcol = _PROJ_OFF["k_a"] // width
    vcol = _PROJ_OFF["v_a"] // width
    return pl.pallas_call(
        _fox_kernel,
        grid=(batch, FOX_HEADS // FOX_PAIR),
        in_specs=[
            pl.BlockSpec((seq, width), lambda b, h: (b, qcol + h)),
            pl.BlockSpec((seq, width), lambda b, h: (b, kcol + h)),
            pl.BlockSpec((seq, width), lambda b, h: (b, vcol + h)),
            pl.BlockSpec((seq, LANES), lambda b, h: (b, 0)),
        ],
        out_specs=pl.BlockSpec((seq, width), lambda b, h: (b, h)),
        out_shape=jax.ShapeDtypeStruct((batch * seq, FOX_WIDTH), BF16),
        scratch_shapes=[
            pltpu.VMEM((FOX_PAIR, FOX_HEAD_DIM + ONES_ROWS, seq), BF16),
            pltpu.VMEM((FOX_PAIR, 2 * FOX_HEAD_DIM, FOX_T), BF16),
            pltpu.VMEM((FOX_PAIR, FOX_HEAD_DIM + ONES_ROWS, FOX_T), F32),
            pltpu.VMEM((FOX_PAIR, FOX_T, FOX_T), F32),
            pltpu.VMEM((FOX_PAIR, 1, FOX_T), F32),
            pltpu.VMEM((FOX_PAIR, FOX_T, FOX_T), F32),
            pltpu.VMEM((FOX_PAIR, 1, FOX_T), F32),
        ],
        compiler_params=pltpu.CompilerParams(
            dimension_semantics=("arbitrary", "arbitrary"),
            vmem_limit_bytes=56 * 1024 * 1024),
        name="fox_attention",
    )(proj, proj, proj, kb_tok)


def _rope_t(xt, cos, sin):
    out = []
    for hd in range(xt.shape[0] // SWA_HEAD_DIM):
        x1 = xt[hd * SWA_HEAD_DIM: hd * SWA_HEAD_DIM + ROPE_HALF]
        x2 = xt[hd * SWA_HEAD_DIM + ROPE_HALF: (hd + 1) * SWA_HEAD_DIM]
        out.append(x1 * cos - x2 * sin)
        out.append(x2 * cos + x1 * sin)
    return jnp.concatenate(out, axis=0)


def _swa_kernel(sink_ref, q_ref, kc_ref, kp_ref, vc_ref, vp_ref, posc_ref, posp_ref,
                inv_ref, *rest):
    n_cast = (len(rest) - 3) // 2
    cast_in, o_ref = rest[:n_cast], rest[n_cast]
    cast_out, (band_ref, eye_ref) = rest[n_cast + 1:2 * n_cast + 1], rest[-2:]
    for src_ref, dst_ref in zip(cast_in, cast_out):
        dst_ref[...] = src_ref[...].astype(dst_ref.dtype)

    n = pl.program_id(1)
    w = WINDOW
    inv = inv_ref[...]

    def tables(pos_row):
        ang = inv * pos_row.astype(F32)
        return jnp.cos(ang), jnp.sin(ang)

    def rope_k(k_nat, cos, sin):
        kt = _rope_t(k_nat.astype(F32).T, cos, sin)
        return kt.T.astype(BF16)

    cos_p, sin_p = tables(posp_ref[...])
    k_prev = rope_k(kp_ref[...], cos_p, sin_p)
    v_prev = vp_ref[...]

    @pl.when((pl.program_id(0) == 0) & (n == 0))
    def _():
        key = lax.broadcasted_iota(jnp.int32, band_ref.shape, 0)
        qry = lax.broadcasted_iota(jnp.int32, band_ref.shape, 1)
        band_ref[...] = jnp.where((key <= qry + w) & (key > qry), 0.0, NEG_BIG).astype(BF16)
        src = lax.broadcasted_iota(jnp.int32, eye_ref.shape, 0)
        dst = lax.broadcasted_iota(jnp.int32, eye_ref.shape, 1) % w
        eye_ref[...] = jnp.where(src == dst, 1.0, 0.0).astype(BF16)

    no_prev = jnp.where(n > 0, 0.0, NEG_BIG)
    q_scale = SWA_HEAD_DIM ** -0.5 * LOG2E
    zeros_half = jnp.zeros((SWA_HEAD_DIM, SWA_GROUP * w), F32)
    ones_rows = jnp.ones((2 * SUBLANES, 2 * w), BF16)

    sinks = [jnp.concatenate(
        [jnp.full((1, w), sink_ref[hd] * LOG2E, F32)
         for hd in range(g * SWA_GROUP, (g + 1) * SWA_GROUP)], axis=1)
        for g in range(SWA_KV_HEADS)]

    def logits(wi, k_prev):
        tok = slice(wi * w, (wi + 1) * w)
        cos, sin = tables(posc_ref[:, tok])
        k_cur = rope_k(kc_ref[tok, :], cos, sin)
        k_all = jnp.concatenate([k_prev, k_cur], axis=0)
        k_aug = jnp.concatenate([k_all, band_ref[...]], axis=1)
        qf = q_ref[tok, :].astype(F32)
        qt = jnp.concatenate(
            [qf[:, c * LANES:(c + 1) * LANES].T for c in range(SWA_WIDTH // LANES)], axis=0)
        qt = _rope_t(qt, cos * q_scale, sin * q_scale)
        out = []
        for g in range(SWA_KV_HEADS):
            heads = range(g * SWA_GROUP, (g + 1) * SWA_GROUP)
            q_g = jnp.concatenate(
                [qt[hd * SWA_HEAD_DIM:(hd + 1) * SWA_HEAD_DIM] for hd in heads], axis=1)
            parts = [zeros_half] * SWA_KV_HEADS
            parts[g] = q_g
            q_z = jnp.concatenate(parts, axis=0).astype(BF16)
            q_aug = jnp.concatenate([q_z, eye_ref[...]], axis=0)
            s = jnp.dot(k_aug, q_aug, preferred_element_type=F32)
            if wi == 0:
                s = jnp.concatenate([s[:w] + no_prev, s[w:]], axis=0)
            out.append(s)
        return out, k_cur

    def finish(wi, scores, v_prev):
        tok = slice(wi * w, (wi + 1) * w)
        v_cur = vc_ref[tok, :]
        v_all_t = jnp.concatenate([v_prev, v_cur], axis=0).astype(F32).T.astype(BF16)
        for g in range(SWA_KV_HEADS):
            s, sink = scores[g], sinks[g]
            m = jnp.maximum(jnp.max(s, axis=0, keepdims=True), sink)
            e = jnp.exp2(s - m).astype(BF16)
            v_aug = jnp.concatenate(
                [v_all_t[g * SWA_HEAD_DIM:(g + 1) * SWA_HEAD_DIM], ones_rows], axis=0)
            pv = jnp.dot(v_aug, e, preferred_element_type=F32)
            denom = pv[SWA_HEAD_DIM:SWA_HEAD_DIM + 1] + jnp.exp2(sink - m)
            out_t = pv[:SWA_HEAD_DIM] / denom
            for jj in range(SWA_GROUP // 2):
                pair = jnp.concatenate(
                    [out_t[:, (2 * jj) * w:(2 * jj + 1) * w],
                     out_t[:, (2 * jj + 1) * w:(2 * jj + 2) * w]], axis=0)
                col = (g * (SWA_GROUP // 2) + jj) * LANES
                o_ref[tok, col:col + LANES] = pair.T.astype(o_ref.dtype)
        return v_cur

    pending, k_prev = logits(0, k_prev)
    for wi in range(SWA_NW):
        if wi + 1 < SWA_NW:
            upcoming, k_prev = logits(wi + 1, k_prev)
        v_prev = finish(wi, pending, v_prev)
        if wi + 1 < SWA_NW:
            pending = upcoming


def _swa_attention(sinks, proj, pos_row, inv_tab, batch, seq, weights_f32):
    nb = seq // SWA_QB
    per = SWA_QB // WINDOW
    qcol = _PROJ_OFF["q_b"] // SWA_WIDTH
    kcol = _PROJ_OFF["k_b"] // LANES
    vcol = _PROJ_OFF["v_b"] // LANES
    steps = batch * nb

    def cur(b, n):
        return b * nb + n

    def prev(b, n):
        return jnp.maximum((b * nb + n) * per - 1, 0)

    def chunk_spec(wgt):
        rows, cols = wgt.shape
        assert rows % (steps * 2 * SUBLANES) == 0
        return pl.BlockSpec((rows // steps, cols), lambda b, n: (cur(b, n), 0))

    cast_specs = [chunk_spec(wgt) for wgt in weights_f32]
    return pl.pallas_call(
        _swa_kernel,
        grid=(batch, nb),
        in_specs=[
            pl.BlockSpec(memory_space=pltpu.SMEM),
            pl.BlockSpec((SWA_QB, SWA_WIDTH), lambda b, n: (cur(b, n), qcol)),
            pl.BlockSpec((SWA_QB, LANES), lambda b, n: (cur(b, n), kcol)),
            pl.BlockSpec((WINDOW, LANES), lambda b, n: (prev(b, n), kcol)),
            pl.BlockSpec((SWA_QB, LANES), lambda b, n: (cur(b, n), vcol)),
            pl.BlockSpec((WINDOW, LANES), lambda b, n: (prev(b, n), vcol)),
            pl.BlockSpec((1, SWA_QB), lambda b, n: (0, cur(b, n))),
            pl.BlockSpec((1, WINDOW), lambda b, n: (0, prev(b, n))),
            pl.BlockSpec((ROPE_HALF, LANES), lambda b, n: (0, 0)),
            *cast_specs,
        ],
        out_specs=[pl.BlockSpec((SWA_QB, SWA_WIDTH), lambda b, n: (cur(b, n), 0)), *cast_specs],
        out_shape=[jax.ShapeDtypeStruct((batch * seq, SWA_WIDTH), BF16),
                   *[jax.ShapeDtypeStruct(wgt.shape, BF16) for wgt in weights_f32]],
        scratch_shapes=[
            pltpu.VMEM((2 * WINDOW, WINDOW), BF16),
            pltpu.VMEM((WINDOW, SWA_GROUP * WINDOW), BF16),
        ],
        compiler_params=pltpu.CompilerParams(
            dimension_semantics=("arbitrary", "arbitrary")),
        name="swa_attention",
    )(sinks, proj, proj, proj, proj, proj, pos_row, pos_row, inv_tab, *weights_f32)


def _silu(z):
    return z * jax.nn.sigmoid(z)


def _epilogue_kernel(ya_ref, za_ref, yb_ref, zb_ref, ga_ref, gb_ref, x_ref, p_ref,
                     wof_ref, wos_ref, wout_ref, gpost_ref, wple_ref, wgate_ref, o_ref):
    ua = (ya_ref[...].astype(F32) * _silu(za_ref[...].astype(F32))).astype(BF16)
    ub = (yb_ref[...].astype(F32) * _silu(zb_ref[...].astype(F32))).astype(BF16)
    oa = jnp.dot(ua, wof_ref[...], preferred_element_type=F32)
    ob = jnp.dot(ub, wos_ref[...], preferred_element_type=F32)
    merged = (jax.nn.sigmoid(ga_ref[...].astype(F32)) * oa
              + jax.nn.sigmoid(gb_ref[...].astype(F32)) * ob)
    out = jnp.dot(merged.astype(BF16), wout_ref[...], preferred_element_type=F32)
    ms = jnp.mean(out * out, axis=-1, keepdims=True)
    x1 = x_ref[...] + (out * lax.rsqrt(ms + NORM_EPS)) * gpost_ref[...]
    e = jnp.dot(p_ref[...].astype(BF16), wple_ref[...], preferred_element_type=F32)
    gate = jax.nn.sigmoid(jnp.dot(x1.astype(BF16), wgate_ref[...],
                                  preferred_element_type=F32))
    o_ref[...] = x1 + gate * e


def _epilogue(ya, yb, proj, x2, p2, wof, wos, wout, gpost, wple, wgate):
    t = x2.shape[0]
    za_col = _PROJ_OFF["z_a"] // FOX_WIDTH
    zb_col = _PROJ_OFF["z_b"] // SWA_WIDTH
    ga_col = _PROJ_OFF["g_a"] // D_MODEL
    gb_col = _PROJ_OFF["g_b"] // D_MODEL
    once = pl.Buffered(1)

    def const(shape):
        return pl.BlockSpec(shape, lambda i: (0, 0), pipeline_mode=once)

    return pl.pallas_call(
        _epilogue_kernel,
        grid=(t // EPI_TM,),
        in_specs=[
            pl.BlockSpec((EPI_TM, FOX_WIDTH), lambda i: (i, 0)),
            pl.BlockSpec((EPI_TM, FOX_WIDTH), lambda i: (i, za_col)),
            pl.BlockSpec((EPI_TM, SWA_WIDTH), lambda i: (i, 0)),
            pl.BlockSpec((EPI_TM, SWA_WIDTH), lambda i: (i, zb_col)),
            pl.BlockSpec((EPI_TM, D_MODEL), lambda i: (i, ga_col)),
            pl.BlockSpec((EPI_TM, D_MODEL), lambda i: (i, gb_col)),
            pl.BlockSpec((EPI_TM, D_MODEL), lambda i: (i, 0)),
            pl.BlockSpec((EPI_TM, PLE_DIM), lambda i: (i, 0)),
            const((FOX_WIDTH, D_MODEL)),
            const((SWA_WIDTH, D_MODEL)),
            const((D_MODEL, D_MODEL)),
            const((1, D_MODEL)),
            const((PLE_DIM, D_MODEL)),
            const((D_MODEL, D_MODEL)),
        ],
        out_specs=pl.BlockSpec((EPI_TM, D_MODEL), lambda i: (i, 0)),
        out_shape=jax.ShapeDtypeStruct((t, D_MODEL), F32),
        compiler_params=pltpu.CompilerParams(
            dimension_semantics=("arbitrary",),
            vmem_limit_bytes=56 * 1024 * 1024),
        name="epilogue",
    )(ya, proj, yb, proj, proj, proj, x2, p2, wof, wos, wout, gpost, wple, wgate)


def _layer(x2, p2, pos_row, batch, seq, pre_g, w_in, b_forget, sinks, w_o_fox, w_o_swa,
           w_out, post_g, w_ple, w_ple_gate):
    w_t = w_in.T
    f0 = _REF_OFF["f_a"]
    wf_t = jnp.pad(w_t[f0:f0 + FOX_HEADS], ((0, F_ROWS - FOX_HEADS), (0, 0))).astype(BF16)
    col_scale = jnp.where(jnp.arange(PROJ_COLS) < FOX_WIDTH, Q_A_SCALE, 1.0
                          ).astype(F32).reshape(1, PROJ_COLS)

    h, ft, proj_partial = _prenorm(x2, pre_g.reshape(1, D_MODEL), wf_t, w_t)
    proj = _inproj(h, w_t, col_scale, proj_partial)
    kb_tok = _forget_cumsum(ft, b_forget.reshape(FOX_HEADS, 1).astype(F32), batch, seq)
    ya = _fox_attention(proj, kb_tok, batch, seq)

    inv = ROPE_THETA ** (-jnp.arange(ROPE_HALF, dtype=F32) / ROPE_HALF)
    inv_tab = jnp.broadcast_to(inv[:, None], (ROPE_HALF, LANES))
    yb, wof, wos, wout, wple, wgate = _swa_attention(
        sinks.astype(F32), proj, pos_row, inv_tab, batch, seq,
        (w_o_fox, w_o_swa, w_out, w_ple, w_ple_gate))

    return _epilogue(ya, yb, proj, x2, p2, wof, wos, wout,
                     post_g.reshape(1, D_MODEL), wple, wgate)


def kernel(x, p, positions, pre_norm_g, w_in, b_forget, sinks, w_o_fox, w_o_swa, w_out,
           post_norm_g, w_ple, w_ple_gate):
    batch, seq, _ = x.shape
    depth = p.shape[0]
    x2 = x.reshape(batch * seq, D_MODEL)
    pos_row = positions.reshape(1, batch * seq)
    for i in range(depth):
        x2 = _layer(x2, p[i].reshape(batch * seq, PLE_DIM), pos_row, batch, seq,
                    pre_norm_g[i], w_in[i], b_forget[i], sinks[i], w_o_fox[i], w_o_swa[i],
                    w_out[i], post_norm_g[i], w_ple[i], w_ple_gate[i])
    return x2.reshape(batch, seq, D_MODEL)
```

```python
import math

import jax
import jax.numpy as jnp
from jax import lax
from jax.experimental import pallas as pl
from jax.experimental.pallas import tpu as pltpu

F32 = jnp.float32
BF16 = jnp.bfloat16

D_MODEL = 2048
FOX_HEADS = 8
FOX_HEAD_DIM = 128
FOX_WIDTH = FOX_HEADS * FOX_HEAD_DIM
SWA_Q_HEADS = 16
SWA_KV_HEADS = 2
SWA_HEAD_DIM = 64
SWA_WIDTH = SWA_Q_HEADS * SWA_HEAD_DIM
SWA_KV_WIDTH = SWA_KV_HEADS * SWA_HEAD_DIM
SWA_GROUP = SWA_Q_HEADS // SWA_KV_HEADS
ROPE_HALF = SWA_HEAD_DIM // 2
WINDOW = 128
ROPE_THETA = 10000.0
PLE_DIM = 256
NORM_EPS = 1e-6
LANES = 128
LOG2E = math.log2(math.e)
NEG_BIG = -1e30

_REF_SPLITS = (FOX_WIDTH, FOX_WIDTH, FOX_WIDTH, FOX_WIDTH, FOX_HEADS, SWA_WIDTH,
               SWA_KV_WIDTH, SWA_KV_WIDTH, SWA_WIDTH, D_MODEL, D_MODEL)
_REF_NAMES = ("q_a", "k_a", "v_a", "z_a", "f_a", "q_b", "k_b", "v_b", "z_b", "g_a", "g_b")
_REF_OFF = {}
_o = 0
for _n, _s in zip(_REF_NAMES, _REF_SPLITS):
    _REF_OFF[_n] = _o
    _o += _s

IN_TM = 1024
IN_TN = 1536
NORM_ROWS = 512
F_ROWS = 16

PREP_TN = 512
_SWA_GROUPS = (
    (_REF_OFF["q_b"], SWA_WIDTH),
    (_REF_OFF["k_b"], 2 * SWA_KV_WIDTH),
)
_MAIN_GROUPS = (
    (_REF_OFF["g_a"], 2 * D_MODEL),
    (_REF_OFF["z_a"], FOX_WIDTH),
    (_REF_OFF["z_b"], SWA_WIDTH),
    (_REF_OFF["q_a"], 3 * FOX_WIDTH),
)
SUBLANES = 8


def _group_table(groups):
    assert all(c0 % SUBLANES == 0 for c0, _ in groups)
    counts = [-(-w // PREP_TN) for _, w in groups]
    return tuple(sum(counts[:i]) for i in range(len(groups))), sum(counts)


_SWA_START, _SWA_WINDOWS = _group_table(_SWA_GROUPS)
_MAIN_START, _MAIN_WINDOWS = _group_table(_MAIN_GROUPS)
W_PER_TILE = IN_TN // PREP_TN
assert _SWA_WINDOWS == W_PER_TILE and _MAIN_WINDOWS % W_PER_TILE == 0
SWA_COLS = _SWA_WINDOWS * PREP_TN
MAIN_COLS = _MAIN_WINDOWS * PREP_TN
_SWA_OFF = {"q_b": 0, "k_b": _SWA_START[1] * PREP_TN,
            "v_b": _SWA_START[1] * PREP_TN + SWA_KV_WIDTH}
_MAIN_OFF = {
    "g_a": 0, "g_b": D_MODEL,
    "z_a": _MAIN_START[1] * PREP_TN,
    "z_b": _MAIN_START[2] * PREP_TN,
    "q_a": _MAIN_START[3] * PREP_TN,
    "k_a": _MAIN_START[3] * PREP_TN + FOX_WIDTH,
    "v_a": _MAIN_START[3] * PREP_TN + 2 * FOX_WIDTH,
}
_NT = (((1,), (1,)), ((), ()))
Q_A_SCALE = FOX_HEAD_DIM ** -0.5 * LOG2E

FOX_T = 512
FOX_CHUNK = 512
FOX_PAIR = 4
BIAS_TERMS = 3
ONES_ROWS = 16

SWA_NW = 8
SWA_QB = SWA_NW * WINDOW

EPI_TM = 256


def _cast_windows(w_refs, wb_ref):
    for k, w_ref in enumerate(w_refs):
        for r in range(PREP_TN // LANES):
            rows = slice(r * LANES, (r + 1) * LANES)
            wb_ref[k * PREP_TN + r * LANES:k * PREP_TN + (r + 1) * LANES, :] = (
                w_ref[rows, :].astype(BF16))


def _w_src_row(window, groups, starts):
    tile_row = jnp.int32(0)
    for (c0, _), start in zip(groups, starts):
        tile_row = jnp.where(window >= start,
                             c0 // SUBLANES + (window - start) * (PREP_TN // SUBLANES), tile_row)
    return tile_row * SUBLANES


def _prenorm_kernel(x_ref, g_ref, wf_ref, *refs):
    w_refs = refs[:W_PER_TILE]
    h_ref, ft_ref, proj_ref, wb_ref = refs[W_PER_TILE:]

    @pl.when(pl.program_id(0) == 0)
    def _():
        _cast_windows(w_refs, wb_ref)

    x = x_ref[...]
    ms = jnp.mean(x * x, axis=-1, keepdims=True)
    h = ((x * lax.rsqrt(ms + NORM_EPS)) * g_ref[...]).astype(BF16)
    h_ref[...] = h
    ft_ref[...] = lax.dot_general(wf_ref[...], h, _NT, preferred_element_type=F32)
    proj_ref[...] = lax.dot_general(h, wb_ref[...], _NT,
                                    preferred_element_type=F32).astype(BF16)


def _prenorm(x2, g, wf_t, w_t):
    t = x2.shape[0]
    once = pl.Buffered(1)

    def w_spec(k):
        return pl.BlockSpec((pl.Element(PREP_TN), pl.Element(D_MODEL)),
                            lambda i: (_w_src_row(k, _SWA_GROUPS, _SWA_START), 0),
                            pipeline_mode=once)

    return pl.pallas_call(
        _prenorm_kernel,
        grid=(t // NORM_ROWS,),
        in_specs=[
            pl.BlockSpec((NORM_ROWS, D_MODEL), lambda i: (i, 0)),
            pl.BlockSpec((1, D_MODEL), lambda i: (0, 0)),
            pl.BlockSpec((F_ROWS, D_MODEL), lambda i: (0, 0)),
            *[w_spec(k) for k in range(W_PER_TILE)],
        ],
        out_specs=[
            pl.BlockSpec((NORM_ROWS, D_MODEL), lambda i: (i, 0)),
            pl.BlockSpec((F_ROWS, NORM_ROWS), lambda i: (0, i)),
            pl.BlockSpec((NORM_ROWS, SWA_COLS), lambda i: (i, 0)),
        ],
        out_shape=[
            jax.ShapeDtypeStruct((t, D_MODEL), BF16),
            jax.ShapeDtypeStruct((F_ROWS, t), F32),
            jax.ShapeDtypeStruct((t, SWA_COLS), BF16),
        ],
        scratch_shapes=[pltpu.VMEM((IN_TN, D_MODEL), BF16)],
        compiler_params=pltpu.CompilerParams(
            dimension_semantics=("arbitrary",),
            vmem_limit_bytes=56 * 1024 * 1024),
        name="prenorm",
    )(x2, g, wf_t, *([w_t] * W_PER_TILE))


def _inproj_kernel(h_ref, *refs):
    w_refs = refs[:W_PER_TILE]
    scale_ref, proj_ref, wb_ref = refs[W_PER_TILE:]

    @pl.when(pl.program_id(1) == 0)
    def _():
        _cast_windows(w_refs, wb_ref)

    acc = lax.dot_general(h_ref[...], wb_ref[...], _NT, preferred_element_type=F32)
    proj_ref[...] = (acc * scale_ref[...]).astype(BF16)


def _inproj(h, w_t, col_scale):
    t = h.shape[0]

    def w_spec(k):
        return pl.BlockSpec(
            (pl.Element(PREP_TN), pl.Element(D_MODEL)),
            lambda j, i: (_w_src_row(j * W_PER_TILE + k, _MAIN_GROUPS, _MAIN_START), 0))

    return pl.pallas_call(
        _inproj_kernel,
        grid=(MAIN_COLS // IN_TN, t // IN_TM),
        in_specs=[
            pl.BlockSpec((IN_TM, D_MODEL), lambda j, i: (i, 0)),
            *[w_spec(k) for k in range(W_PER_TILE)],
            pl.BlockSpec((1, IN_TN), lambda j, i: (0, j)),
        ],
        out_specs=pl.BlockSpec((IN_TM, IN_TN), lambda j, i: (i, j)),
        out_shape=jax.ShapeDtypeStruct((t, MAIN_COLS), BF16),
        scratch_shapes=[pltpu.VMEM((IN_TN, D_MODEL), BF16)],
        compiler_params=pltpu.CompilerParams(
            dimension_semantics=("arbitrary", "arbitrary"),
            vmem_limit_bytes=56 * 1024 * 1024),
        name="inproj",
    )(h, *([w_t] * W_PER_TILE), col_scale)


def _cumsum_kernel(ft_ref, b_ref, kb_ref):
    f = ft_ref[0:FOX_HEADS, :] + b_ref[...]
    lf = jnp.minimum(f, 0.0) - jnp.log1p(jnp.exp(-jnp.abs(f)))
    seq = lf.shape[1]
    lane = lax.broadcasted_iota(jnp.int32, lf.shape, 1)
    c = lf
    shift = 1
    while shift < seq:
        c = c + jnp.where(lane >= shift, pltpu.roll(c, shift, axis=1), 0.0)
        shift *= 2
    rest = c * (-LOG2E)
    pieces = []
    for _ in range(BIAS_TERMS):
        piece = rest.astype(BF16).astype(F32)
        pieces.append(piece)
        rest = rest - piece
    pieces.append(jnp.zeros((LANES - BIAS_TERMS * FOX_HEADS, seq), F32))
    kb_ref[...] = jnp.concatenate(pieces, axis=0).T.astype(BF16)


def _forget_cumsum(ft, b_col, batch, seq):
    return pl.pallas_call(
        _cumsum_kernel,
        grid=(batch,),
        in_specs=[
            pl.BlockSpec((F_ROWS, seq), lambda b: (0, b)),
            pl.BlockSpec((FOX_HEADS, 1), lambda b: (0, 0)),
        ],
        out_specs=pl.BlockSpec((seq, LANES), lambda b: (b, 0)),
        out_shape=jax.ShapeDtypeStruct((batch * seq, LANES), BF16),
        name="forget_cumsum",
    )(ft, b_col)


def _fox_kernel(q_ref, k_ref, v_ref, kb_ref, o_ref, vt_ref, qt_ref, acc_ref,
                sa_ref, ma_ref, sb_ref, mb_ref):
    s0_ref, s1_ref = (sa_ref, ma_ref), (sb_ref, mb_ref)
    hp = pl.program_id(1)
    seq = k_ref.shape[0]
    t = FOX_T
    d = FOX_HEAD_DIM

    def transposed(x):
        return x.astype(F32).T.astype(BF16)

    row = lax.broadcasted_iota(jnp.int32, (d, t), 0)
    for hh in range(FOX_PAIR):
        mine = (row < BIAS_TERMS * FOX_HEADS) & (row % FOX_HEADS == hp * FOX_PAIR + hh)
        qt_ref[hh, d:2 * d, :] = jnp.where(mine, 1.0, 0.0).astype(BF16)
        vt_ref[hh, d:, :] = jnp.ones((ONES_ROWS, seq), BF16)
        for r in range(seq // FOX_CHUNK):
            rows = slice(r * FOX_CHUNK, (r + 1) * FOX_CHUNK)
            vt_ref[hh, 0:d, rows] = transposed(v_ref[rows, hh * d:(hh + 1) * d])

    def scores(i, buf):
        s_ref, max_ref = buf
        rows = pl.ds(pl.multiple_of(i * t, t), t)
        for hh in range(FOX_PAIR):
            k_aug = jnp.concatenate([k_ref[rows, hh * d:(hh + 1) * d], kb_ref[rows, :]], axis=1)
            s = jnp.dot(k_aug, qt_ref[hh], preferred_element_type=F32)
            s_ref[hh] = s
            max_ref[hh] = jnp.max(s, axis=0, keepdims=True)

    def absorb(i, buf, carry, masked):
        s_ref, max_ref = buf
        rows = pl.ds(pl.multiple_of(i * t, t), t)
        new = []
        for hh in range(FOX_PAIR):
            m = carry[hh]
            s = s_ref[hh]
            if masked:
                key = lax.broadcasted_iota(jnp.int32, s.shape, 0)
                qry = lax.broadcasted_iota(jnp.int32, s.shape, 1)
                s = jnp.where(key <= qry, s, NEG_BIG)
                tile_max = jnp.max(s, axis=0, keepdims=True)
            else:
                tile_max = max_ref[hh]
            m_new = jnp.maximum(m, tile_max)
            alpha = jnp.exp2(m - m_new)
            p = jnp.exp2(s - m_new)
            acc_ref[hh] = alpha * acc_ref[hh] + jnp.dot(
                vt_ref[hh, :, rows], p.astype(BF16), preferred_element_type=F32)
            new.append(m_new)
        return tuple(new)

    def step(i, s_cur, s_next, carry):
        scores(i + 1, s_next)
        return absorb(i, s_cur, carry, False)

    def pair(j, carry):
        carry = step(2 * j, s0_ref, s1_ref, carry)
        return step(2 * j + 1, s1_ref, s0_ref, carry)

    def q_tile(qi, _):
        q_rows = pl.ds(pl.multiple_of(qi * t, t), t)
        for hh in range(FOX_PAIR):
            qt_ref[hh, 0:d, :] = transposed(q_ref[q_rows, hh * d:(hh + 1) * d])
            acc_ref[hh] = jnp.zeros((d + ONES_ROWS, t), F32)

        def odd_tail(carry):
            carry = step(qi - 1, s0_ref, s1_ref, carry)
            return absorb(qi, s1_ref, carry, True)

        def even_tail(carry):
            return absorb(qi, s0_ref, carry, True)

        init = tuple(jnp.full((1, t), NEG_BIG, F32) for _ in range(FOX_PAIR))
        scores(0, s0_ref)
        carry = lax.fori_loop(0, qi // 2, pair, init)
        carry = lax.cond(qi % 2 == 1, odd_tail, even_tail, carry)
        for hh in range(FOX_PAIR):
            o_ref[q_rows, hh * d:(hh + 1) * d] = (
                acc_ref[hh, 0:d, :] / acc_ref[hh, d:d + 1, :]).T.astype(o_ref.dtype)
        return 0

    lax.fori_loop(0, seq // t, q_tile, 0)


def _fox_attention(proj, kb_tok, batch, seq):
    width = FOX_PAIR * FOX_HEAD_DIM
    qcol = _MAIN_OFF["q_a"] // width
    kcol = _MAIN_OFF["k_a"] // width
    vcol = _MAIN_OFF["v_a"] // width
    return pl.pallas_call(
        _fox_kernel,
        grid=(batch, FOX_HEADS // FOX_PAIR),
        in_specs=[
            pl.BlockSpec((seq, width), lambda b, h: (b, qcol + h)),
            pl.BlockSpec((seq, width), lambda b, h: (b, kcol + h)),
            pl.BlockSpec((seq, width), lambda b, h: (b, vcol + h)),
            pl.BlockSpec((seq, LANES), lambda b, h: (b, 0)),
        ],
        out_specs=pl.BlockSpec((seq, width), lambda b, h: (b, h)),
        out_shape=jax.ShapeDtypeStruct((batch * seq, FOX_WIDTH), BF16),
        scratch_shapes=[
            pltpu.VMEM((FOX_PAIR, FOX_HEAD_DIM + ONES_ROWS, seq), BF16),
            pltpu.VMEM((FOX_PAIR, 2 * FOX_HEAD_DIM, FOX_T), BF16),
            pltpu.VMEM((FOX_PAIR, FOX_HEAD_DIM + ONES_ROWS, FOX_T), F32),
            pltpu.VMEM((FOX_PAIR, FOX_T, FOX_T), F32),
            pltpu.VMEM((FOX_PAIR, 1, FOX_T), F32),
            pltpu.VMEM((FOX_PAIR, FOX_T, FOX_T), F32),
            pltpu.VMEM((FOX_PAIR, 1, FOX_T), F32),
        ],
        compiler_params=pltpu.CompilerParams(
            dimension_semantics=("arbitrary", "arbitrary"),
            vmem_limit_bytes=56 * 1024 * 1024),
        name="fox_attention",
    )(proj, proj, proj, kb_tok)


def _rope_t(xt, cos, sin):
    out = []
    for hd in range(xt.shape[0] // SWA_HEAD_DIM):
        x1 = xt[hd * SWA_HEAD_DIM: hd * SWA_HEAD_DIM + ROPE_HALF]
        x2 = xt[hd * SWA_HEAD_DIM + ROPE_HALF: (hd + 1) * SWA_HEAD_DIM]
        out.append(x1 * cos - x2 * sin)
        out.append(x2 * cos + x1 * sin)
    return jnp.concatenate(out, axis=0)


def _swa_kernel(sink_ref, q_ref, kc_ref, kp_ref, vc_ref, vp_ref, posc_ref, posp_ref,
                inv_ref, *rest):
    n_cast = (len(rest) - 3) // 2
    cast_in, o_ref = rest[:n_cast], rest[n_cast]
    cast_out, (band_ref, eye_ref) = rest[n_cast + 1:2 * n_cast + 1], rest[-2:]
    for src_ref, dst_ref in zip(cast_in, cast_out):
        dst_ref[...] = src_ref[...].astype(dst_ref.dtype)

    n = pl.program_id(1)
    w = WINDOW
    inv = inv_ref[...]

    def tables(pos_row):
        ang = inv * pos_row.astype(F32)
        return jnp.cos(ang), jnp.sin(ang)

    def rope_k(k_nat, cos, sin):
        kt = _rope_t(k_nat.astype(F32).T, cos, sin)
        return kt.T.astype(BF16)

    cos_p, sin_p = tables(posp_ref[...])
    k_prev = rope_k(kp_ref[...], cos_p, sin_p)
    v_prev = vp_ref[...]

    @pl.when((pl.program_id(0) == 0) & (n == 0))
    def _():
        key = lax.broadcasted_iota(jnp.int32, band_ref.shape, 0)
        qry = lax.broadcasted_iota(jnp.int32, band_ref.shape, 1)
        band_ref[...] = jnp.where((key <= qry + w) & (key > qry), 0.0, NEG_BIG).astype(BF16)
        src = lax.broadcasted_iota(jnp.int32, eye_ref.shape, 0)
        dst = lax.broadcasted_iota(jnp.int32, eye_ref.shape, 1) % w
        eye_ref[...] = jnp.where(src == dst, 1.0, 0.0).astype(BF16)

    no_prev = jnp.where(n > 0, 0.0, NEG_BIG)
    q_scale = SWA_HEAD_DIM ** -0.5 * LOG2E
    zeros_half = jnp.zeros((SWA_HEAD_DIM, SWA_GROUP * w), F32)
    ones_rows = jnp.ones((2 * SUBLANES, 2 * w), BF16)

    sinks = [jnp.concatenate(
        [jnp.full((1, w), sink_ref[hd] * LOG2E, F32)
         for hd in range(g * SWA_GROUP, (g + 1) * SWA_GROUP)], axis=1)
        for g in range(SWA_KV_HEADS)]

    def logits(wi, k_prev):
        tok = slice(wi * w, (wi + 1) * w)
        cos, sin = tables(posc_ref[:, tok])
        k_cur = rope_k(kc_ref[tok, :], cos, sin)
        k_all = jnp.concatenate([k_prev, k_cur], axis=0)
        k_aug = jnp.concatenate([k_all, band_ref[...]], axis=1)
        qf = q_ref[tok, :].astype(F32)
        qt = jnp.concatenate(
            [qf[:, c * LANES:(c + 1) * LANES].T for c in range(SWA_WIDTH // LANES)], axis=0)
        qt = _rope_t(qt, cos * q_scale, sin * q_scale)
        out = []
        for g in range(SWA_KV_HEADS):
            heads = range(g * SWA_GROUP, (g + 1) * SWA_GROUP)
            q_g = jnp.concatenate(
                [qt[hd * SWA_HEAD_DIM:(hd + 1) * SWA_HEAD_DIM] for hd in heads], axis=1)
            parts = [zeros_half] * SWA_KV_HEADS
            parts[g] = q_g
            q_z = jnp.concatenate(parts, axis=0).astype(BF16)
            q_aug = jnp.concatenate([q_z, eye_ref[...]], axis=0)
            s = jnp.dot(k_aug, q_aug, preferred_element_type=F32)
            if wi == 0:
                s = jnp.concatenate([s[:w] + no_prev, s[w:]], axis=0)
            out.append(s)
        return out, k_cur

    def finish(wi, scores, v_prev):
        tok = slice(wi * w, (wi + 1) * w)
        v_cur = vc_ref[tok, :]
        v_all_t = jnp.concatenate([v_prev, v_cur], axis=0).astype(F32).T.astype(BF16)
        for g in range(SWA_KV_HEADS):
            s, sink = scores[g], sinks[g]
            m = jnp.maximum(jnp.max(s, axis=0, keepdims=True), sink)
            e = jnp.exp2(s - m).astype(BF16)
            v_aug = jnp.concatenate(
                [v_all_t[g * SWA_HEAD_DIM:(g + 1) * SWA_HEAD_DIM], ones_rows], axis=0)
            pv = jnp.dot(v_aug, e, preferred_element_type=F32)
            denom = pv[SWA_HEAD_DIM:SWA_HEAD_DIM + 1] + jnp.exp2(sink - m)
            out_t = pv[:SWA_HEAD_DIM] / denom
            for jj in range(SWA_GROUP // 2):
                pair = jnp.concatenate(
                    [out_t[:, (2 * jj) * w:(2 * jj + 1) * w],
                     out_t[:, (2 * jj + 1) * w:(2 * jj + 2) * w]], axis=0)
                col = (g * (SWA_GROUP // 2) + jj) * LANES
                o_ref[tok, col:col + LANES] = pair.T.astype(o_ref.dtype)
        return v_cur

    pending, k_prev = logits(0, k_prev)
    for wi in range(SWA_NW):
        if wi + 1 < SWA_NW:
            upcoming, k_prev = logits(wi + 1, k_prev)
        v_prev = finish(wi, pending, v_prev)
        if wi + 1 < SWA_NW:
            pending = upcoming


def _swa_attention(sinks, proj, pos_row, inv_tab, batch, seq, weights_f32):
    nb = seq // SWA_QB
    per = SWA_QB // WINDOW
    qcol = _SWA_OFF["q_b"] // SWA_WIDTH
    kcol = _SWA_OFF["k_b"] // LANES
    vcol = _SWA_OFF["v_b"] // LANES
    steps = batch * nb

    def cur(b, n):
        return b * nb + n

    def prev(b, n):
        return jnp.maximum((b * nb + n) * per - 1, 0)

    def chunk_spec(wgt):
        rows, cols = wgt.shape
        assert rows % (steps * 2 * SUBLANES) == 0
        return pl.BlockSpec((rows // steps, cols), lambda b, n: (cur(b, n), 0))

    cast_specs = [chunk_spec(wgt) for wgt in weights_f32]
    return pl.pallas_call(
        _swa_kernel,
        grid=(batch, nb),
        in_specs=[
            pl.BlockSpec(memory_space=pltpu.SMEM),
            pl.BlockSpec((SWA_QB, SWA_WIDTH), lambda b, n: (cur(b, n), qcol)),
            pl.BlockSpec((SWA_QB, LANES), lambda b, n: (cur(b, n), kcol)),
            pl.BlockSpec((WINDOW, LANES), lambda b, n: (prev(b, n), kcol)),
            pl.BlockSpec((SWA_QB, LANES), lambda b, n: (cur(b, n), vcol)),
            pl.BlockSpec((WINDOW, LANES), lambda b, n: (prev(b, n), vcol)),
            pl.BlockSpec((1, SWA_QB), lambda b, n: (0, cur(b, n))),
            pl.BlockSpec((1, WINDOW), lambda b, n: (0, prev(b, n))),
            pl.BlockSpec((ROPE_HALF, LANES), lambda b, n: (0, 0)),
            *cast_specs,
        ],
        out_specs=[pl.BlockSpec((SWA_QB, SWA_WIDTH), lambda b, n: (cur(b, n), 0)), *cast_specs],
        out_shape=[jax.ShapeDtypeStruct((batch * seq, SWA_WIDTH), BF16),
                   *[jax.ShapeDtypeStruct(wgt.shape, BF16) for wgt in weights_f32]],
        scratch_shapes=[
            pltpu.VMEM((2 * WINDOW, WINDOW), BF16),
            pltpu.VMEM((WINDOW, SWA_GROUP * WINDOW), BF16),
        ],
        compiler_params=pltpu.CompilerParams(
            dimension_semantics=("arbitrary", "arbitrary")),
        name="swa_attention",
    )(sinks, proj, proj, proj, proj, proj, pos_row, pos_row, inv_tab, *weights_f32)


def _silu(z):
    return z * jax.nn.sigmoid(z)


def _epilogue_kernel(ya_ref, za_ref, yb_ref, zb_ref, ga_ref, gb_ref, x_ref, p_ref,
                     wof_ref, wos_ref, wout_ref, gpost_ref, wple_ref, wgate_ref, o_ref):
    ua = (ya_ref[...].astype(F32) * _silu(za_ref[...].astype(F32))).astype(BF16)
    ub = (yb_ref[...].astype(F32) * _silu(zb_ref[...].astype(F32))).astype(BF16)
    oa = jnp.dot(ua, wof_ref[...], preferred_element_type=F32)
    ob = jnp.dot(ub, wos_ref[...], preferred_element_type=F32)
    merged = (jax.nn.sigmoid(ga_ref[...].astype(F32)) * oa
              + jax.nn.sigmoid(gb_ref[...].astype(F32)) * ob)
    out = jnp.dot(merged.astype(BF16), wout_ref[...], preferred_element_type=F32)
    ms = jnp.mean(out * out, axis=-1, keepdims=True)
    x1 = x_ref[...] + (out * lax.rsqrt(ms + NORM_EPS)) * gpost_ref[...]
    e = jnp.dot(p_ref[...].astype(BF16), wple_ref[...], preferred_element_type=F32)
    gate = jax.nn.sigmoid(jnp.dot(x1.astype(BF16), wgate_ref[...],
                                  preferred_element_type=F32))
    o_ref[...] = x1 + gate * e


def _epilogue(ya, yb, proj, x2, p2, wof, wos, wout, gpost, wple, wgate):
    t = x2.shape[0]
    za_col = _MAIN_OFF["z_a"] // FOX_WIDTH
    zb_col = _MAIN_OFF["z_b"] // SWA_WIDTH
    ga_col = _MAIN_OFF["g_a"] // D_MODEL
    gb_col = _MAIN_OFF["g_b"] // D_MODEL
    once = pl.Buffered(1)

    def const(shape):
        return pl.BlockSpec(shape, lambda i: (0, 0), pipeline_mode=once)

    return pl.pallas_call(
        _epilogue_kernel,
        grid=(t // EPI_TM,),
        in_specs=[
            pl.BlockSpec((EPI_TM, FOX_WIDTH), lambda i: (i, 0)),
            pl.BlockSpec((EPI_TM, FOX_WIDTH), lambda i: (i, za_col)),
            pl.BlockSpec((EPI_TM, SWA_WIDTH), lambda i: (i, 0)),
            pl.BlockSpec((EPI_TM, SWA_WIDTH), lambda i: (i, zb_col)),
            pl.BlockSpec((EPI_TM, D_MODEL), lambda i: (i, ga_col)),
            pl.BlockSpec((EPI_TM, D_MODEL), lambda i: (i, gb_col)),
            pl.BlockSpec((EPI_TM, D_MODEL), lambda i: (i, 0)),
            pl.BlockSpec((EPI_TM, PLE_DIM), lambda i: (i, 0)),
            const((FOX_WIDTH, D_MODEL)),
            const((SWA_WIDTH, D_MODEL)),
            const((D_MODEL, D_MODEL)),
            const((1, D_MODEL)),
            const((PLE_DIM, D_MODEL)),
            const((D_MODEL, D_MODEL)),
        ],
        out_specs=pl.BlockSpec((EPI_TM, D_MODEL), lambda i: (i, 0)),
        out_shape=jax.ShapeDtypeStruct((t, D_MODEL), F32),
        compiler_params=pltpu.CompilerParams(
            dimension_semantics=("arbitrary",),
            vmem_limit_bytes=56 * 1024 * 1024),
        name="epilogue",
    )(ya, proj, yb, proj, proj, proj, x2, p2, wof, wos, wout, gpost, wple, wgate)


def _layer(x2, p2, pos_row, batch, seq, pre_g, w_in, b_forget, sinks, w_o_fox, w_o_swa,
           w_out, post_g, w_ple, w_ple_gate):
    w_t = w_in.T
    f0 = _REF_OFF["f_a"]
    wf_t = jnp.pad(w_t[f0:f0 + FOX_HEADS], ((0, F_ROWS - FOX_HEADS), (0, 0))).astype(BF16)
    col = jnp.arange(MAIN_COLS)
    in_q_a = (col >= _MAIN_OFF["q_a"]) & (col < _MAIN_OFF["q_a"] + FOX_WIDTH)
    col_scale = jnp.where(in_q_a, Q_A_SCALE, 1.0).astype(F32).reshape(1, MAIN_COLS)

    h, ft, proj_swa = _prenorm(x2, pre_g.reshape(1, D_MODEL), wf_t, w_t)
    proj = _inproj(h, w_t, col_scale)
    kb_tok = _forget_cumsum(ft, b_forget.reshape(FOX_HEADS, 1).astype(F32), batch, seq)
    ya = _fox_attention(proj, kb_tok, batch, seq)

    inv = ROPE_THETA ** (-jnp.arange(ROPE_HALF, dtype=F32) / ROPE_HALF)
    inv_tab = jnp.broadcast_to(inv[:, None], (ROPE_HALF, LANES))
    yb, wof, wos, wout, wple, wgate = _swa_attention(
        sinks.astype(F32), proj_swa, pos_row, inv_tab, batch, seq,
        (w_o_fox, w_o_swa, w_out, w_ple, w_ple_gate))

    return _epilogue(ya, yb, proj, x2, p2, wof, wos, wout,
                     post_g.reshape(1, D_MODEL), wple, wgate)


def kernel(x, p, positions, pre_norm_g, w_in, b_forget, sinks, w_o_fox, w_o_swa, w_out,
           post_norm_g, w_ple, w_ple_gate):
    batch, seq, _ = x.shape
    depth = p.shape[0]
    x2 = x.reshape(batch * seq, D_MODEL)
    pos_row = positions.reshape(1, batch * seq)
    for i in range(depth):
        x2 = _layer(x2, p[i].reshape(batch * seq, PLE_DIM), pos_row, batch, seq,
                    pre_norm_g[i], w_in[i], b_forget[i], sinks[i], w_o_fox[i], w_o_swa[i],
                    w_out[i], post_norm_g[i], w_ple[i], w_ple_gate[i])
    return x2.reshape(batch, seq, D_MODEL)
```

```python
import functools
import math

import jax
import jax.numpy as jnp
from jax import lax
from jax.experimental import pallas as pl
from jax.experimental.pallas import tpu as pltpu

F32 = jnp.float32
BF16 = jnp.bfloat16

D_MODEL = 2048
FOX_HEADS = 8
FOX_HEAD_DIM = 128
FOX_WIDTH = FOX_HEADS * FOX_HEAD_DIM
SWA_Q_HEADS = 16
SWA_KV_HEADS = 2
SWA_HEAD_DIM = 64
SWA_WIDTH = SWA_Q_HEADS * SWA_HEAD_DIM
SWA_KV_WIDTH = SWA_KV_HEADS * SWA_HEAD_DIM
SWA_GROUP = SWA_Q_HEADS // SWA_KV_HEADS
ROPE_HALF = SWA_HEAD_DIM // 2
WINDOW = 128
ROPE_THETA = 10000.0
PLE_DIM = 256
NORM_EPS = 1e-6
LANES = 128
LOG2E = math.log2(math.e)
NEG_BIG = -1e30

_REF_SPLITS = (FOX_WIDTH, FOX_WIDTH, FOX_WIDTH, FOX_WIDTH, FOX_HEADS, SWA_WIDTH,
               SWA_KV_WIDTH, SWA_KV_WIDTH, SWA_WIDTH, D_MODEL, D_MODEL)
_REF_NAMES = ("q_a", "k_a", "v_a", "z_a", "f_a", "q_b", "k_b", "v_b", "z_b", "g_a", "g_b")
_REF_OFF = {}
_o = 0
for _n, _s in zip(_REF_NAMES, _REF_SPLITS):
    _REF_OFF[_n] = _o
    _o += _s

IN_TM = 1024
IN_TN = 1536
NORM_ROWS = 512
F_ROWS = 16

PREP_TN = 512
_SWA_GROUPS = (
    (_REF_OFF["q_b"], SWA_WIDTH),
    (_REF_OFF["k_b"], 2 * SWA_KV_WIDTH),
)
_MAIN_GROUPS = (
    (_REF_OFF["g_a"], 2 * D_MODEL),
    (_REF_OFF["z_a"], FOX_WIDTH),
    (_REF_OFF["z_b"], SWA_WIDTH),
    (_REF_OFF["q_a"], 3 * FOX_WIDTH),
)
SUBLANES = 8


def _group_table(groups):
    assert all(c0 % SUBLANES == 0 for c0, _ in groups)
    counts = [-(-w // PREP_TN) for _, w in groups]
    return tuple(sum(counts[:i]) for i in range(len(groups))), sum(counts)


_SWA_START, _SWA_WINDOWS = _group_table(_SWA_GROUPS)
_MAIN_START, _MAIN_WINDOWS = _group_table(_MAIN_GROUPS)
W_PER_TILE = IN_TN // PREP_TN
assert _SWA_WINDOWS == W_PER_TILE and _MAIN_WINDOWS % W_PER_TILE == 0
SWA_COLS = _SWA_WINDOWS * PREP_TN
MAIN_COLS = _MAIN_WINDOWS * PREP_TN
_SWA_OFF = {"q_b": 0, "k_b": _SWA_START[1] * PREP_TN,
            "v_b": _SWA_START[1] * PREP_TN + SWA_KV_WIDTH}
_MAIN_OFF = {
    "g_a": 0, "g_b": D_MODEL,
    "z_a": _MAIN_START[1] * PREP_TN,
    "z_b": _MAIN_START[2] * PREP_TN,
    "q_a": _MAIN_START[3] * PREP_TN,
    "k_a": _MAIN_START[3] * PREP_TN + FOX_WIDTH,
    "v_a": _MAIN_START[3] * PREP_TN + 2 * FOX_WIDTH,
}
_NT = (((1,), (1,)), ((), ()))
Q_A_SCALE = FOX_HEAD_DIM ** -0.5 * LOG2E

FOX_T = 512
FOX_CHUNK = 512
FOX_PAIR = 4
BIAS_TERMS = 3
ONES_ROWS = 16

SWA_NW = 8
SWA_QB = SWA_NW * WINDOW

EPI_TM = 256


def _cast_windows(w_refs, wb_ref):
    for k, w_ref in enumerate(w_refs):
        for r in range(PREP_TN // LANES):
            rows = slice(r * LANES, (r + 1) * LANES)
            wb_ref[k * PREP_TN + r * LANES:k * PREP_TN + (r + 1) * LANES, :] = (
                w_ref[rows, :].astype(BF16))


def _w_src_row(window, groups, starts):
    tile_row = jnp.int32(0)
    for (c0, _), start in zip(groups, starts):
        tile_row = jnp.where(window >= start,
                             c0 // SUBLANES + (window - start) * (PREP_TN // SUBLANES), tile_row)
    return tile_row * SUBLANES


def _prenorm_kernel(steps_per_seq, x_ref, g_ref, wf_ref, b_ref, *refs):
    w_refs = refs[:W_PER_TILE]
    h_ref, kb_ref, proj_ref, wb_ref, carry_ref = refs[W_PER_TILE:]
    i = pl.program_id(0)

    @pl.when(i == 0)
    def _():
        _cast_windows(w_refs, wb_ref)

    @pl.when(i % steps_per_seq == 0)
    def _():
        carry_ref[...] = jnp.zeros(carry_ref.shape, F32)

    x = x_ref[...]
    ms = jnp.mean(x * x, axis=-1, keepdims=True)
    h = ((x * lax.rsqrt(ms + NORM_EPS)) * g_ref[...]).astype(BF16)
    h_ref[...] = h
    ft = lax.dot_general(wf_ref[...], h, _NT, preferred_element_type=F32)
    proj_ref[...] = lax.dot_general(h, wb_ref[...], _NT,
                                    preferred_element_type=F32).astype(BF16)

    f = ft[0:FOX_HEADS, :] + b_ref[...]
    c = jnp.minimum(f, 0.0) - jnp.log1p(jnp.exp(-jnp.abs(f)))
    rows = c.shape[1]
    lane = lax.broadcasted_iota(jnp.int32, c.shape, 1)
    shift = 1
    while shift < rows:
        c = c + jnp.where(lane >= shift, pltpu.roll(c, shift, axis=1), 0.0)
        shift *= 2
    c = c + carry_ref[:, 0:1]
    carry_ref[...] = jnp.broadcast_to(c[:, rows - 1:rows], carry_ref.shape)
    rest = c * (-LOG2E)
    pieces = []
    for _ in range(BIAS_TERMS):
        piece = rest.astype(BF16).astype(F32)
        pieces.append(piece)
        rest = rest - piece
    pieces.append(jnp.zeros((LANES - BIAS_TERMS * FOX_HEADS, rows), F32))
    kb_ref[...] = jnp.concatenate(pieces, axis=0).T.astype(BF16)


def _prenorm(x2, g, wf_t, b_col, w_t, seq):
    t = x2.shape[0]
    once = pl.Buffered(1)

    def w_spec(k):
        return pl.BlockSpec((pl.Element(PREP_TN), pl.Element(D_MODEL)),
                            lambda i: (_w_src_row(k, _SWA_GROUPS, _SWA_START), 0),
                            pipeline_mode=once)

    return pl.pallas_call(
        functools.partial(_prenorm_kernel, seq // NORM_ROWS),
        grid=(t // NORM_ROWS,),
        in_specs=[
            pl.BlockSpec((NORM_ROWS, D_MODEL), lambda i: (i, 0)),
            pl.BlockSpec((1, D_MODEL), lambda i: (0, 0)),
            pl.BlockSpec((F_ROWS, D_MODEL), lambda i: (0, 0)),
            pl.BlockSpec((FOX_HEADS, 1), lambda i: (0, 0)),
            *[w_spec(k) for k in range(W_PER_TILE)],
        ],
        out_specs=[
            pl.BlockSpec((NORM_ROWS, D_MODEL), lambda i: (i, 0)),
            pl.BlockSpec((NORM_ROWS, LANES), lambda i: (i, 0)),
            pl.BlockSpec((NORM_ROWS, SWA_COLS), lambda i: (i, 0)),
        ],
        out_shape=[
            jax.ShapeDtypeStruct((t, D_MODEL), BF16),
            jax.ShapeDtypeStruct((t, LANES), BF16),
            jax.ShapeDtypeStruct((t, SWA_COLS), BF16),
        ],
        scratch_shapes=[
            pltpu.VMEM((IN_TN, D_MODEL), BF16),
            pltpu.VMEM((FOX_HEADS, LANES), F32),
        ],
        compiler_params=pltpu.CompilerParams(
            dimension_semantics=("arbitrary",),
            vmem_limit_bytes=56 * 1024 * 1024),
        name="prenorm",
    )(x2, g, wf_t, b_col, *([w_t] * W_PER_TILE))


def _inproj_kernel(h_ref, *refs):
    w_refs = refs[:W_PER_TILE]
    scale_ref, proj_ref, wb_ref = refs[W_PER_TILE:]

    @pl.when(pl.program_id(1) == 0)
    def _():
        _cast_windows(w_refs, wb_ref)

    acc = lax.dot_general(h_ref[...], wb_ref[...], _NT, preferred_element_type=F32)
    proj_ref[...] = (acc * scale_ref[...]).astype(BF16)


def _inproj(h, w_t, col_scale):
    t = h.shape[0]

    def w_spec(k):
        return pl.BlockSpec(
            (pl.Element(PREP_TN), pl.Element(D_MODEL)),
            lambda j, i: (_w_src_row(j * W_PER_TILE + k, _MAIN_GROUPS, _MAIN_START), 0))

    return pl.pallas_call(
        _inproj_kernel,
        grid=(MAIN_COLS // IN_TN, t // IN_TM),
        in_specs=[
            pl.BlockSpec((IN_TM, D_MODEL), lambda j, i: (i, 0)),
            *[w_spec(k) for k in range(W_PER_TILE)],
            pl.BlockSpec((1, IN_TN), lambda j, i: (0, j)),
        ],
        out_specs=pl.BlockSpec((IN_TM, IN_TN), lambda j, i: (i, j)),
        out_shape=jax.ShapeDtypeStruct((t, MAIN_COLS), BF16),
        scratch_shapes=[pltpu.VMEM((IN_TN, D_MODEL), BF16)],
        compiler_params=pltpu.CompilerParams(
            dimension_semantics=("arbitrary", "arbitrary"),
            vmem_limit_bytes=56 * 1024 * 1024),
        name="inproj",
    )(h, *([w_t] * W_PER_TILE), col_scale)


def _fox_kernel(q_ref, k_ref, v_ref, kb_ref, o_ref, vt_ref, qt_ref, acc_ref,
                sa_ref, ma_ref, sb_ref, mb_ref):
    s0_ref, s1_ref = (sa_ref, ma_ref), (sb_ref, mb_ref)
    hp = pl.program_id(1)
    seq = k_ref.shape[0]
    t = FOX_T
    d = FOX_HEAD_DIM

    def transposed(x):
        return x.astype(F32).T.astype(BF16)

    row = lax.broadcasted_iota(jnp.int32, (d, t), 0)
    for hh in range(FOX_PAIR):
        mine = (row < BIAS_TERMS * FOX_HEADS) & (row % FOX_HEADS == hp * FOX_PAIR + hh)
        qt_ref[hh, d:2 * d, :] = jnp.where(mine, 1.0, 0.0).astype(BF16)
        vt_ref[hh, d:, :] = jnp.ones((ONES_ROWS, seq), BF16)
        for r in range(seq // FOX_CHUNK):
            rows = slice(r * FOX_CHUNK, (r + 1) * FOX_CHUNK)
            vt_ref[hh, 0:d, rows] = transposed(v_ref[rows, hh * d:(hh + 1) * d])

    def scores(i, buf):
        s_ref, max_ref = buf
        rows = pl.ds(pl.multiple_of(i * t, t), t)
        for hh in range(FOX_PAIR):
            k_aug = jnp.concatenate([k_ref[rows, hh * d:(hh + 1) * d], kb_ref[rows, :]], axis=1)
            s = jnp.dot(k_aug, qt_ref[hh], preferred_element_type=F32)
            s_ref[hh] = s
            max_ref[hh] = jnp.max(s, axis=0, keepdims=True)

    def absorb(i, buf, carry, masked):
        s_ref, max_ref = buf
        rows = pl.ds(pl.multiple_of(i * t, t), t)
        new = []
        for hh in range(FOX_PAIR):
            m = carry[hh]
            s = s_ref[hh]
            if masked:
                key = lax.broadcasted_iota(jnp.int32, s.shape, 0)
                qry = lax.broadcasted_iota(jnp.int32, s.shape, 1)
                s = jnp.where(key <= qry, s, NEG_BIG)
                tile_max = jnp.max(s, axis=0, keepdims=True)
            else:
                tile_max = max_ref[hh]
            m_new = jnp.maximum(m, tile_max)
            alpha = jnp.exp2(m - m_new)
            p = jnp.exp2(s - m_new)
            acc_ref[hh] = alpha * acc_ref[hh] + jnp.dot(
                vt_ref[hh, :, rows], p.astype(BF16), preferred_element_type=F32)
            new.append(m_new)
        return tuple(new)

    def step(i, s_cur, s_next, carry):
        scores(i + 1, s_next)
        return absorb(i, s_cur, carry, False)

    def pair(j, carry):
        carry = step(2 * j, s0_ref, s1_ref, carry)
        return step(2 * j + 1, s1_ref, s0_ref, carry)

    def q_tile(qi, _):
        q_rows = pl.ds(pl.multiple_of(qi * t, t), t)
        for hh in range(FOX_PAIR):
            qt_ref[hh, 0:d, :] = transposed(q_ref[q_rows, hh * d:(hh + 1) * d])
            acc_ref[hh] = jnp.zeros((d + ONES_ROWS, t), F32)

        def odd_tail(carry):
            carry = step(qi - 1, s0_ref, s1_ref, carry)
            return absorb(qi, s1_ref, carry, True)

        def even_tail(carry):
            return absorb(qi, s0_ref, carry, True)

        init = tuple(jnp.full((1, t), NEG_BIG, F32) for _ in range(FOX_PAIR))
        scores(0, s0_ref)
        carry = lax.fori_loop(0, qi // 2, pair, init)
        carry = lax.cond(qi % 2 == 1, odd_tail, even_tail, carry)
        for hh in range(FOX_PAIR):
            o_ref[q_rows, hh * d:(hh + 1) * d] = (
                acc_ref[hh, 0:d, :] / acc_ref[hh, d:d + 1, :]).T.astype(o_ref.dtype)
        return 0

    lax.fori_loop(0, seq // t, q_tile, 0)


def _fox_attention(proj, kb_tok, batch, seq):
    width = FOX_PAIR * FOX_HEAD_DIM
    qcol = _MAIN_OFF["q_a"] // width
    kcol = _MAIN_OFF["k_a"] // width
    vcol = _MAIN_OFF["v_a"] // width
    return pl.pallas_call(
        _fox_kernel,
        grid=(batch, FOX_HEADS // FOX_PAIR),
        in_specs=[
            pl.BlockSpec((seq, width), lambda b, h: (b, qcol + h)),
            pl.BlockSpec((seq, width), lambda b, h: (b, kcol + h)),
            pl.BlockSpec((seq, width), lambda b, h: (b, vcol + h)),
            pl.BlockSpec((seq, LANES), lambda b, h: (b, 0)),
        ],
        out_specs=pl.BlockSpec((seq, width), lambda b, h: (b, h)),
        out_shape=jax.ShapeDtypeStruct((batch * seq, FOX_WIDTH), BF16),
        scratch_shapes=[
            pltpu.VMEM((FOX_PAIR, FOX_HEAD_DIM + ONES_ROWS, seq), BF16),
            pltpu.VMEM((FOX_PAIR, 2 * FOX_HEAD_DIM, FOX_T), BF16),
            pltpu.VMEM((FOX_PAIR, FOX_HEAD_DIM + ONES_ROWS, FOX_T), F32),
            pltpu.VMEM((FOX_PAIR, FOX_T, FOX_T), F32),
            pltpu.VMEM((FOX_PAIR, 1, FOX_T), F32),
            pltpu.VMEM((FOX_PAIR, FOX_T, FOX_T), F32),
            pltpu.VMEM((FOX_PAIR, 1, FOX_T), F32),
        ],
        compiler_params=pltpu.CompilerParams(
            dimension_semantics=("arbitrary", "arbitrary"),
            vmem_limit_bytes=56 * 1024 * 1024),
        name="fox_attention",
    )(proj, proj, proj, kb_tok)


def _rope_t(xt, cos, sin):
    out = []
    for hd in range(xt.shape[0] // SWA_HEAD_DIM):
        x1 = xt[hd * SWA_HEAD_DIM: hd * SWA_HEAD_DIM + ROPE_HALF]
        x2 = xt[hd * SWA_HEAD_DIM + ROPE_HALF: (hd + 1) * SWA_HEAD_DIM]
        out.append(x1 * cos - x2 * sin)
        out.append(x2 * cos + x1 * sin)
    return jnp.concatenate(out, axis=0)


def _swa_kernel(sink_ref, q_ref, kc_ref, kp_ref, vc_ref, vp_ref, posc_ref, posp_ref,
                inv_ref, *rest):
    n_cast = (len(rest) - 3) // 2
    cast_in, o_ref = rest[:n_cast], rest[n_cast]
    cast_out, (band_ref, eye_ref) = rest[n_cast + 1:2 * n_cast + 1], rest[-2:]
    for src_ref, dst_ref in zip(cast_in, cast_out):
        dst_ref[...] = src_ref[...].astype(dst_ref.dtype)

    n = pl.program_id(1)
    w = WINDOW
    inv = inv_ref[...]

    def tables(pos_row):
        ang = inv * pos_row.astype(F32)
        return jnp.cos(ang), jnp.sin(ang)

    def rope_k(k_nat, cos, sin):
        kt = _rope_t(k_nat.astype(F32).T, cos, sin)
        return kt.T.astype(BF16)

    cos_p, sin_p = tables(posp_ref[...])
    k_prev = rope_k(kp_ref[...], cos_p, sin_p)
    v_prev = vp_ref[...]

    @pl.when((pl.program_id(0) == 0) & (n == 0))
    def _():
        key = lax.broadcasted_iota(jnp.int32, band_ref.shape, 0)
        qry = lax.broadcasted_iota(jnp.int32, band_ref.shape, 1)
        band_ref[...] = jnp.where((key <= qry + w) & (key > qry), 0.0, NEG_BIG).astype(BF16)
        src = lax.broadcasted_iota(jnp.int32, eye_ref.shape, 0)
        dst = lax.broadcasted_iota(jnp.int32, eye_ref.shape, 1) % w
        eye_ref[...] = jnp.where(src == dst, 1.0, 0.0).astype(BF16)

    no_prev = jnp.where(n > 0, 0.0, NEG_BIG)
    q_scale = SWA_HEAD_DIM ** -0.5 * LOG2E
    zeros_half = jnp.zeros((SWA_HEAD_DIM, SWA_GROUP * w), F32)
    ones_rows = jnp.ones((2 * SUBLANES, 2 * w), BF16)

    sinks = [jnp.concatenate(
        [jnp.full((1, w), sink_ref[hd] * LOG2E, F32)
         for hd in range(g * SWA_GROUP, (g + 1) * SWA_GROUP)], axis=1)
        for g in range(SWA_KV_HEADS)]

    def logits(wi, k_prev):
        tok = slice(wi * w, (wi + 1) * w)
        cos, sin = tables(posc_ref[:, tok])
        k_cur = rope_k(kc_ref[tok, :], cos, sin)
        k_all = jnp.concatenate([k_prev, k_cur], axis=0)
        k_aug = jnp.concatenate([k_all, band_ref[...]], axis=1)
        qf = q_ref[tok, :].astype(F32)
        qt = jnp.concatenate(
            [qf[:, c * LANES:(c + 1) * LANES].T for c in range(SWA_WIDTH // LANES)], axis=0)
        qt = _rope_t(qt, cos * q_scale, sin * q_scale)
        out = []
        for g in range(SWA_KV_HEADS):
            heads = range(g * SWA_GROUP, (g + 1) * SWA_GROUP)
            q_g = jnp.concatenate(
                [qt[hd * SWA_HEAD_DIM:(hd + 1) * SWA_HEAD_DIM] for hd in heads], axis=1)
            parts = [zeros_half] * SWA_KV_HEADS
            parts[g] = q_g
            q_z = jnp.concatenate(parts, axis=0).astype(BF16)
            q_aug = jnp.concatenate([q_z, eye_ref[...]], axis=0)
            s = jnp.dot(k_aug, q_aug, preferred_element_type=F32)
            if wi == 0:
                s = jnp.concatenate([s[:w] + no_prev, s[w:]], axis=0)
            out.append(s)
        return out, k_cur

    def finish(wi, scores, v_prev):
        tok = slice(wi * w, (wi + 1) * w)
        v_cur = vc_ref[tok, :]
        v_all_t = jnp.concatenate([v_prev, v_cur], axis=0).astype(F32).T.astype(BF16)
        for g in range(SWA_KV_HEADS):
            s, sink = scores[g], sinks[g]
            m = jnp.maximum(jnp.max(s, axis=0, keepdims=True), sink)
            e = jnp.exp2(s - m).astype(BF16)
            v_aug = jnp.concatenate(
                [v_all_t[g * SWA_HEAD_DIM:(g + 1) * SWA_HEAD_DIM], ones_rows], axis=0)
            pv = jnp.dot(v_aug, e, preferred_element_type=F32)
            denom = pv[SWA_HEAD_DIM:SWA_HEAD_DIM + 1] + jnp.exp2(sink - m)
            out_t = pv[:SWA_HEAD_DIM] / denom
            for jj in range(SWA_GROUP // 2):
                pair = jnp.concatenate(
                    [out_t[:, (2 * jj) * w:(2 * jj + 1) * w],
                     out_t[:, (2 * jj + 1) * w:(2 * jj + 2) * w]], axis=0)
                col = (g * (SWA_GROUP // 2) + jj) * LANES
                o_ref[tok, col:col + LANES] = pair.T.astype(o_ref.dtype)
        return v_cur

    pending, k_prev = logits(0, k_prev)
    for wi in range(SWA_NW):
        if wi + 1 < SWA_NW:
            upcoming, k_prev = logits(wi + 1, k_prev)
        v_prev = finish(wi, pending, v_prev)
        if wi + 1 < SWA_NW:
            pending = upcoming


def _swa_attention(sinks, proj, pos_row, inv_tab, batch, seq, weights_f32):
    nb = seq // SWA_QB
    per = SWA_QB // WINDOW
    qcol = _SWA_OFF["q_b"] // SWA_WIDTH
    kcol = _SWA_OFF["k_b"] // LANES
    vcol = _SWA_OFF["v_b"] // LANES
    steps = batch * nb

    def cur(b, n):
        return b * nb + n

    def prev(b, n):
        return jnp.maximum((b * nb + n) * per - 1, 0)

    def chunk_spec(wgt):
        rows, cols = wgt.shape
        assert rows % (steps * 2 * SUBLANES) == 0
        return pl.BlockSpec((rows // steps, cols), lambda b, n: (cur(b, n), 0))

    cast_specs = [chunk_spec(wgt) for wgt in weights_f32]
    return pl.pallas_call(
        _swa_kernel,
        grid=(batch, nb),
        in_specs=[
            pl.BlockSpec(memory_space=pltpu.SMEM),
            pl.BlockSpec((SWA_QB, SWA_WIDTH), lambda b, n: (cur(b, n), qcol)),
            pl.BlockSpec((SWA_QB, LANES), lambda b, n: (cur(b, n), kcol)),
            pl.BlockSpec((WINDOW, LANES), lambda b, n: (prev(b, n), kcol)),
            pl.BlockSpec((SWA_QB, LANES), lambda b, n: (cur(b, n), vcol)),
            pl.BlockSpec((WINDOW, LANES), lambda b, n: (prev(b, n), vcol)),
            pl.BlockSpec((1, SWA_QB), lambda b, n: (0, cur(b, n))),
            pl.BlockSpec((1, WINDOW), lambda b, n: (0, prev(b, n))),
            pl.BlockSpec((ROPE_HALF, LANES), lambda b, n: (0, 0)),
            *cast_specs,
        ],
        out_specs=[pl.BlockSpec((SWA_QB, SWA_WIDTH), lambda b, n: (cur(b, n), 0)), *cast_specs],
        out_shape=[jax.ShapeDtypeStruct((batch * seq, SWA_WIDTH), BF16),
                   *[jax.ShapeDtypeStruct(wgt.shape, BF16) for wgt in weights_f32]],
        scratch_shapes=[
            pltpu.VMEM((2 * WINDOW, WINDOW), BF16),
            pltpu.VMEM((WINDOW, SWA_GROUP * WINDOW), BF16),
        ],
        compiler_params=pltpu.CompilerParams(
            dimension_semantics=("arbitrary", "arbitrary")),
        name="swa_attention",
    )(sinks, proj, proj, proj, proj, proj, pos_row, pos_row, inv_tab, *weights_f32)


def _silu(z):
    return z * jax.nn.sigmoid(z)


def _epilogue_kernel(ya_ref, yb_ref, gz_ref, x_ref, p_ref,
                     wof_ref, wos_ref, wout_ref, gpost_ref, wple_ref, wgate_ref, o_ref):
    def gz(name, width):
        return gz_ref[:, _MAIN_OFF[name]:_MAIN_OFF[name] + width].astype(F32)

    ua = (ya_ref[...].astype(F32) * _silu(gz("z_a", FOX_WIDTH))).astype(BF16)
    ub = (yb_ref[...].astype(F32) * _silu(gz("z_b", SWA_WIDTH))).astype(BF16)
    oa = jnp.dot(ua, wof_ref[...], preferred_element_type=F32)
    ob = jnp.dot(ub, wos_ref[...], preferred_element_type=F32)
    merged = (jax.nn.sigmoid(gz("g_a", D_MODEL)) * oa
              + jax.nn.sigmoid(gz("g_b", D_MODEL)) * ob)
    out = jnp.dot(merged.astype(BF16), wout_ref[...], preferred_element_type=F32)
    ms = jnp.mean(out * out, axis=-1, keepdims=True)
    x1 = x_ref[...] + (out * lax.rsqrt(ms + NORM_EPS)) * gpost_ref[...]
    e = jnp.dot(p_ref[...].astype(BF16), wple_ref[...], preferred_element_type=F32)
    gate = jax.nn.sigmoid(jnp.dot(x1.astype(BF16), wgate_ref[...],
                                  preferred_element_type=F32))
    o_ref[...] = x1 + gate * e


def _epilogue(ya, yb, proj, x2, p2, wof, wos, wout, gpost, wple, wgate):
    t = x2.shape[0]
    gz_cols = _MAIN_OFF["z_b"] + SWA_WIDTH
    assert _MAIN_OFF["g_a"] == 0 and gz_cols == 2 * D_MODEL + FOX_WIDTH + SWA_WIDTH
    once = pl.Buffered(1)

    def const(shape):
        return pl.BlockSpec(shape, lambda i: (0, 0), pipeline_mode=once)

    return pl.pallas_call(
        _epilogue_kernel,
        grid=(t // EPI_TM,),
        in_specs=[
            pl.BlockSpec((EPI_TM, FOX_WIDTH), lambda i: (i, 0)),
            pl.BlockSpec((EPI_TM, SWA_WIDTH), lambda i: (i, 0)),
            pl.BlockSpec((EPI_TM, gz_cols), lambda i: (i, 0)),
            pl.BlockSpec((EPI_TM, D_MODEL), lambda i: (i, 0)),
            pl.BlockSpec((EPI_TM, PLE_DIM), lambda i: (i, 0)),
            const((FOX_WIDTH, D_MODEL)),
            const((SWA_WIDTH, D_MODEL)),
            const((D_MODEL, D_MODEL)),
            const((1, D_MODEL)),
            const((PLE_DIM, D_MODEL)),
            const((D_MODEL, D_MODEL)),
        ],
        out_specs=pl.BlockSpec((EPI_TM, D_MODEL), lambda i: (i, 0)),
        out_shape=jax.ShapeDtypeStruct((t, D_MODEL), F32),
        compiler_params=pltpu.CompilerParams(
            dimension_semantics=("arbitrary",),
            vmem_limit_bytes=56 * 1024 * 1024),
        name="epilogue",
    )(ya, yb, proj, x2, p2, wof, wos, wout, gpost, wple, wgate)


def _layer(x2, p2, pos_row, batch, seq, pre_g, w_in, b_forget, sinks, w_o_fox, w_o_swa,
           w_out, post_g, w_ple, w_ple_gate):
    w_t = w_in.T
    f0 = _REF_OFF["f_a"]
    wf_t = jnp.pad(w_t[f0:f0 + FOX_HEADS], ((0, F_ROWS - FOX_HEADS), (0, 0))).astype(BF16)
    col = jnp.arange(MAIN_COLS)
    in_q_a = (col >= _MAIN_OFF["q_a"]) & (col < _MAIN_OFF["q_a"] + FOX_WIDTH)
    col_scale = jnp.where(in_q_a, Q_A_SCALE, 1.0).astype(F32).reshape(1, MAIN_COLS)

    h, kb_tok, proj_swa = _prenorm(x2, pre_g.reshape(1, D_MODEL), wf_t,
                                   b_forget.reshape(FOX_HEADS, 1).astype(F32), w_t, seq)
    proj = _inproj(h, w_t, col_scale)
    ya = _fox_attention(proj, kb_tok, batch, seq)

    inv = ROPE_THETA ** (-jnp.arange(ROPE_HALF, dtype=F32) / ROPE_HALF)
    inv_tab = jnp.broadcast_to(inv[:, None], (ROPE_HALF, LANES))
    yb, wof, wos, wout, wple, wgate = _swa_attention(
        sinks.astype(F32), proj_swa, pos_row, inv_tab, batch, seq,
        (w_o_fox, w_o_swa, w_out, w_ple, w_ple_gate))

    return _epilogue(ya, yb, proj, x2, p2, wof, wos, wout,
                     post_g.reshape(1, D_MODEL), wple, wgate)


def kernel(x, p, positions, pre_norm_g, w_in, b_forget, sinks, w_o_fox, w_o_swa, w_out,
           post_norm_g, w_ple, w_ple_gate):
    batch, seq, _ = x.shape
    depth = p.shape[0]
    x2 = x.reshape(batch * seq, D_MODEL)
    pos_row = positions.reshape(1, batch * seq)
    for i in range(depth):
        x2 = _layer(x2, p[i].reshape(batch * seq, PLE_DIM), pos_row, batch, seq,
                    pre_norm_g[i], w_in[i], b_forget[i], sinks[i], w_o_fox[i], w_o_swa[i],
                    w_out[i], post_norm_g[i], w_ple[i], w_ple_gate[i])
    return x2.reshape(batch, seq, D_MODEL)
```

```python
import functools
import math

import jax
import jax.numpy as jnp
from jax import lax
from jax.experimental import pallas as pl
from jax.experimental.pallas import tpu as pltpu

F32 = jnp.float32
BF16 = jnp.bfloat16

D_MODEL = 2048
FOX_HEADS = 8
FOX_HEAD_DIM = 128
FOX_WIDTH = FOX_HEADS * FOX_HEAD_DIM
SWA_Q_HEADS = 16
SWA_KV_HEADS = 2
SWA_HEAD_DIM = 64
SWA_WIDTH = SWA_Q_HEADS * SWA_HEAD_DIM
SWA_KV_WIDTH = SWA_KV_HEADS * SWA_HEAD_DIM
SWA_GROUP = SWA_Q_HEADS // SWA_KV_HEADS
ROPE_HALF = SWA_HEAD_DIM // 2
WINDOW = 128
ROPE_THETA = 10000.0
PLE_DIM = 256
NORM_EPS = 1e-6
LANES = 128
LOG2E = math.log2(math.e)
NEG_BIG = -1e30

_REF_SPLITS = (FOX_WIDTH, FOX_WIDTH, FOX_WIDTH, FOX_WIDTH, FOX_HEADS, SWA_WIDTH,
               SWA_KV_WIDTH, SWA_KV_WIDTH, SWA_WIDTH, D_MODEL, D_MODEL)
_REF_NAMES = ("q_a", "k_a", "v_a", "z_a", "f_a", "q_b", "k_b", "v_b", "z_b", "g_a", "g_b")
_REF_OFF = {}
_o = 0
for _n, _s in zip(_REF_NAMES, _REF_SPLITS):
    _REF_OFF[_n] = _o
    _o += _s

IN_TM = 1024
IN_TN = 1536
NORM_ROWS = 512
F_ROWS = 16

PREP_TN = 512
_SWA_GROUPS = (
    (_REF_OFF["q_b"], SWA_WIDTH),
    (_REF_OFF["k_b"], 2 * SWA_KV_WIDTH),
)
_MAIN_GROUPS = (
    (_REF_OFF["g_a"], 2 * D_MODEL),
    (_REF_OFF["z_a"], FOX_WIDTH),
    (_REF_OFF["z_b"], SWA_WIDTH),
    (_REF_OFF["q_a"], 3 * FOX_WIDTH),
)
SUBLANES = 8


def _group_table(groups):
    assert all(c0 % SUBLANES == 0 for c0, _ in groups)
    counts = [-(-w // PREP_TN) for _, w in groups]
    return tuple(sum(counts[:i]) for i in range(len(groups))), sum(counts)


_SWA_START, _SWA_WINDOWS = _group_table(_SWA_GROUPS)
_MAIN_START, _MAIN_WINDOWS = _group_table(_MAIN_GROUPS)
W_PER_TILE = IN_TN // PREP_TN
assert _SWA_WINDOWS == W_PER_TILE and _MAIN_WINDOWS % W_PER_TILE == 0
SWA_COLS = _SWA_WINDOWS * PREP_TN
MAIN_COLS = _MAIN_WINDOWS * PREP_TN
_SWA_OFF = {"q_b": 0, "k_b": _SWA_START[1] * PREP_TN,
            "v_b": _SWA_START[1] * PREP_TN + SWA_KV_WIDTH}
_MAIN_OFF = {
    "g_a": 0, "g_b": D_MODEL,
    "z_a": _MAIN_START[1] * PREP_TN,
    "z_b": _MAIN_START[2] * PREP_TN,
    "q_a": _MAIN_START[3] * PREP_TN,
    "k_a": _MAIN_START[3] * PREP_TN + FOX_WIDTH,
    "v_a": _MAIN_START[3] * PREP_TN + 2 * FOX_WIDTH,
}
_NT = (((1,), (1,)), ((), ()))
Q_A_SCALE = FOX_HEAD_DIM ** -0.5 * LOG2E

FOX_T = 512
FOX_CHUNK = 512
FOX_PAIR = 4
BIAS_TERMS = 3
ONES_ROWS = 16

SWA_NW = 8
SWA_QB = SWA_NW * WINDOW

EPI_TM = 256


def _cast_windows(w_refs, wb_ref):
    for k, w_ref in enumerate(w_refs):
        for r in range(PREP_TN // LANES):
            rows = slice(r * LANES, (r + 1) * LANES)
            wb_ref[k * PREP_TN + r * LANES:k * PREP_TN + (r + 1) * LANES, :] = (
                w_ref[rows, :].astype(BF16))


def _w_src_row(window, groups, starts):
    tile_row = jnp.int32(0)
    for (c0, _), start in zip(groups, starts):
        tile_row = jnp.where(window >= start,
                             c0 // SUBLANES + (window - start) * (PREP_TN // SUBLANES), tile_row)
    return tile_row * SUBLANES


def _prenorm_kernel(steps_per_seq, x_ref, g_ref, wf_ref, b_ref, *refs):
    w_refs = refs[:W_PER_TILE]
    h_ref, kb_ref, proj_ref, wb_ref, carry_ref = refs[W_PER_TILE:]
    i = pl.program_id(0)

    @pl.when(i == 0)
    def _():
        _cast_windows(w_refs, wb_ref)

    @pl.when(i % steps_per_seq == 0)
    def _():
        carry_ref[...] = jnp.zeros(carry_ref.shape, F32)

    x = x_ref[...]
    ms = jnp.mean(x * x, axis=-1, keepdims=True)
    h = ((x * lax.rsqrt(ms + NORM_EPS)) * g_ref[...]).astype(BF16)
    h_ref[...] = h
    ft = lax.dot_general(wf_ref[...], h, _NT, preferred_element_type=F32)
    proj_ref[...] = lax.dot_general(h, wb_ref[...], _NT,
                                    preferred_element_type=F32).astype(BF16)

    f = ft[0:FOX_HEADS, :] + b_ref[...]
    c = jnp.minimum(f, 0.0) - jnp.log1p(jnp.exp(-jnp.abs(f)))
    rows = c.shape[1]
    lane = lax.broadcasted_iota(jnp.int32, c.shape, 1)
    shift = 1
    while shift < rows:
        c = c + jnp.where(lane >= shift, pltpu.roll(c, shift, axis=1), 0.0)
        shift *= 2
    c = c + carry_ref[:, 0:1]
    carry_ref[...] = jnp.broadcast_to(c[:, rows - 1:rows], carry_ref.shape)
    rest = c * (-LOG2E)
    pieces = []
    for _ in range(BIAS_TERMS):
        piece = rest.astype(BF16).astype(F32)
        pieces.append(piece)
        rest = rest - piece
    pieces.append(jnp.zeros((LANES - BIAS_TERMS * FOX_HEADS, rows), F32))
    kb_ref[...] = jnp.concatenate(pieces, axis=0).T.astype(BF16)


def _prenorm(x2, g, wf_t, b_col, w_t, seq):
    t = x2.shape[0]
    once = pl.Buffered(1)

    def w_spec(k):
        return pl.BlockSpec((pl.Element(PREP_TN), pl.Element(D_MODEL)),
                            lambda i: (_w_src_row(k, _SWA_GROUPS, _SWA_START), 0),
                            pipeline_mode=once)

    return pl.pallas_call(
        functools.partial(_prenorm_kernel, seq // NORM_ROWS),
        grid=(t // NORM_ROWS,),
        in_specs=[
            pl.BlockSpec((NORM_ROWS, D_MODEL), lambda i: (i, 0)),
            pl.BlockSpec((1, D_MODEL), lambda i: (0, 0)),
            pl.BlockSpec((F_ROWS, D_MODEL), lambda i: (0, 0)),
            pl.BlockSpec((FOX_HEADS, 1), lambda i: (0, 0)),
            *[w_spec(k) for k in range(W_PER_TILE)],
        ],
        out_specs=[
            pl.BlockSpec((NORM_ROWS, D_MODEL), lambda i: (i, 0)),
            pl.BlockSpec((NORM_ROWS, LANES), lambda i: (i, 0)),
            pl.BlockSpec((NORM_ROWS, SWA_COLS), lambda i: (i, 0)),
        ],
        out_shape=[
            jax.ShapeDtypeStruct((t, D_MODEL), BF16),
            jax.ShapeDtypeStruct((t, LANES), BF16),
            jax.ShapeDtypeStruct((t, SWA_COLS), BF16),
        ],
        scratch_shapes=[
            pltpu.VMEM((IN_TN, D_MODEL), BF16),
            pltpu.VMEM((FOX_HEADS, LANES), F32),
        ],
        compiler_params=pltpu.CompilerParams(
            dimension_semantics=("arbitrary",),
            vmem_limit_bytes=56 * 1024 * 1024),
        name="prenorm",
    )(x2, g, wf_t, b_col, *([w_t] * W_PER_TILE))


def _inproj_kernel(h_ref, *refs):
    w_refs = refs[:W_PER_TILE]
    scale_ref, proj_ref, wb_ref = refs[W_PER_TILE:]

    @pl.when(pl.program_id(1) == 0)
    def _():
        _cast_windows(w_refs, wb_ref)

    acc = lax.dot_general(h_ref[...], wb_ref[...], _NT, preferred_element_type=F32)
    proj_ref[...] = (acc * scale_ref[...]).astype(BF16)


def _inproj(h, w_t, col_scale):
    t = h.shape[0]

    def w_spec(k):
        return pl.BlockSpec(
            (pl.Element(PREP_TN), pl.Element(D_MODEL)),
            lambda j, i: (_w_src_row(j * W_PER_TILE + k, _MAIN_GROUPS, _MAIN_START), 0))

    return pl.pallas_call(
        _inproj_kernel,
        grid=(MAIN_COLS // IN_TN, t // IN_TM),
        in_specs=[
            pl.BlockSpec((IN_TM, D_MODEL), lambda j, i: (i, 0)),
            *[w_spec(k) for k in range(W_PER_TILE)],
            pl.BlockSpec((1, IN_TN), lambda j, i: (0, j)),
        ],
        out_specs=pl.BlockSpec((IN_TM, IN_TN), lambda j, i: (i, j)),
        out_shape=jax.ShapeDtypeStruct((t, MAIN_COLS), BF16),
        scratch_shapes=[pltpu.VMEM((IN_TN, D_MODEL), BF16)],
        compiler_params=pltpu.CompilerParams(
            dimension_semantics=("arbitrary", "arbitrary"),
            vmem_limit_bytes=56 * 1024 * 1024),
        name="inproj",
    )(h, *([w_t] * W_PER_TILE), col_scale)


def _fox_kernel(q_ref, k_ref, v_ref, kb_ref, o_ref, vt_ref, qt_ref, acc_ref,
                sa_ref, ma_ref, sb_ref, mb_ref, sc_ref, mc_ref):
    s0_ref, s1_ref, sh_ref = (sa_ref, ma_ref), (sb_ref, mb_ref), (sc_ref, mc_ref)
    hp = pl.program_id(1)
    seq = k_ref.shape[0]
    t = FOX_T
    d = FOX_HEAD_DIM

    def transposed(x):
        return x.astype(F32).T.astype(BF16)

    row = lax.broadcasted_iota(jnp.int32, (d, t), 0)
    for hh in range(FOX_PAIR):
        mine = (row < BIAS_TERMS * FOX_HEADS) & (row % FOX_HEADS == hp * FOX_PAIR + hh)
        qt_ref[hh, d:2 * d, :] = jnp.where(mine, 1.0, 0.0).astype(BF16)
        vt_ref[hh, d:, :] = jnp.ones((ONES_ROWS, seq), BF16)
        for r in range(seq // FOX_CHUNK):
            rows = slice(r * FOX_CHUNK, (r + 1) * FOX_CHUNK)
            vt_ref[hh, 0:d, rows] = transposed(v_ref[rows, hh * d:(hh + 1) * d])

    def scores(i, buf):
        s_ref, max_ref = buf
        rows = pl.ds(pl.multiple_of(i * t, t), t)
        for hh in range(FOX_PAIR):
            k_aug = jnp.concatenate([k_ref[rows, hh * d:(hh + 1) * d], kb_ref[rows, :]], axis=1)
            s = jnp.dot(k_aug, qt_ref[hh], preferred_element_type=F32)
            s_ref[hh] = s
            max_ref[hh] = jnp.max(s, axis=0, keepdims=True)

    def absorb(i, buf, carry, masked):
        s_ref, max_ref = buf
        rows = pl.ds(pl.multiple_of(i * t, t), t)
        new = []
        for hh in range(FOX_PAIR):
            m = carry[hh]
            s = s_ref[hh]
            if masked:
                key = lax.broadcasted_iota(jnp.int32, s.shape, 0)
                qry = lax.broadcasted_iota(jnp.int32, s.shape, 1)
                s = jnp.where(key <= qry, s, NEG_BIG)
                tile_max = jnp.max(s, axis=0, keepdims=True)
            else:
                tile_max = max_ref[hh]
            m_new = jnp.maximum(m, tile_max)
            alpha = jnp.exp2(m - m_new)
            p = jnp.exp2(s - m_new)
            acc_ref[hh] = alpha * acc_ref[hh] + jnp.dot(
                vt_ref[hh, :, rows], p.astype(BF16), preferred_element_type=F32)
            new.append(m_new)
        return tuple(new)

    def step(i, s_cur, s_next, carry):
        scores(i + 1, s_next)
        return absorb(i, s_cur, carry, False)

    def pair(j, carry):
        carry = step(2 * j + 1, s0_ref, s1_ref, carry)
        return step(2 * j + 2, s1_ref, s0_ref, carry)

    n_q = seq // t

    def open_tile(qi):
        q_rows = pl.ds(pl.multiple_of(qi * t, t), t)
        for hh in range(FOX_PAIR):
            qt_ref[hh, 0:d, :] = transposed(q_ref[q_rows, hh * d:(hh + 1) * d])
        scores(0, sh_ref)

    def q_tile(qi, _):
        for hh in range(FOX_PAIR):
            acc_ref[hh] = jnp.zeros((d + ONES_ROWS, t), F32)
        nxt = jnp.minimum(qi + 1, n_q - 1)

        def store_output():
            q_rows = pl.ds(pl.multiple_of(qi * t, t), t)
            for hh in range(FOX_PAIR):
                o_ref[q_rows, hh * d:(hh + 1) * d] = (
                    acc_ref[hh, 0:d, :] / acc_ref[hh, d:d + 1, :]).T.astype(o_ref.dtype)

        def finish(s_last, carry):
            open_tile(nxt)
            absorb(qi, s_last, carry, True)
            store_output()

        def only_diagonal(carry):
            absorb(0, sh_ref, carry, True)
            open_tile(nxt)
            store_output()

        def with_history(carry):
            carry = step(0, sh_ref, s0_ref, carry)
            carry = lax.fori_loop(0, (qi - 1) // 2, pair, carry)
            lax.cond((qi - 1) % 2 == 1,
                     lambda c: finish(s1_ref, step(qi - 1, s0_ref, s1_ref, c)),
                     lambda c: finish(s0_ref, c), carry)

        init = tuple(jnp.full((1, t), NEG_BIG, F32) for _ in range(FOX_PAIR))
        lax.cond(qi == 0, only_diagonal, with_history, init)
        return 0

    open_tile(0)
    lax.fori_loop(0, n_q, q_tile, 0)


def _fox_attention(proj, kb_tok, batch, seq):
    width = FOX_PAIR * FOX_HEAD_DIM
    qcol = _MAIN_OFF["q_a"] // width
    kcol = _MAIN_OFF["k_a"] // width
    vcol = _MAIN_OFF["v_a"] // width
    return pl.pallas_call(
        _fox_kernel,
        grid=(batch, FOX_HEADS // FOX_PAIR),
        in_specs=[
            pl.BlockSpec((seq, width), lambda b, h: (b, qcol + h)),
            pl.BlockSpec((seq, width), lambda b, h: (b, kcol + h)),
            pl.BlockSpec((seq, width), lambda b, h: (b, vcol + h)),
            pl.BlockSpec((seq, LANES), lambda b, h: (b, 0)),
        ],
        out_specs=pl.BlockSpec((seq, width), lambda b, h: (b, h)),
        out_shape=jax.ShapeDtypeStruct((batch * seq, FOX_WIDTH), BF16),
        scratch_shapes=[
            pltpu.VMEM((FOX_PAIR, FOX_HEAD_DIM + ONES_ROWS, seq), BF16),
            pltpu.VMEM((FOX_PAIR, 2 * FOX_HEAD_DIM, FOX_T), BF16),
            pltpu.VMEM((FOX_PAIR, FOX_HEAD_DIM + ONES_ROWS, FOX_T), F32),
            pltpu.VMEM((FOX_PAIR, FOX_T, FOX_T), F32),
            pltpu.VMEM((FOX_PAIR, 1, FOX_T), F32),
            pltpu.VMEM((FOX_PAIR, FOX_T, FOX_T), F32),
            pltpu.VMEM((FOX_PAIR, 1, FOX_T), F32),
            pltpu.VMEM((FOX_PAIR, FOX_T, FOX_T), F32),
            pltpu.VMEM((FOX_PAIR, 1, FOX_T), F32),
        ],
        compiler_params=pltpu.CompilerParams(
            dimension_semantics=("arbitrary", "arbitrary"),
            vmem_limit_bytes=56 * 1024 * 1024),
        name="fox_attention",
    )(proj, proj, proj, kb_tok)


def _rope_t(xt, cos, sin):
    out = []
    for hd in range(xt.shape[0] // SWA_HEAD_DIM):
        x1 = xt[hd * SWA_HEAD_DIM: hd * SWA_HEAD_DIM + ROPE_HALF]
        x2 = xt[hd * SWA_HEAD_DIM + ROPE_HALF: (hd + 1) * SWA_HEAD_DIM]
        out.append(x1 * cos - x2 * sin)
        out.append(x2 * cos + x1 * sin)
    return jnp.concatenate(out, axis=0)


def _swa_kernel(sink_ref, q_ref, kc_ref, kp_ref, vc_ref, vp_ref, posc_ref, posp_ref,
                inv_ref, *rest):
    n_cast = (len(rest) - 3) // 2
    cast_in, o_ref = rest[:n_cast], rest[n_cast]
    cast_out, (band_ref, eye_ref) = rest[n_cast + 1:2 * n_cast + 1], rest[-2:]
    for src_ref, dst_ref in zip(cast_in, cast_out):
        dst_ref[...] = src_ref[...].astype(dst_ref.dtype)

    n = pl.program_id(1)
    w = WINDOW
    inv = inv_ref[...]

    def tables(pos_row):
        ang = inv * pos_row.astype(F32)
        return jnp.cos(ang), jnp.sin(ang)

    def rope_k(k_nat, cos, sin):
        kt = _rope_t(k_nat.astype(F32).T, cos, sin)
        return kt.T.astype(BF16)

    cos_p, sin_p = tables(posp_ref[...])
    k_prev = rope_k(kp_ref[...], cos_p, sin_p)
    v_prev = vp_ref[...]

    @pl.when((pl.program_id(0) == 0) & (n == 0))
    def _():
        key = lax.broadcasted_iota(jnp.int32, band_ref.shape, 0)
        qry = lax.broadcasted_iota(jnp.int32, band_ref.shape, 1)
        band_ref[...] = jnp.where((key <= qry + w) & (key > qry), 0.0, NEG_BIG).astype(BF16)
        src = lax.broadcasted_iota(jnp.int32, eye_ref.shape, 0)
        dst = lax.broadcasted_iota(jnp.int32, eye_ref.shape, 1) % w
        eye_ref[...] = jnp.where(src == dst, 1.0, 0.0).astype(BF16)

    no_prev = jnp.where(n > 0, 0.0, NEG_BIG)
    q_scale = SWA_HEAD_DIM ** -0.5 * LOG2E
    zeros_half = jnp.zeros((SWA_HEAD_DIM, SWA_GROUP * w), F32)
    ones_rows = jnp.ones((2 * SUBLANES, 2 * w), BF16)

    sinks = [jnp.concatenate(
        [jnp.full((1, w), sink_ref[hd] * LOG2E, F32)
         for hd in range(g * SWA_GROUP, (g + 1) * SWA_GROUP)], axis=1)
        for g in range(SWA_KV_HEADS)]

    def logits(wi, k_prev):
        tok = slice(wi * w, (wi + 1) * w)
        cos, sin = tables(posc_ref[:, tok])
        k_cur = rope_k(kc_ref[tok, :], cos, sin)
        k_all = jnp.concatenate([k_prev, k_cur], axis=0)
        k_aug = jnp.concatenate([k_all, band_ref[...]], axis=1)
        qf = q_ref[tok, :].astype(F32)
        qt = jnp.concatenate(
            [qf[:, c * LANES:(c + 1) * LANES].T for c in range(SWA_WIDTH // LANES)], axis=0)
        qt = _rope_t(qt, cos * q_scale, sin * q_scale)
        out = []
        for g in range(SWA_KV_HEADS):
            heads = range(g * SWA_GROUP, (g + 1) * SWA_GROUP)
            q_g = jnp.concatenate(
                [qt[hd * SWA_HEAD_DIM:(hd + 1) * SWA_HEAD_DIM] for hd in heads], axis=1)
            parts = [zeros_half] * SWA_KV_HEADS
            parts[g] = q_g
            q_z = jnp.concatenate(parts, axis=0).astype(BF16)
            q_aug = jnp.concatenate([q_z, eye_ref[...]], axis=0)
            s = jnp.dot(k_aug, q_aug, preferred_element_type=F32)
            if wi == 0:
                s = jnp.concatenate([s[:w] + no_prev, s[w:]], axis=0)
            out.append(s)
        return out, k_cur

    def finish(wi, scores, v_prev):
        tok = slice(wi * w, (wi + 1) * w)
        v_cur = vc_ref[tok, :]
        v_all_t = jnp.concatenate([v_prev, v_cur], axis=0).astype(F32).T.astype(BF16)
        for g in range(SWA_KV_HEADS):
            s, sink = scores[g], sinks[g]
            m = jnp.maximum(jnp.max(s, axis=0, keepdims=True), sink)
            e = jnp.exp2(s - m).astype(BF16)
            v_aug = jnp.concatenate(
                [v_all_t[g * SWA_HEAD_DIM:(g + 1) * SWA_HEAD_DIM], ones_rows], axis=0)
            pv = jnp.dot(v_aug, e, preferred_element_type=F32)
            denom = pv[SWA_HEAD_DIM:SWA_HEAD_DIM + 1] + jnp.exp2(sink - m)
            out_t = pv[:SWA_HEAD_DIM] / denom
            for jj in range(SWA_GROUP // 2):
                pair = jnp.concatenate(
                    [out_t[:, (2 * jj) * w:(2 * jj + 1) * w],
                     out_t[:, (2 * jj + 1) * w:(2 * jj + 2) * w]], axis=0)
                col = (g * (SWA_GROUP // 2) + jj) * LANES
                o_ref[tok, col:col + LANES] = pair.T.astype(o_ref.dtype)
        return v_cur

    pending, k_prev = logits(0, k_prev)
    for wi in range(SWA_NW):
        if wi + 1 < SWA_NW:
            upcoming, k_prev = logits(wi + 1, k_prev)
        v_prev = finish(wi, pending, v_prev)
        if wi + 1 < SWA_NW:
            pending = upcoming


def _swa_attention(sinks, proj, pos_row, inv_tab, batch, seq, weights_f32):
    nb = seq // SWA_QB
    per = SWA_QB // WINDOW
    qcol = _SWA_OFF["q_b"] // SWA_WIDTH
    kcol = _SWA_OFF["k_b"] // LANES
    vcol = _SWA_OFF["v_b"] // LANES
    steps = batch * nb

    def cur(b, n):
        return b * nb + n

    def prev(b, n):
        return jnp.maximum((b * nb + n) * per - 1, 0)

    def chunk_spec(wgt):
        rows, cols = wgt.shape
        assert rows % (steps * 2 * SUBLANES) == 0
        return pl.BlockSpec((rows // steps, cols), lambda b, n: (cur(b, n), 0))

    cast_specs = [chunk_spec(wgt) for wgt in weights_f32]
    return pl.pallas_call(
        _swa_kernel,
        grid=(batch, nb),
        in_specs=[
            pl.BlockSpec(memory_space=pltpu.SMEM),
            pl.BlockSpec((SWA_QB, SWA_WIDTH), lambda b, n: (cur(b, n), qcol)),
            pl.BlockSpec((SWA_QB, LANES), lambda b, n: (cur(b, n), kcol)),
            pl.BlockSpec((WINDOW, LANES), lambda b, n: (prev(b, n), kcol)),
            pl.BlockSpec((SWA_QB, LANES), lambda b, n: (cur(b, n), vcol)),
            pl.BlockSpec((WINDOW, LANES), lambda b, n: (prev(b, n), vcol)),
            pl.BlockSpec((1, SWA_QB), lambda b, n: (0, cur(b, n))),
            pl.BlockSpec((1, WINDOW), lambda b, n: (0, prev(b, n))),
            pl.BlockSpec((ROPE_HALF, LANES), lambda b, n: (0, 0)),
            *cast_specs,
        ],
        out_specs=[pl.BlockSpec((SWA_QB, SWA_WIDTH), lambda b, n: (cur(b, n), 0)), *cast_specs],
        out_shape=[jax.ShapeDtypeStruct((batch * seq, SWA_WIDTH), BF16),
                   *[jax.ShapeDtypeStruct(wgt.shape, BF16) for wgt in weights_f32]],
        scratch_shapes=[
            pltpu.VMEM((2 * WINDOW, WINDOW), BF16),
            pltpu.VMEM((WINDOW, SWA_GROUP * WINDOW), BF16),
        ],
        compiler_params=pltpu.CompilerParams(
            dimension_semantics=("arbitrary", "arbitrary")),
        name="swa_attention",
    )(sinks, proj, proj, proj, proj, proj, pos_row, pos_row, inv_tab, *weights_f32)


def _silu(z):
    return z * jax.nn.sigmoid(z)


def _epilogue_kernel(ya_ref, yb_ref, gz_ref, x_ref, p_ref,
                     wof_ref, wos_ref, wout_ref, gpost_ref, wple_ref, wgate_ref, o_ref):
    def gz(name, width):
        return gz_ref[:, _MAIN_OFF[name]:_MAIN_OFF[name] + width].astype(F32)

    ua = (ya_ref[...].astype(F32) * _silu(gz("z_a", FOX_WIDTH))).astype(BF16)
    ub = (yb_ref[...].astype(F32) * _silu(gz("z_b", SWA_WIDTH))).astype(BF16)
    oa = jnp.dot(ua, wof_ref[...], preferred_element_type=F32)
    ob = jnp.dot(ub, wos_ref[...], preferred_element_type=F32)
    merged = (jax.nn.sigmoid(gz("g_a", D_MODEL)) * oa
              + jax.nn.sigmoid(gz("g_b", D_MODEL)) * ob)
    out = jnp.dot(merged.astype(BF16), wout_ref[...], preferred_element_type=F32)
    ms = jnp.mean(out * out, axis=-1, keepdims=True)
    x1 = x_ref[...] + (out * lax.rsqrt(ms + NORM_EPS)) * gpost_ref[...]
    e = jnp.dot(p_ref[...].astype(BF16), wple_ref[...], preferred_element_type=F32)
    gate = jax.nn.sigmoid(jnp.dot(x1.astype(BF16), wgate_ref[...],
                                  preferred_element_type=F32))
    o_ref[...] = x1 + gate * e


def _epilogue(ya, yb, proj, x2, p2, wof, wos, wout, gpost, wple, wgate):
    t = x2.shape[0]
    gz_cols = _MAIN_OFF["z_b"] + SWA_WIDTH
    assert _MAIN_OFF["g_a"] == 0 and gz_cols == 2 * D_MODEL + FOX_WIDTH + SWA_WIDTH
    once = pl.Buffered(1)

    def const(shape):
        return pl.BlockSpec(shape, lambda i: (0, 0), pipeline_mode=once)

    return pl.pallas_call(
        _epilogue_kernel,
        grid=(t // EPI_TM,),
        in_specs=[
            pl.BlockSpec((EPI_TM, FOX_WIDTH), lambda i: (i, 0)),
            pl.BlockSpec((EPI_TM, SWA_WIDTH), lambda i: (i, 0)),
            pl.BlockSpec((EPI_TM, gz_cols), lambda i: (i, 0)),
            pl.BlockSpec((EPI_TM, D_MODEL), lambda i: (i, 0)),
            pl.BlockSpec((EPI_TM, PLE_DIM), lambda i: (i, 0)),
            const((FOX_WIDTH, D_MODEL)),
            const((SWA_WIDTH, D_MODEL)),
            const((D_MODEL, D_MODEL)),
            const((1, D_MODEL)),
            const((PLE_DIM, D_MODEL)),
            const((D_MODEL, D_MODEL)),
        ],
        out_specs=pl.BlockSpec((EPI_TM, D_MODEL), lambda i: (i, 0)),
        out_shape=jax.ShapeDtypeStruct((t, D_MODEL), F32),
        compiler_params=pltpu.CompilerParams(
            dimension_semantics=("arbitrary",),
            vmem_limit_bytes=56 * 1024 * 1024),
        name="epilogue",
    )(ya, yb, proj, x2, p2, wof, wos, wout, gpost, wple, wgate)


def _layer(x2, p2, pos_row, batch, seq, pre_g, w_in, b_forget, sinks, w_o_fox, w_o_swa,
           w_out, post_g, w_ple, w_ple_gate):
    w_t = w_in.T
    f0 = _REF_OFF["f_a"]
    wf_t = jnp.pad(w_t[f0:f0 + FOX_HEADS], ((0, F_ROWS - FOX_HEADS), (0, 0))).astype(BF16)
    col = jnp.arange(MAIN_COLS)
    in_q_a = (col >= _MAIN_OFF["q_a"]) & (col < _MAIN_OFF["q_a"] + FOX_WIDTH)
    col_scale = jnp.where(in_q_a, Q_A_SCALE, 1.0).astype(F32).reshape(1, MAIN_COLS)

    h, kb_tok, proj_swa = _prenorm(x2, pre_g.reshape(1, D_MODEL), wf_t,
                                   b_forget.reshape(FOX_HEADS, 1).astype(F32), w_t, seq)
    proj = _inproj(h, w_t, col_scale)
    ya = _fox_attention(proj, kb_tok, batch, seq)

    inv = ROPE_THETA ** (-jnp.arange(ROPE_HALF, dtype=F32) / ROPE_HALF)
    inv_tab = jnp.broadcast_to(inv[:, None], (ROPE_HALF, LANES))
    yb, wof, wos, wout, wple, wgate = _swa_attention(
        sinks.astype(F32), proj_swa, pos_row, inv_tab, batch, seq,
        (w_o_fox, w_o_swa, w_out, w_ple, w_ple_gate))

    return _epilogue(ya, yb, proj, x2, p2, wof, wos, wout,
                     post_g.reshape(1, D_MODEL), wple, wgate)


def kernel(x, p, positions, pre_norm_g, w_in, b_forget, sinks, w_o_fox, w_o_swa, w_out,
           post_norm_g, w_ple, w_ple_gate):
    batch, seq, _ = x.shape
    depth = p.shape[0]
    x2 = x.reshape(batch * seq, D_MODEL)
    pos_row = positions.reshape(1, batch * seq)
    for i in range(depth):
        x2 = _layer(x2, p[i].reshape(batch * seq, PLE_DIM), pos_row, batch, seq,
                    pre_norm_g[i], w_in[i], b_forget[i], sinks[i], w_o_fox[i], w_o_swa[i],
                    w_out[i], post_norm_g[i], w_ple[i], w_ple_gate[i])
    return x2.reshape(batch, seq, D_MODEL)
```

```python
import functools
import math

import jax
import jax.numpy as jnp
from jax import lax
from jax.experimental import pallas as pl
from jax.experimental.pallas import tpu as pltpu

F32 = jnp.float32
BF16 = jnp.bfloat16

D_MODEL = 2048
FOX_HEADS = 8
FOX_HEAD_DIM = 128
FOX_WIDTH = FOX_HEADS * FOX_HEAD_DIM
SWA_Q_HEADS = 16
SWA_KV_HEADS = 2
SWA_HEAD_DIM = 64
SWA_WIDTH = SWA_Q_HEADS * SWA_HEAD_DIM
SWA_KV_WIDTH = SWA_KV_HEADS * SWA_HEAD_DIM
SWA_GROUP = SWA_Q_HEADS // SWA_KV_HEADS
ROPE_HALF = SWA_HEAD_DIM // 2
WINDOW = 128
ROPE_THETA = 10000.0
PLE_DIM = 256
NORM_EPS = 1e-6
LANES = 128
LOG2E = math.log2(math.e)
NEG_BIG = -1e30

_REF_SPLITS = (FOX_WIDTH, FOX_WIDTH, FOX_WIDTH, FOX_WIDTH, FOX_HEADS, SWA_WIDTH,
               SWA_KV_WIDTH, SWA_KV_WIDTH, SWA_WIDTH, D_MODEL, D_MODEL)
_REF_NAMES = ("q_a", "k_a", "v_a", "z_a", "f_a", "q_b", "k_b", "v_b", "z_b", "g_a", "g_b")
_REF_OFF = {}
_o = 0
for _n, _s in zip(_REF_NAMES, _REF_SPLITS):
    _REF_OFF[_n] = _o
    _o += _s

IN_TM = 1024
IN_TN = 1536
NORM_ROWS = 512
F_ROWS = 16

PREP_TN = 512
_SWA_GROUPS = (
    (_REF_OFF["q_b"], SWA_WIDTH),
    (_REF_OFF["k_b"], 2 * SWA_KV_WIDTH),
)
_MAIN_GROUPS = (
    (_REF_OFF["g_a"], 2 * D_MODEL),
    (_REF_OFF["z_a"], FOX_WIDTH),
    (_REF_OFF["z_b"], SWA_WIDTH),
    (_REF_OFF["q_a"], 3 * FOX_WIDTH),
)
SUBLANES = 8


def _group_table(groups):
    assert all(c0 % SUBLANES == 0 for c0, _ in groups)
    counts = [-(-w // PREP_TN) for _, w in groups]
    return tuple(sum(counts[:i]) for i in range(len(groups))), sum(counts)


_SWA_START, _SWA_WINDOWS = _group_table(_SWA_GROUPS)
_MAIN_START, _MAIN_WINDOWS = _group_table(_MAIN_GROUPS)
W_PER_TILE = IN_TN // PREP_TN
assert _SWA_WINDOWS == W_PER_TILE and _MAIN_WINDOWS % W_PER_TILE == 0
SWA_COLS = _SWA_WINDOWS * PREP_TN
MAIN_COLS = _MAIN_WINDOWS * PREP_TN
_SWA_OFF = {"q_b": 0, "k_b": _SWA_START[1] * PREP_TN,
            "v_b": _SWA_START[1] * PREP_TN + SWA_KV_WIDTH}
_MAIN_OFF = {
    "g_a": 0, "g_b": D_MODEL,
    "z_a": _MAIN_START[1] * PREP_TN,
    "z_b": _MAIN_START[2] * PREP_TN,
    "q_a": _MAIN_START[3] * PREP_TN,
    "k_a": _MAIN_START[3] * PREP_TN + FOX_WIDTH,
    "v_a": _MAIN_START[3] * PREP_TN + 2 * FOX_WIDTH,
}
_NT = (((1,), (1,)), ((), ()))
Q_A_SCALE = FOX_HEAD_DIM ** -0.5 * LOG2E

FOX_T = 512
FOX_CHUNK = 512
FOX_PAIR = 4
BIAS_TERMS = 3
ONES_ROWS = 16

SWA_NW = 8
SWA_QB = SWA_NW * WINDOW

EPI_TM = 256


def _cast_windows(w_refs, wb_ref):
    for k, w_ref in enumerate(w_refs):
        for r in range(PREP_TN // LANES):
            rows = slice(r * LANES, (r + 1) * LANES)
            wb_ref[k * PREP_TN + r * LANES:k * PREP_TN + (r + 1) * LANES, :] = (
                w_ref[rows, :].astype(BF16))


def _w_src_row(window, groups, starts):
    tile_row = jnp.int32(0)
    for (c0, _), start in zip(groups, starts):
        tile_row = jnp.where(window >= start,
                             c0 // SUBLANES + (window - start) * (PREP_TN // SUBLANES), tile_row)
    return tile_row * SUBLANES


def _prenorm_kernel(steps_per_seq, x_ref, g_ref, wf_ref, b_ref, *refs):
    w_refs = refs[:W_PER_TILE]
    h_ref, kb_ref, proj_ref, wb_ref, carry_ref = refs[W_PER_TILE:]
    i = pl.program_id(0)

    @pl.when(i == 0)
    def _():
        _cast_windows(w_refs, wb_ref)

    @pl.when(i % steps_per_seq == 0)
    def _():
        carry_ref[...] = jnp.zeros(carry_ref.shape, F32)

    x = x_ref[...]
    ms = jnp.mean(x * x, axis=-1, keepdims=True)
    h = ((x * lax.rsqrt(ms + NORM_EPS)) * g_ref[...]).astype(BF16)
    h_ref[...] = h
    ft = lax.dot_general(wf_ref[...], h, _NT, preferred_element_type=F32)
    proj_ref[...] = lax.dot_general(h, wb_ref[...], _NT,
                                    preferred_element_type=F32).astype(BF16)

    f = ft[0:FOX_HEADS, :] + b_ref[...]
    c = jnp.minimum(f, 0.0) - jnp.log1p(jnp.exp(-jnp.abs(f)))
    rows = c.shape[1]
    lane = lax.broadcasted_iota(jnp.int32, c.shape, 1)
    shift = 1
    while shift < rows:
        c = c + jnp.where(lane >= shift, pltpu.roll(c, shift, axis=1), 0.0)
        shift *= 2
    c = c + carry_ref[:, 0:1]
    carry_ref[...] = jnp.broadcast_to(c[:, rows - 1:rows], carry_ref.shape)
    rest = c * (-LOG2E)
    pieces = []
    for _ in range(BIAS_TERMS):
        piece = rest.astype(BF16).astype(F32)
        pieces.append(piece)
        rest = rest - piece
    pieces.append(jnp.zeros((LANES - BIAS_TERMS * FOX_HEADS, rows), F32))
    kb_ref[...] = jnp.concatenate(pieces, axis=0).T.astype(BF16)


def _prenorm(x2, g, wf_t, b_col, w_t, seq):
    t = x2.shape[0]
    once = pl.Buffered(1)

    def w_spec(k):
        return pl.BlockSpec((pl.Element(PREP_TN), pl.Element(D_MODEL)),
                            lambda i: (_w_src_row(k, _SWA_GROUPS, _SWA_START), 0),
                            pipeline_mode=once)

    return pl.pallas_call(
        functools.partial(_prenorm_kernel, seq // NORM_ROWS),
        grid=(t // NORM_ROWS,),
        in_specs=[
            pl.BlockSpec((NORM_ROWS, D_MODEL), lambda i: (i, 0)),
            pl.BlockSpec((1, D_MODEL), lambda i: (0, 0)),
            pl.BlockSpec((F_ROWS, D_MODEL), lambda i: (0, 0)),
            pl.BlockSpec((FOX_HEADS, 1), lambda i: (0, 0)),
            *[w_spec(k) for k in range(W_PER_TILE)],
        ],
        out_specs=[
            pl.BlockSpec((NORM_ROWS, D_MODEL), lambda i: (i, 0)),
            pl.BlockSpec((NORM_ROWS, LANES), lambda i: (i, 0)),
            pl.BlockSpec((NORM_ROWS, SWA_COLS), lambda i: (i, 0)),
        ],
        out_shape=[
            jax.ShapeDtypeStruct((t, D_MODEL), BF16),
            jax.ShapeDtypeStruct((t, LANES), BF16),
            jax.ShapeDtypeStruct((t, SWA_COLS), BF16),
        ],
        scratch_shapes=[
            pltpu.VMEM((IN_TN, D_MODEL), BF16),
            pltpu.VMEM((FOX_HEADS, LANES), F32),
        ],
        compiler_params=pltpu.CompilerParams(
            dimension_semantics=("arbitrary",),
            vmem_limit_bytes=56 * 1024 * 1024),
        name="prenorm",
    )(x2, g, wf_t, b_col, *([w_t] * W_PER_TILE))


def _inproj_kernel(h_ref, *refs):
    w_refs = refs[:W_PER_TILE]
    scale_ref, proj_ref, wb_ref = refs[W_PER_TILE:]

    @pl.when(pl.program_id(1) == 0)
    def _():
        _cast_windows(w_refs, wb_ref)

    acc = lax.dot_general(h_ref[...], wb_ref[...], _NT, preferred_element_type=F32)
    proj_ref[...] = (acc * scale_ref[...]).astype(BF16)


def _inproj(h, w_t, col_scale):
    t = h.shape[0]

    def w_spec(k):
        return pl.BlockSpec(
            (pl.Element(PREP_TN), pl.Element(D_MODEL)),
            lambda j, i: (_w_src_row(j * W_PER_TILE + k, _MAIN_GROUPS, _MAIN_START), 0))

    return pl.pallas_call(
        _inproj_kernel,
        grid=(MAIN_COLS // IN_TN, t // IN_TM),
        in_specs=[
            pl.BlockSpec((IN_TM, D_MODEL), lambda j, i: (i, 0)),
            *[w_spec(k) for k in range(W_PER_TILE)],
            pl.BlockSpec((1, IN_TN), lambda j, i: (0, j)),
        ],
        out_specs=pl.BlockSpec((IN_TM, IN_TN), lambda j, i: (i, j)),
        out_shape=jax.ShapeDtypeStruct((t, MAIN_COLS), BF16),
        scratch_shapes=[pltpu.VMEM((IN_TN, D_MODEL), BF16)],
        compiler_params=pltpu.CompilerParams(
            dimension_semantics=("arbitrary", "arbitrary"),
            vmem_limit_bytes=56 * 1024 * 1024),
        name="inproj",
    )(h, *([w_t] * W_PER_TILE), col_scale)


def _fox_kernel(q_ref, k_ref, v_ref, kb_ref, o_ref, vt_ref, qt_ref, acc_ref,
                sa_ref, ma_ref, sb_ref, mb_ref):
    s0_ref, s1_ref = (sa_ref, ma_ref), (sb_ref, mb_ref)
    hp = pl.program_id(1)
    seq = k_ref.shape[0]
    t = FOX_T
    d = FOX_HEAD_DIM

    def transposed(x):
        return x.astype(F32).T.astype(BF16)

    row = lax.broadcasted_iota(jnp.int32, (d, t), 0)
    for hh in range(FOX_PAIR):
        mine = (row < BIAS_TERMS * FOX_HEADS) & (row % FOX_HEADS == hp * FOX_PAIR + hh)
        qt_ref[hh, d:2 * d, :] = jnp.where(mine, 1.0, 0.0).astype(BF16)
        vt_ref[hh, d:, :] = jnp.ones((ONES_ROWS, seq), BF16)
        for r in range(seq // FOX_CHUNK):
            rows = slice(r * FOX_CHUNK, (r + 1) * FOX_CHUNK)
            vt_ref[hh, 0:d, rows] = transposed(v_ref[rows, hh * d:(hh + 1) * d])

    def scores(i, buf):
        s_ref, max_ref = buf
        rows = pl.ds(pl.multiple_of(i * t, t), t)
        for hh in range(FOX_PAIR):
            k_aug = jnp.concatenate([k_ref[rows, hh * d:(hh + 1) * d], kb_ref[rows, :]], axis=1)
            s = jnp.dot(k_aug, qt_ref[hh], preferred_element_type=F32)
            s_ref[hh] = s
            max_ref[hh] = jnp.max(s, axis=0, keepdims=True)

    def absorb(i, buf, carry, masked):
        s_ref, max_ref = buf
        rows = pl.ds(pl.multiple_of(i * t, t), t)
        new = []
        for hh in range(FOX_PAIR):
            m = carry[hh]
            s = s_ref[hh]
            if masked:
                key = lax.broadcasted_iota(jnp.int32, s.shape, 0)
                qry = lax.broadcasted_iota(jnp.int32, s.shape, 1)
                s = jnp.where(key <= qry, s, NEG_BIG)
                tile_max = jnp.max(s, axis=0, keepdims=True)
            else:
                tile_max = max_ref[hh]
            m_new = jnp.maximum(m, tile_max)
            alpha = jnp.exp2(m - m_new)
            p = jnp.exp2(s - m_new)
            acc_ref[hh] = alpha * acc_ref[hh] + jnp.dot(
                vt_ref[hh, :, rows], p.astype(BF16), preferred_element_type=F32)
            new.append(m_new)
        return tuple(new)

    def step(i, s_cur, s_next, carry):
        scores(i + 1, s_next)
        return absorb(i, s_cur, carry, False)

    def pair(j, carry):
        carry = step(2 * j, s0_ref, s1_ref, carry)
        return step(2 * j + 1, s1_ref, s0_ref, carry)

    n_q = seq // t

    def load_q(qi):
        q_rows = pl.ds(pl.multiple_of(qi * t, t), t)
        for hh in range(FOX_PAIR):
            qt_ref[hh, 0:d, :] = transposed(q_ref[q_rows, hh * d:(hh + 1) * d])

    def q_tile(qi, _):
        for hh in range(FOX_PAIR):
            acc_ref[hh] = jnp.zeros((d + ONES_ROWS, t), F32)

        def finish(s_last, carry):
            load_q(jnp.minimum(qi + 1, n_q - 1))
            absorb(qi, s_last, carry, True)
            q_rows = pl.ds(pl.multiple_of(qi * t, t), t)
            for hh in range(FOX_PAIR):
                o_ref[q_rows, hh * d:(hh + 1) * d] = (
                    acc_ref[hh, 0:d, :] / acc_ref[hh, d:d + 1, :]).T.astype(o_ref.dtype)

        def odd_tail(carry):
            finish(s1_ref, step(qi - 1, s0_ref, s1_ref, carry))

        def even_tail(carry):
            finish(s0_ref, carry)

        init = tuple(jnp.full((1, t), NEG_BIG, F32) for _ in range(FOX_PAIR))
        scores(0, s0_ref)
        carry = lax.fori_loop(0, qi // 2, pair, init)
        lax.cond(qi % 2 == 1, odd_tail, even_tail, carry)
        return 0

    load_q(0)
    lax.fori_loop(0, n_q, q_tile, 0)


def _fox_attention(proj, kb_tok, batch, seq):
    width = FOX_PAIR * FOX_HEAD_DIM
    qcol = _MAIN_OFF["q_a"] // width
    kcol = _MAIN_OFF["k_a"] // width
    vcol = _MAIN_OFF["v_a"] // width
    return pl.pallas_call(
        _fox_kernel,
        grid=(batch, FOX_HEADS // FOX_PAIR),
        in_specs=[
            pl.BlockSpec((seq, width), lambda b, h: (b, qcol + h)),
            pl.BlockSpec((seq, width), lambda b, h: (b, kcol + h)),
            pl.BlockSpec((seq, width), lambda b, h: (b, vcol + h)),
            pl.BlockSpec((seq, LANES), lambda b, h: (b, 0)),
        ],
        out_specs=pl.BlockSpec((seq, width), lambda b, h: (b, h)),
        out_shape=jax.ShapeDtypeStruct((batch * seq, FOX_WIDTH), BF16),
        scratch_shapes=[
            pltpu.VMEM((FOX_PAIR, FOX_HEAD_DIM + ONES_ROWS, seq), BF16),
            pltpu.VMEM((FOX_PAIR, 2 * FOX_HEAD_DIM, FOX_T), BF16),
            pltpu.VMEM((FOX_PAIR, FOX_HEAD_DIM + ONES_ROWS, FOX_T), F32),
            pltpu.VMEM((FOX_PAIR, FOX_T, FOX_T), F32),
            pltpu.VMEM((FOX_PAIR, 1, FOX_T), F32),
            pltpu.VMEM((FOX_PAIR, FOX_T, FOX_T), F32),
            pltpu.VMEM((FOX_PAIR, 1, FOX_T), F32),
        ],
        compiler_params=pltpu.CompilerParams(
            dimension_semantics=("arbitrary", "arbitrary"),
            vmem_limit_bytes=56 * 1024 * 1024),
        name="fox_attention",
    )(proj, proj, proj, kb_tok)


def _rope_t(xt, cos, sin):
    out = []
    for hd in range(xt.shape[0] // SWA_HEAD_DIM):
        x1 = xt[hd * SWA_HEAD_DIM: hd * SWA_HEAD_DIM + ROPE_HALF]
        x2 = xt[hd * SWA_HEAD_DIM + ROPE_HALF: (hd + 1) * SWA_HEAD_DIM]
        out.append(x1 * cos - x2 * sin)
        out.append(x2 * cos + x1 * sin)
    return jnp.concatenate(out, axis=0)


def _swa_kernel(sink_ref, q_ref, kc_ref, kp_ref, vc_ref, vp_ref, posc_ref, posp_ref,
                inv_ref, *rest):
    n_cast = (len(rest) - 3) // 2
    cast_in, o_ref = rest[:n_cast], rest[n_cast]
    cast_out, (band_ref, eye_ref) = rest[n_cast + 1:2 * n_cast + 1], rest[-2:]
    for src_ref, dst_ref in zip(cast_in, cast_out):
        dst_ref[...] = src_ref[...].astype(dst_ref.dtype)

    n = pl.program_id(1)
    w = WINDOW
    inv = inv_ref[...]

    def tables(pos_row):
        ang = inv * pos_row.astype(F32)
        return jnp.cos(ang), jnp.sin(ang)

    def rope_k(k_nat, cos, sin):
        kt = _rope_t(k_nat.astype(F32).T, cos, sin)
        return kt.T.astype(BF16)

    cos_p, sin_p = tables(posp_ref[...])
    k_prev = rope_k(kp_ref[...], cos_p, sin_p)
    v_prev = vp_ref[...]

    @pl.when((pl.program_id(0) == 0) & (n == 0))
    def _():
        key = lax.broadcasted_iota(jnp.int32, band_ref.shape, 0)
        qry = lax.broadcasted_iota(jnp.int32, band_ref.shape, 1)
        band_ref[...] = jnp.where((key <= qry + w) & (key > qry), 0.0, NEG_BIG).astype(BF16)
        src = lax.broadcasted_iota(jnp.int32, eye_ref.shape, 0)
        dst = lax.broadcasted_iota(jnp.int32, eye_ref.shape, 1) % w
        eye_ref[...] = jnp.where(src == dst, 1.0, 0.0).astype(BF16)

    no_prev = jnp.where(n > 0, 0.0, NEG_BIG)
    q_scale = SWA_HEAD_DIM ** -0.5 * LOG2E
    zeros_half = jnp.zeros((SWA_HEAD_DIM, SWA_GROUP * w), F32)
    ones_rows = jnp.ones((2 * SUBLANES, 2 * w), BF16)

    sinks = [jnp.concatenate(
        [jnp.full((1, w), sink_ref[hd] * LOG2E, F32)
         for hd in range(g * SWA_GROUP, (g + 1) * SWA_GROUP)], axis=1)
        for g in range(SWA_KV_HEADS)]

    def logits(wi, k_prev):
        tok = slice(wi * w, (wi + 1) * w)
        cos, sin = tables(posc_ref[:, tok])
        k_cur = rope_k(kc_ref[tok, :], cos, sin)
        k_all = jnp.concatenate([k_prev, k_cur], axis=0)
        k_aug = jnp.concatenate([k_all, band_ref[...]], axis=1)
        qf = q_ref[tok, :].astype(F32)
        qt = jnp.concatenate(
            [qf[:, c * LANES:(c + 1) * LANES].T for c in range(SWA_WIDTH // LANES)], axis=0)
        qt = _rope_t(qt, cos * q_scale, sin * q_scale)
        out = []
        for g in range(SWA_KV_HEADS):
            heads = range(g * SWA_GROUP, (g + 1) * SWA_GROUP)
            q_g = jnp.concatenate(
                [qt[hd * SWA_HEAD_DIM:(hd + 1) * SWA_HEAD_DIM] for hd in heads], axis=1)
            parts = [zeros_half] * SWA_KV_HEADS
            parts[g] = q_g
            q_z = jnp.concatenate(parts, axis=0).astype(BF16)
            q_aug = jnp.concatenate([q_z, eye_ref[...]], axis=0)
            s = jnp.dot(k_aug, q_aug, preferred_element_type=F32)
            if wi == 0:
                s = jnp.concatenate([s[:w] + no_prev, s[w:]], axis=0)
            out.append(s)
        return out, k_cur

    def finish(wi, scores, v_prev):
        tok = slice(wi * w, (wi + 1) * w)
        v_cur = vc_ref[tok, :]
        v_all_t = jnp.concatenate([v_prev, v_cur], axis=0).astype(F32).T.astype(BF16)
        for g in range(SWA_KV_HEADS):
            s, sink = scores[g], sinks[g]
            m = jnp.maximum(jnp.max(s, axis=0, keepdims=True), sink)
            e = jnp.exp2(s - m).astype(BF16)
            v_aug = jnp.concatenate(
                [v_all_t[g * SWA_HEAD_DIM:(g + 1) * SWA_HEAD_DIM], ones_rows], axis=0)
            pv = jnp.dot(v_aug, e, preferred_element_type=F32)
            denom = pv[SWA_HEAD_DIM:SWA_HEAD_DIM + 1] + jnp.exp2(sink - m)
            out_t = pv[:SWA_HEAD_DIM] / denom
            for jj in range(SWA_GROUP // 2):
                pair = jnp.concatenate(
                    [out_t[:, (2 * jj) * w:(2 * jj + 1) * w],
                     out_t[:, (2 * jj + 1) * w:(2 * jj + 2) * w]], axis=0)
                col = (g * (SWA_GROUP // 2) + jj) * LANES
                o_ref[tok, col:col + LANES] = pair.T.astype(o_ref.dtype)
        return v_cur

    pending, k_prev = logits(0, k_prev)
    for wi in range(SWA_NW):
        if wi + 1 < SWA_NW:
            upcoming, k_prev = logits(wi + 1, k_prev)
        v_prev = finish(wi, pending, v_prev)
        if wi + 1 < SWA_NW:
            pending = upcoming


def _swa_attention(sinks, proj, pos_row, inv_tab, batch, seq, weights_f32):
    nb = seq // SWA_QB
    per = SWA_QB // WINDOW
    qcol = _SWA_OFF["q_b"] // SWA_WIDTH
    kcol = _SWA_OFF["k_b"] // LANES
    vcol = _SWA_OFF["v_b"] // LANES
    steps = batch * nb

    def cur(b, n):
        return b * nb + n

    def prev(b, n):
        return jnp.maximum((b * nb + n) * per - 1, 0)

    def chunk_spec(wgt):
        rows, cols = wgt.shape
        assert rows % (steps * 2 * SUBLANES) == 0
        return pl.BlockSpec((rows // steps, cols), lambda b, n: (cur(b, n), 0))

    cast_specs = [chunk_spec(wgt) for wgt in weights_f32]
    return pl.pallas_call(
        _swa_kernel,
        grid=(batch, nb),
        in_specs=[
            pl.BlockSpec(memory_space=pltpu.SMEM),
            pl.BlockSpec((SWA_QB, SWA_WIDTH), lambda b, n: (cur(b, n), qcol)),
            pl.BlockSpec((SWA_QB, LANES), lambda b, n: (cur(b, n), kcol)),
            pl.BlockSpec((WINDOW, LANES), lambda b, n: (prev(b, n), kcol)),
            pl.BlockSpec((SWA_QB, LANES), lambda b, n: (cur(b, n), vcol)),
            pl.BlockSpec((WINDOW, LANES), lambda b, n: (prev(b, n), vcol)),
            pl.BlockSpec((1, SWA_QB), lambda b, n: (0, cur(b, n))),
            pl.BlockSpec((1, WINDOW), lambda b, n: (0, prev(b, n))),
            pl.BlockSpec((ROPE_HALF, LANES), lambda b, n: (0, 0)),
            *cast_specs,
        ],
        out_specs=[pl.BlockSpec((SWA_QB, SWA_WIDTH), lambda b, n: (cur(b, n), 0)), *cast_specs],
        out_shape=[jax.ShapeDtypeStruct((batch * seq, SWA_WIDTH), BF16),
                   *[jax.ShapeDtypeStruct(wgt.shape, BF16) for wgt in weights_f32]],
        scratch_shapes=[
            pltpu.VMEM((2 * WINDOW, WINDOW), BF16),
            pltpu.VMEM((WINDOW, SWA_GROUP * WINDOW), BF16),
        ],
        compiler_params=pltpu.CompilerParams(
            dimension_semantics=("arbitrary", "arbitrary")),
        name="swa_attention",
    )(sinks, proj, proj, proj, proj, proj, pos_row, pos_row, inv_tab, *weights_f32)


def _silu(z):
    return z * jax.nn.sigmoid(z)


def _epilogue_kernel(ya_ref, yb_ref, gz_ref, x_ref, p_ref,
                     wof_ref, wos_ref, wout_ref, gpost_ref, wple_ref, wgate_ref, o_ref):
    def gz(name, width):
        return gz_ref[:, _MAIN_OFF[name]:_MAIN_OFF[name] + width].astype(F32)

    ua = (ya_ref[...].astype(F32) * _silu(gz("z_a", FOX_WIDTH))).astype(BF16)
    ub = (yb_ref[...].astype(F32) * _silu(gz("z_b", SWA_WIDTH))).astype(BF16)
    oa = jnp.dot(ua, wof_ref[...], preferred_element_type=F32)
    ob = jnp.dot(ub, wos_ref[...], preferred_element_type=F32)
    merged = (jax.nn.sigmoid(gz("g_a", D_MODEL)) * oa
              + jax.nn.sigmoid(gz("g_b", D_MODEL)) * ob)
    out = jnp.dot(merged.astype(BF16), wout_ref[...], preferred_element_type=F32)
    ms = jnp.mean(out * out, axis=-1, keepdims=True)
    x1 = x_ref[...] + (out * lax.rsqrt(ms + NORM_EPS)) * gpost_ref[...]
    e = jnp.dot(p_ref[...].astype(BF16), wple_ref[...], preferred_element_type=F32)
    gate = jax.nn.sigmoid(jnp.dot(x1.astype(BF16), wgate_ref[...],
                                  preferred_element_type=F32))
    o_ref[...] = x1 + gate * e


def _epilogue(ya, yb, proj, x2, p2, wof, wos, wout, gpost, wple, wgate):
    t = x2.shape[0]
    gz_cols = _MAIN_OFF["z_b"] + SWA_WIDTH
    assert _MAIN_OFF["g_a"] == 0 and gz_cols == 2 * D_MODEL + FOX_WIDTH + SWA_WIDTH
    once = pl.Buffered(1)

    def const(shape):
        return pl.BlockSpec(shape, lambda i: (0, 0), pipeline_mode=once)

    return pl.pallas_call(
        _epilogue_kernel,
        grid=(t // EPI_TM,),
        in_specs=[
            pl.BlockSpec((EPI_TM, FOX_WIDTH), lambda i: (i, 0)),
            pl.BlockSpec((EPI_TM, SWA_WIDTH), lambda i: (i, 0)),
            pl.BlockSpec((EPI_TM, gz_cols), lambda i: (i, 0)),
            pl.BlockSpec((EPI_TM, D_MODEL), lambda i: (i, 0)),
            pl.BlockSpec((EPI_TM, PLE_DIM), lambda i: (i, 0)),
            const((FOX_WIDTH, D_MODEL)),
            const((SWA_WIDTH, D_MODEL)),
            const((D_MODEL, D_MODEL)),
            const((1, D_MODEL)),
            const((PLE_DIM, D_MODEL)),
            const((D_MODEL, D_MODEL)),
        ],
        out_specs=pl.BlockSpec((EPI_TM, D_MODEL), lambda i: (i, 0)),
        out_shape=jax.ShapeDtypeStruct((t, D_MODEL), F32),
        compiler_params=pltpu.CompilerParams(
            dimension_semantics=("arbitrary",),
            vmem_limit_bytes=56 * 1024 * 1024),
        name="epilogue",
    )(ya, yb, proj, x2, p2, wof, wos, wout, gpost, wple, wgate)


def _layer(x2, p2, pos_row, batch, seq, pre_g, w_in, b_forget, sinks, w_o_fox, w_o_swa,
           w_out, post_g, w_ple, w_ple_gate):
    w_t = w_in.T
    f0 = _REF_OFF["f_a"]
    wf_t = jnp.pad(w_t[f0:f0 + FOX_HEADS], ((0, F_ROWS - FOX_HEADS), (0, 0))).astype(BF16)
    col = jnp.arange(MAIN_COLS)
    in_q_a = (col >= _MAIN_OFF["q_a"]) & (col < _MAIN_OFF["q_a"] + FOX_WIDTH)
    col_scale = jnp.where(in_q_a, Q_A_SCALE, 1.0).astype(F32).reshape(1, MAIN_COLS)

    h, kb_tok, proj_swa = _prenorm(x2, pre_g.reshape(1, D_MODEL), wf_t,
                                   b_forget.reshape(FOX_HEADS, 1).astype(F32), w_t, seq)
    proj = _inproj(h, w_t, col_scale)
    ya = _fox_attention(proj, kb_tok, batch, seq)

    inv = ROPE_THETA ** (-jnp.arange(ROPE_HALF, dtype=F32) / ROPE_HALF)
    inv_tab = jnp.broadcast_to(inv[:, None], (ROPE_HALF, LANES))
    yb, wof, wos, wout, wple, wgate = _swa_attention(
        sinks.astype(F32), proj_swa, pos_row, inv_tab, batch, seq,
        (w_o_fox, w_o_swa, w_out, w_ple, w_ple_gate))

    return _epilogue(ya, yb, proj, x2, p2, wof, wos, wout,
                     post_g.reshape(1, D_MODEL), wple, wgate)


def kernel(x, p, positions, pre_norm_g, w_in, b_forget, sinks, w_o_fox, w_o_swa, w_out,
           post_norm_g, w_ple, w_ple_gate):
    batch, seq, _ = x.shape
    depth = p.shape[0]
    x2 = x.reshape(batch * seq, D_MODEL)
    pos_row = positions.reshape(1, batch * seq)
    for i in range(depth):
        x2 = _layer(x2, p[i].reshape(batch * seq, PLE_DIM), pos_row, batch, seq,
                    pre_norm_g[i], w_in[i], b_forget[i], sinks[i], w_o_fox[i], w_o_swa[i],
                    w_out[i], post_norm_g[i], w_ple[i], w_ple_gate[i])
    return x2.reshape(batch, seq, D_MODEL)
```

```python
import functools
import math

import jax
import jax.numpy as jnp
from jax import lax
from jax.experimental import pallas as pl
from jax.experimental.pallas import tpu as pltpu

F32 = jnp.float32
BF16 = jnp.bfloat16

D_MODEL = 2048
FOX_HEADS = 8
FOX_HEAD_DIM = 128
FOX_WIDTH = FOX_HEADS * FOX_HEAD_DIM
SWA_Q_HEADS = 16
SWA_KV_HEADS = 2
SWA_HEAD_DIM = 64
SWA_WIDTH = SWA_Q_HEADS * SWA_HEAD_DIM
SWA_KV_WIDTH = SWA_KV_HEADS * SWA_HEAD_DIM
SWA_GROUP = SWA_Q_HEADS // SWA_KV_HEADS
ROPE_HALF = SWA_HEAD_DIM // 2
WINDOW = 128
ROPE_THETA = 10000.0
PLE_DIM = 256
NORM_EPS = 1e-6
LANES = 128
VMEM_LIMIT_BYTES = 56 * 1024 * 1024
LOG2E = math.log2(math.e)
NEG_BIG = -1e30

_REF_SPLITS = (FOX_WIDTH, FOX_WIDTH, FOX_WIDTH, FOX_WIDTH, FOX_HEADS, SWA_WIDTH,
               SWA_KV_WIDTH, SWA_KV_WIDTH, SWA_WIDTH, D_MODEL, D_MODEL)
_REF_NAMES = ("q_a", "k_a", "v_a", "z_a", "f_a", "q_b", "k_b", "v_b", "z_b", "g_a", "g_b")
_REF_OFF = {}
_o = 0
for _n, _s in zip(_REF_NAMES, _REF_SPLITS):
    _REF_OFF[_n] = _o
    _o += _s

IN_TM = 1024
IN_TN = 1536
NORM_ROWS = 512
F_ROWS = 16

PREP_TN = 512
_SWA_GROUPS = (
    (_REF_OFF["q_b"], SWA_WIDTH),
    (_REF_OFF["k_b"], 2 * SWA_KV_WIDTH),
)
_MAIN_GROUPS = (
    (_REF_OFF["g_a"], 2 * D_MODEL),
    (_REF_OFF["z_a"], FOX_WIDTH),
    (_REF_OFF["z_b"], SWA_WIDTH),
    (_REF_OFF["q_a"], 3 * FOX_WIDTH),
)
SUBLANES = 8


def _group_table(groups):
    assert all(c0 % SUBLANES == 0 for c0, _ in groups)
    counts = [-(-w // PREP_TN) for _, w in groups]
    return tuple(sum(counts[:i]) for i in range(len(groups))), sum(counts)


_SWA_START, _SWA_WINDOWS = _group_table(_SWA_GROUPS)
_MAIN_START, _MAIN_WINDOWS = _group_table(_MAIN_GROUPS)
W_PER_TILE = IN_TN // PREP_TN
assert _SWA_WINDOWS == W_PER_TILE and _MAIN_WINDOWS % W_PER_TILE == 0
SWA_COLS = _SWA_WINDOWS * PREP_TN
MAIN_COLS = _MAIN_WINDOWS * PREP_TN
_SWA_OFF = {"q_b": 0, "k_b": _SWA_START[1] * PREP_TN,
            "v_b": _SWA_START[1] * PREP_TN + SWA_KV_WIDTH}
_MAIN_OFF = {
    "g_a": 0, "g_b": D_MODEL,
    "z_a": _MAIN_START[1] * PREP_TN,
    "z_b": _MAIN_START[2] * PREP_TN,
    "q_a": _MAIN_START[3] * PREP_TN,
    "k_a": _MAIN_START[3] * PREP_TN + FOX_WIDTH,
    "v_a": _MAIN_START[3] * PREP_TN + 2 * FOX_WIDTH,
}
_NT = (((1,), (1,)), ((), ()))
Q_A_SCALE = FOX_HEAD_DIM ** -0.5 * LOG2E

FOX_T = 512
FOX_CHUNK = 512
FOX_PAIR = 4
BIAS_TERMS = 3
ONES_ROWS = 16

SWA_NW = 8
SWA_QB = SWA_NW * WINDOW

EPI_TM = 256


def _cast_windows(w_refs, wb_ref):
    for k, w_ref in enumerate(w_refs):
        for r in range(PREP_TN // LANES):
            rows = slice(r * LANES, (r + 1) * LANES)
            wb_ref[k * PREP_TN + r * LANES:k * PREP_TN + (r + 1) * LANES, :] = (
                w_ref[rows, :].astype(BF16))


def _w_src_row(window, groups, starts):
    tile_row = jnp.int32(0)
    for (c0, _), start in zip(groups, starts):
        tile_row = jnp.where(window >= start,
                             c0 // SUBLANES + (window - start) * (PREP_TN // SUBLANES), tile_row)
    return tile_row * SUBLANES


def _prenorm_kernel(steps_per_seq, x_ref, g_ref, wf_ref, b_ref, *refs):
    w_refs = refs[:W_PER_TILE]
    h_ref, kb_ref, proj_ref, wb_ref, carry_ref = refs[W_PER_TILE:]
    i = pl.program_id(0)

    @pl.when(i == 0)
    def _():
        _cast_windows(w_refs, wb_ref)

    @pl.when(i % steps_per_seq == 0)
    def _():
        carry_ref[...] = jnp.zeros(carry_ref.shape, F32)

    x = x_ref[...]
    ms = jnp.mean(x * x, axis=-1, keepdims=True)
    h = ((x * lax.rsqrt(ms + NORM_EPS)) * g_ref[...]).astype(BF16)
    h_ref[...] = h
    ft = lax.dot_general(wf_ref[...], h, _NT, preferred_element_type=F32)
    proj_ref[...] = lax.dot_general(h, wb_ref[...], _NT,
                                    preferred_element_type=F32).astype(BF16)

    f = ft[0:FOX_HEADS, :] + b_ref[...]
    c = jnp.minimum(f, 0.0) - jnp.log1p(jnp.exp(-jnp.abs(f)))
    rows = c.shape[1]
    lane = lax.broadcasted_iota(jnp.int32, c.shape, 1)
    shift = 1
    while shift < rows:
        c = c + jnp.where(lane >= shift, pltpu.roll(c, shift, axis=1), 0.0)
        shift *= 2
    c = c + carry_ref[:, 0:1]
    carry_ref[...] = jnp.broadcast_to(c[:, rows - 1:rows], carry_ref.shape)
    rest = c * (-LOG2E)
    pieces = []
    for _ in range(BIAS_TERMS):
        piece = rest.astype(BF16).astype(F32)
        pieces.append(piece)
        rest = rest - piece
    pieces.append(jnp.zeros((LANES - BIAS_TERMS * FOX_HEADS, rows), F32))
    kb_ref[...] = jnp.concatenate(pieces, axis=0).T.astype(BF16)


def _prenorm(x2, g, wf_t, b_col, w_t, seq):
    t = x2.shape[0]
    once = pl.Buffered(1)

    def w_spec(k):
        return pl.BlockSpec((pl.Element(PREP_TN), pl.Element(D_MODEL)),
                            lambda i: (_w_src_row(k, _SWA_GROUPS, _SWA_START), 0),
                            pipeline_mode=once)

    return pl.pallas_call(
        functools.partial(_prenorm_kernel, seq // NORM_ROWS),
        grid=(t // NORM_ROWS,),
        in_specs=[
            pl.BlockSpec((NORM_ROWS, D_MODEL), lambda i: (i, 0)),
            pl.BlockSpec((1, D_MODEL), lambda i: (0, 0)),
            pl.BlockSpec((F_ROWS, D_MODEL), lambda i: (0, 0)),
            pl.BlockSpec((FOX_HEADS, 1), lambda i: (0, 0)),
            *[w_spec(k) for k in range(W_PER_TILE)],
        ],
        out_specs=[
            pl.BlockSpec((NORM_ROWS, D_MODEL), lambda i: (i, 0)),
            pl.BlockSpec((NORM_ROWS, LANES), lambda i: (i, 0)),
            pl.BlockSpec((NORM_ROWS, SWA_COLS), lambda i: (i, 0)),
        ],
        out_shape=[
            jax.ShapeDtypeStruct((t, D_MODEL), BF16),
            jax.ShapeDtypeStruct((t, LANES), BF16),
            jax.ShapeDtypeStruct((t, SWA_COLS), BF16),
        ],
        scratch_shapes=[
            pltpu.VMEM((IN_TN, D_MODEL), BF16),
            pltpu.VMEM((FOX_HEADS, LANES), F32),
        ],
        compiler_params=pltpu.CompilerParams(
            dimension_semantics=("arbitrary",),
            vmem_limit_bytes=VMEM_LIMIT_BYTES),
        name="prenorm",
    )(x2, g, wf_t, b_col, *([w_t] * W_PER_TILE))


def _inproj_kernel(h_ref, *refs):
    w_refs = refs[:W_PER_TILE]
    scale_ref, proj_ref, wb_ref = refs[W_PER_TILE:]

    @pl.when(pl.program_id(1) == 0)
    def _():
        _cast_windows(w_refs, wb_ref)

    acc = lax.dot_general(h_ref[...], wb_ref[...], _NT, preferred_element_type=F32)
    proj_ref[...] = (acc * scale_ref[...]).astype(BF16)


def _inproj(h, w_t, col_scale):
    t = h.shape[0]

    def w_spec(k):
        return pl.BlockSpec(
            (pl.Element(PREP_TN), pl.Element(D_MODEL)),
            lambda j, i: (_w_src_row(j * W_PER_TILE + k, _MAIN_GROUPS, _MAIN_START), 0))

    return pl.pallas_call(
        _inproj_kernel,
        grid=(MAIN_COLS // IN_TN, t // IN_TM),
        in_specs=[
            pl.BlockSpec((IN_TM, D_MODEL), lambda j, i: (i, 0)),
            *[w_spec(k) for k in range(W_PER_TILE)],
            pl.BlockSpec((1, IN_TN), lambda j, i: (0, j)),
        ],
        out_specs=pl.BlockSpec((IN_TM, IN_TN), lambda j, i: (i, j)),
        out_shape=jax.ShapeDtypeStruct((t, MAIN_COLS), BF16),
        scratch_shapes=[pltpu.VMEM((IN_TN, D_MODEL), BF16)],
        compiler_params=pltpu.CompilerParams(
            dimension_semantics=("arbitrary", "arbitrary"),
            vmem_limit_bytes=VMEM_LIMIT_BYTES),
        name="inproj",
    )(h, *([w_t] * W_PER_TILE), col_scale)


def _fox_kernel(q_ref, k_ref, v_ref, kb_ref, o_ref, vt_ref, qt_ref, acc_ref,
                sa_ref, ma_ref, sb_ref, mb_ref):
    s0_ref, s1_ref = (sa_ref, ma_ref), (sb_ref, mb_ref)
    hp = pl.program_id(1)
    seq = k_ref.shape[0]
    t = FOX_T
    d = FOX_HEAD_DIM

    def transposed(x):
        return x.astype(F32).T.astype(BF16)

    row = lax.broadcasted_iota(jnp.int32, (d, t), 0)
    for hh in range(FOX_PAIR):
        mine = (row < BIAS_TERMS * FOX_HEADS) & (row % FOX_HEADS == hp * FOX_PAIR + hh)
        qt_ref[hh, d:2 * d, :] = jnp.where(mine, 1.0, 0.0).astype(BF16)
        vt_ref[hh, d:, :] = jnp.ones((ONES_ROWS, seq), BF16)
        for r in range(seq // FOX_CHUNK):
            rows = slice(r * FOX_CHUNK, (r + 1) * FOX_CHUNK)
            vt_ref[hh, 0:d, rows] = transposed(v_ref[rows, hh * d:(hh + 1) * d])

    def scores(i, buf):
        s_ref, max_ref = buf
        rows = pl.ds(pl.multiple_of(i * t, t), t)
        for hh in range(FOX_PAIR):
            k_aug = jnp.concatenate([k_ref[rows, hh * d:(hh + 1) * d], kb_ref[rows, :]], axis=1)
            s = jnp.dot(k_aug, qt_ref[hh], preferred_element_type=F32)
            s_ref[hh] = s
            max_ref[hh] = jnp.max(s, axis=0, keepdims=True)

    def absorb(i, buf, carry, masked):
        s_ref, max_ref = buf
        rows = pl.ds(pl.multiple_of(i * t, t), t)
        new = []
        for hh in range(FOX_PAIR):
            m = carry[hh]
            s = s_ref[hh]
            if masked:
                key = lax.broadcasted_iota(jnp.int32, s.shape, 0)
                qry = lax.broadcasted_iota(jnp.int32, s.shape, 1)
                s = jnp.where(key <= qry, s, NEG_BIG)
                tile_max = jnp.max(s, axis=0, keepdims=True)
            else:
                tile_max = max_ref[hh]
            m_new = jnp.maximum(m, tile_max)
            alpha = jnp.exp2(m - m_new)
            p = jnp.exp2(s - m_new)
            acc_ref[hh] = alpha * acc_ref[hh] + jnp.dot(
                vt_ref[hh, :, rows], p.astype(BF16), preferred_element_type=F32)
            new.append(m_new)
        return tuple(new)

    def step(i, s_cur, s_next, carry):
        scores(i + 1, s_next)
        return absorb(i, s_cur, carry, False)

    def pair(j, carry):
        carry = step(2 * j, s0_ref, s1_ref, carry)
        return step(2 * j + 1, s1_ref, s0_ref, carry)

    n_q = seq // t

    def load_q(qi):
        q_rows = pl.ds(pl.multiple_of(qi * t, t), t)
        for hh in range(FOX_PAIR):
            qt_ref[hh, 0:d, :] = transposed(q_ref[q_rows, hh * d:(hh + 1) * d])

    def q_tile(qi, _):
        for hh in range(FOX_PAIR):
            acc_ref[hh] = jnp.zeros((d + ONES_ROWS, t), F32)

        def finish(s_last, carry):
            load_q(jnp.minimum(qi + 1, n_q - 1))
            absorb(qi, s_last, carry, True)
            q_rows = pl.ds(pl.multiple_of(qi * t, t), t)
            for hh in range(FOX_PAIR):
                o_ref[q_rows, hh * d:(hh + 1) * d] = (
                    acc_ref[hh, 0:d, :] / acc_ref[hh, d:d + 1, :]).T.astype(o_ref.dtype)

        def odd_tail(carry):
            finish(s1_ref, step(qi - 1, s0_ref, s1_ref, carry))

        def even_tail(carry):
            finish(s0_ref, carry)

        init = tuple(jnp.full((1, t), NEG_BIG, F32) for _ in range(FOX_PAIR))
        scores(0, s0_ref)
        carry = lax.fori_loop(0, qi // 2, pair, init)
        lax.cond(qi % 2 == 1, odd_tail, even_tail, carry)
        return 0

    load_q(0)
    lax.fori_loop(0, n_q, q_tile, 0)


def _fox_attention(proj, kb_tok, batch, seq):
    width = FOX_PAIR * FOX_HEAD_DIM
    qcol = _MAIN_OFF["q_a"] // width
    kcol = _MAIN_OFF["k_a"] // width
    vcol = _MAIN_OFF["v_a"] // width
    return pl.pallas_call(
        _fox_kernel,
        grid=(batch, FOX_HEADS // FOX_PAIR),
        in_specs=[
            pl.BlockSpec((seq, width), lambda b, h: (b, qcol + h)),
            pl.BlockSpec((seq, width), lambda b, h: (b, kcol + h)),
            pl.BlockSpec((seq, width), lambda b, h: (b, vcol + h)),
            pl.BlockSpec((seq, LANES), lambda b, h: (b, 0)),
        ],
        out_specs=pl.BlockSpec((seq, width), lambda b, h: (b, h)),
        out_shape=jax.ShapeDtypeStruct((batch * seq, FOX_WIDTH), BF16),
        scratch_shapes=[
            pltpu.VMEM((FOX_PAIR, FOX_HEAD_DIM + ONES_ROWS, seq), BF16),
            pltpu.VMEM((FOX_PAIR, 2 * FOX_HEAD_DIM, FOX_T), BF16),
            pltpu.VMEM((FOX_PAIR, FOX_HEAD_DIM + ONES_ROWS, FOX_T), F32),
            pltpu.VMEM((FOX_PAIR, FOX_T, FOX_T), F32),
            pltpu.VMEM((FOX_PAIR, 1, FOX_T), F32),
            pltpu.VMEM((FOX_PAIR, FOX_T, FOX_T), F32),
            pltpu.VMEM((FOX_PAIR, 1, FOX_T), F32),
        ],
        compiler_params=pltpu.CompilerParams(
            dimension_semantics=("arbitrary", "arbitrary"),
            vmem_limit_bytes=VMEM_LIMIT_BYTES),
        name="fox_attention",
    )(proj, proj, proj, kb_tok)


def _rope_t(xt, cos, sin):
    out = []
    for hd in range(xt.shape[0] // SWA_HEAD_DIM):
        x1 = xt[hd * SWA_HEAD_DIM: hd * SWA_HEAD_DIM + ROPE_HALF]
        x2 = xt[hd * SWA_HEAD_DIM + ROPE_HALF: (hd + 1) * SWA_HEAD_DIM]
        out.append(x1 * cos - x2 * sin)
        out.append(x2 * cos + x1 * sin)
    return jnp.concatenate(out, axis=0)


def _swa_kernel(sink_ref, q_ref, kc_ref, kp_ref, vc_ref, vp_ref, posc_ref, posp_ref,
                inv_ref, *rest):
    n_cast = (len(rest) - 3) // 2
    cast_in, o_ref = rest[:n_cast], rest[n_cast]
    cast_out, (band_ref, eye_ref) = rest[n_cast + 1:2 * n_cast + 1], rest[-2:]
    for src_ref, dst_ref in zip(cast_in, cast_out):
        dst_ref[...] = src_ref[...].astype(dst_ref.dtype)

    n = pl.program_id(1)
    w = WINDOW
    inv = inv_ref[...]

    def tables(pos_row):
        ang = inv * pos_row.astype(F32)
        return jnp.cos(ang), jnp.sin(ang)

    def rope_k(k_nat, cos, sin):
        kt = _rope_t(k_nat.astype(F32).T, cos, sin)
        return kt.T.astype(BF16)

    cos_p, sin_p = tables(posp_ref[...])
    k_prev = rope_k(kp_ref[...], cos_p, sin_p)
    v_prev = vp_ref[...]

    @pl.when((pl.program_id(0) == 0) & (n == 0))
    def _():
        key = lax.broadcasted_iota(jnp.int32, band_ref.shape, 0)
        qry = lax.broadcasted_iota(jnp.int32, band_ref.shape, 1)
        band_ref[...] = jnp.where((key <= qry + w) & (key > qry), 0.0, NEG_BIG).astype(BF16)
        src = lax.broadcasted_iota(jnp.int32, eye_ref.shape, 0)
        dst = lax.broadcasted_iota(jnp.int32, eye_ref.shape, 1) % w
        eye_ref[...] = jnp.where(src == dst, 1.0, 0.0).astype(BF16)

    no_prev = jnp.where(n > 0, 0.0, NEG_BIG)
    q_scale = SWA_HEAD_DIM ** -0.5 * LOG2E
    zeros_half = jnp.zeros((SWA_HEAD_DIM, SWA_GROUP * w), F32)
    ones_rows = jnp.ones((2 * SUBLANES, 2 * w), BF16)

    sinks = [jnp.concatenate(
        [jnp.full((1, w), sink_ref[hd] * LOG2E, F32)
         for hd in range(g * SWA_GROUP, (g + 1) * SWA_GROUP)], axis=1)
        for g in range(SWA_KV_HEADS)]

    def logits(wi, k_prev):
        tok = slice(wi * w, (wi + 1) * w)
        cos, sin = tables(posc_ref[:, tok])
        k_cur = rope_k(kc_ref[tok, :], cos, sin)
        k_all = jnp.concatenate([k_prev, k_cur], axis=0)
        k_aug = jnp.concatenate([k_all, band_ref[...]], axis=1)
        qf = q_ref[tok, :].astype(F32)
        qt = jnp.concatenate(
            [qf[:, c * LANES:(c + 1) * LANES].T for c in range(SWA_WIDTH // LANES)], axis=0)
        qt = _rope_t(qt, cos * q_scale, sin * q_scale)
        out = []
        for g in range(SWA_KV_HEADS):
            heads = range(g * SWA_GROUP, (g + 1) * SWA_GROUP)
            q_g = jnp.concatenate(
                [qt[hd * SWA_HEAD_DIM:(hd + 1) * SWA_HEAD_DIM] for hd in heads], axis=1)
            parts = [zeros_half] * SWA_KV_HEADS
            parts[g] = q_g
            q_z = jnp.concatenate(parts, axis=0).astype(BF16)
            q_aug = jnp.concatenate([q_z, eye_ref[...]], axis=0)
            s = jnp.dot(k_aug, q_aug, preferred_element_type=F32)
            if wi == 0:
                s = jnp.concatenate([s[:w] + no_prev, s[w:]], axis=0)
            out.append(s)
        return out, k_cur

    def finish(wi, scores, v_prev):
        tok = slice(wi * w, (wi + 1) * w)
        v_cur = vc_ref[tok, :]
        v_all_t = jnp.concatenate([v_prev, v_cur], axis=0).astype(F32).T.astype(BF16)
        for g in range(SWA_KV_HEADS):
            s, sink = scores[g], sinks[g]
            m = jnp.maximum(jnp.max(s, axis=0, keepdims=True), sink)
            e = jnp.exp2(s - m).astype(BF16)
            v_aug = jnp.concatenate(
                [v_all_t[g * SWA_HEAD_DIM:(g + 1) * SWA_HEAD_DIM], ones_rows], axis=0)
            pv = jnp.dot(v_aug, e, preferred_element_type=F32)
            denom = pv[SWA_HEAD_DIM:SWA_HEAD_DIM + 1] + jnp.exp2(sink - m)
            out_t = pv[:SWA_HEAD_DIM] / denom
            for jj in range(SWA_GROUP // 2):
                pair = jnp.concatenate(
                    [out_t[:, (2 * jj) * w:(2 * jj + 1) * w],
                     out_t[:, (2 * jj + 1) * w:(2 * jj + 2) * w]], axis=0)
                col = (g * (SWA_GROUP // 2) + jj) * LANES
                o_ref[tok, col:col + LANES] = pair.T.astype(o_ref.dtype)
        return v_cur

    pending, k_prev = logits(0, k_prev)
    for wi in range(SWA_NW):
        if wi + 1 < SWA_NW:
            upcoming, k_prev = logits(wi + 1, k_prev)
        v_prev = finish(wi, pending, v_prev)
        if wi + 1 < SWA_NW:
            pending = upcoming


def _swa_attention(sinks, proj, pos_row, inv_tab, batch, seq, weights_f32):
    nb = seq // SWA_QB
    per = SWA_QB // WINDOW
    qcol = _SWA_OFF["q_b"] // SWA_WIDTH
    kcol = _SWA_OFF["k_b"] // LANES
    vcol = _SWA_OFF["v_b"] // LANES
    steps = batch * nb

    def cur(b, n):
        return b * nb + n

    def prev(b, n):
        return jnp.maximum((b * nb + n) * per - 1, 0)

    def chunk_spec(wgt):
        rows, cols = wgt.shape
        assert rows % (steps * 2 * SUBLANES) == 0
        return pl.BlockSpec((rows // steps, cols), lambda b, n: (cur(b, n), 0))

    cast_specs = [chunk_spec(wgt) for wgt in weights_f32]
    return pl.pallas_call(
        _swa_kernel,
        grid=(batch, nb),
        in_specs=[
            pl.BlockSpec(memory_space=pltpu.SMEM),
            pl.BlockSpec((SWA_QB, SWA_WIDTH), lambda b, n: (cur(b, n), qcol)),
            pl.BlockSpec((SWA_QB, LANES), lambda b, n: (cur(b, n), kcol)),
            pl.BlockSpec((WINDOW, LANES), lambda b, n: (prev(b, n), kcol)),
            pl.BlockSpec((SWA_QB, LANES), lambda b, n: (cur(b, n), vcol)),
            pl.BlockSpec((WINDOW, LANES), lambda b, n: (prev(b, n), vcol)),
            pl.BlockSpec((1, SWA_QB), lambda b, n: (0, cur(b, n))),
            pl.BlockSpec((1, WINDOW), lambda b, n: (0, prev(b, n))),
            pl.BlockSpec((ROPE_HALF, LANES), lambda b, n: (0, 0)),
            *cast_specs,
        ],
        out_specs=[pl.BlockSpec((SWA_QB, SWA_WIDTH), lambda b, n: (cur(b, n), 0)), *cast_specs],
        out_shape=[jax.ShapeDtypeStruct((batch * seq, SWA_WIDTH), BF16),
                   *[jax.ShapeDtypeStruct(wgt.shape, BF16) for wgt in weights_f32]],
        scratch_shapes=[
            pltpu.VMEM((2 * WINDOW, WINDOW), BF16),
            pltpu.VMEM((WINDOW, SWA_GROUP * WINDOW), BF16),
        ],
        compiler_params=pltpu.CompilerParams(
            dimension_semantics=("arbitrary", "arbitrary")),
        name="swa_attention",
    )(sinks, proj, proj, proj, proj, proj, pos_row, pos_row, inv_tab, *weights_f32)


def _silu(z):
    return z * jax.nn.sigmoid(z)


def _epilogue_kernel(ya_ref, yb_ref, gz_ref, x_ref, p_ref,
                     wof_ref, wos_ref, wout_ref, gpost_ref, wple_ref, wgate_ref, o_ref):
    def gz(name, width):
        return gz_ref[:, _MAIN_OFF[name]:_MAIN_OFF[name] + width].astype(F32)

    ua = (ya_ref[...].astype(F32) * _silu(gz("z_a", FOX_WIDTH))).astype(BF16)
    ub = (yb_ref[...].astype(F32) * _silu(gz("z_b", SWA_WIDTH))).astype(BF16)
    oa = jnp.dot(ua, wof_ref[...], preferred_element_type=F32)
    ob = jnp.dot(ub, wos_ref[...], preferred_element_type=F32)
    merged = (jax.nn.sigmoid(gz("g_a", D_MODEL)) * oa
              + jax.nn.sigmoid(gz("g_b", D_MODEL)) * ob)
    out = jnp.dot(merged.astype(BF16), wout_ref[...], preferred_element_type=F32)
    ms = jnp.mean(out * out, axis=-1, keepdims=True)
    x1 = x_ref[...] + (out * lax.rsqrt(ms + NORM_EPS)) * gpost_ref[...]
    e = jnp.dot(p_ref[...].astype(BF16), wple_ref[...], preferred_element_type=F32)
    gate = jax.nn.sigmoid(jnp.dot(x1.astype(BF16), wgate_ref[...],
                                  preferred_element_type=F32))
    o_ref[...] = x1 + gate * e


def _epilogue(ya, yb, proj, x2, p2, wof, wos, wout, gpost, wple, wgate):
    t = x2.shape[0]
    gz_cols = _MAIN_OFF["z_b"] + SWA_WIDTH
    assert _MAIN_OFF["g_a"] == 0 and gz_cols == 2 * D_MODEL + FOX_WIDTH + SWA_WIDTH
    once = pl.Buffered(1)

    def const(shape):
        return pl.BlockSpec(shape, lambda i: (0, 0), pipeline_mode=once)

    return pl.pallas_call(
        _epilogue_kernel,
        grid=(t // EPI_TM,),
        in_specs=[
            pl.BlockSpec((EPI_TM, FOX_WIDTH), lambda i: (i, 0)),
            pl.BlockSpec((EPI_TM, SWA_WIDTH), lambda i: (i, 0)),
            pl.BlockSpec((EPI_TM, gz_cols), lambda i: (i, 0)),
            pl.BlockSpec((EPI_TM, D_MODEL), lambda i: (i, 0)),
            pl.BlockSpec((EPI_TM, PLE_DIM), lambda i: (i, 0)),
            const((FOX_WIDTH, D_MODEL)),
            const((SWA_WIDTH, D_MODEL)),
            const((D_MODEL, D_MODEL)),
            const((1, D_MODEL)),
            const((PLE_DIM, D_MODEL)),
            const((D_MODEL, D_MODEL)),
        ],
        out_specs=pl.BlockSpec((EPI_TM, D_MODEL), lambda i: (i, 0)),
        out_shape=jax.ShapeDtypeStruct((t, D_MODEL), F32),
        compiler_params=pltpu.CompilerParams(
            dimension_semantics=("arbitrary",),
            vmem_limit_bytes=VMEM_LIMIT_BYTES),
        name="epilogue",
    )(ya, yb, proj, x2, p2, wof, wos, wout, gpost, wple, wgate)


def _layer(x2, p2, pos_row, batch, seq, pre_g, w_in, b_forget, sinks, w_o_fox, w_o_swa,
           w_out, post_g, w_ple, w_ple_gate):
    w_t = w_in.T
    f0 = _REF_OFF["f_a"]
    wf_t = jnp.pad(w_t[f0:f0 + FOX_HEADS], ((0, F_ROWS - FOX_HEADS), (0, 0))).astype(BF16)
    col = jnp.arange(MAIN_COLS)
    in_q_a = (col >= _MAIN_OFF["q_a"]) & (col < _MAIN_OFF["q_a"] + FOX_WIDTH)
    col_scale = jnp.where(in_q_a, Q_A_SCALE, 1.0).astype(F32).reshape(1, MAIN_COLS)

    h, kb_tok, proj_swa = _prenorm(x2, pre_g.reshape(1, D_MODEL), wf_t,
                                   b_forget.reshape(FOX_HEADS, 1).astype(F32), w_t, seq)
    proj = _inproj(h, w_t, col_scale)
    ya = _fox_attention(proj, kb_tok, batch, seq)

    inv = ROPE_THETA ** (-jnp.arange(ROPE_HALF, dtype=F32) / ROPE_HALF)
    inv_tab = jnp.broadcast_to(inv[:, None], (ROPE_HALF, LANES))
    yb, wof, wos, wout, wple, wgate = _swa_attention(
        sinks.astype(F32), proj_swa, pos_row, inv_tab, batch, seq,
        (w_o_fox, w_o_swa, w_out, w_ple, w_ple_gate))

    return _epilogue(ya, yb, proj, x2, p2, wof, wos, wout,
                     post_g.reshape(1, D_MODEL), wple, wgate)


def kernel(x, p, positions, pre_norm_g, w_in, b_forget, sinks, w_o_fox, w_o_swa, w_out,
           post_norm_g, w_ple, w_ple_gate):
    batch, seq, _ = x.shape
    depth = p.shape[0]
    x2 = x.reshape(batch * seq, D_MODEL)
    pos_row = positions.reshape(1, batch * seq)
    for i in range(depth):
        x2 = _layer(x2, p[i].reshape(batch * seq, PLE_DIM), pos_row, batch, seq,
                    pre_norm_g[i], w_in[i], b_forget[i], sinks[i], w_o_fox[i], w_o_swa[i],
                    w_out[i], post_norm_g[i], w_ple[i], w_ple_gate[i])
    return x2.reshape(batch, seq, D_MODEL)
```

```python
import functools
import math

import jax
import jax.numpy as jnp
from jax import lax
from jax.experimental import pallas as pl
from jax.experimental.pallas import tpu as pltpu

F32 = jnp.float32
BF16 = jnp.bfloat16

D_MODEL = 2048
FOX_HEADS = 8
FOX_HEAD_DIM = 128
FOX_WIDTH = FOX_HEADS * FOX_HEAD_DIM
SWA_Q_HEADS = 16
SWA_KV_HEADS = 2
SWA_HEAD_DIM = 64
SWA_WIDTH = SWA_Q_HEADS * SWA_HEAD_DIM
SWA_KV_WIDTH = SWA_KV_HEADS * SWA_HEAD_DIM
SWA_GROUP = SWA_Q_HEADS // SWA_KV_HEADS
ROPE_HALF = SWA_HEAD_DIM // 2
WINDOW = 128
ROPE_THETA = 10000.0
PLE_DIM = 256
NORM_EPS = 1e-6
LANES = 128
V7X_VMEM_BYTES = 64 * 1024 * 1024
VMEM_LIMIT_BYTES = V7X_VMEM_BYTES * 7 // 8
LOG2E = math.log2(math.e)
NEG_BIG = -1e30

_REF_SPLITS = (FOX_WIDTH, FOX_WIDTH, FOX_WIDTH, FOX_WIDTH, FOX_HEADS, SWA_WIDTH,
               SWA_KV_WIDTH, SWA_KV_WIDTH, SWA_WIDTH, D_MODEL, D_MODEL)
_REF_NAMES = ("q_a", "k_a", "v_a", "z_a", "f_a", "q_b", "k_b", "v_b", "z_b", "g_a", "g_b")
_REF_OFF = {}
_o = 0
for _n, _s in zip(_REF_NAMES, _REF_SPLITS):
    _REF_OFF[_n] = _o
    _o += _s

IN_TM = 1024
IN_TN = 1536
NORM_ROWS = 1024
F_ROWS = 16

PREP_TN = 512
_SWA_GROUPS = (
    (_REF_OFF["q_b"], SWA_WIDTH),
    (_REF_OFF["k_b"], 2 * SWA_KV_WIDTH),
)
_MAIN_GROUPS = (
    (_REF_OFF["g_a"], 2 * D_MODEL),
    (_REF_OFF["z_a"], FOX_WIDTH),
    (_REF_OFF["z_b"], SWA_WIDTH),
    (_REF_OFF["q_a"], 3 * FOX_WIDTH),
)
SUBLANES = 8


def _group_table(groups):
    assert all(c0 % SUBLANES == 0 for c0, _ in groups)
    counts = [-(-w // PREP_TN) for _, w in groups]
    return tuple(sum(counts[:i]) for i in range(len(groups))), sum(counts)


_SWA_START, _SWA_WINDOWS = _group_table(_SWA_GROUPS)
_MAIN_START, _MAIN_WINDOWS = _group_table(_MAIN_GROUPS)
W_PER_TILE = IN_TN // PREP_TN
assert _SWA_WINDOWS == W_PER_TILE and _MAIN_WINDOWS % W_PER_TILE == 0
SWA_COLS = _SWA_WINDOWS * PREP_TN
MAIN_COLS = _MAIN_WINDOWS * PREP_TN
_SWA_OFF = {"q_b": 0, "k_b": _SWA_START[1] * PREP_TN,
            "v_b": _SWA_START[1] * PREP_TN + SWA_KV_WIDTH}
_MAIN_OFF = {
    "g_a": 0, "g_b": D_MODEL,
    "z_a": _MAIN_START[1] * PREP_TN,
    "z_b": _MAIN_START[2] * PREP_TN,
    "q_a": _MAIN_START[3] * PREP_TN,
    "k_a": _MAIN_START[3] * PREP_TN + FOX_WIDTH,
    "v_a": _MAIN_START[3] * PREP_TN + 2 * FOX_WIDTH,
}
_NT = (((1,), (1,)), ((), ()))
Q_A_SCALE = FOX_HEAD_DIM ** -0.5 * LOG2E

FOX_T = 512
FOX_CHUNK = 512
FOX_PAIR = 4
BIAS_TERMS = 3
ONES_ROWS = 16

SWA_NW = 8
SWA_QB = SWA_NW * WINDOW

EPI_TM = 256


def _cast_windows(w_refs, wb_ref):
    for k, w_ref in enumerate(w_refs):
        for r in range(PREP_TN // LANES):
            rows = slice(r * LANES, (r + 1) * LANES)
            wb_ref[k * PREP_TN + r * LANES:k * PREP_TN + (r + 1) * LANES, :] = (
                w_ref[rows, :].astype(BF16))


def _w_src_row(window, groups, starts):
    tile_row = jnp.int32(0)
    for (c0, _), start in zip(groups, starts):
        tile_row = jnp.where(window >= start,
                             c0 // SUBLANES + (window - start) * (PREP_TN // SUBLANES), tile_row)
    return tile_row * SUBLANES


def _prenorm_kernel(steps_per_seq, x_ref, g_ref, wf_ref, b_ref, *refs):
    w_refs = refs[:W_PER_TILE]
    h_ref, kb_ref, proj_ref, wb_ref, carry_ref = refs[W_PER_TILE:]
    i = pl.program_id(0)

    @pl.when(i == 0)
    def _():
        _cast_windows(w_refs, wb_ref)

    @pl.when(i % steps_per_seq == 0)
    def _():
        carry_ref[...] = jnp.zeros(carry_ref.shape, F32)

    x = x_ref[...]
    ms = jnp.mean(x * x, axis=-1, keepdims=True)
    h = ((x * lax.rsqrt(ms + NORM_EPS)) * g_ref[...]).astype(BF16)
    h_ref[...] = h
    ft = lax.dot_general(wf_ref[...], h, _NT, preferred_element_type=F32)
    proj_ref[...] = lax.dot_general(h, wb_ref[...], _NT,
                                    preferred_element_type=F32).astype(BF16)

    f = ft[0:FOX_HEADS, :] + b_ref[...]
    c = jnp.minimum(f, 0.0) - jnp.log1p(jnp.exp(-jnp.abs(f)))
    rows = c.shape[1]
    lane = lax.broadcasted_iota(jnp.int32, c.shape, 1)
    shift = 1
    while shift < rows:
        c = c + jnp.where(lane >= shift, pltpu.roll(c, shift, axis=1), 0.0)
        shift *= 2
    c = c + carry_ref[:, 0:1]
    carry_ref[...] = jnp.broadcast_to(c[:, rows - 1:rows], carry_ref.shape)
    rest = c * (-LOG2E)
    pieces = []
    for _ in range(BIAS_TERMS):
        piece = rest.astype(BF16).astype(F32)
        pieces.append(piece)
        rest = rest - piece
    pieces.append(jnp.zeros((LANES - BIAS_TERMS * FOX_HEADS, rows), F32))
    kb_ref[...] = jnp.concatenate(pieces, axis=0).T.astype(BF16)


def _prenorm(x2, g, wf_t, b_col, w_t, seq):
    t = x2.shape[0]
    once = pl.Buffered(1)

    def w_spec(k):
        return pl.BlockSpec((pl.Element(PREP_TN), pl.Element(D_MODEL)),
                            lambda i: (_w_src_row(k, _SWA_GROUPS, _SWA_START), 0),
                            pipeline_mode=once)

    return pl.pallas_call(
        functools.partial(_prenorm_kernel, seq // NORM_ROWS),
        grid=(t // NORM_ROWS,),
        in_specs=[
            pl.BlockSpec((NORM_ROWS, D_MODEL), lambda i: (i, 0)),
            pl.BlockSpec((1, D_MODEL), lambda i: (0, 0)),
            pl.BlockSpec((F_ROWS, D_MODEL), lambda i: (0, 0)),
            pl.BlockSpec((FOX_HEADS, 1), lambda i: (0, 0)),
            *[w_spec(k) for k in range(W_PER_TILE)],
        ],
        out_specs=[
            pl.BlockSpec((NORM_ROWS, D_MODEL), lambda i: (i, 0)),
            pl.BlockSpec((NORM_ROWS, LANES), lambda i: (i, 0)),
            pl.BlockSpec((NORM_ROWS, SWA_COLS), lambda i: (i, 0)),
        ],
        out_shape=[
            jax.ShapeDtypeStruct((t, D_MODEL), BF16),
            jax.ShapeDtypeStruct((t, LANES), BF16),
            jax.ShapeDtypeStruct((t, SWA_COLS), BF16),
        ],
        scratch_shapes=[
            pltpu.VMEM((IN_TN, D_MODEL), BF16),
            pltpu.VMEM((FOX_HEADS, LANES), F32),
        ],
        compiler_params=pltpu.CompilerParams(
            dimension_semantics=("arbitrary",),
            vmem_limit_bytes=VMEM_LIMIT_BYTES),
        name="prenorm",
    )(x2, g, wf_t, b_col, *([w_t] * W_PER_TILE))


def _inproj_kernel(h_ref, *refs):
    w_refs = refs[:W_PER_TILE]
    scale_ref, proj_ref, wb_ref = refs[W_PER_TILE:]

    @pl.when(pl.program_id(1) == 0)
    def _():
        _cast_windows(w_refs, wb_ref)

    acc = lax.dot_general(h_ref[...], wb_ref[...], _NT, preferred_element_type=F32)
    proj_ref[...] = (acc * scale_ref[...]).astype(BF16)


def _inproj(h, w_t, col_scale):
    t = h.shape[0]

    def w_spec(k):
        return pl.BlockSpec(
            (pl.Element(PREP_TN), pl.Element(D_MODEL)),
            lambda j, i: (_w_src_row(j * W_PER_TILE + k, _MAIN_GROUPS, _MAIN_START), 0))

    return pl.pallas_call(
        _inproj_kernel,
        grid=(MAIN_COLS // IN_TN, t // IN_TM),
        in_specs=[
            pl.BlockSpec((IN_TM, D_MODEL), lambda j, i: (i, 0)),
            *[w_spec(k) for k in range(W_PER_TILE)],
            pl.BlockSpec((1, IN_TN), lambda j, i: (0, j)),
        ],
        out_specs=pl.BlockSpec((IN_TM, IN_TN), lambda j, i: (i, j)),
        out_shape=jax.ShapeDtypeStruct((t, MAIN_COLS), BF16),
        scratch_shapes=[pltpu.VMEM((IN_TN, D_MODEL), BF16)],
        compiler_params=pltpu.CompilerParams(
            dimension_semantics=("arbitrary", "arbitrary"),
            vmem_limit_bytes=VMEM_LIMIT_BYTES),
        name="inproj",
    )(h, *([w_t] * W_PER_TILE), col_scale)


def _fox_kernel(q_ref, k_ref, v_ref, kb_ref, o_ref, vt_ref, qt_ref, acc_ref,
                sa_ref, ma_ref, sb_ref, mb_ref):
    s0_ref, s1_ref = (sa_ref, ma_ref), (sb_ref, mb_ref)
    hp = pl.program_id(1)
    seq = k_ref.shape[0]
    t = FOX_T
    d = FOX_HEAD_DIM

    def transposed(x):
        return x.astype(F32).T.astype(BF16)

    row = lax.broadcasted_iota(jnp.int32, (d, t), 0)
    for hh in range(FOX_PAIR):
        mine = (row < BIAS_TERMS * FOX_HEADS) & (row % FOX_HEADS == hp * FOX_PAIR + hh)
        qt_ref[hh, d:2 * d, :] = jnp.where(mine, 1.0, 0.0).astype(BF16)
        vt_ref[hh, d:, :] = jnp.ones((ONES_ROWS, seq), BF16)
        for r in range(seq // FOX_CHUNK):
            rows = slice(r * FOX_CHUNK, (r + 1) * FOX_CHUNK)
            vt_ref[hh, 0:d, rows] = transposed(v_ref[rows, hh * d:(hh + 1) * d])

    def scores(i, buf):
        s_ref, max_ref = buf
        rows = pl.ds(pl.multiple_of(i * t, t), t)
        for hh in range(FOX_PAIR):
            k_aug = jnp.concatenate([k_ref[rows, hh * d:(hh + 1) * d], kb_ref[rows, :]], axis=1)
            s = jnp.dot(k_aug, qt_ref[hh], preferred_element_type=F32)
            s_ref[hh] = s
            max_ref[hh] = jnp.max(s, axis=0, keepdims=True)

    def absorb(i, buf, carry, masked):
        s_ref, max_ref = buf
        rows = pl.ds(pl.multiple_of(i * t, t), t)
        new = []
        for hh in range(FOX_PAIR):
            m = carry[hh]
            s = s_ref[hh]
            if masked:
                key = lax.broadcasted_iota(jnp.int32, s.shape, 0)
                qry = lax.broadcasted_iota(jnp.int32, s.shape, 1)
                s = jnp.where(key <= qry, s, NEG_BIG)
                tile_max = jnp.max(s, axis=0, keepdims=True)
            else:
                tile_max = max_ref[hh]
            m_new = jnp.maximum(m, tile_max)
            alpha = jnp.exp2(m - m_new)
            p = jnp.exp2(s - m_new)
            acc_ref[hh] = alpha * acc_ref[hh] + jnp.dot(
                vt_ref[hh, :, rows], p.astype(BF16), preferred_element_type=F32)
            new.append(m_new)
        return tuple(new)

    def step(i, s_cur, s_next, carry):
        scores(i + 1, s_next)
        return absorb(i, s_cur, carry, False)

    def pair(j, carry):
        carry = step(2 * j, s0_ref, s1_ref, carry)
        return step(2 * j + 1, s1_ref, s0_ref, carry)

    n_q = seq // t

    def load_q(qi):
        q_rows = pl.ds(pl.multiple_of(qi * t, t), t)
        for hh in range(FOX_PAIR):
            qt_ref[hh, 0:d, :] = transposed(q_ref[q_rows, hh * d:(hh + 1) * d])

    def q_tile(qi, _):
        for hh in range(FOX_PAIR):
            acc_ref[hh] = jnp.zeros((d + ONES_ROWS, t), F32)

        def finish(s_last, carry):
            load_q(jnp.minimum(qi + 1, n_q - 1))
            absorb(qi, s_last, carry, True)
            q_rows = pl.ds(pl.multiple_of(qi * t, t), t)
            for hh in range(FOX_PAIR):
                o_ref[q_rows, hh * d:(hh + 1) * d] = (
                    acc_ref[hh, 0:d, :] / acc_ref[hh, d:d + 1, :]).T.astype(o_ref.dtype)

        def odd_tail(carry):
            finish(s1_ref, step(qi - 1, s0_ref, s1_ref, carry))

        def even_tail(carry):
            finish(s0_ref, carry)

        init = tuple(jnp.full((1, t), NEG_BIG, F32) for _ in range(FOX_PAIR))
        scores(0, s0_ref)
        carry = lax.fori_loop(0, qi // 2, pair, init)
        lax.cond(qi % 2 == 1, odd_tail, even_tail, carry)
        return 0

    load_q(0)
    lax.fori_loop(0, n_q, q_tile, 0)


def _fox_attention(proj, kb_tok, batch, seq):
    width = FOX_PAIR * FOX_HEAD_DIM
    qcol = _MAIN_OFF["q_a"] // width
    kcol = _MAIN_OFF["k_a"] // width
    vcol = _MAIN_OFF["v_a"] // width
    return pl.pallas_call(
        _fox_kernel,
        grid=(batch, FOX_HEADS // FOX_PAIR),
        in_specs=[
            pl.BlockSpec((seq, width), lambda b, h: (b, qcol + h)),
            pl.BlockSpec((seq, width), lambda b, h: (b, kcol + h)),
            pl.BlockSpec((seq, width), lambda b, h: (b, vcol + h)),
            pl.BlockSpec((seq, LANES), lambda b, h: (b, 0)),
        ],
        out_specs=pl.BlockSpec((seq, width), lambda b, h: (b, h)),
        out_shape=jax.ShapeDtypeStruct((batch * seq, FOX_WIDTH), BF16),
        scratch_shapes=[
            pltpu.VMEM((FOX_PAIR, FOX_HEAD_DIM + ONES_ROWS, seq), BF16),
            pltpu.VMEM((FOX_PAIR, 2 * FOX_HEAD_DIM, FOX_T), BF16),
            pltpu.VMEM((FOX_PAIR, FOX_HEAD_DIM + ONES_ROWS, FOX_T), F32),
            pltpu.VMEM((FOX_PAIR, FOX_T, FOX_T), F32),
            pltpu.VMEM((FOX_PAIR, 1, FOX_T), F32),
            pltpu.VMEM((FOX_PAIR, FOX_T, FOX_T), F32),
            pltpu.VMEM((FOX_PAIR, 1, FOX_T), F32),
        ],
        compiler_params=pltpu.CompilerParams(
            dimension_semantics=("arbitrary", "arbitrary"),
            vmem_limit_bytes=VMEM_LIMIT_BYTES),
        name="fox_attention",
    )(proj, proj, proj, kb_tok)


def _rope_t(xt, cos, sin):
    out = []
    for hd in range(xt.shape[0] // SWA_HEAD_DIM):
        x1 = xt[hd * SWA_HEAD_DIM: hd * SWA_HEAD_DIM + ROPE_HALF]
        x2 = xt[hd * SWA_HEAD_DIM + ROPE_HALF: (hd + 1) * SWA_HEAD_DIM]
        out.append(x1 * cos - x2 * sin)
        out.append(x2 * cos + x1 * sin)
    return jnp.concatenate(out, axis=0)


def _swa_kernel(sink_ref, q_ref, kc_ref, kp_ref, vc_ref, vp_ref, posc_ref, posp_ref,
                inv_ref, *rest):
    n_cast = (len(rest) - 3) // 2
    cast_in, o_ref = rest[:n_cast], rest[n_cast]
    cast_out, (band_ref, eye_ref) = rest[n_cast + 1:2 * n_cast + 1], rest[-2:]
    for src_ref, dst_ref in zip(cast_in, cast_out):
        dst_ref[...] = src_ref[...].astype(dst_ref.dtype)

    n = pl.program_id(1)
    w = WINDOW
    inv = inv_ref[...]

    def tables(pos_row):
        ang = inv * pos_row.astype(F32)
        return jnp.cos(ang), jnp.sin(ang)

    def rope_k(k_nat, cos, sin):
        kt = _rope_t(k_nat.astype(F32).T, cos, sin)
        return kt.T.astype(BF16)

    cos_p, sin_p = tables(posp_ref[...])
    k_prev = rope_k(kp_ref[...], cos_p, sin_p)
    v_prev = vp_ref[...]

    @pl.when((pl.program_id(0) == 0) & (n == 0))
    def _():
        key = lax.broadcasted_iota(jnp.int32, band_ref.shape, 0)
        qry = lax.broadcasted_iota(jnp.int32, band_ref.shape, 1)
        band_ref[...] = jnp.where((key <= qry + w) & (key > qry), 0.0, NEG_BIG).astype(BF16)
        src = lax.broadcasted_iota(jnp.int32, eye_ref.shape, 0)
        dst = lax.broadcasted_iota(jnp.int32, eye_ref.shape, 1) % w
        eye_ref[...] = jnp.where(src == dst, 1.0, 0.0).astype(BF16)

    no_prev = jnp.where(n > 0, 0.0, NEG_BIG)
    q_scale = SWA_HEAD_DIM ** -0.5 * LOG2E
    zeros_half = jnp.zeros((SWA_HEAD_DIM, SWA_GROUP * w), F32)
    ones_rows = jnp.ones((2 * SUBLANES, 2 * w), BF16)

    sinks = [jnp.concatenate(
        [jnp.full((1, w), sink_ref[hd] * LOG2E, F32)
         for hd in range(g * SWA_GROUP, (g + 1) * SWA_GROUP)], axis=1)
        for g in range(SWA_KV_HEADS)]

    def logits(wi, k_prev):
        tok = slice(wi * w, (wi + 1) * w)
        cos, sin = tables(posc_ref[:, tok])
        k_cur = rope_k(kc_ref[tok, :], cos, sin)
        k_all = jnp.concatenate([k_prev, k_cur], axis=0)
        k_aug = jnp.concatenate([k_all, band_ref[...]], axis=1)
        qf = q_ref[tok, :].astype(F32)
        qt = jnp.concatenate(
            [qf[:, c * LANES:(c + 1) * LANES].T for c in range(SWA_WIDTH // LANES)], axis=0)
        qt = _rope_t(qt, cos * q_scale, sin * q_scale)
        out = []
        for g in range(SWA_KV_HEADS):
            heads = range(g * SWA_GROUP, (g + 1) * SWA_GROUP)
            q_g = jnp.concatenate(
                [qt[hd * SWA_HEAD_DIM:(hd + 1) * SWA_HEAD_DIM] for hd in heads], axis=1)
            parts = [zeros_half] * SWA_KV_HEADS
            parts[g] = q_g
            q_z = jnp.concatenate(parts, axis=0).astype(BF16)
            q_aug = jnp.concatenate([q_z, eye_ref[...]], axis=0)
            s = jnp.dot(k_aug, q_aug, preferred_element_type=F32)
            if wi == 0:
                s = jnp.concatenate([s[:w] + no_prev, s[w:]], axis=0)
            out.append(s)
        return out, k_cur

    def finish(wi, scores, v_prev):
        tok = slice(wi * w, (wi + 1) * w)
        v_cur = vc_ref[tok, :]
        v_all_t = jnp.concatenate([v_prev, v_cur], axis=0).astype(F32).T.astype(BF16)
        for g in range(SWA_KV_HEADS):
            s, sink = scores[g], sinks[g]
            m = jnp.maximum(jnp.max(s, axis=0, keepdims=True), sink)
            e = jnp.exp2(s - m).astype(BF16)
            v_aug = jnp.concatenate(
                [v_all_t[g * SWA_HEAD_DIM:(g + 1) * SWA_HEAD_DIM], ones_rows], axis=0)
            pv = jnp.dot(v_aug, e, preferred_element_type=F32)
            denom = pv[SWA_HEAD_DIM:SWA_HEAD_DIM + 1] + jnp.exp2(sink - m)
            out_t = pv[:SWA_HEAD_DIM] / denom
            for jj in range(SWA_GROUP // 2):
                pair = jnp.concatenate(
                    [out_t[:, (2 * jj) * w:(2 * jj + 1) * w],
                     out_t[:, (2 * jj + 1) * w:(2 * jj + 2) * w]], axis=0)
                col = (g * (SWA_GROUP // 2) + jj) * LANES
                o_ref[tok, col:col + LANES] = pair.T.astype(o_ref.dtype)
        return v_cur

    pending, k_prev = logits(0, k_prev)
    for wi in range(SWA_NW):
        if wi + 1 < SWA_NW:
            upcoming, k_prev = logits(wi + 1, k_prev)
        v_prev = finish(wi, pending, v_prev)
        if wi + 1 < SWA_NW:
            pending = upcoming


def _swa_attention(sinks, proj, pos_row, inv_tab, batch, seq, weights_f32):
    nb = seq // SWA_QB
    per = SWA_QB // WINDOW
    qcol = _SWA_OFF["q_b"] // SWA_WIDTH
    kcol = _SWA_OFF["k_b"] // LANES
    vcol = _SWA_OFF["v_b"] // LANES
    steps = batch * nb

    def cur(b, n):
        return b * nb + n

    def prev(b, n):
        return jnp.maximum((b * nb + n) * per - 1, 0)

    def chunk_spec(wgt):
        rows, cols = wgt.shape
        assert rows % (steps * 2 * SUBLANES) == 0
        return pl.BlockSpec((rows // steps, cols), lambda b, n: (cur(b, n), 0))

    cast_specs = [chunk_spec(wgt) for wgt in weights_f32]
    return pl.pallas_call(
        _swa_kernel,
        grid=(batch, nb),
        in_specs=[
            pl.BlockSpec(memory_space=pltpu.SMEM),
            pl.BlockSpec((SWA_QB, SWA_WIDTH), lambda b, n: (cur(b, n), qcol)),
            pl.BlockSpec((SWA_QB, LANES), lambda b, n: (cur(b, n), kcol)),
            pl.BlockSpec((WINDOW, LANES), lambda b, n: (prev(b, n), kcol)),
            pl.BlockSpec((SWA_QB, LANES), lambda b, n: (cur(b, n), vcol)),
            pl.BlockSpec((WINDOW, LANES), lambda b, n: (prev(b, n), vcol)),
            pl.BlockSpec((1, SWA_QB), lambda b, n: (0, cur(b, n))),
            pl.BlockSpec((1, WINDOW), lambda b, n: (0, prev(b, n))),
            pl.BlockSpec((ROPE_HALF, LANES), lambda b, n: (0, 0)),
            *cast_specs,
        ],
        out_specs=[pl.BlockSpec((SWA_QB, SWA_WIDTH), lambda b, n: (cur(b, n), 0)), *cast_specs],
        out_shape=[jax.ShapeDtypeStruct((batch * seq, SWA_WIDTH), BF16),
                   *[jax.ShapeDtypeStruct(wgt.shape, BF16) for wgt in weights_f32]],
        scratch_shapes=[
            pltpu.VMEM((2 * WINDOW, WINDOW), BF16),
            pltpu.VMEM((WINDOW, SWA_GROUP * WINDOW), BF16),
        ],
        compiler_params=pltpu.CompilerParams(
            dimension_semantics=("arbitrary", "arbitrary")),
        name="swa_attention",
    )(sinks, proj, proj, proj, proj, proj, pos_row, pos_row, inv_tab, *weights_f32)


def _silu(z):
    return z * jax.nn.sigmoid(z)


def _epilogue_kernel(ya_ref, yb_ref, gz_ref, x_ref, p_ref,
                     wof_ref, wos_ref, wout_ref, gpost_ref, wple_ref, wgate_ref, o_ref):
    def gz(name, width):
        return gz_ref[:, _MAIN_OFF[name]:_MAIN_OFF[name] + width].astype(F32)

    ua = (ya_ref[...].astype(F32) * _silu(gz("z_a", FOX_WIDTH))).astype(BF16)
    ub = (yb_ref[...].astype(F32) * _silu(gz("z_b", SWA_WIDTH))).astype(BF16)
    oa = jnp.dot(ua, wof_ref[...], preferred_element_type=F32)
    ob = jnp.dot(ub, wos_ref[...], preferred_element_type=F32)
    merged = (jax.nn.sigmoid(gz("g_a", D_MODEL)) * oa
              + jax.nn.sigmoid(gz("g_b", D_MODEL)) * ob)
    out = jnp.dot(merged.astype(BF16), wout_ref[...], preferred_element_type=F32)
    ms = jnp.mean(out * out, axis=-1, keepdims=True)
    x1 = x_ref[...] + (out * lax.rsqrt(ms + NORM_EPS)) * gpost_ref[...]
    e = jnp.dot(p_ref[...].astype(BF16), wple_ref[...], preferred_element_type=F32)
    gate = jax.nn.sigmoid(jnp.dot(x1.astype(BF16), wgate_ref[...],
                                  preferred_element_type=F32))
    o_ref[...] = x1 + gate * e


def _epilogue(ya, yb, proj, x2, p2, wof, wos, wout, gpost, wple, wgate):
    t = x2.shape[0]
    gz_cols = _MAIN_OFF["z_b"] + SWA_WIDTH
    assert _MAIN_OFF["g_a"] == 0 and gz_cols == 2 * D_MODEL + FOX_WIDTH + SWA_WIDTH
    once = pl.Buffered(1)

    def const(shape):
        return pl.BlockSpec(shape, lambda i: (0, 0), pipeline_mode=once)

    return pl.pallas_call(
        _epilogue_kernel,
        grid=(t // EPI_TM,),
        in_specs=[
            pl.BlockSpec((EPI_TM, FOX_WIDTH), lambda i: (i, 0)),
            pl.BlockSpec((EPI_TM, SWA_WIDTH), lambda i: (i, 0)),
            pl.BlockSpec((EPI_TM, gz_cols), lambda i: (i, 0)),
            pl.BlockSpec((EPI_TM, D_MODEL), lambda i: (i, 0)),
            pl.BlockSpec((EPI_TM, PLE_DIM), lambda i: (i, 0)),
            const((FOX_WIDTH, D_MODEL)),
            const((SWA_WIDTH, D_MODEL)),
            const((D_MODEL, D_MODEL)),
            const((1, D_MODEL)),
            const((PLE_DIM, D_MODEL)),
            const((D_MODEL, D_MODEL)),
        ],
        out_specs=pl.BlockSpec((EPI_TM, D_MODEL), lambda i: (i, 0)),
        out_shape=jax.ShapeDtypeStruct((t, D_MODEL), F32),
        compiler_params=pltpu.CompilerParams(
            dimension_semantics=("arbitrary",),
            vmem_limit_bytes=VMEM_LIMIT_BYTES),
        name="epilogue",
    )(ya, yb, proj, x2, p2, wof, wos, wout, gpost, wple, wgate)


def _layer(x2, p2, pos_row, batch, seq, pre_g, w_in, b_forget, sinks, w_o_fox, w_o_swa,
           w_out, post_g, w_ple, w_ple_gate):
    w_t = w_in.T
    f0 = _REF_OFF["f_a"]
    wf_t = jnp.pad(w_t[f0:f0 + FOX_HEADS], ((0, F_ROWS - FOX_HEADS), (0, 0))).astype(BF16)
    col = jnp.arange(MAIN_COLS)
    in_q_a = (col >= _MAIN_OFF["q_a"]) & (col < _MAIN_OFF["q_a"] + FOX_WIDTH)
    col_scale = jnp.where(in_q_a, Q_A_SCALE, 1.0).astype(F32).reshape(1, MAIN_COLS)

    h, kb_tok, proj_swa = _prenorm(x2, pre_g.reshape(1, D_MODEL), wf_t,
                                   b_forget.reshape(FOX_HEADS, 1).astype(F32), w_t, seq)
    proj = _inproj(h, w_t, col_scale)
    ya = _fox_attention(proj, kb_tok, batch, seq)

    inv = ROPE_THETA ** (-jnp.arange(ROPE_HALF, dtype=F32) / ROPE_HALF)
    inv_tab = jnp.broadcast_to(inv[:, None], (ROPE_HALF, LANES))
    yb, wof, wos, wout, wple, wgate = _swa_attention(
        sinks.astype(F32), proj_swa, pos_row, inv_tab, batch, seq,
        (w_o_fox, w_o_swa, w_out, w_ple, w_ple_gate))

    return _epilogue(ya, yb, proj, x2, p2, wof, wos, wout,
                     post_g.reshape(1, D_MODEL), wple, wgate)


def kernel(x, p, positions, pre_norm_g, w_in, b_forget, sinks, w_o_fox, w_o_swa, w_out,
           post_norm_g, w_ple, w_ple_gate):
    batch, seq, _ = x.shape
    depth = p.shape[0]
    x2 = x.reshape(batch * seq, D_MODEL)
    pos_row = positions.reshape(1, batch * seq)
    for i in range(depth):
        x2 = _layer(x2, p[i].reshape(batch * seq, PLE_DIM), pos_row, batch, seq,
                    pre_norm_g[i], w_in[i], b_forget[i], sinks[i], w_o_fox[i], w_o_swa[i],
                    w_out[i], post_norm_g[i], w_ple[i], w_ple_gate[i])
    return x2.reshape(batch, seq, D_MODEL)
```

```python
import functools
import math

import jax
import jax.numpy as jnp
from jax import lax
from jax.experimental import pallas as pl
from jax.experimental.pallas import tpu as pltpu

F32 = jnp.float32
BF16 = jnp.bfloat16

D_MODEL = 2048
FOX_HEADS = 8
FOX_HEAD_DIM = 128
FOX_WIDTH = FOX_HEADS * FOX_HEAD_DIM
SWA_Q_HEADS = 16
SWA_KV_HEADS = 2
SWA_HEAD_DIM = 64
SWA_WIDTH = SWA_Q_HEADS * SWA_HEAD_DIM
SWA_KV_WIDTH = SWA_KV_HEADS * SWA_HEAD_DIM
SWA_GROUP = SWA_Q_HEADS // SWA_KV_HEADS
ROPE_HALF = SWA_HEAD_DIM // 2
WINDOW = 128
ROPE_THETA = 10000.0
PLE_DIM = 256
NORM_EPS = 1e-6
LANES = 128
V7X_VMEM_BYTES = 64 * 1024 * 1024
VMEM_LIMIT_BYTES = V7X_VMEM_BYTES * 7 // 8
LOG2E = math.log2(math.e)
NEG_BIG = -1e30

_REF_SPLITS = (FOX_WIDTH, FOX_WIDTH, FOX_WIDTH, FOX_WIDTH, FOX_HEADS, SWA_WIDTH,
               SWA_KV_WIDTH, SWA_KV_WIDTH, SWA_WIDTH, D_MODEL, D_MODEL)
_REF_NAMES = ("q_a", "k_a", "v_a", "z_a", "f_a", "q_b", "k_b", "v_b", "z_b", "g_a", "g_b")
_REF_OFF = {}
_o = 0
for _n, _s in zip(_REF_NAMES, _REF_SPLITS):
    _REF_OFF[_n] = _o
    _o += _s

IN_TM = 1024
IN_TN = 1536
NORM_ROWS = 1024
F_ROWS = 16

PREP_TN = 256
_SWA_GROUPS = (
    (_REF_OFF["q_b"], SWA_WIDTH),
    (_REF_OFF["k_b"], 2 * SWA_KV_WIDTH),
)
_MAIN_GROUPS = (
    (_REF_OFF["g_a"], 2 * D_MODEL),
    (_REF_OFF["z_a"], FOX_WIDTH),
    (_REF_OFF["z_b"], SWA_WIDTH),
    (_REF_OFF["q_a"], 3 * FOX_WIDTH),
)
SUBLANES = 8


def _group_table(groups):
    assert all(c0 % SUBLANES == 0 for c0, _ in groups)
    counts = [-(-w // PREP_TN) for _, w in groups]
    return tuple(sum(counts[:i]) for i in range(len(groups))), sum(counts)


_SWA_START, _SWA_WINDOWS = _group_table(_SWA_GROUPS)
_MAIN_START, _MAIN_WINDOWS = _group_table(_MAIN_GROUPS)
W_PER_TILE = IN_TN // PREP_TN
assert _MAIN_WINDOWS % W_PER_TILE == 0
SWA_COLS = _SWA_WINDOWS * PREP_TN
MAIN_COLS = _MAIN_WINDOWS * PREP_TN
_SWA_OFF = {"q_b": 0, "k_b": _SWA_START[1] * PREP_TN,
            "v_b": _SWA_START[1] * PREP_TN + SWA_KV_WIDTH}
_MAIN_OFF = {
    "g_a": 0, "g_b": D_MODEL,
    "z_a": _MAIN_START[1] * PREP_TN,
    "z_b": _MAIN_START[2] * PREP_TN,
    "q_a": _MAIN_START[3] * PREP_TN,
    "k_a": _MAIN_START[3] * PREP_TN + FOX_WIDTH,
    "v_a": _MAIN_START[3] * PREP_TN + 2 * FOX_WIDTH,
}
_NT = (((1,), (1,)), ((), ()))
Q_A_SCALE = FOX_HEAD_DIM ** -0.5 * LOG2E

FOX_T = 512
FOX_CHUNK = 512
FOX_PAIR = 4
BIAS_TERMS = 3
ONES_ROWS = 16

SWA_NW = 8
SWA_QB = SWA_NW * WINDOW

EPI_TM = 256


def _cast_windows(w_refs, wb_ref):
    for k, w_ref in enumerate(w_refs):
        for r in range(PREP_TN // LANES):
            rows = slice(r * LANES, (r + 1) * LANES)
            wb_ref[k * PREP_TN + r * LANES:k * PREP_TN + (r + 1) * LANES, :] = (
                w_ref[rows, :].astype(BF16))


def _w_src_row(window, groups, starts):
    tile_row = jnp.int32(0)
    for (c0, _), start in zip(groups, starts):
        tile_row = jnp.where(window >= start,
                             c0 // SUBLANES + (window - start) * (PREP_TN // SUBLANES), tile_row)
    return tile_row * SUBLANES


def _prenorm_kernel(steps_per_seq, x_ref, g_ref, wf_ref, b_ref, *refs):
    w_refs = refs[:_SWA_WINDOWS]
    h_ref, kb_ref, proj_ref, wb_ref, carry_ref = refs[_SWA_WINDOWS:]
    i = pl.program_id(0)

    @pl.when(i == 0)
    def _():
        _cast_windows(w_refs, wb_ref)

    @pl.when(i % steps_per_seq == 0)
    def _():
        carry_ref[...] = jnp.zeros(carry_ref.shape, F32)

    x = x_ref[...]
    ms = jnp.mean(x * x, axis=-1, keepdims=True)
    h = ((x * lax.rsqrt(ms + NORM_EPS)) * g_ref[...]).astype(BF16)
    h_ref[...] = h
    ft = lax.dot_general(wf_ref[...], h, _NT, preferred_element_type=F32)
    proj_ref[...] = lax.dot_general(h, wb_ref[...], _NT,
                                    preferred_element_type=F32).astype(BF16)

    f = ft[0:FOX_HEADS, :] + b_ref[...]
    c = jnp.minimum(f, 0.0) - jnp.log1p(jnp.exp(-jnp.abs(f)))
    rows = c.shape[1]
    lane = lax.broadcasted_iota(jnp.int32, c.shape, 1)
    shift = 1
    while shift < rows:
        c = c + jnp.where(lane >= shift, pltpu.roll(c, shift, axis=1), 0.0)
        shift *= 2
    c = c + carry_ref[:, 0:1]
    carry_ref[...] = jnp.broadcast_to(c[:, rows - 1:rows], carry_ref.shape)
    rest = c * (-LOG2E)
    pieces = []
    for _ in range(BIAS_TERMS):
        piece = rest.astype(BF16).astype(F32)
        pieces.append(piece)
        rest = rest - piece
    pieces.append(jnp.zeros((LANES - BIAS_TERMS * FOX_HEADS, rows), F32))
    kb_ref[...] = jnp.concatenate(pieces, axis=0).T.astype(BF16)


def _prenorm(x2, g, wf_t, b_col, w_t, seq):
    t = x2.shape[0]
    once = pl.Buffered(1)

    def w_spec(k):
        return pl.BlockSpec((pl.Element(PREP_TN), pl.Element(D_MODEL)),
                            lambda i: (_w_src_row(k, _SWA_GROUPS, _SWA_START), 0),
                            pipeline_mode=once)

    return pl.pallas_call(
        functools.partial(_prenorm_kernel, seq // NORM_ROWS),
        grid=(t // NORM_ROWS,),
        in_specs=[
            pl.BlockSpec((NORM_ROWS, D_MODEL), lambda i: (i, 0)),
            pl.BlockSpec((1, D_MODEL), lambda i: (0, 0)),
            pl.BlockSpec((F_ROWS, D_MODEL), lambda i: (0, 0)),
            pl.BlockSpec((FOX_HEADS, 1), lambda i: (0, 0)),
            *[w_spec(k) for k in range(_SWA_WINDOWS)],
        ],
        out_specs=[
            pl.BlockSpec((NORM_ROWS, D_MODEL), lambda i: (i, 0)),
            pl.BlockSpec((NORM_ROWS, LANES), lambda i: (i, 0)),
            pl.BlockSpec((NORM_ROWS, SWA_COLS), lambda i: (i, 0)),
        ],
        out_shape=[
            jax.ShapeDtypeStruct((t, D_MODEL), BF16),
            jax.ShapeDtypeStruct((t, LANES), BF16),
            jax.ShapeDtypeStruct((t, SWA_COLS), BF16),
        ],
        scratch_shapes=[
            pltpu.VMEM((SWA_COLS, D_MODEL), BF16),
            pltpu.VMEM((FOX_HEADS, LANES), F32),
        ],
        compiler_params=pltpu.CompilerParams(
            dimension_semantics=("arbitrary",),
            vmem_limit_bytes=VMEM_LIMIT_BYTES),
        name="prenorm",
    )(x2, g, wf_t, b_col, *([w_t] * _SWA_WINDOWS))


def _inproj_kernel(h_ref, *refs):
    w_refs = refs[:W_PER_TILE]
    scale_ref, proj_ref, wb_ref = refs[W_PER_TILE:]

    @pl.when(pl.program_id(1) == 0)
    def _():
        _cast_windows(w_refs, wb_ref)

    acc = lax.dot_general(h_ref[...], wb_ref[...], _NT, preferred_element_type=F32)
    proj_ref[...] = (acc * scale_ref[...]).astype(BF16)


def _inproj(h, w_t, col_scale):
    t = h.shape[0]

    def w_spec(k):
        return pl.BlockSpec(
            (pl.Element(PREP_TN), pl.Element(D_MODEL)),
            lambda j, i: (_w_src_row(j * W_PER_TILE + k, _MAIN_GROUPS, _MAIN_START), 0))

    return pl.pallas_call(
        _inproj_kernel,
        grid=(MAIN_COLS // IN_TN, t // IN_TM),
        in_specs=[
            pl.BlockSpec((IN_TM, D_MODEL), lambda j, i: (i, 0)),
            *[w_spec(k) for k in range(W_PER_TILE)],
            pl.BlockSpec((1, IN_TN), lambda j, i: (0, j)),
        ],
        out_specs=pl.BlockSpec((IN_TM, IN_TN), lambda j, i: (i, j)),
        out_shape=jax.ShapeDtypeStruct((t, MAIN_COLS), BF16),
        scratch_shapes=[pltpu.VMEM((IN_TN, D_MODEL), BF16)],
        compiler_params=pltpu.CompilerParams(
            dimension_semantics=("arbitrary", "arbitrary"),
            vmem_limit_bytes=VMEM_LIMIT_BYTES),
        name="inproj",
    )(h, *([w_t] * W_PER_TILE), col_scale)


def _fox_kernel(q_ref, k_ref, v_ref, kb_ref, o_ref, vt_ref, qt_ref, acc_ref,
                sa_ref, ma_ref, sb_ref, mb_ref):
    s0_ref, s1_ref = (sa_ref, ma_ref), (sb_ref, mb_ref)
    hp = pl.program_id(1)
    seq = k_ref.shape[0]
    t = FOX_T
    d = FOX_HEAD_DIM

    def transposed(x):
        return x.astype(F32).T.astype(BF16)

    row = lax.broadcasted_iota(jnp.int32, (d, t), 0)
    for hh in range(FOX_PAIR):
        mine = (row < BIAS_TERMS * FOX_HEADS) & (row % FOX_HEADS == hp * FOX_PAIR + hh)
        qt_ref[hh, d:2 * d, :] = jnp.where(mine, 1.0, 0.0).astype(BF16)
        vt_ref[hh, d:, :] = jnp.ones((ONES_ROWS, seq), BF16)
        for r in range(seq // FOX_CHUNK):
            rows = slice(r * FOX_CHUNK, (r + 1) * FOX_CHUNK)
            vt_ref[hh, 0:d, rows] = transposed(v_ref[rows, hh * d:(hh + 1) * d])

    def scores(i, buf):
        s_ref, max_ref = buf
        rows = pl.ds(pl.multiple_of(i * t, t), t)
        for hh in range(FOX_PAIR):
            k_aug = jnp.concatenate([k_ref[rows, hh * d:(hh + 1) * d], kb_ref[rows, :]], axis=1)
            s = jnp.dot(k_aug, qt_ref[hh], preferred_element_type=F32)
            s_ref[hh] = s
            max_ref[hh] = jnp.max(s, axis=0, keepdims=True)

    def absorb(i, buf, carry, masked):
        s_ref, max_ref = buf
        rows = pl.ds(pl.multiple_of(i * t, t), t)
        new = []
        for hh in range(FOX_PAIR):
            m = carry[hh]
            s = s_ref[hh]
            if masked:
                key = lax.broadcasted_iota(jnp.int32, s.shape, 0)
                qry = lax.broadcasted_iota(jnp.int32, s.shape, 1)
                s = jnp.where(key <= qry, s, NEG_BIG)
                tile_max = jnp.max(s, axis=0, keepdims=True)
            else:
                tile_max = max_ref[hh]
            m_new = jnp.maximum(m, tile_max)
            alpha = jnp.exp2(m - m_new)
            p = jnp.exp2(s - m_new)
            acc_ref[hh] = alpha * acc_ref[hh] + jnp.dot(
                vt_ref[hh, :, rows], p.astype(BF16), preferred_element_type=F32)
            new.append(m_new)
        return tuple(new)

    def step(i, s_cur, s_next, carry):
        scores(i + 1, s_next)
        return absorb(i, s_cur, carry, False)

    def pair(j, carry):
        carry = step(2 * j, s0_ref, s1_ref, carry)
        return step(2 * j + 1, s1_ref, s0_ref, carry)

    n_q = seq // t

    def load_q(qi):
        q_rows = pl.ds(pl.multiple_of(qi * t, t), t)
        for hh in range(FOX_PAIR):
            qt_ref[hh, 0:d, :] = transposed(q_ref[q_rows, hh * d:(hh + 1) * d])

    def q_tile(qi, _):
        for hh in range(FOX_PAIR):
            acc_ref[hh] = jnp.zeros((d + ONES_ROWS, t), F32)

        def finish(s_last, carry):
            load_q(jnp.minimum(qi + 1, n_q - 1))
            absorb(qi, s_last, carry, True)
            q_rows = pl.ds(pl.multiple_of(qi * t, t), t)
            for hh in range(FOX_PAIR):
                o_ref[q_rows, hh * d:(hh + 1) * d] = (
                    acc_ref[hh, 0:d, :] / acc_ref[hh, d:d + 1, :]).T.astype(o_ref.dtype)

        def odd_tail(carry):
            finish(s1_ref, step(qi - 1, s0_ref, s1_ref, carry))

        def even_tail(carry):
            finish(s0_ref, carry)

        init = tuple(jnp.full((1, t), NEG_BIG, F32) for _ in range(FOX_PAIR))
        scores(0, s0_ref)
        carry = lax.fori_loop(0, qi // 2, pair, init)
        lax.cond(qi % 2 == 1, odd_tail, even_tail, carry)
        return 0

    load_q(0)
    lax.fori_loop(0, n_q, q_tile, 0)


def _fox_attention(proj, kb_tok, batch, seq):
    width = FOX_PAIR * FOX_HEAD_DIM
    qcol = _MAIN_OFF["q_a"] // width
    kcol = _MAIN_OFF["k_a"] // width
    vcol = _MAIN_OFF["v_a"] // width
    return pl.pallas_call(
        _fox_kernel,
        grid=(batch, FOX_HEADS // FOX_PAIR),
        in_specs=[
            pl.BlockSpec((seq, width), lambda b, h: (b, qcol + h)),
            pl.BlockSpec((seq, width), lambda b, h: (b, kcol + h)),
            pl.BlockSpec((seq, width), lambda b, h: (b, vcol + h)),
            pl.BlockSpec((seq, LANES), lambda b, h: (b, 0)),
        ],
        out_specs=pl.BlockSpec((seq, width), lambda b, h: (b, h)),
        out_shape=jax.ShapeDtypeStruct((batch * seq, FOX_WIDTH), BF16),
        scratch_shapes=[
            pltpu.VMEM((FOX_PAIR, FOX_HEAD_DIM + ONES_ROWS, seq), BF16),
            pltpu.VMEM((FOX_PAIR, 2 * FOX_HEAD_DIM, FOX_T), BF16),
            pltpu.VMEM((FOX_PAIR, FOX_HEAD_DIM + ONES_ROWS, FOX_T), F32),
            pltpu.VMEM((FOX_PAIR, FOX_T, FOX_T), F32),
            pltpu.VMEM((FOX_PAIR, 1, FOX_T), F32),
            pltpu.VMEM((FOX_PAIR, FOX_T, FOX_T), F32),
            pltpu.VMEM((FOX_PAIR, 1, FOX_T), F32),
        ],
        compiler_params=pltpu.CompilerParams(
            dimension_semantics=("arbitrary", "arbitrary"),
            vmem_limit_bytes=VMEM_LIMIT_BYTES),
        name="fox_attention",
    )(proj, proj, proj, kb_tok)


def _rope_t(xt, cos, sin):
    out = []
    for hd in range(xt.shape[0] // SWA_HEAD_DIM):
        x1 = xt[hd * SWA_HEAD_DIM: hd * SWA_HEAD_DIM + ROPE_HALF]
        x2 = xt[hd * SWA_HEAD_DIM + ROPE_HALF: (hd + 1) * SWA_HEAD_DIM]
        out.append(x1 * cos - x2 * sin)
        out.append(x2 * cos + x1 * sin)
    return jnp.concatenate(out, axis=0)


def _swa_kernel(sink_ref, q_ref, kc_ref, kp_ref, vc_ref, vp_ref, posc_ref, posp_ref,
                inv_ref, *rest):
    n_cast = (len(rest) - 3) // 2
    cast_in, o_ref = rest[:n_cast], rest[n_cast]
    cast_out, (band_ref, eye_ref) = rest[n_cast + 1:2 * n_cast + 1], rest[-2:]
    for src_ref, dst_ref in zip(cast_in, cast_out):
        dst_ref[...] = src_ref[...].astype(dst_ref.dtype)

    n = pl.program_id(1)
    w = WINDOW
    inv = inv_ref[...]

    def tables(pos_row):
        ang = inv * pos_row.astype(F32)
        return jnp.cos(ang), jnp.sin(ang)

    def rope_k(k_nat, cos, sin):
        kt = _rope_t(k_nat.astype(F32).T, cos, sin)
        return kt.T.astype(BF16)

    cos_p, sin_p = tables(posp_ref[...])
    k_prev = rope_k(kp_ref[...], cos_p, sin_p)
    v_prev = vp_ref[...]

    @pl.when((pl.program_id(0) == 0) & (n == 0))
    def _():
        key = lax.broadcasted_iota(jnp.int32, band_ref.shape, 0)
        qry = lax.broadcasted_iota(jnp.int32, band_ref.shape, 1)
        band_ref[...] = jnp.where((key <= qry + w) & (key > qry), 0.0, NEG_BIG).astype(BF16)
        src = lax.broadcasted_iota(jnp.int32, eye_ref.shape, 0)
        dst = lax.broadcasted_iota(jnp.int32, eye_ref.shape, 1) % w
        eye_ref[...] = jnp.where(src == dst, 1.0, 0.0).astype(BF16)

    no_prev = jnp.where(n > 0, 0.0, NEG_BIG)
    q_scale = SWA_HEAD_DIM ** -0.5 * LOG2E
    zeros_half = jnp.zeros((SWA_HEAD_DIM, SWA_GROUP * w), F32)
    ones_rows = jnp.ones((2 * SUBLANES, 2 * w), BF16)

    sinks = [jnp.concatenate(
        [jnp.full((1, w), sink_ref[hd] * LOG2E, F32)
         for hd in range(g * SWA_GROUP, (g + 1) * SWA_GROUP)], axis=1)
        for g in range(SWA_KV_HEADS)]

    def logits(wi, k_prev):
        tok = slice(wi * w, (wi + 1) * w)
        cos, sin = tables(posc_ref[:, tok])
        k_cur = rope_k(kc_ref[tok, :], cos, sin)
        k_all = jnp.concatenate([k_prev, k_cur], axis=0)
        k_aug = jnp.concatenate([k_all, band_ref[...]], axis=1)
        qf = q_ref[tok, :].astype(F32)
        qt = jnp.concatenate(
            [qf[:, c * LANES:(c + 1) * LANES].T for c in range(SWA_WIDTH // LANES)], axis=0)
        qt = _rope_t(qt, cos * q_scale, sin * q_scale)
        out = []
        for g in range(SWA_KV_HEADS):
            heads = range(g * SWA_GROUP, (g + 1) * SWA_GROUP)
            q_g = jnp.concatenate(
                [qt[hd * SWA_HEAD_DIM:(hd + 1) * SWA_HEAD_DIM] for hd in heads], axis=1)
            parts = [zeros_half] * SWA_KV_HEADS
            parts[g] = q_g
            q_z = jnp.concatenate(parts, axis=0).astype(BF16)
            q_aug = jnp.concatenate([q_z, eye_ref[...]], axis=0)
            s = jnp.dot(k_aug, q_aug, preferred_element_type=F32)
            if wi == 0:
                s = jnp.concatenate([s[:w] + no_prev, s[w:]], axis=0)
            out.append(s)
        return out, k_cur

    def finish(wi, scores, v_prev):
        tok = slice(wi * w, (wi + 1) * w)
        v_cur = vc_ref[tok, :]
        v_all_t = jnp.concatenate([v_prev, v_cur], axis=0).astype(F32).T.astype(BF16)
        for g in range(SWA_KV_HEADS):
            s, sink = scores[g], sinks[g]
            m = jnp.maximum(jnp.max(s, axis=0, keepdims=True), sink)
            e = jnp.exp2(s - m).astype(BF16)
            v_aug = jnp.concatenate(
                [v_all_t[g * SWA_HEAD_DIM:(g + 1) * SWA_HEAD_DIM], ones_rows], axis=0)
            pv = jnp.dot(v_aug, e, preferred_element_type=F32)
            denom = pv[SWA_HEAD_DIM:SWA_HEAD_DIM + 1] + jnp.exp2(sink - m)
            out_t = pv[:SWA_HEAD_DIM] / denom
            for jj in range(SWA_GROUP // 2):
                pair = jnp.concatenate(
                    [out_t[:, (2 * jj) * w:(2 * jj + 1) * w],
                     out_t[:, (2 * jj + 1) * w:(2 * jj + 2) * w]], axis=0)
                col = (g * (SWA_GROUP // 2) + jj) * LANES
                o_ref[tok, col:col + LANES] = pair.T.astype(o_ref.dtype)
        return v_cur

    pending, k_prev = logits(0, k_prev)
    for wi in range(SWA_NW):
        if wi + 1 < SWA_NW:
            upcoming, k_prev = logits(wi + 1, k_prev)
        v_prev = finish(wi, pending, v_prev)
        if wi + 1 < SWA_NW:
            pending = upcoming


def _swa_attention(sinks, proj, pos_row, inv_tab, batch, seq, weights_f32):
    nb = seq // SWA_QB
    per = SWA_QB // WINDOW
    qcol = _SWA_OFF["q_b"] // SWA_WIDTH
    kcol = _SWA_OFF["k_b"] // LANES
    vcol = _SWA_OFF["v_b"] // LANES
    steps = batch * nb

    def cur(b, n):
        return b * nb + n

    def prev(b, n):
        return jnp.maximum((b * nb + n) * per - 1, 0)

    def chunk_spec(wgt):
        rows, cols = wgt.shape
        assert rows % (steps * 2 * SUBLANES) == 0
        return pl.BlockSpec((rows // steps, cols), lambda b, n: (cur(b, n), 0))

    cast_specs = [chunk_spec(wgt) for wgt in weights_f32]
    return pl.pallas_call(
        _swa_kernel,
        grid=(batch, nb),
        in_specs=[
            pl.BlockSpec(memory_space=pltpu.SMEM),
            pl.BlockSpec((SWA_QB, SWA_WIDTH), lambda b, n: (cur(b, n), qcol)),
            pl.BlockSpec((SWA_QB, LANES), lambda b, n: (cur(b, n), kcol)),
            pl.BlockSpec((WINDOW, LANES), lambda b, n: (prev(b, n), kcol)),
            pl.BlockSpec((SWA_QB, LANES), lambda b, n: (cur(b, n), vcol)),
            pl.BlockSpec((WINDOW, LANES), lambda b, n: (prev(b, n), vcol)),
            pl.BlockSpec((1, SWA_QB), lambda b, n: (0, cur(b, n))),
            pl.BlockSpec((1, WINDOW), lambda b, n: (0, prev(b, n))),
            pl.BlockSpec((ROPE_HALF, LANES), lambda b, n: (0, 0)),
            *cast_specs,
        ],
        out_specs=[pl.BlockSpec((SWA_QB, SWA_WIDTH), lambda b, n: (cur(b, n), 0)), *cast_specs],
        out_shape=[jax.ShapeDtypeStruct((batch * seq, SWA_WIDTH), BF16),
                   *[jax.ShapeDtypeStruct(wgt.shape, BF16) for wgt in weights_f32]],
        scratch_shapes=[
            pltpu.VMEM((2 * WINDOW, WINDOW), BF16),
            pltpu.VMEM((WINDOW, SWA_GROUP * WINDOW), BF16),
        ],
        compiler_params=pltpu.CompilerParams(
            dimension_semantics=("arbitrary", "arbitrary")),
        name="swa_attention",
    )(sinks, proj, proj, proj, proj, proj, pos_row, pos_row, inv_tab, *weights_f32)


def _silu(z):
    return z * jax.nn.sigmoid(z)


def _epilogue_kernel(ya_ref, yb_ref, gz_ref, x_ref, p_ref,
                     wof_ref, wos_ref, wout_ref, gpost_ref, wple_ref, wgate_ref, o_ref):
    def gz(name, width):
        return gz_ref[:, _MAIN_OFF[name]:_MAIN_OFF[name] + width].astype(F32)

    ua = (ya_ref[...].astype(F32) * _silu(gz("z_a", FOX_WIDTH))).astype(BF16)
    ub = (yb_ref[...].astype(F32) * _silu(gz("z_b", SWA_WIDTH))).astype(BF16)
    oa = jnp.dot(ua, wof_ref[...], preferred_element_type=F32)
    ob = jnp.dot(ub, wos_ref[...], preferred_element_type=F32)
    merged = (jax.nn.sigmoid(gz("g_a", D_MODEL)) * oa
              + jax.nn.sigmoid(gz("g_b", D_MODEL)) * ob)
    out = jnp.dot(merged.astype(BF16), wout_ref[...], preferred_element_type=F32)
    ms = jnp.mean(out * out, axis=-1, keepdims=True)
    x1 = x_ref[...] + (out * lax.rsqrt(ms + NORM_EPS)) * gpost_ref[...]
    e = jnp.dot(p_ref[...].astype(BF16), wple_ref[...], preferred_element_type=F32)
    gate = jax.nn.sigmoid(jnp.dot(x1.astype(BF16), wgate_ref[...],
                                  preferred_element_type=F32))
    o_ref[...] = x1 + gate * e


def _epilogue(ya, yb, proj, x2, p2, wof, wos, wout, gpost, wple, wgate):
    t = x2.shape[0]
    gz_cols = _MAIN_OFF["z_b"] + SWA_WIDTH
    assert _MAIN_OFF["g_a"] == 0 and gz_cols == 2 * D_MODEL + FOX_WIDTH + SWA_WIDTH
    once = pl.Buffered(1)

    def const(shape):
        return pl.BlockSpec(shape, lambda i: (0, 0), pipeline_mode=once)

    return pl.pallas_call(
        _epilogue_kernel,
        grid=(t // EPI_TM,),
        in_specs=[
            pl.BlockSpec((EPI_TM, FOX_WIDTH), lambda i: (i, 0)),
            pl.BlockSpec((EPI_TM, SWA_WIDTH), lambda i: (i, 0)),
            pl.BlockSpec((EPI_TM, gz_cols), lambda i: (i, 0)),
            pl.BlockSpec((EPI_TM, D_MODEL), lambda i: (i, 0)),
            pl.BlockSpec((EPI_TM, PLE_DIM), lambda i: (i, 0)),
            const((FOX_WIDTH, D_MODEL)),
            const((SWA_WIDTH, D_MODEL)),
            const((D_MODEL, D_MODEL)),
            const((1, D_MODEL)),
            const((PLE_DIM, D_MODEL)),
            const((D_MODEL, D_MODEL)),
        ],
        out_specs=pl.BlockSpec((EPI_TM, D_MODEL), lambda i: (i, 0)),
        out_shape=jax.ShapeDtypeStruct((t, D_MODEL), F32),
        compiler_params=pltpu.CompilerParams(
            dimension_semantics=("arbitrary",),
            vmem_limit_bytes=VMEM_LIMIT_BYTES),
        name="epilogue",
    )(ya, yb, proj, x2, p2, wof, wos, wout, gpost, wple, wgate)


def _layer(x2, p2, pos_row, batch, seq, pre_g, w_in, b_forget, sinks, w_o_fox, w_o_swa,
           w_out, post_g, w_ple, w_ple_gate):
    w_t = w_in.T
    f0 = _REF_OFF["f_a"]
    wf_t = jnp.pad(w_t[f0:f0 + FOX_HEADS], ((0, F_ROWS - FOX_HEADS), (0, 0))).astype(BF16)
    col = jnp.arange(MAIN_COLS)
    in_q_a = (col >= _MAIN_OFF["q_a"]) & (col < _MAIN_OFF["q_a"] + FOX_WIDTH)
    col_scale = jnp.where(in_q_a, Q_A_SCALE, 1.0).astype(F32).reshape(1, MAIN_COLS)

    h, kb_tok, proj_swa = _prenorm(x2, pre_g.reshape(1, D_MODEL), wf_t,
                                   b_forget.reshape(FOX_HEADS, 1).astype(F32), w_t, seq)
    proj = _inproj(h, w_t, col_scale)
    ya = _fox_attention(proj, kb_tok, batch, seq)

    inv = ROPE_THETA ** (-jnp.arange(ROPE_HALF, dtype=F32) / ROPE_HALF)
    inv_tab = jnp.broadcast_to(inv[:, None], (ROPE_HALF, LANES))
    yb, wof, wos, wout, wple, wgate = _swa_attention(
        sinks.astype(F32), proj_swa, pos_row, inv_tab, batch, seq,
        (w_o_fox, w_o_swa, w_out, w_ple, w_ple_gate))

    return _epilogue(ya, yb, proj, x2, p2, wof, wos, wout,
                     post_g.reshape(1, D_MODEL), wple, wgate)


def kernel(x, p, positions, pre_norm_g, w_in, b_forget, sinks, w_o_fox, w_o_swa, w_out,
           post_norm_g, w_ple, w_ple_gate):
    batch, seq, _ = x.shape
    depth = p.shape[0]
    x2 = x.reshape(batch * seq, D_MODEL)
    pos_row = positions.reshape(1, batch * seq)
    for i in range(depth):
        x2 = _layer(x2, p[i].reshape(batch * seq, PLE_DIM), pos_row, batch, seq,
                    pre_norm_g[i], w_in[i], b_forget[i], sinks[i], w_o_fox[i], w_o_swa[i],
                    w_out[i], post_norm_g[i], w_ple[i], w_ple_gate[i])
    return x2.reshape(batch, seq, D_MODEL)
```

```python
import functools
import math

import jax
import jax.numpy as jnp
from jax import lax
from jax.experimental import pallas as pl
from jax.experimental.pallas import tpu as pltpu

F32 = jnp.float32
BF16 = jnp.bfloat16

D_MODEL = 2048
FOX_HEADS = 8
FOX_HEAD_DIM = 128
FOX_WIDTH = FOX_HEADS * FOX_HEAD_DIM
SWA_Q_HEADS = 16
SWA_KV_HEADS = 2
SWA_HEAD_DIM = 64
SWA_WIDTH = SWA_Q_HEADS * SWA_HEAD_DIM
SWA_KV_WIDTH = SWA_KV_HEADS * SWA_HEAD_DIM
SWA_GROUP = SWA_Q_HEADS // SWA_KV_HEADS
ROPE_HALF = SWA_HEAD_DIM // 2
WINDOW = 128
ROPE_THETA = 10000.0
PLE_DIM = 256
NORM_EPS = 1e-6
LANES = 128
V7X_VMEM_BYTES = 64 * 1024 * 1024
VMEM_LIMIT_BYTES = V7X_VMEM_BYTES * 7 // 8
LOG2E = math.log2(math.e)
NEG_BIG = -1e30

_REF_SPLITS = (FOX_WIDTH, FOX_WIDTH, FOX_WIDTH, FOX_WIDTH, FOX_HEADS, SWA_WIDTH,
               SWA_KV_WIDTH, SWA_KV_WIDTH, SWA_WIDTH, D_MODEL, D_MODEL)
_REF_NAMES = ("q_a", "k_a", "v_a", "z_a", "f_a", "q_b", "k_b", "v_b", "z_b", "g_a", "g_b")
_REF_OFF = {}
_o = 0
for _n, _s in zip(_REF_NAMES, _REF_SPLITS):
    _REF_OFF[_n] = _o
    _o += _s

IN_TM = 1024
IN_TN = 1536
NORM_ROWS = 1024
F_ROWS = 16

PREP_TN = 256
_SWA_GROUPS = (
    (_REF_OFF["q_b"], SWA_WIDTH),
    (_REF_OFF["k_b"], 2 * SWA_KV_WIDTH),
)
_MAIN_GROUPS = (
    (_REF_OFF["g_a"], 2 * D_MODEL),
    (_REF_OFF["z_a"], FOX_WIDTH),
    (_REF_OFF["z_b"], SWA_WIDTH),
    (_REF_OFF["q_a"], 3 * FOX_WIDTH),
)
SUBLANES = 8


def _group_table(groups):
    assert all(c0 % SUBLANES == 0 for c0, _ in groups)
    counts = [-(-w // PREP_TN) for _, w in groups]
    return tuple(sum(counts[:i]) for i in range(len(groups))), sum(counts)


_SWA_START, _SWA_WINDOWS = _group_table(_SWA_GROUPS)
_MAIN_START, _MAIN_WINDOWS = _group_table(_MAIN_GROUPS)
W_PER_TILE = IN_TN // PREP_TN
assert _MAIN_WINDOWS % W_PER_TILE == 0
SWA_COLS = _SWA_WINDOWS * PREP_TN
MAIN_COLS = _MAIN_WINDOWS * PREP_TN
_SWA_OFF = {"q_b": 0, "k_b": _SWA_START[1] * PREP_TN,
            "v_b": _SWA_START[1] * PREP_TN + SWA_KV_WIDTH}
_MAIN_OFF = {
    "g_a": 0, "g_b": D_MODEL,
    "z_a": _MAIN_START[1] * PREP_TN,
    "z_b": _MAIN_START[2] * PREP_TN,
    "q_a": _MAIN_START[3] * PREP_TN,
    "k_a": _MAIN_START[3] * PREP_TN + FOX_WIDTH,
    "v_a": _MAIN_START[3] * PREP_TN + 2 * FOX_WIDTH,
}
_NT = (((1,), (1,)), ((), ()))
Q_A_SCALE = FOX_HEAD_DIM ** -0.5 * LOG2E

FOX_T = 512
FOX_CHUNK = 512
FOX_PAIR = 4
BIAS_TERMS = 3
ONES_ROWS = 16

SWA_NW = 8
SWA_QB = SWA_NW * WINDOW

EPI_TM = 256


def _cast_windows(w_refs, wb_ref):
    for k, w_ref in enumerate(w_refs):
        for r in range(PREP_TN // LANES):
            rows = slice(r * LANES, (r + 1) * LANES)
            wb_ref[k * PREP_TN + r * LANES:k * PREP_TN + (r + 1) * LANES, :] = (
                w_ref[rows, :].astype(BF16))


def _w_src_row(window, groups, starts):
    tile_row = jnp.int32(0)
    for (c0, _), start in zip(groups, starts):
        tile_row = jnp.where(window >= start,
                             c0 // SUBLANES + (window - start) * (PREP_TN // SUBLANES), tile_row)
    return tile_row * SUBLANES


def _prenorm_kernel(steps_per_seq, x_ref, g_ref, wf_ref, b_ref, *refs):
    w_refs = refs[:_SWA_WINDOWS]
    h_ref, kb_ref, proj_ref, wb_ref, carry_ref = refs[_SWA_WINDOWS:]
    i = pl.program_id(0)

    @pl.when(i == 0)
    def _():
        _cast_windows(w_refs, wb_ref)

    @pl.when(i % steps_per_seq == 0)
    def _():
        carry_ref[...] = jnp.zeros(carry_ref.shape, F32)

    x = x_ref[...]
    ms = jnp.mean(x * x, axis=-1, keepdims=True)
    h = ((x * lax.rsqrt(ms + NORM_EPS)) * g_ref[...]).astype(BF16)
    h_ref[...] = h
    ft = lax.dot_general(wf_ref[...], h, _NT, preferred_element_type=F32)
    proj_ref[...] = lax.dot_general(h, wb_ref[...], _NT,
                                    preferred_element_type=F32).astype(BF16)

    f = ft[0:FOX_HEADS, :] + b_ref[...]
    c = jnp.minimum(f, 0.0) - jnp.log1p(jnp.exp(-jnp.abs(f)))
    rows = c.shape[1]
    lane = lax.broadcasted_iota(jnp.int32, c.shape, 1)
    shift = 1
    while shift < rows:
        c = c + jnp.where(lane >= shift, pltpu.roll(c, shift, axis=1), 0.0)
        shift *= 2
    c = c + carry_ref[:, 0:1]
    carry_ref[...] = jnp.broadcast_to(c[:, rows - 1:rows], carry_ref.shape)
    rest = c * (-LOG2E)
    pieces = []
    for _ in range(BIAS_TERMS):
        piece = rest.astype(BF16).astype(F32)
        pieces.append(piece)
        rest = rest - piece
    pieces.append(jnp.zeros((LANES - BIAS_TERMS * FOX_HEADS, rows), F32))
    kb_ref[...] = jnp.concatenate(pieces, axis=0).T.astype(BF16)


def _prenorm(x2, g, wf_t, b_col, w_t, seq):
    t = x2.shape[0]
    once = pl.Buffered(1)

    def w_spec(k):
        return pl.BlockSpec((pl.Element(PREP_TN), pl.Element(D_MODEL)),
                            lambda i: (_w_src_row(k, _SWA_GROUPS, _SWA_START), 0),
                            pipeline_mode=once)

    return pl.pallas_call(
        functools.partial(_prenorm_kernel, seq // NORM_ROWS),
        grid=(t // NORM_ROWS,),
        in_specs=[
            pl.BlockSpec((NORM_ROWS, D_MODEL), lambda i: (i, 0)),
            pl.BlockSpec((1, D_MODEL), lambda i: (0, 0)),
            pl.BlockSpec((F_ROWS, D_MODEL), lambda i: (0, 0)),
            pl.BlockSpec((FOX_HEADS, 1), lambda i: (0, 0)),
            *[w_spec(k) for k in range(_SWA_WINDOWS)],
        ],
        out_specs=[
            pl.BlockSpec((NORM_ROWS, D_MODEL), lambda i: (i, 0)),
            pl.BlockSpec((NORM_ROWS, LANES), lambda i: (i, 0)),
            pl.BlockSpec((NORM_ROWS, SWA_COLS), lambda i: (i, 0)),
        ],
        out_shape=[
            jax.ShapeDtypeStruct((t, D_MODEL), BF16),
            jax.ShapeDtypeStruct((t, LANES), BF16),
            jax.ShapeDtypeStruct((t, SWA_COLS), BF16),
        ],
        scratch_shapes=[
            pltpu.VMEM((SWA_COLS, D_MODEL), BF16),
            pltpu.VMEM((FOX_HEADS, LANES), F32),
        ],
        compiler_params=pltpu.CompilerParams(
            dimension_semantics=("arbitrary",),
            vmem_limit_bytes=VMEM_LIMIT_BYTES),
        name="prenorm",
    )(x2, g, wf_t, b_col, *([w_t] * _SWA_WINDOWS))


def _inproj_kernel(h_ref, *refs):
    w_refs = refs[:W_PER_TILE]
    scale_ref, proj_ref, wb_ref = refs[W_PER_TILE:]

    @pl.when(pl.program_id(1) == 0)
    def _():
        _cast_windows(w_refs, wb_ref)

    acc = lax.dot_general(h_ref[...], wb_ref[...], _NT, preferred_element_type=F32)
    proj_ref[...] = (acc * scale_ref[...]).astype(BF16)


def _inproj(h, w_t, col_scale):
    t = h.shape[0]

    def w_spec(k):
        return pl.BlockSpec(
            (pl.Element(PREP_TN), pl.Element(D_MODEL)),
            lambda j, i: (_w_src_row(j * W_PER_TILE + k, _MAIN_GROUPS, _MAIN_START), 0))

    return pl.pallas_call(
        _inproj_kernel,
        grid=(MAIN_COLS // IN_TN, t // IN_TM),
        in_specs=[
            pl.BlockSpec((IN_TM, D_MODEL), lambda j, i: (i, 0)),
            *[w_spec(k) for k in range(W_PER_TILE)],
            pl.BlockSpec((1, IN_TN), lambda j, i: (0, j)),
        ],
        out_specs=pl.BlockSpec((IN_TM, IN_TN), lambda j, i: (i, j)),
        out_shape=jax.ShapeDtypeStruct((t, MAIN_COLS), BF16),
        scratch_shapes=[pltpu.VMEM((IN_TN, D_MODEL), BF16)],
        compiler_params=pltpu.CompilerParams(
            dimension_semantics=("arbitrary", "arbitrary"),
            vmem_limit_bytes=VMEM_LIMIT_BYTES),
        name="inproj",
    )(h, *([w_t] * W_PER_TILE), col_scale)


def _fox_kernel(q_ref, k_ref, v_ref, kb_ref, o_ref, vt_ref, qt_ref, acc_ref,
                sa_ref, ma_ref, sb_ref, mb_ref):
    s0_ref, s1_ref = (sa_ref, ma_ref), (sb_ref, mb_ref)
    hp = pl.program_id(1)
    seq = k_ref.shape[0]
    t = FOX_T
    d = FOX_HEAD_DIM

    def transposed(x):
        return x.astype(F32).T.astype(BF16)

    row = lax.broadcasted_iota(jnp.int32, (d, t), 0)
    for hh in range(FOX_PAIR):
        mine = (row < BIAS_TERMS * FOX_HEADS) & (row % FOX_HEADS == hp * FOX_PAIR + hh)
        qt_ref[hh, d:2 * d, :] = jnp.where(mine, 1.0, 0.0).astype(BF16)
        vt_ref[hh, d:, :] = jnp.ones((ONES_ROWS, seq), BF16)
        for r in range(seq // FOX_CHUNK):
            rows = slice(r * FOX_CHUNK, (r + 1) * FOX_CHUNK)
            vt_ref[hh, 0:d, rows] = transposed(v_ref[rows, hh * d:(hh + 1) * d])

    def scores(i, buf):
        s_ref, max_ref = buf
        rows = pl.ds(pl.multiple_of(i * t, t), t)
        for hh in range(FOX_PAIR):
            k_aug = jnp.concatenate([k_ref[rows, hh * d:(hh + 1) * d], kb_ref[rows, :]], axis=1)
            s = jnp.dot(k_aug, qt_ref[hh], preferred_element_type=F32)
            s_ref[hh] = s
            max_ref[hh] = jnp.max(s, axis=0, keepdims=True)

    def absorb(i, buf, carry, masked):
        s_ref, max_ref = buf
        rows = pl.ds(pl.multiple_of(i * t, t), t)
        new = []
        for hh in range(FOX_PAIR):
            m = carry[hh]
            s = s_ref[hh]
            if masked:
                key = lax.broadcasted_iota(jnp.int32, s.shape, 0)
                qry = lax.broadcasted_iota(jnp.int32, s.shape, 1)
                s = jnp.where(key <= qry, s, NEG_BIG)
                tile_max = jnp.max(s, axis=0, keepdims=True)
            else:
                tile_max = max_ref[hh]
            m_new = jnp.maximum(m, tile_max)
            alpha = jnp.exp2(m - m_new)
            p = jnp.exp2(s - m_new)
            acc_ref[hh] = alpha * acc_ref[hh] + jnp.dot(
                vt_ref[hh, :, rows], p.astype(BF16), preferred_element_type=F32)
            new.append(m_new)
        return tuple(new)

    def step(i, s_cur, s_next, carry):
        scores(i + 1, s_next)
        return absorb(i, s_cur, carry, False)

    def pair(j, carry):
        carry = step(2 * j, s0_ref, s1_ref, carry)
        return step(2 * j + 1, s1_ref, s0_ref, carry)

    n_q = seq // t

    def load_q(qi):
        q_rows = pl.ds(pl.multiple_of(qi * t, t), t)
        for hh in range(FOX_PAIR):
            qt_ref[hh, 0:d, :] = transposed(q_ref[q_rows, hh * d:(hh + 1) * d])

    def q_tile(qi, _):
        for hh in range(FOX_PAIR):
            acc_ref[hh] = jnp.zeros((d + ONES_ROWS, t), F32)

        def finish(s_last, carry):
            load_q(jnp.minimum(qi + 1, n_q - 1))
            absorb(qi, s_last, carry, True)
            q_rows = pl.ds(pl.multiple_of(qi * t, t), t)
            for hh in range(FOX_PAIR):
                o_ref[q_rows, hh * d:(hh + 1) * d] = (
                    acc_ref[hh, 0:d, :] / acc_ref[hh, d:d + 1, :]).T.astype(o_ref.dtype)

        def odd_tail(carry):
            finish(s1_ref, step(qi - 1, s0_ref, s1_ref, carry))

        def even_tail(carry):
            finish(s0_ref, carry)

        init = tuple(jnp.full((1, t), NEG_BIG, F32) for _ in range(FOX_PAIR))
        scores(0, s0_ref)
        carry = lax.fori_loop(0, qi // 2, pair, init)
        lax.cond(qi % 2 == 1, odd_tail, even_tail, carry)
        return 0

    load_q(0)
    lax.fori_loop(0, n_q, q_tile, 0)


def _fox_attention(proj, kb_tok, batch, seq):
    width = FOX_PAIR * FOX_HEAD_DIM
    qcol = _MAIN_OFF["q_a"] // width
    kcol = _MAIN_OFF["k_a"] // width
    vcol = _MAIN_OFF["v_a"] // width
    return pl.pallas_call(
        _fox_kernel,
        grid=(batch, FOX_HEADS // FOX_PAIR),
        in_specs=[
            pl.BlockSpec((seq, width), lambda b, h: (b, qcol + h)),
            pl.BlockSpec((seq, width), lambda b, h: (b, kcol + h)),
            pl.BlockSpec((seq, width), lambda b, h: (b, vcol + h)),
            pl.BlockSpec((seq, LANES), lambda b, h: (b, 0)),
        ],
        out_specs=pl.BlockSpec((seq, width), lambda b, h: (b, h)),
        out_shape=jax.ShapeDtypeStruct((batch * seq, FOX_WIDTH), BF16),
        scratch_shapes=[
            pltpu.VMEM((FOX_PAIR, FOX_HEAD_DIM + ONES_ROWS, seq), BF16),
            pltpu.VMEM((FOX_PAIR, 2 * FOX_HEAD_DIM, FOX_T), BF16),
            pltpu.VMEM((FOX_PAIR, FOX_HEAD_DIM + ONES_ROWS, FOX_T), F32),
            pltpu.VMEM((FOX_PAIR, FOX_T, FOX_T), F32),
            pltpu.VMEM((FOX_PAIR, 1, FOX_T), F32),
            pltpu.VMEM((FOX_PAIR, FOX_T, FOX_T), F32),
            pltpu.VMEM((FOX_PAIR, 1, FOX_T), F32),
        ],
        compiler_params=pltpu.CompilerParams(
            dimension_semantics=("arbitrary", "arbitrary"),
            vmem_limit_bytes=VMEM_LIMIT_BYTES),
        name="fox_attention",
    )(proj, proj, proj, kb_tok)


def _rope_t(xt, cos, sin):
    out = []
    for hd in range(xt.shape[0] // SWA_HEAD_DIM):
        x1 = xt[hd * SWA_HEAD_DIM: hd * SWA_HEAD_DIM + ROPE_HALF]
        x2 = xt[hd * SWA_HEAD_DIM + ROPE_HALF: (hd + 1) * SWA_HEAD_DIM]
        out.append(x1 * cos - x2 * sin)
        out.append(x2 * cos + x1 * sin)
    return jnp.concatenate(out, axis=0)


def _swa_kernel(sink_ref, q_ref, kc_ref, kp_ref, vc_ref, vp_ref, posc_ref, posp_ref,
                inv_ref, *rest):
    n_cast = (len(rest) - 3) // 2
    cast_in, o_ref = rest[:n_cast], rest[n_cast]
    cast_out, (band_ref, eye_ref) = rest[n_cast + 1:2 * n_cast + 1], rest[-2:]
    for src_ref, dst_ref in zip(cast_in, cast_out):
        dst_ref[...] = src_ref[...].astype(dst_ref.dtype)

    n = pl.program_id(1)
    w = WINDOW
    inv = inv_ref[...]

    def tables(pos_row):
        ang = inv * pos_row.astype(F32)
        return jnp.cos(ang), jnp.sin(ang)

    def rope_k(k_nat, cos, sin):
        kt = _rope_t(k_nat.astype(F32).T, cos, sin)
        return kt.T.astype(BF16)

    cos_p, sin_p = tables(posp_ref[...])
    k_prev = rope_k(kp_ref[...], cos_p, sin_p)
    v_prev = vp_ref[...]

    @pl.when((pl.program_id(0) == 0) & (n == 0))
    def _():
        key = lax.broadcasted_iota(jnp.int32, band_ref.shape, 0)
        qry = lax.broadcasted_iota(jnp.int32, band_ref.shape, 1)
        band_ref[...] = jnp.where((key <= qry + w) & (key > qry), 0.0, NEG_BIG).astype(BF16)
        src = lax.broadcasted_iota(jnp.int32, eye_ref.shape, 0)
        dst = lax.broadcasted_iota(jnp.int32, eye_ref.shape, 1) % w
        eye_ref[...] = jnp.where(src == dst, 1.0, 0.0).astype(BF16)

    no_prev = jnp.where(n > 0, 0.0, NEG_BIG)
    q_scale = SWA_HEAD_DIM ** -0.5 * LOG2E
    zeros_half = jnp.zeros((SWA_HEAD_DIM, SWA_GROUP * w), F32)
    ones_rows = jnp.ones((2 * SUBLANES, 2 * w), BF16)

    sinks = [jnp.concatenate(
        [jnp.full((1, w), sink_ref[hd] * LOG2E, F32)
         for hd in range(g * SWA_GROUP, (g + 1) * SWA_GROUP)], axis=1)
        for g in range(SWA_KV_HEADS)]

    def logits(wi, k_prev):
        tok = slice(wi * w, (wi + 1) * w)
        cos, sin = tables(posc_ref[:, tok])
        k_cur = rope_k(kc_ref[tok, :], cos, sin)
        k_all = jnp.concatenate([k_prev, k_cur], axis=0)
        k_aug = jnp.concatenate([k_all, band_ref[...]], axis=1)
        qf = q_ref[tok, :].astype(F32)
        qt = jnp.concatenate(
            [qf[:, c * LANES:(c + 1) * LANES].T for c in range(SWA_WIDTH // LANES)], axis=0)
        qt = _rope_t(qt, cos * q_scale, sin * q_scale)
        out = []
        for g in range(SWA_KV_HEADS):
            heads = range(g * SWA_GROUP, (g + 1) * SWA_GROUP)
            q_g = jnp.concatenate(
                [qt[hd * SWA_HEAD_DIM:(hd + 1) * SWA_HEAD_DIM] for hd in heads], axis=1)
            parts = [zeros_half] * SWA_KV_HEADS
            parts[g] = q_g
            q_z = jnp.concatenate(parts, axis=0).astype(BF16)
            q_aug = jnp.concatenate([q_z, eye_ref[...]], axis=0)
            s = jnp.dot(k_aug, q_aug, preferred_element_type=F32)
            if wi == 0:
                s = jnp.concatenate([s[:w] + no_prev, s[w:]], axis=0)
            out.append(s)
        return out, k_cur

    def finish(wi, scores, v_prev):
        tok = slice(wi * w, (wi + 1) * w)
        v_cur = vc_ref[tok, :]
        v_all_t = jnp.concatenate([v_prev, v_cur], axis=0).astype(F32).T.astype(BF16)
        for g in range(SWA_KV_HEADS):
            s, sink = scores[g], sinks[g]
            m = jnp.maximum(jnp.max(s, axis=0, keepdims=True), sink)
            e = jnp.exp2(s - m).astype(BF16)
            v_aug = jnp.concatenate(
                [v_all_t[g * SWA_HEAD_DIM:(g + 1) * SWA_HEAD_DIM], ones_rows], axis=0)
            pv = jnp.dot(v_aug, e, preferred_element_type=F32)
            denom = pv[SWA_HEAD_DIM:SWA_HEAD_DIM + 1] + jnp.exp2(sink - m)
            out_t = pv[:SWA_HEAD_DIM] / denom
            for jj in range(SWA_GROUP // 2):
                pair = jnp.concatenate(
                    [out_t[:, (2 * jj) * w:(2 * jj + 1) * w],
                     out_t[:, (2 * jj + 1) * w:(2 * jj + 2) * w]], axis=0)
                col = (g * (SWA_GROUP // 2) + jj) * LANES
                o_ref[tok, col:col + LANES] = pair.T.astype(o_ref.dtype)
        return v_cur

    pending, k_prev = logits(0, k_prev)
    for wi in range(SWA_NW):
        if wi + 1 < SWA_NW:
            upcoming, k_prev = logits(wi + 1, k_prev)
        v_prev = finish(wi, pending, v_prev)
        if wi + 1 < SWA_NW:
            pending = upcoming


def _swa_attention(sinks, proj, pos_row, inv_tab, batch, seq, weights_f32):
    nb = seq // SWA_QB
    per = SWA_QB // WINDOW
    qcol = _SWA_OFF["q_b"] // SWA_WIDTH
    kcol = _SWA_OFF["k_b"] // LANES
    vcol = _SWA_OFF["v_b"] // LANES
    steps = batch * nb

    def cur(b, n):
        return b * nb + n

    def prev(b, n):
        return jnp.maximum((b * nb + n) * per - 1, 0)

    def chunk_spec(wgt):
        rows, cols = wgt.shape
        assert rows % (steps * 2 * SUBLANES) == 0
        return pl.BlockSpec((rows // steps, cols), lambda b, n: (cur(b, n), 0))

    cast_specs = [chunk_spec(wgt) for wgt in weights_f32]
    return pl.pallas_call(
        _swa_kernel,
        grid=(batch, nb),
        in_specs=[
            pl.BlockSpec(memory_space=pltpu.SMEM),
            pl.BlockSpec((SWA_QB, SWA_WIDTH), lambda b, n: (cur(b, n), qcol)),
            pl.BlockSpec((SWA_QB, LANES), lambda b, n: (cur(b, n), kcol)),
            pl.BlockSpec((WINDOW, LANES), lambda b, n: (prev(b, n), kcol)),
            pl.BlockSpec((SWA_QB, LANES), lambda b, n: (cur(b, n), vcol)),
            pl.BlockSpec((WINDOW, LANES), lambda b, n: (prev(b, n), vcol)),
            pl.BlockSpec((1, SWA_QB), lambda b, n: (0, cur(b, n))),
            pl.BlockSpec((1, WINDOW), lambda b, n: (0, prev(b, n))),
            pl.BlockSpec((ROPE_HALF, LANES), lambda b, n: (0, 0)),
            *cast_specs,
        ],
        out_specs=[pl.BlockSpec((SWA_QB, SWA_WIDTH), lambda b, n: (cur(b, n), 0)), *cast_specs],
        out_shape=[jax.ShapeDtypeStruct((batch * seq, SWA_WIDTH), BF16),
                   *[jax.ShapeDtypeStruct(wgt.shape, BF16) for wgt in weights_f32]],
        scratch_shapes=[
            pltpu.VMEM((2 * WINDOW, WINDOW), BF16),
            pltpu.VMEM((WINDOW, SWA_GROUP * WINDOW), BF16),
        ],
        compiler_params=pltpu.CompilerParams(
            dimension_semantics=("arbitrary", "arbitrary")),
        name="swa_attention",
    )(sinks, proj, proj, proj, proj, proj, pos_row, pos_row, inv_tab, *weights_f32)


def _sigmoid(z):
    return 0.5 * jnp.tanh(0.5 * z) + 0.5


def _silu(z):
    return z * _sigmoid(z)


def _epilogue_kernel(ya_ref, yb_ref, gz_ref, x_ref, p_ref,
                     wof_ref, wos_ref, wout_ref, gpost_ref, wple_ref, wgate_ref, o_ref):
    def gz(name, width):
        return gz_ref[:, _MAIN_OFF[name]:_MAIN_OFF[name] + width].astype(F32)

    ua = (ya_ref[...].astype(F32) * _silu(gz("z_a", FOX_WIDTH))).astype(BF16)
    ub = (yb_ref[...].astype(F32) * _silu(gz("z_b", SWA_WIDTH))).astype(BF16)
    oa = jnp.dot(ua, wof_ref[...], preferred_element_type=F32)
    ob = jnp.dot(ub, wos_ref[...], preferred_element_type=F32)
    merged = (_sigmoid(gz("g_a", D_MODEL)) * oa
              + _sigmoid(gz("g_b", D_MODEL)) * ob)
    out = jnp.dot(merged.astype(BF16), wout_ref[...], preferred_element_type=F32)
    ms = jnp.mean(out * out, axis=-1, keepdims=True)
    x1 = x_ref[...] + (out * lax.rsqrt(ms + NORM_EPS)) * gpost_ref[...]
    e = jnp.dot(p_ref[...].astype(BF16), wple_ref[...], preferred_element_type=F32)
    gate = _sigmoid(jnp.dot(x1.astype(BF16), wgate_ref[...], preferred_element_type=F32))
    o_ref[...] = x1 + gate * e


def _epilogue(ya, yb, proj, x2, p2, wof, wos, wout, gpost, wple, wgate):
    t = x2.shape[0]
    gz_cols = _MAIN_OFF["z_b"] + SWA_WIDTH
    assert _MAIN_OFF["g_a"] == 0 and gz_cols == 2 * D_MODEL + FOX_WIDTH + SWA_WIDTH
    once = pl.Buffered(1)

    def const(shape):
        return pl.BlockSpec(shape, lambda i: (0, 0), pipeline_mode=once)

    return pl.pallas_call(
        _epilogue_kernel,
        grid=(t // EPI_TM,),
        in_specs=[
            pl.BlockSpec((EPI_TM, FOX_WIDTH), lambda i: (i, 0)),
            pl.BlockSpec((EPI_TM, SWA_WIDTH), lambda i: (i, 0)),
            pl.BlockSpec((EPI_TM, gz_cols), lambda i: (i, 0)),
            pl.BlockSpec((EPI_TM, D_MODEL), lambda i: (i, 0)),
            pl.BlockSpec((EPI_TM, PLE_DIM), lambda i: (i, 0)),
            const((FOX_WIDTH, D_MODEL)),
            const((SWA_WIDTH, D_MODEL)),
            const((D_MODEL, D_MODEL)),
            const((1, D_MODEL)),
            const((PLE_DIM, D_MODEL)),
            const((D_MODEL, D_MODEL)),
        ],
        out_specs=pl.BlockSpec((EPI_TM, D_MODEL), lambda i: (i, 0)),
        out_shape=jax.ShapeDtypeStruct((t, D_MODEL), F32),
        compiler_params=pltpu.CompilerParams(
            dimension_semantics=("arbitrary",),
            vmem_limit_bytes=VMEM_LIMIT_BYTES),
        name="epilogue",
    )(ya, yb, proj, x2, p2, wof, wos, wout, gpost, wple, wgate)


def _layer(x2, p2, pos_row, batch, seq, pre_g, w_in, b_forget, sinks, w_o_fox, w_o_swa,
           w_out, post_g, w_ple, w_ple_gate):
    w_t = w_in.T
    f0 = _REF_OFF["f_a"]
    wf_t = jnp.pad(w_t[f0:f0 + FOX_HEADS], ((0, F_ROWS - FOX_HEADS), (0, 0))).astype(BF16)
    col = jnp.arange(MAIN_COLS)
    in_q_a = (col >= _MAIN_OFF["q_a"]) & (col < _MAIN_OFF["q_a"] + FOX_WIDTH)
    col_scale = jnp.where(in_q_a, Q_A_SCALE, 1.0).astype(F32).reshape(1, MAIN_COLS)

    h, kb_tok, proj_swa = _prenorm(x2, pre_g.reshape(1, D_MODEL), wf_t,
                                   b_forget.reshape(FOX_HEADS, 1).astype(F32), w_t, seq)
    proj = _inproj(h, w_t, col_scale)
    ya = _fox_attention(proj, kb_tok, batch, seq)

    inv = ROPE_THETA ** (-jnp.arange(ROPE_HALF, dtype=F32) / ROPE_HALF)
    inv_tab = jnp.broadcast_to(inv[:, None], (ROPE_HALF, LANES))
    yb, wof, wos, wout, wple, wgate = _swa_attention(
        sinks.astype(F32), proj_swa, pos_row, inv_tab, batch, seq,
        (w_o_fox, w_o_swa, w_out, w_ple, w_ple_gate))

    return _epilogue(ya, yb, proj, x2, p2, wof, wos, wout,
                     post_g.reshape(1, D_MODEL), wple, wgate)


def kernel(x, p, positions, pre_norm_g, w_in, b_forget, sinks, w_o_fox, w_o_swa, w_out,
           post_norm_g, w_ple, w_ple_gate):
    batch, seq, _ = x.shape
    depth = p.shape[0]
    x2 = x.reshape(batch * seq, D_MODEL)
    pos_row = positions.reshape(1, batch * seq)
    for i in range(depth):
        x2 = _layer(x2, p[i].reshape(batch * seq, PLE_DIM), pos_row, batch, seq,
                    pre_norm_g[i], w_in[i], b_forget[i], sinks[i], w_o_fox[i], w_o_swa[i],
                    w_out[i], post_norm_g[i], w_ple[i], w_ple_gate[i])
    return x2.reshape(batch, seq, D_MODEL)
```

```python
import functools
import math

import jax
import jax.numpy as jnp
from jax import lax
from jax.experimental import pallas as pl
from jax.experimental.pallas import tpu as pltpu

F32 = jnp.float32
BF16 = jnp.bfloat16

D_MODEL = 2048
FOX_HEADS = 8
FOX_HEAD_DIM = 128
FOX_WIDTH = FOX_HEADS * FOX_HEAD_DIM
SWA_Q_HEADS = 16
SWA_KV_HEADS = 2
SWA_HEAD_DIM = 64
SWA_WIDTH = SWA_Q_HEADS * SWA_HEAD_DIM
SWA_KV_WIDTH = SWA_KV_HEADS * SWA_HEAD_DIM
SWA_GROUP = SWA_Q_HEADS // SWA_KV_HEADS
ROPE_HALF = SWA_HEAD_DIM // 2
WINDOW = 128
ROPE_THETA = 10000.0
PLE_DIM = 256
NORM_EPS = 1e-6
LANES = 128
V7X_VMEM_BYTES = 64 * 1024 * 1024
VMEM_LIMIT_BYTES = V7X_VMEM_BYTES * 7 // 8
LOG2E = math.log2(math.e)
NEG_BIG = -1e30

_REF_SPLITS = (FOX_WIDTH, FOX_WIDTH, FOX_WIDTH, FOX_WIDTH, FOX_HEADS, SWA_WIDTH,
               SWA_KV_WIDTH, SWA_KV_WIDTH, SWA_WIDTH, D_MODEL, D_MODEL)
_REF_NAMES = ("q_a", "k_a", "v_a", "z_a", "f_a", "q_b", "k_b", "v_b", "z_b", "g_a", "g_b")
_REF_OFF = {}
_o = 0
for _n, _s in zip(_REF_NAMES, _REF_SPLITS):
    _REF_OFF[_n] = _o
    _o += _s

IN_TM = 1024
IN_TN = 1536
NORM_ROWS = 1024
F_ROWS = 16

PREP_TN = 256
_SWA_GROUPS = (
    (_REF_OFF["q_b"], SWA_WIDTH),
    (_REF_OFF["k_b"], 2 * SWA_KV_WIDTH),
)
_MAIN_GROUPS = (
    (_REF_OFF["g_a"], 2 * D_MODEL),
    (_REF_OFF["z_a"], FOX_WIDTH),
    (_REF_OFF["z_b"], SWA_WIDTH),
    (_REF_OFF["q_a"], 3 * FOX_WIDTH),
)
SUBLANES = 8


def _group_table(groups):
    assert all(c0 % SUBLANES == 0 for c0, _ in groups)
    counts = [-(-w // PREP_TN) for _, w in groups]
    return tuple(sum(counts[:i]) for i in range(len(groups))), sum(counts)


_SWA_START, _SWA_WINDOWS = _group_table(_SWA_GROUPS)
_MAIN_START, _MAIN_WINDOWS = _group_table(_MAIN_GROUPS)
W_PER_TILE = IN_TN // PREP_TN
assert _MAIN_WINDOWS % W_PER_TILE == 0
SWA_COLS = _SWA_WINDOWS * PREP_TN
MAIN_COLS = _MAIN_WINDOWS * PREP_TN
_SWA_OFF = {"q_b": 0, "k_b": _SWA_START[1] * PREP_TN,
            "v_b": _SWA_START[1] * PREP_TN + SWA_KV_WIDTH}
_MAIN_OFF = {
    "g_a": 0, "g_b": D_MODEL,
    "z_a": _MAIN_START[1] * PREP_TN,
    "z_b": _MAIN_START[2] * PREP_TN,
    "q_a": _MAIN_START[3] * PREP_TN,
    "k_a": _MAIN_START[3] * PREP_TN + FOX_WIDTH,
    "v_a": _MAIN_START[3] * PREP_TN + 2 * FOX_WIDTH,
}
_NT = (((1,), (1,)), ((), ()))
Q_A_SCALE = FOX_HEAD_DIM ** -0.5 * LOG2E

FOX_T = 512
FOX_CHUNK = 512
FOX_PAIR = 4
BIAS_TERMS = 3
ONES_ROWS = 16

SWA_NW = 8
SWA_QB = SWA_NW * WINDOW

EPI_TM = 256


def _cast_windows(w_refs, wb_ref):
    for k, w_ref in enumerate(w_refs):
        for r in range(PREP_TN // LANES):
            rows = slice(r * LANES, (r + 1) * LANES)
            wb_ref[k * PREP_TN + r * LANES:k * PREP_TN + (r + 1) * LANES, :] = (
                w_ref[rows, :].astype(BF16))


def _w_src_row(window, groups, starts):
    tile_row = jnp.int32(0)
    for (c0, _), start in zip(groups, starts):
        tile_row = jnp.where(window >= start,
                             c0 // SUBLANES + (window - start) * (PREP_TN // SUBLANES), tile_row)
    return tile_row * SUBLANES


def _prenorm_kernel(steps_per_seq, x_ref, g_ref, wf_ref, b_ref, pos_ref, inv_ref, *refs):
    w_refs = refs[:_SWA_WINDOWS]
    h_ref, kb_ref, proj_ref, cos_ref, sin_ref, wb_ref, carry_ref = refs[_SWA_WINDOWS:]
    i = pl.program_id(0)

    @pl.when(i == 0)
    def _():
        _cast_windows(w_refs, wb_ref)

    @pl.when(i % steps_per_seq == 0)
    def _():
        carry_ref[...] = jnp.zeros(carry_ref.shape, F32)

    x = x_ref[...]
    ms = jnp.mean(x * x, axis=-1, keepdims=True)
    h = ((x * lax.rsqrt(ms + NORM_EPS)) * g_ref[...]).astype(BF16)
    h_ref[...] = h
    ft = lax.dot_general(wf_ref[...], h, _NT, preferred_element_type=F32)
    proj_ref[...] = lax.dot_general(h, wb_ref[...], _NT,
                                    preferred_element_type=F32).astype(BF16)

    f = ft[0:FOX_HEADS, :] + b_ref[...]
    c = jnp.minimum(f, 0.0) - jnp.log1p(jnp.exp(-jnp.abs(f)))
    rows = c.shape[1]
    lane = lax.broadcasted_iota(jnp.int32, c.shape, 1)
    shift = 1
    while shift < rows:
        c = c + jnp.where(lane >= shift, pltpu.roll(c, shift, axis=1), 0.0)
        shift *= 2
    c = c + carry_ref[:, 0:1]
    carry_ref[...] = jnp.broadcast_to(c[:, rows - 1:rows], carry_ref.shape)
    rest = c * (-LOG2E)
    pieces = []
    for _ in range(BIAS_TERMS):
        piece = rest.astype(BF16).astype(F32)
        pieces.append(piece)
        rest = rest - piece
    pieces.append(jnp.zeros((LANES - BIAS_TERMS * FOX_HEADS, rows), F32))
    kb_ref[...] = jnp.concatenate(pieces, axis=0).T.astype(BF16)

    ang = inv_ref[:, 0:1] * pos_ref[...].astype(F32)
    cos_ref[...] = jnp.cos(ang)
    sin_ref[...] = jnp.sin(ang)


def _prenorm(x2, g, wf_t, b_col, pos_row, inv_tab, w_t, seq):
    t = x2.shape[0]
    once = pl.Buffered(1)

    def w_spec(k):
        return pl.BlockSpec((pl.Element(PREP_TN), pl.Element(D_MODEL)),
                            lambda i: (_w_src_row(k, _SWA_GROUPS, _SWA_START), 0),
                            pipeline_mode=once)

    return pl.pallas_call(
        functools.partial(_prenorm_kernel, seq // NORM_ROWS),
        grid=(t // NORM_ROWS,),
        in_specs=[
            pl.BlockSpec((NORM_ROWS, D_MODEL), lambda i: (i, 0)),
            pl.BlockSpec((1, D_MODEL), lambda i: (0, 0)),
            pl.BlockSpec((F_ROWS, D_MODEL), lambda i: (0, 0)),
            pl.BlockSpec((FOX_HEADS, 1), lambda i: (0, 0)),
            pl.BlockSpec((1, NORM_ROWS), lambda i: (0, i)),
            pl.BlockSpec((ROPE_HALF, LANES), lambda i: (0, 0)),
            *[w_spec(k) for k in range(_SWA_WINDOWS)],
        ],
        out_specs=[
            pl.BlockSpec((NORM_ROWS, D_MODEL), lambda i: (i, 0)),
            pl.BlockSpec((NORM_ROWS, LANES), lambda i: (i, 0)),
            pl.BlockSpec((NORM_ROWS, SWA_COLS), lambda i: (i, 0)),
            pl.BlockSpec((ROPE_HALF, NORM_ROWS), lambda i: (0, i)),
            pl.BlockSpec((ROPE_HALF, NORM_ROWS), lambda i: (0, i)),
        ],
        out_shape=[
            jax.ShapeDtypeStruct((t, D_MODEL), BF16),
            jax.ShapeDtypeStruct((t, LANES), BF16),
            jax.ShapeDtypeStruct((t, SWA_COLS), BF16),
            jax.ShapeDtypeStruct((ROPE_HALF, t), F32),
            jax.ShapeDtypeStruct((ROPE_HALF, t), F32),
        ],
        scratch_shapes=[
            pltpu.VMEM((SWA_COLS, D_MODEL), BF16),
            pltpu.VMEM((FOX_HEADS, LANES), F32),
        ],
        compiler_params=pltpu.CompilerParams(
            dimension_semantics=("arbitrary",),
            vmem_limit_bytes=VMEM_LIMIT_BYTES),
        name="prenorm",
    )(x2, g, wf_t, b_col, pos_row, inv_tab, *([w_t] * _SWA_WINDOWS))


def _inproj_kernel(h_ref, *refs):
    w_refs = refs[:W_PER_TILE]
    scale_ref, proj_ref, wb_ref = refs[W_PER_TILE:]

    @pl.when(pl.program_id(1) == 0)
    def _():
        _cast_windows(w_refs, wb_ref)

    acc = lax.dot_general(h_ref[...], wb_ref[...], _NT, preferred_element_type=F32)
    proj_ref[...] = (acc * scale_ref[...]).astype(BF16)


def _inproj(h, w_t, col_scale):
    t = h.shape[0]

    def w_spec(k):
        return pl.BlockSpec(
            (pl.Element(PREP_TN), pl.Element(D_MODEL)),
            lambda j, i: (_w_src_row(j * W_PER_TILE + k, _MAIN_GROUPS, _MAIN_START), 0))

    return pl.pallas_call(
        _inproj_kernel,
        grid=(MAIN_COLS // IN_TN, t // IN_TM),
        in_specs=[
            pl.BlockSpec((IN_TM, D_MODEL), lambda j, i: (i, 0)),
            *[w_spec(k) for k in range(W_PER_TILE)],
            pl.BlockSpec((1, IN_TN), lambda j, i: (0, j)),
        ],
        out_specs=pl.BlockSpec((IN_TM, IN_TN), lambda j, i: (i, j)),
        out_shape=jax.ShapeDtypeStruct((t, MAIN_COLS), BF16),
        scratch_shapes=[pltpu.VMEM((IN_TN, D_MODEL), BF16)],
        compiler_params=pltpu.CompilerParams(
            dimension_semantics=("arbitrary", "arbitrary"),
            vmem_limit_bytes=VMEM_LIMIT_BYTES),
        name="inproj",
    )(h, *([w_t] * W_PER_TILE), col_scale)


def _fox_kernel(q_ref, k_ref, v_ref, kb_ref, o_ref, vt_ref, qt_ref, acc_ref,
                sa_ref, ma_ref, sb_ref, mb_ref):
    s0_ref, s1_ref = (sa_ref, ma_ref), (sb_ref, mb_ref)
    hp = pl.program_id(1)
    seq = k_ref.shape[0]
    t = FOX_T
    d = FOX_HEAD_DIM

    def transposed(x):
        return x.astype(F32).T.astype(BF16)

    row = lax.broadcasted_iota(jnp.int32, (d, t), 0)
    for hh in range(FOX_PAIR):
        mine = (row < BIAS_TERMS * FOX_HEADS) & (row % FOX_HEADS == hp * FOX_PAIR + hh)
        qt_ref[hh, d:2 * d, :] = jnp.where(mine, 1.0, 0.0).astype(BF16)
        vt_ref[hh, d:, :] = jnp.ones((ONES_ROWS, seq), BF16)
        for r in range(seq // FOX_CHUNK):
            rows = slice(r * FOX_CHUNK, (r + 1) * FOX_CHUNK)
            vt_ref[hh, 0:d, rows] = transposed(v_ref[rows, hh * d:(hh + 1) * d])

    def scores(i, buf):
        s_ref, max_ref = buf
        rows = pl.ds(pl.multiple_of(i * t, t), t)
        for hh in range(FOX_PAIR):
            k_aug = jnp.concatenate([k_ref[rows, hh * d:(hh + 1) * d], kb_ref[rows, :]], axis=1)
            s = jnp.dot(k_aug, qt_ref[hh], preferred_element_type=F32)
            s_ref[hh] = s
            max_ref[hh] = jnp.max(s, axis=0, keepdims=True)

    def absorb(i, buf, carry, masked):
        s_ref, max_ref = buf
        rows = pl.ds(pl.multiple_of(i * t, t), t)
        new = []
        for hh in range(FOX_PAIR):
            m = carry[hh]
            s = s_ref[hh]
            if masked:
                key = lax.broadcasted_iota(jnp.int32, s.shape, 0)
                qry = lax.broadcasted_iota(jnp.int32, s.shape, 1)
                s = jnp.where(key <= qry, s, NEG_BIG)
                tile_max = jnp.max(s, axis=0, keepdims=True)
            else:
                tile_max = max_ref[hh]
            m_new = jnp.maximum(m, tile_max)
            alpha = jnp.exp2(m - m_new)
            p = jnp.exp2(s - m_new)
            acc_ref[hh] = alpha * acc_ref[hh] + jnp.dot(
                vt_ref[hh, :, rows], p.astype(BF16), preferred_element_type=F32)
            new.append(m_new)
        return tuple(new)

    def step(i, s_cur, s_next, carry):
        scores(i + 1, s_next)
        return absorb(i, s_cur, carry, False)

    def pair(j, carry):
        carry = step(2 * j, s0_ref, s1_ref, carry)
        return step(2 * j + 1, s1_ref, s0_ref, carry)

    n_q = seq // t

    def load_q(qi):
        q_rows = pl.ds(pl.multiple_of(qi * t, t), t)
        for hh in range(FOX_PAIR):
            qt_ref[hh, 0:d, :] = transposed(q_ref[q_rows, hh * d:(hh + 1) * d])

    def q_tile(qi, _):
        for hh in range(FOX_PAIR):
            acc_ref[hh] = jnp.zeros((d + ONES_ROWS, t), F32)

        def finish(s_last, carry):
            load_q(jnp.minimum(qi + 1, n_q - 1))
            absorb(qi, s_last, carry, True)
            q_rows = pl.ds(pl.multiple_of(qi * t, t), t)
            for hh in range(FOX_PAIR):
                o_ref[q_rows, hh * d:(hh + 1) * d] = (
                    acc_ref[hh, 0:d, :] / acc_ref[hh, d:d + 1, :]).T.astype(o_ref.dtype)

        def odd_tail(carry):
            finish(s1_ref, step(qi - 1, s0_ref, s1_ref, carry))

        def even_tail(carry):
            finish(s0_ref, carry)

        init = tuple(jnp.full((1, t), NEG_BIG, F32) for _ in range(FOX_PAIR))
        scores(0, s0_ref)
        carry = lax.fori_loop(0, qi // 2, pair, init)
        lax.cond(qi % 2 == 1, odd_tail, even_tail, carry)
        return 0

    load_q(0)
    lax.fori_loop(0, n_q, q_tile, 0)


def _fox_attention(proj, kb_tok, batch, seq):
    width = FOX_PAIR * FOX_HEAD_DIM
    qcol = _MAIN_OFF["q_a"] // width
    kcol = _MAIN_OFF["k_a"] // width
    vcol = _MAIN_OFF["v_a"] // width
    return pl.pallas_call(
        _fox_kernel,
        grid=(batch, FOX_HEADS // FOX_PAIR),
        in_specs=[
            pl.BlockSpec((seq, width), lambda b, h: (b, qcol + h)),
            pl.BlockSpec((seq, width), lambda b, h: (b, kcol + h)),
            pl.BlockSpec((seq, width), lambda b, h: (b, vcol + h)),
            pl.BlockSpec((seq, LANES), lambda b, h: (b, 0)),
        ],
        out_specs=pl.BlockSpec((seq, width), lambda b, h: (b, h)),
        out_shape=jax.ShapeDtypeStruct((batch * seq, FOX_WIDTH), BF16),
        scratch_shapes=[
            pltpu.VMEM((FOX_PAIR, FOX_HEAD_DIM + ONES_ROWS, seq), BF16),
            pltpu.VMEM((FOX_PAIR, 2 * FOX_HEAD_DIM, FOX_T), BF16),
            pltpu.VMEM((FOX_PAIR, FOX_HEAD_DIM + ONES_ROWS, FOX_T), F32),
            pltpu.VMEM((FOX_PAIR, FOX_T, FOX_T), F32),
            pltpu.VMEM((FOX_PAIR, 1, FOX_T), F32),
            pltpu.VMEM((FOX_PAIR, FOX_T, FOX_T), F32),
            pltpu.VMEM((FOX_PAIR, 1, FOX_T), F32),
        ],
        compiler_params=pltpu.CompilerParams(
            dimension_semantics=("arbitrary", "arbitrary"),
            vmem_limit_bytes=VMEM_LIMIT_BYTES),
        name="fox_attention",
    )(proj, proj, proj, kb_tok)


def _rope_t(xt, cos, sin):
    out = []
    for hd in range(xt.shape[0] // SWA_HEAD_DIM):
        x1 = xt[hd * SWA_HEAD_DIM: hd * SWA_HEAD_DIM + ROPE_HALF]
        x2 = xt[hd * SWA_HEAD_DIM + ROPE_HALF: (hd + 1) * SWA_HEAD_DIM]
        out.append(x1 * cos - x2 * sin)
        out.append(x2 * cos + x1 * sin)
    return jnp.concatenate(out, axis=0)


def _swa_kernel(sink_ref, q_ref, kc_ref, kp_ref, vc_ref, vp_ref, cosc_ref, sinc_ref,
                cosp_ref, sinp_ref, *rest):
    n_cast = (len(rest) - 3) // 2
    cast_in, o_ref = rest[:n_cast], rest[n_cast]
    cast_out, (band_ref, eye_ref) = rest[n_cast + 1:2 * n_cast + 1], rest[-2:]
    for src_ref, dst_ref in zip(cast_in, cast_out):
        dst_ref[...] = src_ref[...].astype(dst_ref.dtype)

    n = pl.program_id(1)
    w = WINDOW
    def rope_k(k_nat, cos, sin):
        kt = _rope_t(k_nat.astype(F32).T, cos, sin)
        return kt.T.astype(BF16)

    k_prev = rope_k(kp_ref[...], cosp_ref[...], sinp_ref[...])
    v_prev = vp_ref[...]

    @pl.when((pl.program_id(0) == 0) & (n == 0))
    def _():
        key = lax.broadcasted_iota(jnp.int32, band_ref.shape, 0)
        qry = lax.broadcasted_iota(jnp.int32, band_ref.shape, 1)
        band_ref[...] = jnp.where((key <= qry + w) & (key > qry), 0.0, NEG_BIG).astype(BF16)
        src = lax.broadcasted_iota(jnp.int32, eye_ref.shape, 0)
        dst = lax.broadcasted_iota(jnp.int32, eye_ref.shape, 1) % w
        eye_ref[...] = jnp.where(src == dst, 1.0, 0.0).astype(BF16)

    no_prev = jnp.where(n > 0, 0.0, NEG_BIG)
    q_scale = SWA_HEAD_DIM ** -0.5 * LOG2E
    zeros_half = jnp.zeros((SWA_HEAD_DIM, SWA_GROUP * w), F32)
    ones_rows = jnp.ones((2 * SUBLANES, 2 * w), BF16)

    sinks = [jnp.concatenate(
        [jnp.full((1, w), sink_ref[hd] * LOG2E, F32)
         for hd in range(g * SWA_GROUP, (g + 1) * SWA_GROUP)], axis=1)
        for g in range(SWA_KV_HEADS)]

    def logits(wi, k_prev):
        tok = slice(wi * w, (wi + 1) * w)
        cos, sin = cosc_ref[:, tok], sinc_ref[:, tok]
        k_cur = rope_k(kc_ref[tok, :], cos, sin)
        k_all = jnp.concatenate([k_prev, k_cur], axis=0)
        k_aug = jnp.concatenate([k_all, band_ref[...]], axis=1)
        qf = q_ref[tok, :].astype(F32)
        qt = jnp.concatenate(
            [qf[:, c * LANES:(c + 1) * LANES].T for c in range(SWA_WIDTH // LANES)], axis=0)
        qt = _rope_t(qt, cos * q_scale, sin * q_scale)
        out = []
        for g in range(SWA_KV_HEADS):
            heads = range(g * SWA_GROUP, (g + 1) * SWA_GROUP)
            q_g = jnp.concatenate(
                [qt[hd * SWA_HEAD_DIM:(hd + 1) * SWA_HEAD_DIM] for hd in heads], axis=1)
            parts = [zeros_half] * SWA_KV_HEADS
            parts[g] = q_g
            q_z = jnp.concatenate(parts, axis=0).astype(BF16)
            q_aug = jnp.concatenate([q_z, eye_ref[...]], axis=0)
            s = jnp.dot(k_aug, q_aug, preferred_element_type=F32)
            if wi == 0:
                s = jnp.concatenate([s[:w] + no_prev, s[w:]], axis=0)
            out.append(s)
        return out, k_cur

    def finish(wi, scores, v_prev):
        tok = slice(wi * w, (wi + 1) * w)
        v_cur = vc_ref[tok, :]
        v_all_t = jnp.concatenate([v_prev, v_cur], axis=0).astype(F32).T.astype(BF16)
        for g in range(SWA_KV_HEADS):
            s, sink = scores[g], sinks[g]
            m = jnp.maximum(jnp.max(s, axis=0, keepdims=True), sink)
            e = jnp.exp2(s - m).astype(BF16)
            v_aug = jnp.concatenate(
                [v_all_t[g * SWA_HEAD_DIM:(g + 1) * SWA_HEAD_DIM], ones_rows], axis=0)
            pv = jnp.dot(v_aug, e, preferred_element_type=F32)
            denom = pv[SWA_HEAD_DIM:SWA_HEAD_DIM + 1] + jnp.exp2(sink - m)
            out_t = pv[:SWA_HEAD_DIM] / denom
            for jj in range(SWA_GROUP // 2):
                pair = jnp.concatenate(
                    [out_t[:, (2 * jj) * w:(2 * jj + 1) * w],
                     out_t[:, (2 * jj + 1) * w:(2 * jj + 2) * w]], axis=0)
                col = (g * (SWA_GROUP // 2) + jj) * LANES
                o_ref[tok, col:col + LANES] = pair.T.astype(o_ref.dtype)
        return v_cur

    pending, k_prev = logits(0, k_prev)
    for wi in range(SWA_NW):
        if wi + 1 < SWA_NW:
            upcoming, k_prev = logits(wi + 1, k_prev)
        v_prev = finish(wi, pending, v_prev)
        if wi + 1 < SWA_NW:
            pending = upcoming


def _swa_attention(sinks, proj, cos_t, sin_t, batch, seq, weights_f32):
    nb = seq // SWA_QB
    per = SWA_QB // WINDOW
    qcol = _SWA_OFF["q_b"] // SWA_WIDTH
    kcol = _SWA_OFF["k_b"] // LANES
    vcol = _SWA_OFF["v_b"] // LANES
    steps = batch * nb

    def cur(b, n):
        return b * nb + n

    def prev(b, n):
        return jnp.maximum((b * nb + n) * per - 1, 0)

    def chunk_spec(wgt):
        rows, cols = wgt.shape
        assert rows % (steps * 2 * SUBLANES) == 0
        return pl.BlockSpec((rows // steps, cols), lambda b, n: (cur(b, n), 0))

    cast_specs = [chunk_spec(wgt) for wgt in weights_f32]
    return pl.pallas_call(
        _swa_kernel,
        grid=(batch, nb),
        in_specs=[
            pl.BlockSpec(memory_space=pltpu.SMEM),
            pl.BlockSpec((SWA_QB, SWA_WIDTH), lambda b, n: (cur(b, n), qcol)),
            pl.BlockSpec((SWA_QB, LANES), lambda b, n: (cur(b, n), kcol)),
            pl.BlockSpec((WINDOW, LANES), lambda b, n: (prev(b, n), kcol)),
            pl.BlockSpec((SWA_QB, LANES), lambda b, n: (cur(b, n), vcol)),
            pl.BlockSpec((WINDOW, LANES), lambda b, n: (prev(b, n), vcol)),
            pl.BlockSpec((ROPE_HALF, SWA_QB), lambda b, n: (0, cur(b, n))),
            pl.BlockSpec((ROPE_HALF, SWA_QB), lambda b, n: (0, cur(b, n))),
            pl.BlockSpec((ROPE_HALF, WINDOW), lambda b, n: (0, prev(b, n))),
            pl.BlockSpec((ROPE_HALF, WINDOW), lambda b, n: (0, prev(b, n))),
            *cast_specs,
        ],
        out_specs=[pl.BlockSpec((SWA_QB, SWA_WIDTH), lambda b, n: (cur(b, n), 0)), *cast_specs],
        out_shape=[jax.ShapeDtypeStruct((batch * seq, SWA_WIDTH), BF16),
                   *[jax.ShapeDtypeStruct(wgt.shape, BF16) for wgt in weights_f32]],
        scratch_shapes=[
            pltpu.VMEM((2 * WINDOW, WINDOW), BF16),
            pltpu.VMEM((WINDOW, SWA_GROUP * WINDOW), BF16),
        ],
        compiler_params=pltpu.CompilerParams(
            dimension_semantics=("arbitrary", "arbitrary")),
        name="swa_attention",
    )(sinks, proj, proj, proj, proj, proj, cos_t, sin_t, cos_t, sin_t, *weights_f32)


def _silu(z):
    return z * jax.nn.sigmoid(z)


def _epilogue_kernel(ya_ref, yb_ref, gz_ref, x_ref, p_ref,
                     wof_ref, wos_ref, wout_ref, gpost_ref, wple_ref, wgate_ref, o_ref):
    def gz(name, width):
        return gz_ref[:, _MAIN_OFF[name]:_MAIN_OFF[name] + width].astype(F32)

    ua = (ya_ref[...].astype(F32) * _silu(gz("z_a", FOX_WIDTH))).astype(BF16)
    ub = (yb_ref[...].astype(F32) * _silu(gz("z_b", SWA_WIDTH))).astype(BF16)
    oa = jnp.dot(ua, wof_ref[...], preferred_element_type=F32)
    ob = jnp.dot(ub, wos_ref[...], preferred_element_type=F32)
    merged = (jax.nn.sigmoid(gz("g_a", D_MODEL)) * oa
              + jax.nn.sigmoid(gz("g_b", D_MODEL)) * ob)
    out = jnp.dot(merged.astype(BF16), wout_ref[...], preferred_element_type=F32)
    ms = jnp.mean(out * out, axis=-1, keepdims=True)
    x1 = x_ref[...] + (out * lax.rsqrt(ms + NORM_EPS)) * gpost_ref[...]
    e = jnp.dot(p_ref[...].astype(BF16), wple_ref[...], preferred_element_type=F32)
    gate = jax.nn.sigmoid(jnp.dot(x1.astype(BF16), wgate_ref[...],
                                  preferred_element_type=F32))
    o_ref[...] = x1 + gate * e


def _epilogue(ya, yb, proj, x2, p2, wof, wos, wout, gpost, wple, wgate):
    t = x2.shape[0]
    gz_cols = _MAIN_OFF["z_b"] + SWA_WIDTH
    assert _MAIN_OFF["g_a"] == 0 and gz_cols == 2 * D_MODEL + FOX_WIDTH + SWA_WIDTH
    once = pl.Buffered(1)

    def const(shape):
        return pl.BlockSpec(shape, lambda i: (0, 0), pipeline_mode=once)

    return pl.pallas_call(
        _epilogue_kernel,
        grid=(t // EPI_TM,),
        in_specs=[
            pl.BlockSpec((EPI_TM, FOX_WIDTH), lambda i: (i, 0)),
            pl.BlockSpec((EPI_TM, SWA_WIDTH), lambda i: (i, 0)),
            pl.BlockSpec((EPI_TM, gz_cols), lambda i: (i, 0)),
            pl.BlockSpec((EPI_TM, D_MODEL), lambda i: (i, 0)),
            pl.BlockSpec((EPI_TM, PLE_DIM), lambda i: (i, 0)),
            const((FOX_WIDTH, D_MODEL)),
            const((SWA_WIDTH, D_MODEL)),
            const((D_MODEL, D_MODEL)),
            const((1, D_MODEL)),
            const((PLE_DIM, D_MODEL)),
            const((D_MODEL, D_MODEL)),
        ],
        out_specs=pl.BlockSpec((EPI_TM, D_MODEL), lambda i: (i, 0)),
        out_shape=jax.ShapeDtypeStruct((t, D_MODEL), F32),
        compiler_params=pltpu.CompilerParams(
            dimension_semantics=("arbitrary",),
            vmem_limit_bytes=VMEM_LIMIT_BYTES),
        name="epilogue",
    )(ya, yb, proj, x2, p2, wof, wos, wout, gpost, wple, wgate)


def _layer(x2, p2, pos_row, batch, seq, pre_g, w_in, b_forget, sinks, w_o_fox, w_o_swa,
           w_out, post_g, w_ple, w_ple_gate):
    w_t = w_in.T
    f0 = _REF_OFF["f_a"]
    wf_t = jnp.pad(w_t[f0:f0 + FOX_HEADS], ((0, F_ROWS - FOX_HEADS), (0, 0))).astype(BF16)
    col = jnp.arange(MAIN_COLS)
    in_q_a = (col >= _MAIN_OFF["q_a"]) & (col < _MAIN_OFF["q_a"] + FOX_WIDTH)
    col_scale = jnp.where(in_q_a, Q_A_SCALE, 1.0).astype(F32).reshape(1, MAIN_COLS)

    inv = ROPE_THETA ** (-jnp.arange(ROPE_HALF, dtype=F32) / ROPE_HALF)
    inv_tab = jnp.broadcast_to(inv[:, None], (ROPE_HALF, LANES))

    h, kb_tok, proj_swa, cos_t, sin_t = _prenorm(
        x2, pre_g.reshape(1, D_MODEL), wf_t, b_forget.reshape(FOX_HEADS, 1).astype(F32),
        pos_row, inv_tab, w_t, seq)
    proj = _inproj(h, w_t, col_scale)
    ya = _fox_attention(proj, kb_tok, batch, seq)
    yb, wof, wos, wout, wple, wgate = _swa_attention(
        sinks.astype(F32), proj_swa, cos_t, sin_t, batch, seq,
        (w_o_fox, w_o_swa, w_out, w_ple, w_ple_gate))

    return _epilogue(ya, yb, proj, x2, p2, wof, wos, wout,
                     post_g.reshape(1, D_MODEL), wple, wgate)


def kernel(x, p, positions, pre_norm_g, w_in, b_forget, sinks, w_o_fox, w_o_swa, w_out,
           post_norm_g, w_ple, w_ple_gate):
    batch, seq, _ = x.shape
    depth = p.shape[0]
    x2 = x.reshape(batch * seq, D_MODEL)
    pos_row = positions.reshape(1, batch * seq)
    for i in range(depth):
        x2 = _layer(x2, p[i].reshape(batch * seq, PLE_DIM), pos_row, batch, seq,
                    pre_norm_g[i], w_in[i], b_forget[i], sinks[i], w_o_fox[i], w_o_swa[i],
                    w_out[i], post_norm_g[i], w_ple[i], w_ple_gate[i])
    return x2.reshape(batch, seq, D_MODEL)
```

```python
import functools
import math

import jax
import jax.numpy as jnp
from jax import lax
from jax.experimental import pallas as pl
from jax.experimental.pallas import tpu as pltpu

F32 = jnp.float32
BF16 = jnp.bfloat16

D_MODEL = 2048
FOX_HEADS = 8
FOX_HEAD_DIM = 128
FOX_WIDTH = FOX_HEADS * FOX_HEAD_DIM
SWA_Q_HEADS = 16
SWA_KV_HEADS = 2
SWA_HEAD_DIM = 64
SWA_WIDTH = SWA_Q_HEADS * SWA_HEAD_DIM
SWA_KV_WIDTH = SWA_KV_HEADS * SWA_HEAD_DIM
SWA_GROUP = SWA_Q_HEADS // SWA_KV_HEADS
ROPE_HALF = SWA_HEAD_DIM // 2
WINDOW = 128
ROPE_THETA = 10000.0
PLE_DIM = 256
NORM_EPS = 1e-6
LANES = 128
V7X_VMEM_BYTES = 64 * 1024 * 1024
VMEM_LIMIT_BYTES = V7X_VMEM_BYTES * 7 // 8
LOG2E = math.log2(math.e)
NEG_BIG = -1e30

_REF_SPLITS = (FOX_WIDTH, FOX_WIDTH, FOX_WIDTH, FOX_WIDTH, FOX_HEADS, SWA_WIDTH,
               SWA_KV_WIDTH, SWA_KV_WIDTH, SWA_WIDTH, D_MODEL, D_MODEL)
_REF_NAMES = ("q_a", "k_a", "v_a", "z_a", "f_a", "q_b", "k_b", "v_b", "z_b", "g_a", "g_b")
_REF_OFF = {}
_o = 0
for _n, _s in zip(_REF_NAMES, _REF_SPLITS):
    _REF_OFF[_n] = _o
    _o += _s

IN_TM = 1024
IN_TN = 1536
NORM_ROWS = 1024
F_ROWS = 16

PREP_TN = 256
_SWA_GROUPS = (
    (_REF_OFF["q_b"], SWA_WIDTH),
    (_REF_OFF["k_b"], 2 * SWA_KV_WIDTH),
)
_MAIN_GROUPS = (
    (_REF_OFF["g_a"], 2 * D_MODEL),
    (_REF_OFF["z_a"], FOX_WIDTH),
    (_REF_OFF["z_b"], SWA_WIDTH),
    (_REF_OFF["q_a"], 3 * FOX_WIDTH),
)
SUBLANES = 8


def _group_table(groups):
    assert all(c0 % SUBLANES == 0 for c0, _ in groups)
    counts = [-(-w // PREP_TN) for _, w in groups]
    return tuple(sum(counts[:i]) for i in range(len(groups))), sum(counts)


_SWA_START, _SWA_WINDOWS = _group_table(_SWA_GROUPS)
_MAIN_START, _MAIN_WINDOWS = _group_table(_MAIN_GROUPS)
W_PER_TILE = IN_TN // PREP_TN
assert _MAIN_WINDOWS % W_PER_TILE == 0
SWA_COLS = _SWA_WINDOWS * PREP_TN
MAIN_COLS = _MAIN_WINDOWS * PREP_TN
_SWA_OFF = {"q_b": 0, "k_b": _SWA_START[1] * PREP_TN,
            "v_b": _SWA_START[1] * PREP_TN + SWA_KV_WIDTH}
_MAIN_OFF = {
    "g_a": 0, "g_b": D_MODEL,
    "z_a": _MAIN_START[1] * PREP_TN,
    "z_b": _MAIN_START[2] * PREP_TN,
    "q_a": _MAIN_START[3] * PREP_TN,
    "k_a": _MAIN_START[3] * PREP_TN + FOX_WIDTH,
    "v_a": _MAIN_START[3] * PREP_TN + 2 * FOX_WIDTH,
}
_NT = (((1,), (1,)), ((), ()))
Q_A_SCALE = FOX_HEAD_DIM ** -0.5 * LOG2E

FOX_T = 512
FOX_CHUNK = 512
FOX_PAIR = 4
BIAS_TERMS = 3
ONES_ROWS = 16

SWA_NW = 8
SWA_QB = SWA_NW * WINDOW

EPI_TM = 256


def _cast_windows(w_refs, wb_ref):
    for k, w_ref in enumerate(w_refs):
        for r in range(PREP_TN // LANES):
            rows = slice(r * LANES, (r + 1) * LANES)
            wb_ref[k * PREP_TN + r * LANES:k * PREP_TN + (r + 1) * LANES, :] = (
                w_ref[rows, :].astype(BF16))


def _w_src_row(window, groups, starts):
    tile_row = jnp.int32(0)
    for (c0, _), start in zip(groups, starts):
        tile_row = jnp.where(window >= start,
                             c0 // SUBLANES + (window - start) * (PREP_TN // SUBLANES), tile_row)
    return tile_row * SUBLANES


def _prenorm_kernel(steps_per_seq, x_ref, g_ref, wf_ref, b_ref, *refs):
    w_refs = refs[:_SWA_WINDOWS]
    h_ref, kb_ref, proj_ref, wb_ref, carry_ref = refs[_SWA_WINDOWS:]
    i = pl.program_id(0)

    @pl.when(i == 0)
    def _():
        _cast_windows(w_refs, wb_ref)

    @pl.when(i % steps_per_seq == 0)
    def _():
        carry_ref[...] = jnp.zeros(carry_ref.shape, F32)

    x = x_ref[...]
    ms = jnp.mean(x * x, axis=-1, keepdims=True)
    h = ((x * lax.rsqrt(ms + NORM_EPS)) * g_ref[...]).astype(BF16)
    h_ref[...] = h
    ft = lax.dot_general(wf_ref[...], h, _NT, preferred_element_type=F32)
    proj_ref[...] = lax.dot_general(h, wb_ref[...], _NT,
                                    preferred_element_type=F32).astype(BF16)

    f = ft[0:FOX_HEADS, :] + b_ref[...]
    c = jnp.minimum(f, 0.0) - jnp.log1p(jnp.exp(-jnp.abs(f)))
    rows = c.shape[1]
    lane = lax.broadcasted_iota(jnp.int32, c.shape, 1)
    shift = 1
    while shift < rows:
        c = c + jnp.where(lane >= shift, pltpu.roll(c, shift, axis=1), 0.0)
        shift *= 2
    c = c + carry_ref[:, 0:1]
    carry_ref[...] = jnp.broadcast_to(c[:, rows - 1:rows], carry_ref.shape)
    rest = c * (-LOG2E)
    pieces = []
    for _ in range(BIAS_TERMS):
        piece = rest.astype(BF16).astype(F32)
        pieces.append(piece)
        rest = rest - piece
    pieces.append(jnp.zeros((LANES - BIAS_TERMS * FOX_HEADS, rows), F32))
    kb_ref[...] = jnp.concatenate(pieces, axis=0).T.astype(BF16)


def _prenorm(x2, g, wf_t, b_col, w_t, seq):
    t = x2.shape[0]
    once = pl.Buffered(1)

    def w_spec(k):
        return pl.BlockSpec((pl.Element(PREP_TN), pl.Element(D_MODEL)),
                            lambda i: (_w_src_row(k, _SWA_GROUPS, _SWA_START), 0),
                            pipeline_mode=once)

    return pl.pallas_call(
        functools.partial(_prenorm_kernel, seq // NORM_ROWS),
        grid=(t // NORM_ROWS,),
        in_specs=[
            pl.BlockSpec((NORM_ROWS, D_MODEL), lambda i: (i, 0)),
            pl.BlockSpec((1, D_MODEL), lambda i: (0, 0)),
            pl.BlockSpec((F_ROWS, D_MODEL), lambda i: (0, 0)),
            pl.BlockSpec((FOX_HEADS, 1), lambda i: (0, 0)),
            *[w_spec(k) for k in range(_SWA_WINDOWS)],
        ],
        out_specs=[
            pl.BlockSpec((NORM_ROWS, D_MODEL), lambda i: (i, 0)),
            pl.BlockSpec((NORM_ROWS, LANES), lambda i: (i, 0)),
            pl.BlockSpec((NORM_ROWS, SWA_COLS), lambda i: (i, 0)),
        ],
        out_shape=[
            jax.ShapeDtypeStruct((t, D_MODEL), BF16),
            jax.ShapeDtypeStruct((t, LANES), BF16),
            jax.ShapeDtypeStruct((t, SWA_COLS), BF16),
        ],
        scratch_shapes=[
            pltpu.VMEM((SWA_COLS, D_MODEL), BF16),
            pltpu.VMEM((FOX_HEADS, LANES), F32),
        ],
        compiler_params=pltpu.CompilerParams(
            dimension_semantics=("arbitrary",),
            vmem_limit_bytes=VMEM_LIMIT_BYTES),
        name="prenorm",
    )(x2, g, wf_t, b_col, *([w_t] * _SWA_WINDOWS))


def _inproj_kernel(h_ref, w_hbm, scale_ref, proj_ref, wf_ref, wb_ref, sem):
    j = pl.program_id(0)

    def window_copy(tile, k):
        row = pl.multiple_of(
            _w_src_row(tile * W_PER_TILE + k, _MAIN_GROUPS, _MAIN_START), SUBLANES)
        return pltpu.make_async_copy(
            w_hbm.at[pl.ds(row, PREP_TN), :],
            wf_ref.at[tile % 2, pl.ds(k * PREP_TN, PREP_TN), :],
            sem.at[tile % 2, k])

    @pl.when(pl.program_id(1) == 0)
    def _():
        @pl.when(j == 0)
        def _():
            for k in range(W_PER_TILE):
                window_copy(j, k).start()

        slot = j % 2
        for k in range(W_PER_TILE):
            window_copy(j, k).wait()
            for r in range(PREP_TN // LANES):
                rows = pl.ds(k * PREP_TN + r * LANES, LANES)
                wb_ref[rows, :] = wf_ref[slot, rows, :].astype(BF16)

        @pl.when(j + 1 < pl.num_programs(0))
        def _():
            for k in range(W_PER_TILE):
                window_copy(j + 1, k).start()

    acc = lax.dot_general(h_ref[...], wb_ref[...], _NT, preferred_element_type=F32)
    proj_ref[...] = (acc * scale_ref[...]).astype(BF16)


def _inproj(h, w_t, col_scale):
    t = h.shape[0]
    return pl.pallas_call(
        _inproj_kernel,
        grid=(MAIN_COLS // IN_TN, t // IN_TM),
        in_specs=[
            pl.BlockSpec((IN_TM, D_MODEL), lambda j, i: (i, 0)),
            pl.BlockSpec(memory_space=pl.ANY),
            pl.BlockSpec((1, IN_TN), lambda j, i: (0, j)),
        ],
        out_specs=pl.BlockSpec((IN_TM, IN_TN), lambda j, i: (i, j)),
        out_shape=jax.ShapeDtypeStruct((t, MAIN_COLS), BF16),
        scratch_shapes=[
            pltpu.VMEM((2, IN_TN, D_MODEL), F32),
            pltpu.VMEM((IN_TN, D_MODEL), BF16),
            pltpu.SemaphoreType.DMA((2, W_PER_TILE)),
        ],
        compiler_params=pltpu.CompilerParams(
            dimension_semantics=("arbitrary", "arbitrary"),
            vmem_limit_bytes=VMEM_LIMIT_BYTES),
        name="inproj",
    )(h, w_t, col_scale)


def _fox_kernel(q_ref, k_ref, v_ref, kb_ref, o_ref, vt_ref, qt_ref, acc_ref,
                sa_ref, ma_ref, sb_ref, mb_ref):
    s0_ref, s1_ref = (sa_ref, ma_ref), (sb_ref, mb_ref)
    hp = pl.program_id(1)
    seq = k_ref.shape[0]
    t = FOX_T
    d = FOX_HEAD_DIM

    def transposed(x):
        return x.astype(F32).T.astype(BF16)

    row = lax.broadcasted_iota(jnp.int32, (d, t), 0)
    for hh in range(FOX_PAIR):
        mine = (row < BIAS_TERMS * FOX_HEADS) & (row % FOX_HEADS == hp * FOX_PAIR + hh)
        qt_ref[hh, d:2 * d, :] = jnp.where(mine, 1.0, 0.0).astype(BF16)
        vt_ref[hh, d:, :] = jnp.ones((ONES_ROWS, seq), BF16)
        for r in range(seq // FOX_CHUNK):
            rows = slice(r * FOX_CHUNK, (r + 1) * FOX_CHUNK)
            vt_ref[hh, 0:d, rows] = transposed(v_ref[rows, hh * d:(hh + 1) * d])

    def scores(i, buf):
        s_ref, max_ref = buf
        rows = pl.ds(pl.multiple_of(i * t, t), t)
        for hh in range(FOX_PAIR):
            k_aug = jnp.concatenate([k_ref[rows, hh * d:(hh + 1) * d], kb_ref[rows, :]], axis=1)
            s = jnp.dot(k_aug, qt_ref[hh], preferred_element_type=F32)
            s_ref[hh] = s
            max_ref[hh] = jnp.max(s, axis=0, keepdims=True)

    def absorb(i, buf, carry, masked):
        s_ref, max_ref = buf
        rows = pl.ds(pl.multiple_of(i * t, t), t)
        new = []
        for hh in range(FOX_PAIR):
            m = carry[hh]
            s = s_ref[hh]
            if masked:
                key = lax.broadcasted_iota(jnp.int32, s.shape, 0)
                qry = lax.broadcasted_iota(jnp.int32, s.shape, 1)
                s = jnp.where(key <= qry, s, NEG_BIG)
                tile_max = jnp.max(s, axis=0, keepdims=True)
            else:
                tile_max = max_ref[hh]
            m_new = jnp.maximum(m, tile_max)
            alpha = jnp.exp2(m - m_new)
            p = jnp.exp2(s - m_new)
            acc_ref[hh] = alpha * acc_ref[hh] + jnp.dot(
                vt_ref[hh, :, rows], p.astype(BF16), preferred_element_type=F32)
            new.append(m_new)
        return tuple(new)

    def step(i, s_cur, s_next, carry):
        scores(i + 1, s_next)
        return absorb(i, s_cur, carry, False)

    def pair(j, carry):
        carry = step(2 * j, s0_ref, s1_ref, carry)
        return step(2 * j + 1, s1_ref, s0_ref, carry)

    n_q = seq // t

    def load_q(qi):
        q_rows = pl.ds(pl.multiple_of(qi * t, t), t)
        for hh in range(FOX_PAIR):
            qt_ref[hh, 0:d, :] = transposed(q_ref[q_rows, hh * d:(hh + 1) * d])

    def q_tile(qi, _):
        for hh in range(FOX_PAIR):
            acc_ref[hh] = jnp.zeros((d + ONES_ROWS, t), F32)

        def finish(s_last, carry):
            load_q(jnp.minimum(qi + 1, n_q - 1))
            absorb(qi, s_last, carry, True)
            q_rows = pl.ds(pl.multiple_of(qi * t, t), t)
            for hh in range(FOX_PAIR):
                o_ref[q_rows, hh * d:(hh + 1) * d] = (
                    acc_ref[hh, 0:d, :] / acc_ref[hh, d:d + 1, :]).T.astype(o_ref.dtype)

        def odd_tail(carry):
            finish(s1_ref, step(qi - 1, s0_ref, s1_ref, carry))

        def even_tail(carry):
            finish(s0_ref, carry)

        init = tuple(jnp.full((1, t), NEG_BIG, F32) for _ in range(FOX_PAIR))
        scores(0, s0_ref)
        carry = lax.fori_loop(0, qi // 2, pair, init)
        lax.cond(qi % 2 == 1, odd_tail, even_tail, carry)
        return 0

    load_q(0)
    lax.fori_loop(0, n_q, q_tile, 0)


def _fox_attention(proj, kb_tok, batch, seq):
    width = FOX_PAIR * FOX_HEAD_DIM
    qcol = _MAIN_OFF["q_a"] // width
    kcol = _MAIN_OFF["k_a"] // width
    vcol = _MAIN_OFF["v_a"] // width
    return pl.pallas_call(
        _fox_kernel,
        grid=(batch, FOX_HEADS // FOX_PAIR),
        in_specs=[
            pl.BlockSpec((seq, width), lambda b, h: (b, qcol + h)),
            pl.BlockSpec((seq, width), lambda b, h: (b, kcol + h)),
            pl.BlockSpec((seq, width), lambda b, h: (b, vcol + h)),
            pl.BlockSpec((seq, LANES), lambda b, h: (b, 0)),
        ],
        out_specs=pl.BlockSpec((seq, width), lambda b, h: (b, h)),
        out_shape=jax.ShapeDtypeStruct((batch * seq, FOX_WIDTH), BF16),
        scratch_shapes=[
            pltpu.VMEM((FOX_PAIR, FOX_HEAD_DIM + ONES_ROWS, seq), BF16),
            pltpu.VMEM((FOX_PAIR, 2 * FOX_HEAD_DIM, FOX_T), BF16),
            pltpu.VMEM((FOX_PAIR, FOX_HEAD_DIM + ONES_ROWS, FOX_T), F32),
            pltpu.VMEM((FOX_PAIR, FOX_T, FOX_T), F32),
            pltpu.VMEM((FOX_PAIR, 1, FOX_T), F32),
            pltpu.VMEM((FOX_PAIR, FOX_T, FOX_T), F32),
            pltpu.VMEM((FOX_PAIR, 1, FOX_T), F32),
        ],
        compiler_params=pltpu.CompilerParams(
            dimension_semantics=("arbitrary", "arbitrary"),
            vmem_limit_bytes=VMEM_LIMIT_BYTES),
        name="fox_attention",
    )(proj, proj, proj, kb_tok)


def _rope_t(xt, cos, sin):
    out = []
    for hd in range(xt.shape[0] // SWA_HEAD_DIM):
        x1 = xt[hd * SWA_HEAD_DIM: hd * SWA_HEAD_DIM + ROPE_HALF]
        x2 = xt[hd * SWA_HEAD_DIM + ROPE_HALF: (hd + 1) * SWA_HEAD_DIM]
        out.append(x1 * cos - x2 * sin)
        out.append(x2 * cos + x1 * sin)
    return jnp.concatenate(out, axis=0)


def _swa_kernel(sink_ref, q_ref, kc_ref, kp_ref, vc_ref, vp_ref, posc_ref, posp_ref,
                inv_ref, *rest):
    n_cast = (len(rest) - 3) // 2
    cast_in, o_ref = rest[:n_cast], rest[n_cast]
    cast_out, (band_ref, eye_ref) = rest[n_cast + 1:2 * n_cast + 1], rest[-2:]
    for src_ref, dst_ref in zip(cast_in, cast_out):
        dst_ref[...] = src_ref[...].astype(dst_ref.dtype)

    n = pl.program_id(1)
    w = WINDOW
    inv = inv_ref[...]

    def tables(pos_row):
        ang = inv * pos_row.astype(F32)
        return jnp.cos(ang), jnp.sin(ang)

    def rope_k(k_nat, cos, sin):
        kt = _rope_t(k_nat.astype(F32).T, cos, sin)
        return kt.T.astype(BF16)

    cos_p, sin_p = tables(posp_ref[...])
    k_prev = rope_k(kp_ref[...], cos_p, sin_p)
    v_prev = vp_ref[...]

    @pl.when((pl.program_id(0) == 0) & (n == 0))
    def _():
        key = lax.broadcasted_iota(jnp.int32, band_ref.shape, 0)
        qry = lax.broadcasted_iota(jnp.int32, band_ref.shape, 1)
        band_ref[...] = jnp.where((key <= qry + w) & (key > qry), 0.0, NEG_BIG).astype(BF16)
        src = lax.broadcasted_iota(jnp.int32, eye_ref.shape, 0)
        dst = lax.broadcasted_iota(jnp.int32, eye_ref.shape, 1) % w
        eye_ref[...] = jnp.where(src == dst, 1.0, 0.0).astype(BF16)

    no_prev = jnp.where(n > 0, 0.0, NEG_BIG)
    q_scale = SWA_HEAD_DIM ** -0.5 * LOG2E
    zeros_half = jnp.zeros((SWA_HEAD_DIM, SWA_GROUP * w), F32)
    ones_rows = jnp.ones((2 * SUBLANES, 2 * w), BF16)

    sinks = [jnp.concatenate(
        [jnp.full((1, w), sink_ref[hd] * LOG2E, F32)
         for hd in range(g * SWA_GROUP, (g + 1) * SWA_GROUP)], axis=1)
        for g in range(SWA_KV_HEADS)]

    def logits(wi, k_prev):
        tok = slice(wi * w, (wi + 1) * w)
        cos, sin = tables(posc_ref[:, tok])
        k_cur = rope_k(kc_ref[tok, :], cos, sin)
        k_all = jnp.concatenate([k_prev, k_cur], axis=0)
        k_aug = jnp.concatenate([k_all, band_ref[...]], axis=1)
        qf = q_ref[tok, :].astype(F32)
        qt = jnp.concatenate(
            [qf[:, c * LANES:(c + 1) * LANES].T for c in range(SWA_WIDTH // LANES)], axis=0)
        qt = _rope_t(qt, cos * q_scale, sin * q_scale)
        out = []
        for g in range(SWA_KV_HEADS):
            heads = range(g * SWA_GROUP, (g + 1) * SWA_GROUP)
            q_g = jnp.concatenate(
                [qt[hd * SWA_HEAD_DIM:(hd + 1) * SWA_HEAD_DIM] for hd in heads], axis=1)
            parts = [zeros_half] * SWA_KV_HEADS
            parts[g] = q_g
            q_z = jnp.concatenate(parts, axis=0).astype(BF16)
            q_aug = jnp.concatenate([q_z, eye_ref[...]], axis=0)
            s = jnp.dot(k_aug, q_aug, preferred_element_type=F32)
            if wi == 0:
                s = jnp.concatenate([s[:w] + no_prev, s[w:]], axis=0)
            out.append(s)
        return out, k_cur

    def finish(wi, scores, v_prev):
        tok = slice(wi * w, (wi + 1) * w)
        v_cur = vc_ref[tok, :]
        v_all_t = jnp.concatenate([v_prev, v_cur], axis=0).astype(F32).T.astype(BF16)
        for g in range(SWA_KV_HEADS):
            s, sink = scores[g], sinks[g]
            m = jnp.maximum(jnp.max(s, axis=0, keepdims=True), sink)
            e = jnp.exp2(s - m).astype(BF16)
            v_aug = jnp.concatenate(
                [v_all_t[g * SWA_HEAD_DIM:(g + 1) * SWA_HEAD_DIM], ones_rows], axis=0)
            pv = jnp.dot(v_aug, e, preferred_element_type=F32)
            denom = pv[SWA_HEAD_DIM:SWA_HEAD_DIM + 1] + jnp.exp2(sink - m)
            out_t = pv[:SWA_HEAD_DIM] / denom
            for jj in range(SWA_GROUP // 2):
                pair = jnp.concatenate(
                    [out_t[:, (2 * jj) * w:(2 * jj + 1) * w],
                     out_t[:, (2 * jj + 1) * w:(2 * jj + 2) * w]], axis=0)
                col = (g * (SWA_GROUP // 2) + jj) * LANES
                o_ref[tok, col:col + LANES] = pair.T.astype(o_ref.dtype)
        return v_cur

    pending, k_prev = logits(0, k_prev)
    for wi in range(SWA_NW):
        if wi + 1 < SWA_NW:
            upcoming, k_prev = logits(wi + 1, k_prev)
        v_prev = finish(wi, pending, v_prev)
        if wi + 1 < SWA_NW:
            pending = upcoming


def _swa_attention(sinks, proj, pos_row, inv_tab, batch, seq, weights_f32):
    nb = seq // SWA_QB
    per = SWA_QB // WINDOW
    qcol = _SWA_OFF["q_b"] // SWA_WIDTH
    kcol = _SWA_OFF["k_b"] // LANES
    vcol = _SWA_OFF["v_b"] // LANES
    steps = batch * nb

    def cur(b, n):
        return b * nb + n

    def prev(b, n):
        return jnp.maximum((b * nb + n) * per - 1, 0)

    def chunk_spec(wgt):
        rows, cols = wgt.shape
        assert rows % (steps * 2 * SUBLANES) == 0
        return pl.BlockSpec((rows // steps, cols), lambda b, n: (cur(b, n), 0))

    cast_specs = [chunk_spec(wgt) for wgt in weights_f32]
    return pl.pallas_call(
        _swa_kernel,
        grid=(batch, nb),
        in_specs=[
            pl.BlockSpec(memory_space=pltpu.SMEM),
            pl.BlockSpec((SWA_QB, SWA_WIDTH), lambda b, n: (cur(b, n), qcol)),
            pl.BlockSpec((SWA_QB, LANES), lambda b, n: (cur(b, n), kcol)),
            pl.BlockSpec((WINDOW, LANES), lambda b, n: (prev(b, n), kcol)),
            pl.BlockSpec((SWA_QB, LANES), lambda b, n: (cur(b, n), vcol)),
            pl.BlockSpec((WINDOW, LANES), lambda b, n: (prev(b, n), vcol)),
            pl.BlockSpec((1, SWA_QB), lambda b, n: (0, cur(b, n))),
            pl.BlockSpec((1, WINDOW), lambda b, n: (0, prev(b, n))),
            pl.BlockSpec((ROPE_HALF, LANES), lambda b, n: (0, 0)),
            *cast_specs,
        ],
        out_specs=[pl.BlockSpec((SWA_QB, SWA_WIDTH), lambda b, n: (cur(b, n), 0)), *cast_specs],
        out_shape=[jax.ShapeDtypeStruct((batch * seq, SWA_WIDTH), BF16),
                   *[jax.ShapeDtypeStruct(wgt.shape, BF16) for wgt in weights_f32]],
        scratch_shapes=[
            pltpu.VMEM((2 * WINDOW, WINDOW), BF16),
            pltpu.VMEM((WINDOW, SWA_GROUP * WINDOW), BF16),
        ],
        compiler_params=pltpu.CompilerParams(
            dimension_semantics=("arbitrary", "arbitrary")),
        name="swa_attention",
    )(sinks, proj, proj, proj, proj, proj, pos_row, pos_row, inv_tab, *weights_f32)


def _silu(z):
    return z * jax.nn.sigmoid(z)


def _epilogue_kernel(ya_ref, yb_ref, gz_ref, x_ref, p_ref,
                     wof_ref, wos_ref, wout_ref, gpost_ref, wple_ref, wgate_ref, o_ref):
    def gz(name, width):
        return gz_ref[:, _MAIN_OFF[name]:_MAIN_OFF[name] + width].astype(F32)

    ua = (ya_ref[...].astype(F32) * _silu(gz("z_a", FOX_WIDTH))).astype(BF16)
    ub = (yb_ref[...].astype(F32) * _silu(gz("z_b", SWA_WIDTH))).astype(BF16)
    oa = jnp.dot(ua, wof_ref[...], preferred_element_type=F32)
    ob = jnp.dot(ub, wos_ref[...], preferred_element_type=F32)
    merged = (jax.nn.sigmoid(gz("g_a", D_MODEL)) * oa
              + jax.nn.sigmoid(gz("g_b", D_MODEL)) * ob)
    out = jnp.dot(merged.astype(BF16), wout_ref[...], preferred_element_type=F32)
    ms = jnp.mean(out * out, axis=-1, keepdims=True)
    x1 = x_ref[...] + (out * lax.rsqrt(ms + NORM_EPS)) * gpost_ref[...]
    e = jnp.dot(p_ref[...].astype(BF16), wple_ref[...], preferred_element_type=F32)
    gate = jax.nn.sigmoid(jnp.dot(x1.astype(BF16), wgate_ref[...],
                                  preferred_element_type=F32))
    o_ref[...] = x1 + gate * e


def _epilogue(ya, yb, proj, x2, p2, wof, wos, wout, gpost, wple, wgate):
    t = x2.shape[0]
    gz_cols = _MAIN_OFF["z_b"] + SWA_WIDTH
    assert _MAIN_OFF["g_a"] == 0 and gz_cols == 2 * D_MODEL + FOX_WIDTH + SWA_WIDTH
    once = pl.Buffered(1)

    def const(shape):
        return pl.BlockSpec(shape, lambda i: (0, 0), pipeline_mode=once)

    return pl.pallas_call(
        _epilogue_kernel,
        grid=(t // EPI_TM,),
        in_specs=[
            pl.BlockSpec((EPI_TM, FOX_WIDTH), lambda i: (i, 0)),
            pl.BlockSpec((EPI_TM, SWA_WIDTH), lambda i: (i, 0)),
            pl.BlockSpec((EPI_TM, gz_cols), lambda i: (i, 0)),
            pl.BlockSpec((EPI_TM, D_MODEL), lambda i: (i, 0)),
            pl.BlockSpec((EPI_TM, PLE_DIM), lambda i: (i, 0)),
            const((FOX_WIDTH, D_MODEL)),
            const((SWA_WIDTH, D_MODEL)),
            const((D_MODEL, D_MODEL)),
            const((1, D_MODEL)),
            const((PLE_DIM, D_MODEL)),
            const((D_MODEL, D_MODEL)),
        ],
        out_specs=pl.BlockSpec((EPI_TM, D_MODEL), lambda i: (i, 0)),
        out_shape=jax.ShapeDtypeStruct((t, D_MODEL), F32),
        compiler_params=pltpu.CompilerParams(
            dimension_semantics=("arbitrary",),
            vmem_limit_bytes=VMEM_LIMIT_BYTES),
        name="epilogue",
    )(ya, yb, proj, x2, p2, wof, wos, wout, gpost, wple, wgate)


def _layer(x2, p2, pos_row, batch, seq, pre_g, w_in, b_forget, sinks, w_o_fox, w_o_swa,
           w_out, post_g, w_ple, w_ple_gate):
    w_t = w_in.T
    f0 = _REF_OFF["f_a"]
    wf_t = jnp.pad(w_t[f0:f0 + FOX_HEADS], ((0, F_ROWS - FOX_HEADS), (0, 0))).astype(BF16)
    col = jnp.arange(MAIN_COLS)
    in_q_a = (col >= _MAIN_OFF["q_a"]) & (col < _MAIN_OFF["q_a"] + FOX_WIDTH)
    col_scale = jnp.where(in_q_a, Q_A_SCALE, 1.0).astype(F32).reshape(1, MAIN_COLS)

    h, kb_tok, proj_swa = _prenorm(x2, pre_g.reshape(1, D_MODEL), wf_t,
                                   b_forget.reshape(FOX_HEADS, 1).astype(F32), w_t, seq)
    proj = _inproj(h, w_t, col_scale)
    ya = _fox_attention(proj, kb_tok, batch, seq)

    inv = ROPE_THETA ** (-jnp.arange(ROPE_HALF, dtype=F32) / ROPE_HALF)
    inv_tab = jnp.broadcast_to(inv[:, None], (ROPE_HALF, LANES))
    yb, wof, wos, wout, wple, wgate = _swa_attention(
        sinks.astype(F32), proj_swa, pos_row, inv_tab, batch, seq,
        (w_o_fox, w_o_swa, w_out, w_ple, w_ple_gate))

    return _epilogue(ya, yb, proj, x2, p2, wof, wos, wout,
                     post_g.reshape(1, D_MODEL), wple, wgate)


def kernel(x, p, positions, pre_norm_g, w_in, b_forget, sinks, w_o_fox, w_o_swa, w_out,
           post_norm_g, w_ple, w_ple_gate):
    batch, seq, _ = x.shape
    depth = p.shape[0]
    x2 = x.reshape(batch * seq, D_MODEL)
    pos_row = positions.reshape(1, batch * seq)
    for i in range(depth):
        x2 = _layer(x2, p[i].reshape(batch * seq, PLE_DIM), pos_row, batch, seq,
                    pre_norm_g[i], w_in[i], b_forget[i], sinks[i], w_o_fox[i], w_o_swa[i],
                    w_out[i], post_norm_g[i], w_ple[i], w_ple_gate[i])
    return x2.reshape(batch, seq, D_MODEL)
```

```python
import functools
import math

import jax
import jax.numpy as jnp
from jax import lax
from jax.experimental import pallas as pl
from jax.experimental.pallas import tpu as pltpu

F32 = jnp.float32
BF16 = jnp.bfloat16

D_MODEL = 2048
FOX_HEADS = 8
FOX_HEAD_DIM = 128
FOX_WIDTH = FOX_HEADS * FOX_HEAD_DIM
SWA_Q_HEADS = 16
SWA_KV_HEADS = 2
SWA_HEAD_DIM = 64
SWA_WIDTH = SWA_Q_HEADS * SWA_HEAD_DIM
SWA_KV_WIDTH = SWA_KV_HEADS * SWA_HEAD_DIM
SWA_GROUP = SWA_Q_HEADS // SWA_KV_HEADS
ROPE_HALF = SWA_HEAD_DIM // 2
WINDOW = 128
ROPE_THETA = 10000.0
PLE_DIM = 256
NORM_EPS = 1e-6
LANES = 128
V7X_VMEM_BYTES = 64 * 1024 * 1024
VMEM_LIMIT_BYTES = V7X_VMEM_BYTES * 7 // 8
LOG2E = math.log2(math.e)
NEG_BIG = -1e30

_REF_SPLITS = (FOX_WIDTH, FOX_WIDTH, FOX_WIDTH, FOX_WIDTH, FOX_HEADS, SWA_WIDTH,
               SWA_KV_WIDTH, SWA_KV_WIDTH, SWA_WIDTH, D_MODEL, D_MODEL)
_REF_NAMES = ("q_a", "k_a", "v_a", "z_a", "f_a", "q_b", "k_b", "v_b", "z_b", "g_a", "g_b")
_REF_OFF = {}
_o = 0
for _n, _s in zip(_REF_NAMES, _REF_SPLITS):
    _REF_OFF[_n] = _o
    _o += _s

IN_TM = 1024
IN_TN = 1536
NORM_ROWS = 1024
F_ROWS = 16

PREP_TN = 256
_SWA_GROUPS = (
    (_REF_OFF["q_b"], SWA_WIDTH),
    (_REF_OFF["k_b"], 2 * SWA_KV_WIDTH),
)
_MAIN_GROUPS = (
    (_REF_OFF["g_a"], 2 * D_MODEL),
    (_REF_OFF["z_a"], FOX_WIDTH),
    (_REF_OFF["z_b"], SWA_WIDTH),
    (_REF_OFF["q_a"], 3 * FOX_WIDTH),
)
SUBLANES = 8


def _group_table(groups):
    assert all(c0 % SUBLANES == 0 for c0, _ in groups)
    counts = [-(-w // PREP_TN) for _, w in groups]
    return tuple(sum(counts[:i]) for i in range(len(groups))), sum(counts)


_SWA_START, _SWA_WINDOWS = _group_table(_SWA_GROUPS)
_MAIN_START, _MAIN_WINDOWS = _group_table(_MAIN_GROUPS)
W_PER_TILE = IN_TN // PREP_TN
assert _MAIN_WINDOWS % W_PER_TILE == 0
SWA_COLS = _SWA_WINDOWS * PREP_TN
MAIN_COLS = _MAIN_WINDOWS * PREP_TN
_SWA_OFF = {"q_b": 0, "k_b": _SWA_START[1] * PREP_TN,
            "v_b": _SWA_START[1] * PREP_TN + SWA_KV_WIDTH}
_MAIN_OFF = {
    "g_a": 0, "g_b": D_MODEL,
    "z_a": _MAIN_START[1] * PREP_TN,
    "z_b": _MAIN_START[2] * PREP_TN,
    "q_a": _MAIN_START[3] * PREP_TN,
    "k_a": _MAIN_START[3] * PREP_TN + FOX_WIDTH,
    "v_a": _MAIN_START[3] * PREP_TN + 2 * FOX_WIDTH,
}
_NT = (((1,), (1,)), ((), ()))
Q_A_SCALE = FOX_HEAD_DIM ** -0.5 * LOG2E

FOX_T = 512
FOX_CHUNK = 512
FOX_PAIR = 4
BIAS_TERMS = 3
ONES_ROWS = 16

SWA_NW = 8
SWA_QB = SWA_NW * WINDOW

EPI_TM = 256


def _cast_windows(w_refs, wb_ref):
    for k, w_ref in enumerate(w_refs):
        for r in range(PREP_TN // LANES):
            rows = slice(r * LANES, (r + 1) * LANES)
            wb_ref[k * PREP_TN + r * LANES:k * PREP_TN + (r + 1) * LANES, :] = (
                w_ref[rows, :].astype(BF16))


def _w_src_row(window, groups, starts):
    tile_row = jnp.int32(0)
    for (c0, _), start in zip(groups, starts):
        tile_row = jnp.where(window >= start,
                             c0 // SUBLANES + (window - start) * (PREP_TN // SUBLANES), tile_row)
    return tile_row * SUBLANES


def _prenorm_kernel(steps_per_seq, x_ref, g_ref, wf_ref, b_ref, *refs):
    w_refs = refs[:_SWA_WINDOWS]
    h_ref, kb_ref, proj_ref, wb_ref, carry_ref = refs[_SWA_WINDOWS:]
    i = pl.program_id(0)

    @pl.when(i == 0)
    def _():
        _cast_windows(w_refs, wb_ref)

    @pl.when(i % steps_per_seq == 0)
    def _():
        carry_ref[...] = jnp.zeros(carry_ref.shape, F32)

    x = x_ref[...]
    ms = jnp.mean(x * x, axis=-1, keepdims=True)
    h = ((x * lax.rsqrt(ms + NORM_EPS)) * g_ref[...]).astype(BF16)
    h_ref[...] = h
    ft = lax.dot_general(wf_ref[...], h, _NT, preferred_element_type=F32)
    proj_ref[...] = lax.dot_general(h, wb_ref[...], _NT,
                                    preferred_element_type=F32).astype(BF16)

    f = ft[0:FOX_HEADS, :] + b_ref[...]
    c = jnp.minimum(f, 0.0) - jnp.log1p(jnp.exp(-jnp.abs(f)))
    rows = c.shape[1]
    lane = lax.broadcasted_iota(jnp.int32, c.shape, 1)
    shift = 1
    while shift < rows:
        c = c + jnp.where(lane >= shift, pltpu.roll(c, shift, axis=1), 0.0)
        shift *= 2
    c = c + carry_ref[:, 0:1]
    carry_ref[...] = jnp.broadcast_to(c[:, rows - 1:rows], carry_ref.shape)
    rest = c * (-LOG2E)
    pieces = []
    for _ in range(BIAS_TERMS):
        piece = rest.astype(BF16).astype(F32)
        pieces.append(piece)
        rest = rest - piece
    pieces.append(jnp.zeros((LANES - BIAS_TERMS * FOX_HEADS, rows), F32))
    kb_ref[...] = jnp.concatenate(pieces, axis=0).T.astype(BF16)


def _prenorm(x2, g, wf_t, b_col, w_t, seq):
    t = x2.shape[0]
    once = pl.Buffered(1)

    def w_spec(k):
        return pl.BlockSpec((pl.Element(PREP_TN), pl.Element(D_MODEL)),
                            lambda i: (_w_src_row(k, _SWA_GROUPS, _SWA_START), 0),
                            pipeline_mode=once)

    return pl.pallas_call(
        functools.partial(_prenorm_kernel, seq // NORM_ROWS),
        grid=(t // NORM_ROWS,),
        in_specs=[
            pl.BlockSpec((NORM_ROWS, D_MODEL), lambda i: (i, 0)),
            pl.BlockSpec((1, D_MODEL), lambda i: (0, 0)),
            pl.BlockSpec((F_ROWS, D_MODEL), lambda i: (0, 0)),
            pl.BlockSpec((FOX_HEADS, 1), lambda i: (0, 0)),
            *[w_spec(k) for k in range(_SWA_WINDOWS)],
        ],
        out_specs=[
            pl.BlockSpec((NORM_ROWS, D_MODEL), lambda i: (i, 0)),
            pl.BlockSpec((NORM_ROWS, LANES), lambda i: (i, 0)),
            pl.BlockSpec((NORM_ROWS, SWA_COLS), lambda i: (i, 0)),
        ],
        out_shape=[
            jax.ShapeDtypeStruct((t, D_MODEL), BF16),
            jax.ShapeDtypeStruct((t, LANES), BF16),
            jax.ShapeDtypeStruct((t, SWA_COLS), BF16),
        ],
        scratch_shapes=[
            pltpu.VMEM((SWA_COLS, D_MODEL), BF16),
            pltpu.VMEM((FOX_HEADS, LANES), F32),
        ],
        compiler_params=pltpu.CompilerParams(
            dimension_semantics=("arbitrary",),
            vmem_limit_bytes=VMEM_LIMIT_BYTES),
        name="prenorm",
    )(x2, g, wf_t, b_col, *([w_t] * _SWA_WINDOWS))


def _inproj_kernel(h_ref, w_hbm, scale_ref, proj_ref, wf_ref, wb_ref, sem):
    j = pl.program_id(0)

    def window_copy(tile, k):
        row = pl.multiple_of(
            _w_src_row(tile * W_PER_TILE + k, _MAIN_GROUPS, _MAIN_START), SUBLANES)
        return pltpu.make_async_copy(
            w_hbm.at[pl.ds(row, PREP_TN), :],
            wf_ref.at[tile % 2, pl.ds(k * PREP_TN, PREP_TN), :],
            sem.at[tile % 2, k])

    @pl.when(pl.program_id(1) == 0)
    def _():
        @pl.when(j == 0)
        def _():
            for k in range(W_PER_TILE):
                window_copy(j, k).start()

        slot = j % 2
        for k in range(W_PER_TILE):
            window_copy(j, k).wait()
            for r in range(PREP_TN // LANES):
                rows = pl.ds(k * PREP_TN + r * LANES, LANES)
                wb_ref[rows, :] = wf_ref[slot, rows, :].astype(BF16)

        @pl.when(j + 1 < pl.num_programs(0))
        def _():
            for k in range(W_PER_TILE):
                window_copy(j + 1, k).start()

    acc = lax.dot_general(h_ref[...], wb_ref[...], _NT, preferred_element_type=F32)
    proj_ref[...] = (acc * scale_ref[...]).astype(BF16)


def _inproj(h, w_t, col_scale):
    t = h.shape[0]
    return pl.pallas_call(
        _inproj_kernel,
        grid=(MAIN_COLS // IN_TN, t // IN_TM),
        in_specs=[
            pl.BlockSpec((IN_TM, D_MODEL), lambda j, i: (i, 0)),
            pl.BlockSpec(memory_space=pl.ANY),
            pl.BlockSpec((1, IN_TN), lambda j, i: (0, j)),
        ],
        out_specs=pl.BlockSpec((IN_TM, IN_TN), lambda j, i: (i, j)),
        out_shape=jax.ShapeDtypeStruct((t, MAIN_COLS), BF16),
        scratch_shapes=[
            pltpu.VMEM((2, IN_TN, D_MODEL), F32),
            pltpu.VMEM((IN_TN, D_MODEL), BF16),
            pltpu.SemaphoreType.DMA((2, W_PER_TILE)),
        ],
        compiler_params=pltpu.CompilerParams(
            dimension_semantics=("arbitrary", "arbitrary"),
            vmem_limit_bytes=VMEM_LIMIT_BYTES),
        name="inproj",
    )(h, w_t, col_scale)


def _fox_kernel(q_ref, k_ref, v_ref, kb_ref, o_ref, vt_ref, qt_ref, acc_ref,
                sa_ref, ma_ref, sb_ref, mb_ref):
    s0_ref, s1_ref = (sa_ref, ma_ref), (sb_ref, mb_ref)
    hp = pl.program_id(1)
    seq = k_ref.shape[0]
    t = FOX_T
    d = FOX_HEAD_DIM

    def transposed(x):
        return x.astype(F32).T.astype(BF16)

    row = lax.broadcasted_iota(jnp.int32, (d, t), 0)
    for hh in range(FOX_PAIR):
        mine = (row < BIAS_TERMS * FOX_HEADS) & (row % FOX_HEADS == hp * FOX_PAIR + hh)
        qt_ref[hh, d:2 * d, :] = jnp.where(mine, 1.0, 0.0).astype(BF16)
        vt_ref[hh, d:, :] = jnp.ones((ONES_ROWS, seq), BF16)
        for r in range(seq // FOX_CHUNK):
            rows = slice(r * FOX_CHUNK, (r + 1) * FOX_CHUNK)
            vt_ref[hh, 0:d, rows] = transposed(v_ref[rows, hh * d:(hh + 1) * d])

    def scores(i, buf):
        s_ref, max_ref = buf
        rows = pl.ds(pl.multiple_of(i * t, t), t)
        for hh in range(FOX_PAIR):
            k_aug = jnp.concatenate([k_ref[rows, hh * d:(hh + 1) * d], kb_ref[rows, :]], axis=1)
            s = jnp.dot(k_aug, qt_ref[hh], preferred_element_type=F32)
            s_ref[hh] = s
            max_ref[hh] = jnp.max(s, axis=0, keepdims=True)

    def absorb(i, buf, carry, masked):
        s_ref, max_ref = buf
        rows = pl.ds(pl.multiple_of(i * t, t), t)
        new = []
        for hh in range(FOX_PAIR):
            m = carry[hh]
            s = s_ref[hh]
            if masked:
                key = lax.broadcasted_iota(jnp.int32, s.shape, 0)
                qry = lax.broadcasted_iota(jnp.int32, s.shape, 1)
                s = jnp.where(key <= qry, s, NEG_BIG)
                tile_max = jnp.max(s, axis=0, keepdims=True)
            else:
                tile_max = max_ref[hh]
            m_new = jnp.maximum(m, tile_max)
            alpha = jnp.exp2(m - m_new)
            p = jnp.exp2(s - m_new)
            acc_ref[hh] = alpha * acc_ref[hh] + jnp.dot(
                vt_ref[hh, :, rows], p.astype(BF16), preferred_element_type=F32)
            new.append(m_new)
        return tuple(new)

    def step(i, s_cur, s_next, carry):
        scores(i + 1, s_next)
        return absorb(i, s_cur, carry, False)

    def pair(j, carry):
        carry = step(2 * j, s0_ref, s1_ref, carry)
        return step(2 * j + 1, s1_ref, s0_ref, carry)

    n_q = seq // t

    def load_q(qi):
        q_rows = pl.ds(pl.multiple_of(qi * t, t), t)
        for hh in range(FOX_PAIR):
            qt_ref[hh, 0:d, :] = transposed(q_ref[q_rows, hh * d:(hh + 1) * d])

    def q_tile(qi, _):
        for hh in range(FOX_PAIR):
            acc_ref[hh] = jnp.zeros((d + ONES_ROWS, t), F32)

        def finish(s_last, carry):
            load_q(jnp.minimum(qi + 1, n_q - 1))
            absorb(qi, s_last, carry, True)
            q_rows = pl.ds(pl.multiple_of(qi * t, t), t)
            for hh in range(FOX_PAIR):
                o_ref[q_rows, hh * d:(hh + 1) * d] = (
                    acc_ref[hh, 0:d, :] / acc_ref[hh, d:d + 1, :]).T.astype(o_ref.dtype)

        def odd_tail(carry):
            finish(s1_ref, step(qi - 1, s0_ref, s1_ref, carry))

        def even_tail(carry):
            finish(s0_ref, carry)

        init = tuple(jnp.full((1, t), NEG_BIG, F32) for _ in range(FOX_PAIR))
        scores(0, s0_ref)
        carry = lax.fori_loop(0, qi // 2, pair, init)
        lax.cond(qi % 2 == 1, odd_tail, even_tail, carry)
        return 0

    load_q(0)
    lax.fori_loop(0, n_q, q_tile, 0)


def _fox_attention(proj, kb_tok, batch, seq):
    width = FOX_PAIR * FOX_HEAD_DIM
    qcol = _MAIN_OFF["q_a"] // width
    kcol = _MAIN_OFF["k_a"] // width
    vcol = _MAIN_OFF["v_a"] // width
    return pl.pallas_call(
        _fox_kernel,
        grid=(batch, FOX_HEADS // FOX_PAIR),
        in_specs=[
            pl.BlockSpec((seq, width), lambda b, h: (b, qcol + h)),
            pl.BlockSpec((seq, width), lambda b, h: (b, kcol + h)),
            pl.BlockSpec((seq, width), lambda b, h: (b, vcol + h)),
            pl.BlockSpec((seq, LANES), lambda b, h: (b, 0)),
        ],
        out_specs=pl.BlockSpec((seq, width), lambda b, h: (b, h)),
        out_shape=jax.ShapeDtypeStruct((batch * seq, FOX_WIDTH), BF16),
        scratch_shapes=[
            pltpu.VMEM((FOX_PAIR, FOX_HEAD_DIM + ONES_ROWS, seq), BF16),
            pltpu.VMEM((FOX_PAIR, 2 * FOX_HEAD_DIM, FOX_T), BF16),
            pltpu.VMEM((FOX_PAIR, FOX_HEAD_DIM + ONES_ROWS, FOX_T), F32),
            pltpu.VMEM((FOX_PAIR, FOX_T, FOX_T), F32),
            pltpu.VMEM((FOX_PAIR, 1, FOX_T), F32),
            pltpu.VMEM((FOX_PAIR, FOX_T, FOX_T), F32),
            pltpu.VMEM((FOX_PAIR, 1, FOX_T), F32),
        ],
        compiler_params=pltpu.CompilerParams(
            dimension_semantics=("arbitrary", "arbitrary"),
            vmem_limit_bytes=VMEM_LIMIT_BYTES),
        name="fox_attention",
    )(proj, proj, proj, kb_tok)


def _rope_t(xt, cos, sin):
    out = []
    for hd in range(xt.shape[0] // SWA_HEAD_DIM):
        x1 = xt[hd * SWA_HEAD_DIM: hd * SWA_HEAD_DIM + ROPE_HALF]
        x2 = xt[hd * SWA_HEAD_DIM + ROPE_HALF: (hd + 1) * SWA_HEAD_DIM]
        out.append(x1 * cos - x2 * sin)
        out.append(x2 * cos + x1 * sin)
    return jnp.concatenate(out, axis=0)


def _swa_kernel(sink_ref, q_ref, kc_ref, kp_ref, vc_ref, vp_ref, posc_ref, posp_ref,
                inv_ref, *rest):
    n_cast = (len(rest) - 3) // 2
    cast_in, o_ref = rest[:n_cast], rest[n_cast]
    cast_out, (band_ref, eye_ref) = rest[n_cast + 1:2 * n_cast + 1], rest[-2:]
    for src_ref, dst_ref in zip(cast_in, cast_out):
        dst_ref[...] = src_ref[...].astype(dst_ref.dtype)

    n = pl.program_id(1)
    w = WINDOW
    inv = inv_ref[...]

    def tables(pos_row):
        ang = inv * pos_row.astype(F32)
        return jnp.cos(ang), jnp.sin(ang)

    def rope_k(k_nat, cos, sin):
        kt = _rope_t(k_nat.astype(F32).T, cos, sin)
        return kt.T.astype(BF16)

    cos_p, sin_p = tables(posp_ref[...])
    k_prev = rope_k(kp_ref[...], cos_p, sin_p)
    v_prev = vp_ref[...]

    @pl.when((pl.program_id(0) == 0) & (n == 0))
    def _():
        key = lax.broadcasted_iota(jnp.int32, band_ref.shape, 0)
        qry = lax.broadcasted_iota(jnp.int32, band_ref.shape, 1)
        band_ref[...] = jnp.where((key <= qry + w) & (key > qry), 0.0, NEG_BIG).astype(BF16)
        src = lax.broadcasted_iota(jnp.int32, eye_ref.shape, 0)
        dst = lax.broadcasted_iota(jnp.int32, eye_ref.shape, 1) % w
        eye_ref[...] = jnp.where(src == dst, 1.0, 0.0).astype(BF16)

    no_prev = jnp.where(n > 0, 0.0, NEG_BIG)
    q_scale = SWA_HEAD_DIM ** -0.5 * LOG2E
    zeros_half = jnp.zeros((SWA_HEAD_DIM, SWA_GROUP * w), F32)
    ones_rows = jnp.ones((2 * SUBLANES, 2 * w), BF16)

    sinks = [jnp.concatenate(
        [jnp.full((1, w), sink_ref[hd] * LOG2E, F32)
         for hd in range(g * SWA_GROUP, (g + 1) * SWA_GROUP)], axis=1)
        for g in range(SWA_KV_HEADS)]

    def logits(wi, k_prev):
        tok = slice(wi * w, (wi + 1) * w)
        cos, sin = tables(posc_ref[:, tok])
        k_cur = rope_k(kc_ref[tok, :], cos, sin)
        k_all = jnp.concatenate([k_prev, k_cur], axis=0)
        k_aug = jnp.concatenate([k_all, band_ref[...]], axis=1)
        qf = q_ref[tok, :].astype(F32)
        qt = jnp.concatenate(
            [qf[:, c * LANES:(c + 1) * LANES].T for c in range(SWA_WIDTH // LANES)], axis=0)
        qt = _rope_t(qt, cos * q_scale, sin * q_scale)
        out = []
        for g in range(SWA_KV_HEADS):
            heads = range(g * SWA_GROUP, (g + 1) * SWA_GROUP)
            q_g = jnp.concatenate(
                [qt[hd * SWA_HEAD_DIM:(hd + 1) * SWA_HEAD_DIM] for hd in heads], axis=1)
            parts = [zeros_half] * SWA_KV_HEADS
            parts[g] = q_g
            q_z = jnp.concatenate(parts, axis=0).astype(BF16)
            q_aug = jnp.concatenate([q_z, eye_ref[...]], axis=0)
            s = jnp.dot(k_aug, q_aug, preferred_element_type=F32)
            if wi == 0:
                s = jnp.concatenate([s[:w] + no_prev, s[w:]], axis=0)
            out.append(s)
        return out, k_cur

    def finish(wi, scores, v_prev):
        tok = slice(wi * w, (wi + 1) * w)
        v_cur = vc_ref[tok, :]
        v_all_t = jnp.concatenate([v_prev, v_cur], axis=0).astype(F32).T.astype(BF16)
        for g in range(SWA_KV_HEADS):
            s, sink = scores[g], sinks[g]
            m = jnp.maximum(jnp.max(s, axis=0, keepdims=True), sink)
            e = jnp.exp2(s - m).astype(BF16)
            v_aug = jnp.concatenate(
                [v_all_t[g * SWA_HEAD_DIM:(g + 1) * SWA_HEAD_DIM], ones_rows], axis=0)
            pv = jnp.dot(v_aug, e, preferred_element_type=F32)
            denom = pv[SWA_HEAD_DIM:SWA_HEAD_DIM + 1] + jnp.exp2(sink - m)
            out_t = pv[:SWA_HEAD_DIM] / denom
            for jj in range(SWA_GROUP // 2):
                pair = jnp.concatenate(
                    [out_t[:, (2 * jj) * w:(2 * jj + 1) * w],
                     out_t[:, (2 * jj + 1) * w:(2 * jj + 2) * w]], axis=0)
                col = (g * (SWA_GROUP // 2) + jj) * LANES
                o_ref[tok, col:col + LANES] = pair.T.astype(o_ref.dtype)
        return v_cur

    pending, k_prev = logits(0, k_prev)
    for wi in range(SWA_NW):
        if wi + 1 < SWA_NW:
            upcoming, k_prev = logits(wi + 1, k_prev)
        v_prev = finish(wi, pending, v_prev)
        if wi + 1 < SWA_NW:
            pending = upcoming


def _swa_attention(sinks, proj, pos_row, inv_tab, batch, seq, weights_f32):
    nb = seq // SWA_QB
    per = SWA_QB // WINDOW
    qcol = _SWA_OFF["q_b"] // SWA_WIDTH
    kcol = _SWA_OFF["k_b"] // LANES
    vcol = _SWA_OFF["v_b"] // LANES
    steps = batch * nb

    def cur(b, n):
        return b * nb + n

    def prev(b, n):
        return jnp.maximum((b * nb + n) * per - 1, 0)

    def chunk_spec(wgt):
        rows, cols = wgt.shape
        assert rows % (steps * 2 * SUBLANES) == 0
        return pl.BlockSpec((rows // steps, cols), lambda b, n: (cur(b, n), 0))

    cast_specs = [chunk_spec(wgt) for wgt in weights_f32]
    return pl.pallas_call(
        _swa_kernel,
        grid=(batch, nb),
        in_specs=[
            pl.BlockSpec(memory_space=pltpu.SMEM),
            pl.BlockSpec((SWA_QB, SWA_WIDTH), lambda b, n: (cur(b, n), qcol)),
            pl.BlockSpec((SWA_QB, LANES), lambda b, n: (cur(b, n), kcol)),
            pl.BlockSpec((WINDOW, LANES), lambda b, n: (prev(b, n), kcol)),
            pl.BlockSpec((SWA_QB, LANES), lambda b, n: (cur(b, n), vcol)),
            pl.BlockSpec((WINDOW, LANES), lambda b, n: (prev(b, n), vcol)),
            pl.BlockSpec((1, SWA_QB), lambda b, n: (0, cur(b, n))),
            pl.BlockSpec((1, WINDOW), lambda b, n: (0, prev(b, n))),
            pl.BlockSpec((ROPE_HALF, LANES), lambda b, n: (0, 0)),
            *cast_specs,
        ],
        out_specs=[pl.BlockSpec((SWA_QB, SWA_WIDTH), lambda b, n: (cur(b, n), 0)), *cast_specs],
        out_shape=[jax.ShapeDtypeStruct((batch * seq, SWA_WIDTH), BF16),
                   *[jax.ShapeDtypeStruct(wgt.shape, BF16) for wgt in weights_f32]],
        scratch_shapes=[
            pltpu.VMEM((2 * WINDOW, WINDOW), BF16),
            pltpu.VMEM((WINDOW, SWA_GROUP * WINDOW), BF16),
        ],
        compiler_params=pltpu.CompilerParams(
            dimension_semantics=("arbitrary", "arbitrary")),
        name="swa_attention",
    )(sinks, proj, proj, proj, proj, proj, pos_row, pos_row, inv_tab, *weights_f32)


def _silu(z):
    return z * jax.nn.sigmoid(z)


def _epilogue_kernel(ya_ref, yb_ref, gz_ref, x_ref, p_ref,
                     wof_hbm, wos_hbm, wout_hbm, gpost_ref, wple_hbm, wgate_hbm, o_ref,
                     wof_ref, wos_ref, wout_ref, wple_ref, wgate_ref, sem):
    def gz(name, width):
        return gz_ref[:, _MAIN_OFF[name]:_MAIN_OFF[name] + width].astype(F32)

    fetch = [pltpu.make_async_copy(src, dst, sem.at[k]) for k, (src, dst) in enumerate((
        (wof_hbm, wof_ref), (wos_hbm, wos_ref), (wout_hbm, wout_ref),
        (wple_hbm, wple_ref), (wgate_hbm, wgate_ref)))]

    def body(first):
        if first:
            fetch[0].start()
            fetch[1].start()
        ua = (ya_ref[...].astype(F32) * _silu(gz("z_a", FOX_WIDTH))).astype(BF16)
        ub = (yb_ref[...].astype(F32) * _silu(gz("z_b", SWA_WIDTH))).astype(BF16)
        if first:
            fetch[0].wait()
            fetch[2].start()
        oa = jnp.dot(ua, wof_ref[...], preferred_element_type=F32)
        if first:
            fetch[1].wait()
            fetch[3].start()
            fetch[4].start()
        ob = jnp.dot(ub, wos_ref[...], preferred_element_type=F32)
        merged = (jax.nn.sigmoid(gz("g_a", D_MODEL)) * oa
                  + jax.nn.sigmoid(gz("g_b", D_MODEL)) * ob)
        if first:
            fetch[2].wait()
        out = jnp.dot(merged.astype(BF16), wout_ref[...], preferred_element_type=F32)
        ms = jnp.mean(out * out, axis=-1, keepdims=True)
        x1 = x_ref[...] + (out * lax.rsqrt(ms + NORM_EPS)) * gpost_ref[...]
        if first:
            fetch[3].wait()
            fetch[4].wait()
        e = jnp.dot(p_ref[...].astype(BF16), wple_ref[...], preferred_element_type=F32)
        gate = jax.nn.sigmoid(jnp.dot(x1.astype(BF16), wgate_ref[...],
                                      preferred_element_type=F32))
        o_ref[...] = x1 + gate * e

    @pl.when(pl.program_id(0) == 0)
    def _():
        body(True)

    @pl.when(pl.program_id(0) > 0)
    def _():
        body(False)


def _epilogue(ya, yb, proj, x2, p2, wof, wos, wout, gpost, wple, wgate):
    t = x2.shape[0]
    gz_cols = _MAIN_OFF["z_b"] + SWA_WIDTH
    assert _MAIN_OFF["g_a"] == 0 and gz_cols == 2 * D_MODEL + FOX_WIDTH + SWA_WIDTH
    once = pl.Buffered(1)
    in_hbm = pl.BlockSpec(memory_space=pl.ANY)

    def const(shape):
        return pl.BlockSpec(shape, lambda i: (0, 0), pipeline_mode=once)

    return pl.pallas_call(
        _epilogue_kernel,
        grid=(t // EPI_TM,),
        in_specs=[
            pl.BlockSpec((EPI_TM, FOX_WIDTH), lambda i: (i, 0)),
            pl.BlockSpec((EPI_TM, SWA_WIDTH), lambda i: (i, 0)),
            pl.BlockSpec((EPI_TM, gz_cols), lambda i: (i, 0)),
            pl.BlockSpec((EPI_TM, D_MODEL), lambda i: (i, 0)),
            pl.BlockSpec((EPI_TM, PLE_DIM), lambda i: (i, 0)),
            in_hbm, in_hbm, in_hbm,
            const((1, D_MODEL)),
            in_hbm, in_hbm,
        ],
        out_specs=pl.BlockSpec((EPI_TM, D_MODEL), lambda i: (i, 0)),
        out_shape=jax.ShapeDtypeStruct((t, D_MODEL), F32),
        scratch_shapes=[
            pltpu.VMEM((FOX_WIDTH, D_MODEL), BF16),
            pltpu.VMEM((SWA_WIDTH, D_MODEL), BF16),
            pltpu.VMEM((D_MODEL, D_MODEL), BF16),
            pltpu.VMEM((PLE_DIM, D_MODEL), BF16),
            pltpu.VMEM((D_MODEL, D_MODEL), BF16),
            pltpu.SemaphoreType.DMA((5,)),
        ],
        compiler_params=pltpu.CompilerParams(
            dimension_semantics=("arbitrary",),
            vmem_limit_bytes=VMEM_LIMIT_BYTES),
        name="epilogue",
    )(ya, yb, proj, x2, p2, wof, wos, wout, gpost, wple, wgate)


def _layer(x2, p2, pos_row, batch, seq, pre_g, w_in, b_forget, sinks, w_o_fox, w_o_swa,
           w_out, post_g, w_ple, w_ple_gate):
    w_t = w_in.T
    f0 = _REF_OFF["f_a"]
    wf_t = jnp.pad(w_t[f0:f0 + FOX_HEADS], ((0, F_ROWS - FOX_HEADS), (0, 0))).astype(BF16)
    col = jnp.arange(MAIN_COLS)
    in_q_a = (col >= _MAIN_OFF["q_a"]) & (col < _MAIN_OFF["q_a"] + FOX_WIDTH)
    col_scale = jnp.where(in_q_a, Q_A_SCALE, 1.0).astype(F32).reshape(1, MAIN_COLS)

    h, kb_tok, proj_swa = _prenorm(x2, pre_g.reshape(1, D_MODEL), wf_t,
                                   b_forget.reshape(FOX_HEADS, 1).astype(F32), w_t, seq)
    proj = _inproj(h, w_t, col_scale)
    ya = _fox_attention(proj, kb_tok, batch, seq)

    inv = ROPE_THETA ** (-jnp.arange(ROPE_HALF, dtype=F32) / ROPE_HALF)
    inv_tab = jnp.broadcast_to(inv[:, None], (ROPE_HALF, LANES))
    yb, wof, wos, wout, wple, wgate = _swa_attention(
        sinks.astype(F32), proj_swa, pos_row, inv_tab, batch, seq,
        (w_o_fox, w_o_swa, w_out, w_ple, w_ple_gate))

    return _epilogue(ya, yb, proj, x2, p2, wof, wos, wout,
                     post_g.reshape(1, D_MODEL), wple, wgate)


def kernel(x, p, positions, pre_norm_g, w_in, b_forget, sinks, w_o_fox, w_o_swa, w_out,
           post_norm_g, w_ple, w_ple_gate):
    batch, seq, _ = x.shape
    depth = p.shape[0]
    x2 = x.reshape(batch * seq, D_MODEL)
    pos_row = positions.reshape(1, batch * seq)
    for i in range(depth):
        x2 = _layer(x2, p[i].reshape(batch * seq, PLE_DIM), pos_row, batch, seq,
                    pre_norm_g[i], w_in[i], b_forget[i], sinks[i], w_o_fox[i], w_o_swa[i],
                    w_out[i], post_norm_g[i], w_ple[i], w_ple_gate[i])
    return x2.reshape(batch, seq, D_MODEL)
```

```python
import functools
import math

import jax
import jax.numpy as jnp
from jax import lax
from jax.experimental import pallas as pl
from jax.experimental.pallas import tpu as pltpu

F32 = jnp.float32
BF16 = jnp.bfloat16

D_MODEL = 2048
FOX_HEADS = 8
FOX_HEAD_DIM = 128
FOX_WIDTH = FOX_HEADS * FOX_HEAD_DIM
SWA_Q_HEADS = 16
SWA_KV_HEADS = 2
SWA_HEAD_DIM = 64
SWA_WIDTH = SWA_Q_HEADS * SWA_HEAD_DIM
SWA_KV_WIDTH = SWA_KV_HEADS * SWA_HEAD_DIM
SWA_GROUP = SWA_Q_HEADS // SWA_KV_HEADS
ROPE_HALF = SWA_HEAD_DIM // 2
WINDOW = 128
ROPE_THETA = 10000.0
PLE_DIM = 256
NORM_EPS = 1e-6
LANES = 128
V7X_VMEM_BYTES = 64 * 1024 * 1024
VMEM_LIMIT_BYTES = V7X_VMEM_BYTES * 7 // 8
LOG2E = math.log2(math.e)
NEG_BIG = -1e30

_REF_SPLITS = (FOX_WIDTH, FOX_WIDTH, FOX_WIDTH, FOX_WIDTH, FOX_HEADS, SWA_WIDTH,
               SWA_KV_WIDTH, SWA_KV_WIDTH, SWA_WIDTH, D_MODEL, D_MODEL)
_REF_NAMES = ("q_a", "k_a", "v_a", "z_a", "f_a", "q_b", "k_b", "v_b", "z_b", "g_a", "g_b")
_REF_OFF = {}
_o = 0
for _n, _s in zip(_REF_NAMES, _REF_SPLITS):
    _REF_OFF[_n] = _o
    _o += _s

IN_TM = 1024
IN_TN = 1536
NORM_ROWS = 1024
F_ROWS = 16

PREP_TN = 256
_SWA_GROUPS = (
    (_REF_OFF["q_b"], SWA_WIDTH),
    (_REF_OFF["k_b"], 2 * SWA_KV_WIDTH),
)
_MAIN_GROUPS = (
    (_REF_OFF["g_a"], 2 * D_MODEL),
    (_REF_OFF["z_a"], FOX_WIDTH),
    (_REF_OFF["z_b"], SWA_WIDTH),
    (_REF_OFF["q_a"], 3 * FOX_WIDTH),
)
SUBLANES = 8


def _group_table(groups):
    assert all(c0 % SUBLANES == 0 for c0, _ in groups)
    counts = [-(-w // PREP_TN) for _, w in groups]
    return tuple(sum(counts[:i]) for i in range(len(groups))), sum(counts)


_SWA_START, _SWA_WINDOWS = _group_table(_SWA_GROUPS)
_MAIN_START, _MAIN_WINDOWS = _group_table(_MAIN_GROUPS)
W_PER_TILE = IN_TN // PREP_TN
assert _MAIN_WINDOWS % W_PER_TILE == 0
SWA_COLS = _SWA_WINDOWS * PREP_TN
MAIN_COLS = _MAIN_WINDOWS * PREP_TN
_SWA_OFF = {"q_b": 0, "k_b": _SWA_START[1] * PREP_TN,
            "v_b": _SWA_START[1] * PREP_TN + SWA_KV_WIDTH}
_MAIN_OFF = {
    "g_a": 0, "g_b": D_MODEL,
    "z_a": _MAIN_START[1] * PREP_TN,
    "z_b": _MAIN_START[2] * PREP_TN,
    "q_a": _MAIN_START[3] * PREP_TN,
    "k_a": _MAIN_START[3] * PREP_TN + FOX_WIDTH,
    "v_a": _MAIN_START[3] * PREP_TN + 2 * FOX_WIDTH,
}
_NT = (((1,), (1,)), ((), ()))
Q_A_SCALE = FOX_HEAD_DIM ** -0.5 * LOG2E

FOX_T = 512
FOX_CHUNK = 512
FOX_PAIR = 4
BIAS_TERMS = 3
ONES_ROWS = 16

SWA_NW = 8
SWA_QB = SWA_NW * WINDOW

EPI_TM = 256


def _w_src_row(window, groups, starts):
    tile_row = jnp.int32(0)
    for (c0, _), start in zip(groups, starts):
        tile_row = jnp.where(window >= start,
                             c0 // SUBLANES + (window - start) * (PREP_TN // SUBLANES), tile_row)
    return tile_row * SUBLANES


def _prenorm_kernel(steps_per_seq, x_ref, g_ref, wf_ref, b_ref, *refs):
    w_hbm, h_ref, kb_ref, proj_ref, wf32_ref, wb_ref, carry_ref, sem = refs
    i = pl.program_id(0)

    def window_copy(k):
        row = [c0 + (k - start) * PREP_TN
               for (c0, _), start in zip(_SWA_GROUPS, _SWA_START) if k >= start][-1]
        return pltpu.make_async_copy(
            w_hbm.at[pl.ds(row, PREP_TN), :],
            wf32_ref.at[pl.ds(k * PREP_TN, PREP_TN), :], sem.at[k])

    @pl.when(i % steps_per_seq == 0)
    def _():
        carry_ref[...] = jnp.zeros(carry_ref.shape, F32)

    def body(first):
        if first:
            for k in range(_SWA_WINDOWS):
                window_copy(k).start()
        x = x_ref[...]
        ms = jnp.mean(x * x, axis=-1, keepdims=True)
        h = ((x * lax.rsqrt(ms + NORM_EPS)) * g_ref[...]).astype(BF16)
        h_ref[...] = h
        ft = lax.dot_general(wf_ref[...], h, _NT, preferred_element_type=F32)
        if first:
            for k in range(_SWA_WINDOWS):
                window_copy(k).wait()
                for r in range(PREP_TN // LANES):
                    rows = pl.ds(k * PREP_TN + r * LANES, LANES)
                    wb_ref[rows, :] = wf32_ref[rows, :].astype(BF16)
        proj_ref[...] = lax.dot_general(h, wb_ref[...], _NT,
                                        preferred_element_type=F32).astype(BF16)

        f = ft[0:FOX_HEADS, :] + b_ref[...]
        c = jnp.minimum(f, 0.0) - jnp.log1p(jnp.exp(-jnp.abs(f)))
        rows = c.shape[1]
        lane = lax.broadcasted_iota(jnp.int32, c.shape, 1)
        shift = 1
        while shift < rows:
            c = c + jnp.where(lane >= shift, pltpu.roll(c, shift, axis=1), 0.0)
            shift *= 2
        c = c + carry_ref[:, 0:1]
        carry_ref[...] = jnp.broadcast_to(c[:, rows - 1:rows], carry_ref.shape)
        rest = c * (-LOG2E)
        pieces = []
        for _ in range(BIAS_TERMS):
            piece = rest.astype(BF16).astype(F32)
            pieces.append(piece)
            rest = rest - piece
        pieces.append(jnp.zeros((LANES - BIAS_TERMS * FOX_HEADS, rows), F32))
        kb_ref[...] = jnp.concatenate(pieces, axis=0).T.astype(BF16)

    @pl.when(i == 0)
    def _():
        body(True)

    @pl.when(i > 0)
    def _():
        body(False)


def _prenorm(x2, g, wf_t, b_col, w_t, seq):
    t = x2.shape[0]
    return pl.pallas_call(
        functools.partial(_prenorm_kernel, seq // NORM_ROWS),
        grid=(t // NORM_ROWS,),
        in_specs=[
            pl.BlockSpec((NORM_ROWS, D_MODEL), lambda i: (i, 0)),
            pl.BlockSpec((1, D_MODEL), lambda i: (0, 0)),
            pl.BlockSpec((F_ROWS, D_MODEL), lambda i: (0, 0)),
            pl.BlockSpec((FOX_HEADS, 1), lambda i: (0, 0)),
            pl.BlockSpec(memory_space=pl.ANY),
        ],
        out_specs=[
            pl.BlockSpec((NORM_ROWS, D_MODEL), lambda i: (i, 0)),
            pl.BlockSpec((NORM_ROWS, LANES), lambda i: (i, 0)),
            pl.BlockSpec((NORM_ROWS, SWA_COLS), lambda i: (i, 0)),
        ],
        out_shape=[
            jax.ShapeDtypeStruct((t, D_MODEL), BF16),
            jax.ShapeDtypeStruct((t, LANES), BF16),
            jax.ShapeDtypeStruct((t, SWA_COLS), BF16),
        ],
        scratch_shapes=[
            pltpu.VMEM((SWA_COLS, D_MODEL), F32),
            pltpu.VMEM((SWA_COLS, D_MODEL), BF16),
            pltpu.VMEM((FOX_HEADS, LANES), F32),
            pltpu.SemaphoreType.DMA((_SWA_WINDOWS,)),
        ],
        compiler_params=pltpu.CompilerParams(
            dimension_semantics=("arbitrary",),
            vmem_limit_bytes=VMEM_LIMIT_BYTES),
        name="prenorm",
    )(x2, g, wf_t, b_col, w_t)


def _inproj_kernel(h_ref, w_hbm, scale_ref, proj_ref, wf_ref, wb_ref, sem):
    j = pl.program_id(0)

    def window_copy(tile, k):
        row = pl.multiple_of(
            _w_src_row(tile * W_PER_TILE + k, _MAIN_GROUPS, _MAIN_START), SUBLANES)
        return pltpu.make_async_copy(
            w_hbm.at[pl.ds(row, PREP_TN), :],
            wf_ref.at[tile % 2, pl.ds(k * PREP_TN, PREP_TN), :],
            sem.at[tile % 2, k])

    @pl.when(pl.program_id(1) == 0)
    def _():
        @pl.when(j == 0)
        def _():
            for k in range(W_PER_TILE):
                window_copy(j, k).start()

        slot = j % 2
        for k in range(W_PER_TILE):
            window_copy(j, k).wait()
            for r in range(PREP_TN // LANES):
                rows = pl.ds(k * PREP_TN + r * LANES, LANES)
                wb_ref[rows, :] = wf_ref[slot, rows, :].astype(BF16)

        @pl.when(j + 1 < pl.num_programs(0))
        def _():
            for k in range(W_PER_TILE):
                window_copy(j + 1, k).start()

    acc = lax.dot_general(h_ref[...], wb_ref[...], _NT, preferred_element_type=F32)
    proj_ref[...] = (acc * scale_ref[...]).astype(BF16)


def _inproj(h, w_t, col_scale):
    t = h.shape[0]
    return pl.pallas_call(
        _inproj_kernel,
        grid=(MAIN_COLS // IN_TN, t // IN_TM),
        in_specs=[
            pl.BlockSpec((IN_TM, D_MODEL), lambda j, i: (i, 0)),
            pl.BlockSpec(memory_space=pl.ANY),
            pl.BlockSpec((1, IN_TN), lambda j, i: (0, j)),
        ],
        out_specs=pl.BlockSpec((IN_TM, IN_TN), lambda j, i: (i, j)),
        out_shape=jax.ShapeDtypeStruct((t, MAIN_COLS), BF16),
        scratch_shapes=[
            pltpu.VMEM((2, IN_TN, D_MODEL), F32),
            pltpu.VMEM((IN_TN, D_MODEL), BF16),
            pltpu.SemaphoreType.DMA((2, W_PER_TILE)),
        ],
        compiler_params=pltpu.CompilerParams(
            dimension_semantics=("arbitrary", "arbitrary"),
            vmem_limit_bytes=VMEM_LIMIT_BYTES),
        name="inproj",
    )(h, w_t, col_scale)


def _fox_kernel(q_ref, k_ref, v_ref, kb_ref, o_ref, vt_ref, qt_ref, acc_ref,
                sa_ref, ma_ref, sb_ref, mb_ref):
    s0_ref, s1_ref = (sa_ref, ma_ref), (sb_ref, mb_ref)
    hp = pl.program_id(1)
    seq = k_ref.shape[0]
    t = FOX_T
    d = FOX_HEAD_DIM

    def transposed(x):
        return x.astype(F32).T.astype(BF16)

    row = lax.broadcasted_iota(jnp.int32, (d, t), 0)
    for hh in range(FOX_PAIR):
        mine = (row < BIAS_TERMS * FOX_HEADS) & (row % FOX_HEADS == hp * FOX_PAIR + hh)
        qt_ref[hh, d:2 * d, :] = jnp.where(mine, 1.0, 0.0).astype(BF16)
        vt_ref[hh, d:, :] = jnp.ones((ONES_ROWS, seq), BF16)
        for r in range(seq // FOX_CHUNK):
            rows = slice(r * FOX_CHUNK, (r + 1) * FOX_CHUNK)
            vt_ref[hh, 0:d, rows] = transposed(v_ref[rows, hh * d:(hh + 1) * d])

    def scores(i, buf):
        s_ref, max_ref = buf
        rows = pl.ds(pl.multiple_of(i * t, t), t)
        for hh in range(FOX_PAIR):
            k_aug = jnp.concatenate([k_ref[rows, hh * d:(hh + 1) * d], kb_ref[rows, :]], axis=1)
            s = jnp.dot(k_aug, qt_ref[hh], preferred_element_type=F32)
            s_ref[hh] = s
            max_ref[hh] = jnp.max(s, axis=0, keepdims=True)

    def absorb(i, buf, carry, masked):
        s_ref, max_ref = buf
        rows = pl.ds(pl.multiple_of(i * t, t), t)
        new = []
        for hh in range(FOX_PAIR):
            m = carry[hh]
            s = s_ref[hh]
            if masked:
                key = lax.broadcasted_iota(jnp.int32, s.shape, 0)
                qry = lax.broadcasted_iota(jnp.int32, s.shape, 1)
                s = jnp.where(key <= qry, s, NEG_BIG)
                tile_max = jnp.max(s, axis=0, keepdims=True)
            else:
                tile_max = max_ref[hh]
            m_new = jnp.maximum(m, tile_max)
            alpha = jnp.exp2(m - m_new)
            p = jnp.exp2(s - m_new)
            acc_ref[hh] = alpha * acc_ref[hh] + jnp.dot(
                vt_ref[hh, :, rows], p.astype(BF16), preferred_element_type=F32)
            new.append(m_new)
        return tuple(new)

    def step(i, s_cur, s_next, carry):
        scores(i + 1, s_next)
        return absorb(i, s_cur, carry, False)

    def pair(j, carry):
        carry = step(2 * j, s0_ref, s1_ref, carry)
        return step(2 * j + 1, s1_ref, s0_ref, carry)

    n_q = seq // t

    def load_q(qi):
        q_rows = pl.ds(pl.multiple_of(qi * t, t), t)
        for hh in range(FOX_PAIR):
            qt_ref[hh, 0:d, :] = transposed(q_ref[q_rows, hh * d:(hh + 1) * d])

    def q_tile(qi, _):
        for hh in range(FOX_PAIR):
            acc_ref[hh] = jnp.zeros((d + ONES_ROWS, t), F32)

        def finish(s_last, carry):
            load_q(jnp.minimum(qi + 1, n_q - 1))
            absorb(qi, s_last, carry, True)
            q_rows = pl.ds(pl.multiple_of(qi * t, t), t)
            for hh in range(FOX_PAIR):
                o_ref[q_rows, hh * d:(hh + 1) * d] = (
                    acc_ref[hh, 0:d, :] / acc_ref[hh, d:d + 1, :]).T.astype(o_ref.dtype)

        def odd_tail(carry):
            finish(s1_ref, step(qi - 1, s0_ref, s1_ref, carry))

        def even_tail(carry):
            finish(s0_ref, carry)

        init = tuple(jnp.full((1, t), NEG_BIG, F32) for _ in range(FOX_PAIR))
        scores(0, s0_ref)
        carry = lax.fori_loop(0, qi // 2, pair, init)
        lax.cond(qi % 2 == 1, odd_tail, even_tail, carry)
        return 0

    load_q(0)
    lax.fori_loop(0, n_q, q_tile, 0)


def _fox_attention(proj, kb_tok, batch, seq):
    width = FOX_PAIR * FOX_HEAD_DIM
    qcol = _MAIN_OFF["q_a"] // width
    kcol = _MAIN_OFF["k_a"] // width
    vcol = _MAIN_OFF["v_a"] // width
    return pl.pallas_call(
        _fox_kernel,
        grid=(batch, FOX_HEADS // FOX_PAIR),
        in_specs=[
            pl.BlockSpec((seq, width), lambda b, h: (b, qcol + h)),
            pl.BlockSpec((seq, width), lambda b, h: (b, kcol + h)),
            pl.BlockSpec((seq, width), lambda b, h: (b, vcol + h)),
            pl.BlockSpec((seq, LANES), lambda b, h: (b, 0)),
        ],
        out_specs=pl.BlockSpec((seq, width), lambda b, h: (b, h)),
        out_shape=jax.ShapeDtypeStruct((batch * seq, FOX_WIDTH), BF16),
        scratch_shapes=[
            pltpu.VMEM((FOX_PAIR, FOX_HEAD_DIM + ONES_ROWS, seq), BF16),
            pltpu.VMEM((FOX_PAIR, 2 * FOX_HEAD_DIM, FOX_T), BF16),
            pltpu.VMEM((FOX_PAIR, FOX_HEAD_DIM + ONES_ROWS, FOX_T), F32),
            pltpu.VMEM((FOX_PAIR, FOX_T, FOX_T), F32),
            pltpu.VMEM((FOX_PAIR, 1, FOX_T), F32),
            pltpu.VMEM((FOX_PAIR, FOX_T, FOX_T), F32),
            pltpu.VMEM((FOX_PAIR, 1, FOX_T), F32),
        ],
        compiler_params=pltpu.CompilerParams(
            dimension_semantics=("arbitrary", "arbitrary"),
            vmem_limit_bytes=VMEM_LIMIT_BYTES),
        name="fox_attention",
    )(proj, proj, proj, kb_tok)


def _rope_t(xt, cos, sin):
    out = []
    for hd in range(xt.shape[0] // SWA_HEAD_DIM):
        x1 = xt[hd * SWA_HEAD_DIM: hd * SWA_HEAD_DIM + ROPE_HALF]
        x2 = xt[hd * SWA_HEAD_DIM + ROPE_HALF: (hd + 1) * SWA_HEAD_DIM]
        out.append(x1 * cos - x2 * sin)
        out.append(x2 * cos + x1 * sin)
    return jnp.concatenate(out, axis=0)


def _swa_kernel(sink_ref, q_ref, kc_ref, kp_ref, vc_ref, vp_ref, posc_ref, posp_ref,
                inv_ref, *rest):
    n_cast = (len(rest) - 3) // 2
    cast_in, o_ref = rest[:n_cast], rest[n_cast]
    cast_out, (band_ref, eye_ref) = rest[n_cast + 1:2 * n_cast + 1], rest[-2:]
    for src_ref, dst_ref in zip(cast_in, cast_out):
        dst_ref[...] = src_ref[...].astype(dst_ref.dtype)

    n = pl.program_id(1)
    w = WINDOW
    inv = inv_ref[...]

    def tables(pos_row):
        ang = inv * pos_row.astype(F32)
        return jnp.cos(ang), jnp.sin(ang)

    def rope_k(k_nat, cos, sin):
        kt = _rope_t(k_nat.astype(F32).T, cos, sin)
        return kt.T.astype(BF16)

    cos_p, sin_p = tables(posp_ref[...])
    k_prev = rope_k(kp_ref[...], cos_p, sin_p)
    v_prev = vp_ref[...]

    @pl.when((pl.program_id(0) == 0) & (n == 0))
    def _():
        key = lax.broadcasted_iota(jnp.int32, band_ref.shape, 0)
        qry = lax.broadcasted_iota(jnp.int32, band_ref.shape, 1)
        band_ref[...] = jnp.where((key <= qry + w) & (key > qry), 0.0, NEG_BIG).astype(BF16)
        src = lax.broadcasted_iota(jnp.int32, eye_ref.shape, 0)
        dst = lax.broadcasted_iota(jnp.int32, eye_ref.shape, 1) % w
        eye_ref[...] = jnp.where(src == dst, 1.0, 0.0).astype(BF16)

    no_prev = jnp.where(n > 0, 0.0, NEG_BIG)
    q_scale = SWA_HEAD_DIM ** -0.5 * LOG2E
    zeros_half = jnp.zeros((SWA_HEAD_DIM, SWA_GROUP * w), F32)
    ones_rows = jnp.ones((2 * SUBLANES, 2 * w), BF16)

    sinks = [jnp.concatenate(
        [jnp.full((1, w), sink_ref[hd] * LOG2E, F32)
         for hd in range(g * SWA_GROUP, (g + 1) * SWA_GROUP)], axis=1)
        for g in range(SWA_KV_HEADS)]

    def logits(wi, k_prev):
        tok = slice(wi * w, (wi + 1) * w)
        cos, sin = tables(posc_ref[:, tok])
        k_cur = rope_k(kc_ref[tok, :], cos, sin)
        k_all = jnp.concatenate([k_prev, k_cur], axis=0)
        k_aug = jnp.concatenate([k_all, band_ref[...]], axis=1)
        qf = q_ref[tok, :].astype(F32)
        qt = jnp.concatenate(
            [qf[:, c * LANES:(c + 1) * LANES].T for c in range(SWA_WIDTH // LANES)], axis=0)
        qt = _rope_t(qt, cos * q_scale, sin * q_scale)
        out = []
        for g in range(SWA_KV_HEADS):
            heads = range(g * SWA_GROUP, (g + 1) * SWA_GROUP)
            q_g = jnp.concatenate(
                [qt[hd * SWA_HEAD_DIM:(hd + 1) * SWA_HEAD_DIM] for hd in heads], axis=1)
            parts = [zeros_half] * SWA_KV_HEADS
            parts[g] = q_g
            q_z = jnp.concatenate(parts, axis=0).astype(BF16)
            q_aug = jnp.concatenate([q_z, eye_ref[...]], axis=0)
            s = jnp.dot(k_aug, q_aug, preferred_element_type=F32)
            if wi == 0:
                s = jnp.concatenate([s[:w] + no_prev, s[w:]], axis=0)
            out.append(s)
        return out, k_cur

    def finish(wi, scores, v_prev):
        tok = slice(wi * w, (wi + 1) * w)
        v_cur = vc_ref[tok, :]
        v_all_t = jnp.concatenate([v_prev, v_cur], axis=0).astype(F32).T.astype(BF16)
        for g in range(SWA_KV_HEADS):
            s, sink = scores[g], sinks[g]
            m = jnp.maximum(jnp.max(s, axis=0, keepdims=True), sink)
            e = jnp.exp2(s - m).astype(BF16)
            v_aug = jnp.concatenate(
                [v_all_t[g * SWA_HEAD_DIM:(g + 1) * SWA_HEAD_DIM], ones_rows], axis=0)
            pv = jnp.dot(v_aug, e, preferred_element_type=F32)
            denom = pv[SWA_HEAD_DIM:SWA_HEAD_DIM + 1] + jnp.exp2(sink - m)
            out_t = pv[:SWA_HEAD_DIM] / denom
            for jj in range(SWA_GROUP // 2):
                pair = jnp.concatenate(
                    [out_t[:, (2 * jj) * w:(2 * jj + 1) * w],
                     out_t[:, (2 * jj + 1) * w:(2 * jj + 2) * w]], axis=0)
                col = (g * (SWA_GROUP // 2) + jj) * LANES
                o_ref[tok, col:col + LANES] = pair.T.astype(o_ref.dtype)
        return v_cur

    pending, k_prev = logits(0, k_prev)
    for wi in range(SWA_NW):
        if wi + 1 < SWA_NW:
            upcoming, k_prev = logits(wi + 1, k_prev)
        v_prev = finish(wi, pending, v_prev)
        if wi + 1 < SWA_NW:
            pending = upcoming


def _swa_attention(sinks, proj, pos_row, inv_tab, batch, seq, weights_f32):
    nb = seq // SWA_QB
    per = SWA_QB // WINDOW
    qcol = _SWA_OFF["q_b"] // SWA_WIDTH
    kcol = _SWA_OFF["k_b"] // LANES
    vcol = _SWA_OFF["v_b"] // LANES
    steps = batch * nb

    def cur(b, n):
        return b * nb + n

    def prev(b, n):
        return jnp.maximum((b * nb + n) * per - 1, 0)

    def chunk_spec(wgt):
        rows, cols = wgt.shape
        assert rows % (steps * 2 * SUBLANES) == 0
        return pl.BlockSpec((rows // steps, cols), lambda b, n: (cur(b, n), 0))

    cast_specs = [chunk_spec(wgt) for wgt in weights_f32]
    return pl.pallas_call(
        _swa_kernel,
        grid=(batch, nb),
        in_specs=[
            pl.BlockSpec(memory_space=pltpu.SMEM),
            pl.BlockSpec((SWA_QB, SWA_WIDTH), lambda b, n: (cur(b, n), qcol)),
            pl.BlockSpec((SWA_QB, LANES), lambda b, n: (cur(b, n), kcol)),
            pl.BlockSpec((WINDOW, LANES), lambda b, n: (prev(b, n), kcol)),
            pl.BlockSpec((SWA_QB, LANES), lambda b, n: (cur(b, n), vcol)),
            pl.BlockSpec((WINDOW, LANES), lambda b, n: (prev(b, n), vcol)),
            pl.BlockSpec((1, SWA_QB), lambda b, n: (0, cur(b, n))),
            pl.BlockSpec((1, WINDOW), lambda b, n: (0, prev(b, n))),
            pl.BlockSpec((ROPE_HALF, LANES), lambda b, n: (0, 0)),
            *cast_specs,
        ],
        out_specs=[pl.BlockSpec((SWA_QB, SWA_WIDTH), lambda b, n: (cur(b, n), 0)), *cast_specs],
        out_shape=[jax.ShapeDtypeStruct((batch * seq, SWA_WIDTH), BF16),
                   *[jax.ShapeDtypeStruct(wgt.shape, BF16) for wgt in weights_f32]],
        scratch_shapes=[
            pltpu.VMEM((2 * WINDOW, WINDOW), BF16),
            pltpu.VMEM((WINDOW, SWA_GROUP * WINDOW), BF16),
        ],
        compiler_params=pltpu.CompilerParams(
            dimension_semantics=("arbitrary", "arbitrary")),
        name="swa_attention",
    )(sinks, proj, proj, proj, proj, proj, pos_row, pos_row, inv_tab, *weights_f32)


def _silu(z):
    return z * jax.nn.sigmoid(z)


def _epilogue_kernel(ya_ref, yb_ref, gz_ref, x_ref, p_ref,
                     wof_hbm, wos_hbm, wout_hbm, gpost_ref, wple_hbm, wgate_hbm, o_ref,
                     wof_ref, wos_ref, wout_ref, wple_ref, wgate_ref, sem):
    def gz(name, width):
        return gz_ref[:, _MAIN_OFF[name]:_MAIN_OFF[name] + width].astype(F32)

    fetch = [pltpu.make_async_copy(src, dst, sem.at[k]) for k, (src, dst) in enumerate((
        (wof_hbm, wof_ref), (wos_hbm, wos_ref), (wout_hbm, wout_ref),
        (wple_hbm, wple_ref), (wgate_hbm, wgate_ref)))]

    def body(first):
        if first:
            fetch[0].start()
            fetch[1].start()
        ua = (ya_ref[...].astype(F32) * _silu(gz("z_a", FOX_WIDTH))).astype(BF16)
        ub = (yb_ref[...].astype(F32) * _silu(gz("z_b", SWA_WIDTH))).astype(BF16)
        if first:
            fetch[0].wait()
            fetch[2].start()
        oa = jnp.dot(ua, wof_ref[...], preferred_element_type=F32)
        if first:
            fetch[1].wait()
            fetch[3].start()
            fetch[4].start()
        ob = jnp.dot(ub, wos_ref[...], preferred_element_type=F32)
        merged = (jax.nn.sigmoid(gz("g_a", D_MODEL)) * oa
                  + jax.nn.sigmoid(gz("g_b", D_MODEL)) * ob)
        if first:
            fetch[2].wait()
        out = jnp.dot(merged.astype(BF16), wout_ref[...], preferred_element_type=F32)
        ms = jnp.mean(out * out, axis=-1, keepdims=True)
        x1 = x_ref[...] + (out * lax.rsqrt(ms + NORM_EPS)) * gpost_ref[...]
        if first:
            fetch[3].wait()
            fetch[4].wait()
        e = jnp.dot(p_ref[...].astype(BF16), wple_ref[...], preferred_element_type=F32)
        gate = jax.nn.sigmoid(jnp.dot(x1.astype(BF16), wgate_ref[...],
                                      preferred_element_type=F32))
        o_ref[...] = x1 + gate * e

    @pl.when(pl.program_id(0) == 0)
    def _():
        body(True)

    @pl.when(pl.program_id(0) > 0)
    def _():
        body(False)


def _epilogue(ya, yb, proj, x2, p2, wof, wos, wout, gpost, wple, wgate):
    t = x2.shape[0]
    gz_cols = _MAIN_OFF["z_b"] + SWA_WIDTH
    assert _MAIN_OFF["g_a"] == 0 and gz_cols == 2 * D_MODEL + FOX_WIDTH + SWA_WIDTH
    once = pl.Buffered(1)
    in_hbm = pl.BlockSpec(memory_space=pl.ANY)

    def const(shape):
        return pl.BlockSpec(shape, lambda i: (0, 0), pipeline_mode=once)

    return pl.pallas_call(
        _epilogue_kernel,
        grid=(t // EPI_TM,),
        in_specs=[
            pl.BlockSpec((EPI_TM, FOX_WIDTH), lambda i: (i, 0)),
            pl.BlockSpec((EPI_TM, SWA_WIDTH), lambda i: (i, 0)),
            pl.BlockSpec((EPI_TM, gz_cols), lambda i: (i, 0)),
            pl.BlockSpec((EPI_TM, D_MODEL), lambda i: (i, 0)),
            pl.BlockSpec((EPI_TM, PLE_DIM), lambda i: (i, 0)),
            in_hbm, in_hbm, in_hbm,
            const((1, D_MODEL)),
            in_hbm, in_hbm,
        ],
        out_specs=pl.BlockSpec((EPI_TM, D_MODEL), lambda i: (i, 0)),
        out_shape=jax.ShapeDtypeStruct((t, D_MODEL), F32),
        scratch_shapes=[
            pltpu.VMEM((FOX_WIDTH, D_MODEL), BF16),
            pltpu.VMEM((SWA_WIDTH, D_MODEL), BF16),
            pltpu.VMEM((D_MODEL, D_MODEL), BF16),
            pltpu.VMEM((PLE_DIM, D_MODEL), BF16),
            pltpu.VMEM((D_MODEL, D_MODEL), BF16),
            pltpu.SemaphoreType.DMA((5,)),
        ],
        compiler_params=pltpu.CompilerParams(
            dimension_semantics=("arbitrary",),
            vmem_limit_bytes=VMEM_LIMIT_BYTES),
        name="epilogue",
    )(ya, yb, proj, x2, p2, wof, wos, wout, gpost, wple, wgate)


def _layer(x2, p2, pos_row, batch, seq, pre_g, w_in, b_forget, sinks, w_o_fox, w_o_swa,
           w_out, post_g, w_ple, w_ple_gate):
    w_t = w_in.T
    f0 = _REF_OFF["f_a"]
    wf_t = jnp.pad(w_t[f0:f0 + FOX_HEADS], ((0, F_ROWS - FOX_HEADS), (0, 0))).astype(BF16)
    col = jnp.arange(MAIN_COLS)
    in_q_a = (col >= _MAIN_OFF["q_a"]) & (col < _MAIN_OFF["q_a"] + FOX_WIDTH)
    col_scale = jnp.where(in_q_a, Q_A_SCALE, 1.0).astype(F32).reshape(1, MAIN_COLS)

    h, kb_tok, proj_swa = _prenorm(x2, pre_g.reshape(1, D_MODEL), wf_t,
                                   b_forget.reshape(FOX_HEADS, 1).astype(F32), w_t, seq)
    proj = _inproj(h, w_t, col_scale)
    ya = _fox_attention(proj, kb_tok, batch, seq)

    inv = ROPE_THETA ** (-jnp.arange(ROPE_HALF, dtype=F32) / ROPE_HALF)
    inv_tab = jnp.broadcast_to(inv[:, None], (ROPE_HALF, LANES))
    yb, wof, wos, wout, wple, wgate = _swa_attention(
        sinks.astype(F32), proj_swa, pos_row, inv_tab, batch, seq,
        (w_o_fox, w_o_swa, w_out, w_ple, w_ple_gate))

    return _epilogue(ya, yb, proj, x2, p2, wof, wos, wout,
                     post_g.reshape(1, D_MODEL), wple, wgate)


def kernel(x, p, positions, pre_norm_g, w_in, b_forget, sinks, w_o_fox, w_o_swa, w_out,
           post_norm_g, w_ple, w_ple_gate):
    batch, seq, _ = x.shape
    depth = p.shape[0]
    x2 = x.reshape(batch * seq, D_MODEL)
    pos_row = positions.reshape(1, batch * seq)
    for i in range(depth):
        x2 = _layer(x2, p[i].reshape(batch * seq, PLE_DIM), pos_row, batch, seq,
                    pre_norm_g[i], w_in[i], b_forget[i], sinks[i], w_o_fox[i], w_o_swa[i],
                    w_out[i], post_norm_g[i], w_ple[i], w_ple_gate[i])
    return x2.reshape(batch, seq, D_MODEL)
```

```python
import functools
import math

import jax
import jax.numpy as jnp
from jax import lax
from jax.experimental import pallas as pl
from jax.experimental.pallas import tpu as pltpu

F32 = jnp.float32
BF16 = jnp.bfloat16

D_MODEL = 2048
FOX_HEADS = 8
FOX_HEAD_DIM = 128
FOX_WIDTH = FOX_HEADS * FOX_HEAD_DIM
SWA_Q_HEADS = 16
SWA_KV_HEADS = 2
SWA_HEAD_DIM = 64
SWA_WIDTH = SWA_Q_HEADS * SWA_HEAD_DIM
SWA_KV_WIDTH = SWA_KV_HEADS * SWA_HEAD_DIM
SWA_GROUP = SWA_Q_HEADS // SWA_KV_HEADS
ROPE_HALF = SWA_HEAD_DIM // 2
WINDOW = 128
ROPE_THETA = 10000.0
PLE_DIM = 256
NORM_EPS = 1e-6
LANES = 128
V7X_VMEM_BYTES = 64 * 1024 * 1024
VMEM_LIMIT_BYTES = V7X_VMEM_BYTES * 7 // 8
LOG2E = math.log2(math.e)
NEG_BIG = -1e30

_REF_SPLITS = (FOX_WIDTH, FOX_WIDTH, FOX_WIDTH, FOX_WIDTH, FOX_HEADS, SWA_WIDTH,
               SWA_KV_WIDTH, SWA_KV_WIDTH, SWA_WIDTH, D_MODEL, D_MODEL)
_REF_NAMES = ("q_a", "k_a", "v_a", "z_a", "f_a", "q_b", "k_b", "v_b", "z_b", "g_a", "g_b")
_REF_OFF = {}
_o = 0
for _n, _s in zip(_REF_NAMES, _REF_SPLITS):
    _REF_OFF[_n] = _o
    _o += _s

IN_TM = 1024
IN_TN = 2304
NORM_ROWS = 1024
F_ROWS = 16

PREP_TN = 256
_SWA_GROUPS = (
    (_REF_OFF["q_b"], SWA_WIDTH),
    (_REF_OFF["k_b"], 2 * SWA_KV_WIDTH),
)
_MAIN_GROUPS = (
    (_REF_OFF["g_a"], 2 * D_MODEL),
    (_REF_OFF["z_a"], FOX_WIDTH),
    (_REF_OFF["z_b"], SWA_WIDTH),
    (_REF_OFF["q_a"], 3 * FOX_WIDTH),
)
SUBLANES = 8


def _group_table(groups):
    assert all(c0 % SUBLANES == 0 for c0, _ in groups)
    counts = [-(-w // PREP_TN) for _, w in groups]
    return tuple(sum(counts[:i]) for i in range(len(groups))), sum(counts)


_SWA_START, _SWA_WINDOWS = _group_table(_SWA_GROUPS)
_MAIN_START, _MAIN_WINDOWS = _group_table(_MAIN_GROUPS)
W_PER_TILE = IN_TN // PREP_TN
assert _MAIN_WINDOWS % W_PER_TILE == 0
SWA_COLS = _SWA_WINDOWS * PREP_TN
MAIN_COLS = _MAIN_WINDOWS * PREP_TN
_SWA_OFF = {"q_b": 0, "k_b": _SWA_START[1] * PREP_TN,
            "v_b": _SWA_START[1] * PREP_TN + SWA_KV_WIDTH}
_MAIN_OFF = {
    "g_a": 0, "g_b": D_MODEL,
    "z_a": _MAIN_START[1] * PREP_TN,
    "z_b": _MAIN_START[2] * PREP_TN,
    "q_a": _MAIN_START[3] * PREP_TN,
    "k_a": _MAIN_START[3] * PREP_TN + FOX_WIDTH,
    "v_a": _MAIN_START[3] * PREP_TN + 2 * FOX_WIDTH,
}
_NT = (((1,), (1,)), ((), ()))
Q_A_SCALE = FOX_HEAD_DIM ** -0.5 * LOG2E

FOX_T = 512
FOX_CHUNK = 512
FOX_PAIR = 4
BIAS_TERMS = 3
ONES_ROWS = 16

SWA_NW = 8
SWA_QB = SWA_NW * WINDOW

EPI_TM = 256


def _cast_windows(w_refs, wb_ref):
    for k, w_ref in enumerate(w_refs):
        for r in range(PREP_TN // LANES):
            rows = slice(r * LANES, (r + 1) * LANES)
            wb_ref[k * PREP_TN + r * LANES:k * PREP_TN + (r + 1) * LANES, :] = (
                w_ref[rows, :].astype(BF16))


def _w_src_row(window, groups, starts):
    tile_row = jnp.int32(0)
    for (c0, _), start in zip(groups, starts):
        tile_row = jnp.where(window >= start,
                             c0 // SUBLANES + (window - start) * (PREP_TN // SUBLANES), tile_row)
    return tile_row * SUBLANES


def _prenorm_kernel(steps_per_seq, x_ref, g_ref, wf_ref, b_ref, *refs):
    w_refs = refs[:_SWA_WINDOWS]
    h_ref, kb_ref, proj_ref, wb_ref, carry_ref = refs[_SWA_WINDOWS:]
    i = pl.program_id(0)

    @pl.when(i == 0)
    def _():
        _cast_windows(w_refs, wb_ref)

    @pl.when(i % steps_per_seq == 0)
    def _():
        carry_ref[...] = jnp.zeros(carry_ref.shape, F32)

    x = x_ref[...]
    ms = jnp.mean(x * x, axis=-1, keepdims=True)
    h = ((x * lax.rsqrt(ms + NORM_EPS)) * g_ref[...]).astype(BF16)
    h_ref[...] = h
    ft = lax.dot_general(wf_ref[...], h, _NT, preferred_element_type=F32)
    proj_ref[...] = lax.dot_general(h, wb_ref[...], _NT,
                                    preferred_element_type=F32).astype(BF16)

    f = ft[0:FOX_HEADS, :] + b_ref[...]
    c = jnp.minimum(f, 0.0) - jnp.log1p(jnp.exp(-jnp.abs(f)))
    rows = c.shape[1]
    lane = lax.broadcasted_iota(jnp.int32, c.shape, 1)
    shift = 1
    while shift < rows:
        c = c + jnp.where(lane >= shift, pltpu.roll(c, shift, axis=1), 0.0)
        shift *= 2
    c = c + carry_ref[:, 0:1]
    carry_ref[...] = jnp.broadcast_to(c[:, rows - 1:rows], carry_ref.shape)
    rest = c * (-LOG2E)
    pieces = []
    for _ in range(BIAS_TERMS):
        piece = rest.astype(BF16).astype(F32)
        pieces.append(piece)
        rest = rest - piece
    pieces.append(jnp.zeros((LANES - BIAS_TERMS * FOX_HEADS, rows), F32))
    kb_ref[...] = jnp.concatenate(pieces, axis=0).T.astype(BF16)


def _prenorm(x2, g, wf_t, b_col, w_t, seq):
    t = x2.shape[0]
    once = pl.Buffered(1)

    def w_spec(k):
        return pl.BlockSpec((pl.Element(PREP_TN), pl.Element(D_MODEL)),
                            lambda i: (_w_src_row(k, _SWA_GROUPS, _SWA_START), 0),
                            pipeline_mode=once)

    return pl.pallas_call(
        functools.partial(_prenorm_kernel, seq // NORM_ROWS),
        grid=(t // NORM_ROWS,),
        in_specs=[
            pl.BlockSpec((NORM_ROWS, D_MODEL), lambda i: (i, 0)),
            pl.BlockSpec((1, D_MODEL), lambda i: (0, 0)),
            pl.BlockSpec((F_ROWS, D_MODEL), lambda i: (0, 0)),
            pl.BlockSpec((FOX_HEADS, 1), lambda i: (0, 0)),
            *[w_spec(k) for k in range(_SWA_WINDOWS)],
        ],
        out_specs=[
            pl.BlockSpec((NORM_ROWS, D_MODEL), lambda i: (i, 0)),
            pl.BlockSpec((NORM_ROWS, LANES), lambda i: (i, 0)),
            pl.BlockSpec((NORM_ROWS, SWA_COLS), lambda i: (i, 0)),
        ],
        out_shape=[
            jax.ShapeDtypeStruct((t, D_MODEL), BF16),
            jax.ShapeDtypeStruct((t, LANES), BF16),
            jax.ShapeDtypeStruct((t, SWA_COLS), BF16),
        ],
        scratch_shapes=[
            pltpu.VMEM((SWA_COLS, D_MODEL), BF16),
            pltpu.VMEM((FOX_HEADS, LANES), F32),
        ],
        compiler_params=pltpu.CompilerParams(
            dimension_semantics=("arbitrary",),
            vmem_limit_bytes=VMEM_LIMIT_BYTES),
        name="prenorm",
    )(x2, g, wf_t, b_col, *([w_t] * _SWA_WINDOWS))


def _inproj_kernel(h_ref, w_hbm, scale_ref, proj_ref, wf_ref, wb_ref, sem):
    j = pl.program_id(0)

    def window_copy(tile, k):
        row = pl.multiple_of(
            _w_src_row(tile * W_PER_TILE + k, _MAIN_GROUPS, _MAIN_START), SUBLANES)
        return pltpu.make_async_copy(
            w_hbm.at[pl.ds(row, PREP_TN), :],
            wf_ref.at[pl.ds(k * PREP_TN, PREP_TN), :],
            sem.at[k])

    @pl.when(pl.program_id(1) == 0)
    def _():
        @pl.when(j == 0)
        def _():
            for k in range(W_PER_TILE):
                window_copy(j, k).start()

        for k in range(W_PER_TILE):
            window_copy(j, k).wait()
            for r in range(PREP_TN // LANES):
                rows = pl.ds(k * PREP_TN + r * LANES, LANES)
                wb_ref[rows, :] = wf_ref[rows, :].astype(BF16)

        @pl.when(j + 1 < pl.num_programs(0))
        def _():
            for k in range(W_PER_TILE):
                window_copy(j + 1, k).start()

    acc = lax.dot_general(h_ref[...], wb_ref[...], _NT, preferred_element_type=F32)
    proj_ref[...] = (acc * scale_ref[...]).astype(BF16)


def _inproj(h, w_t, col_scale):
    t = h.shape[0]
    return pl.pallas_call(
        _inproj_kernel,
        grid=(MAIN_COLS // IN_TN, t // IN_TM),
        in_specs=[
            pl.BlockSpec((IN_TM, D_MODEL), lambda j, i: (i, 0)),
            pl.BlockSpec(memory_space=pl.ANY),
            pl.BlockSpec((1, IN_TN), lambda j, i: (0, j)),
        ],
        out_specs=pl.BlockSpec((IN_TM, IN_TN), lambda j, i: (i, j)),
        out_shape=jax.ShapeDtypeStruct((t, MAIN_COLS), BF16),
        scratch_shapes=[
            pltpu.VMEM((IN_TN, D_MODEL), F32),
            pltpu.VMEM((IN_TN, D_MODEL), BF16),
            pltpu.SemaphoreType.DMA((W_PER_TILE,)),
        ],
        compiler_params=pltpu.CompilerParams(
            dimension_semantics=("arbitrary", "arbitrary"),
            vmem_limit_bytes=VMEM_LIMIT_BYTES),
        name="inproj",
    )(h, w_t, col_scale)


def _fox_kernel(q_ref, k_ref, v_ref, kb_ref, o_ref, vt_ref, qt_ref, acc_ref,
                sa_ref, ma_ref, sb_ref, mb_ref):
    s0_ref, s1_ref = (sa_ref, ma_ref), (sb_ref, mb_ref)
    hp = pl.program_id(1)
    seq = k_ref.shape[0]
    t = FOX_T
    d = FOX_HEAD_DIM

    def transposed(x):
        return x.astype(F32).T.astype(BF16)

    row = lax.broadcasted_iota(jnp.int32, (d, t), 0)
    for hh in range(FOX_PAIR):
        mine = (row < BIAS_TERMS * FOX_HEADS) & (row % FOX_HEADS == hp * FOX_PAIR + hh)
        qt_ref[hh, d:2 * d, :] = jnp.where(mine, 1.0, 0.0).astype(BF16)
        vt_ref[hh, d:, :] = jnp.ones((ONES_ROWS, seq), BF16)
        for r in range(seq // FOX_CHUNK):
            rows = slice(r * FOX_CHUNK, (r + 1) * FOX_CHUNK)
            vt_ref[hh, 0:d, rows] = transposed(v_ref[rows, hh * d:(hh + 1) * d])

    def scores(i, buf):
        s_ref, max_ref = buf
        rows = pl.ds(pl.multiple_of(i * t, t), t)
        for hh in range(FOX_PAIR):
            k_aug = jnp.concatenate([k_ref[rows, hh * d:(hh + 1) * d], kb_ref[rows, :]], axis=1)
            s = jnp.dot(k_aug, qt_ref[hh], preferred_element_type=F32)
            s_ref[hh] = s
            max_ref[hh] = jnp.max(s, axis=0, keepdims=True)

    def absorb(i, buf, carry, masked):
        s_ref, max_ref = buf
        rows = pl.ds(pl.multiple_of(i * t, t), t)
        new = []
        for hh in range(FOX_PAIR):
            m = carry[hh]
            s = s_ref[hh]
            if masked:
                key = lax.broadcasted_iota(jnp.int32, s.shape, 0)
                qry = lax.broadcasted_iota(jnp.int32, s.shape, 1)
                s = jnp.where(key <= qry, s, NEG_BIG)
                tile_max = jnp.max(s, axis=0, keepdims=True)
            else:
                tile_max = max_ref[hh]
            m_new = jnp.maximum(m, tile_max)
            alpha = jnp.exp2(m - m_new)
            p = jnp.exp2(s - m_new)
            acc_ref[hh] = alpha * acc_ref[hh] + jnp.dot(
                vt_ref[hh, :, rows], p.astype(BF16), preferred_element_type=F32)
            new.append(m_new)
        return tuple(new)

    def step(i, s_cur, s_next, carry):
        scores(i + 1, s_next)
        return absorb(i, s_cur, carry, False)

    def pair(j, carry):
        carry = step(2 * j, s0_ref, s1_ref, carry)
        return step(2 * j + 1, s1_ref, s0_ref, carry)

    n_q = seq // t

    def load_q(qi):
        q_rows = pl.ds(pl.multiple_of(qi * t, t), t)
        for hh in range(FOX_PAIR):
            qt_ref[hh, 0:d, :] = transposed(q_ref[q_rows, hh * d:(hh + 1) * d])

    def q_tile(qi, _):
        for hh in range(FOX_PAIR):
            acc_ref[hh] = jnp.zeros((d + ONES_ROWS, t), F32)

        def finish(s_last, carry):
            load_q(jnp.minimum(qi + 1, n_q - 1))
            absorb(qi, s_last, carry, True)
            q_rows = pl.ds(pl.multiple_of(qi * t, t), t)
            for hh in range(FOX_PAIR):
                o_ref[q_rows, hh * d:(hh + 1) * d] = (
                    acc_ref[hh, 0:d, :] / acc_ref[hh, d:d + 1, :]).T.astype(o_ref.dtype)

        def odd_tail(carry):
            finish(s1_ref, step(qi - 1, s0_ref, s1_ref, carry))

        def even_tail(carry):
            finish(s0_ref, carry)

        init = tuple(jnp.full((1, t), NEG_BIG, F32) for _ in range(FOX_PAIR))
        scores(0, s0_ref)
        carry = lax.fori_loop(0, qi // 2, pair, init)
        lax.cond(qi % 2 == 1, odd_tail, even_tail, carry)
        return 0

    load_q(0)
    lax.fori_loop(0, n_q, q_tile, 0)


def _fox_attention(proj, kb_tok, batch, seq):
    width = FOX_PAIR * FOX_HEAD_DIM
    qcol = _MAIN_OFF["q_a"] // width
    kcol = _MAIN_OFF["k_a"] // width
    vcol = _MAIN_OFF["v_a"] // width
    return pl.pallas_call(
        _fox_kernel,
        grid=(batch, FOX_HEADS // FOX_PAIR),
        in_specs=[
            pl.BlockSpec((seq, width), lambda b, h: (b, qcol + h)),
            pl.BlockSpec((seq, width), lambda b, h: (b, kcol + h)),
            pl.BlockSpec((seq, width), lambda b, h: (b, vcol + h)),
            pl.BlockSpec((seq, LANES), lambda b, h: (b, 0)),
        ],
        out_specs=pl.BlockSpec((seq, width), lambda b, h: (b, h)),
        out_shape=jax.ShapeDtypeStruct((batch * seq, FOX_WIDTH), BF16),
        scratch_shapes=[
            pltpu.VMEM((FOX_PAIR, FOX_HEAD_DIM + ONES_ROWS, seq), BF16),
            pltpu.VMEM((FOX_PAIR, 2 * FOX_HEAD_DIM, FOX_T), BF16),
            pltpu.VMEM((FOX_PAIR, FOX_HEAD_DIM + ONES_ROWS, FOX_T), F32),
            pltpu.VMEM((FOX_PAIR, FOX_T, FOX_T), F32),
            pltpu.VMEM((FOX_PAIR, 1, FOX_T), F32),
            pltpu.VMEM((FOX_PAIR, FOX_T, FOX_T), F32),
            pltpu.VMEM((FOX_PAIR, 1, FOX_T), F32),
        ],
        compiler_params=pltpu.CompilerParams(
            dimension_semantics=("arbitrary", "arbitrary"),
            vmem_limit_bytes=VMEM_LIMIT_BYTES),
        name="fox_attention",
    )(proj, proj, proj, kb_tok)


def _rope_t(xt, cos, sin):
    out = []
    for hd in range(xt.shape[0] // SWA_HEAD_DIM):
        x1 = xt[hd * SWA_HEAD_DIM: hd * SWA_HEAD_DIM + ROPE_HALF]
        x2 = xt[hd * SWA_HEAD_DIM + ROPE_HALF: (hd + 1) * SWA_HEAD_DIM]
        out.append(x1 * cos - x2 * sin)
        out.append(x2 * cos + x1 * sin)
    return jnp.concatenate(out, axis=0)


def _swa_kernel(sink_ref, q_ref, kc_ref, kp_ref, vc_ref, vp_ref, posc_ref, posp_ref,
                inv_ref, *rest):
    n_cast = (len(rest) - 3) // 2
    cast_in, o_ref = rest[:n_cast], rest[n_cast]
    cast_out, (band_ref, eye_ref) = rest[n_cast + 1:2 * n_cast + 1], rest[-2:]
    for src_ref, dst_ref in zip(cast_in, cast_out):
        dst_ref[...] = src_ref[...].astype(dst_ref.dtype)

    n = pl.program_id(1)
    w = WINDOW
    inv = inv_ref[...]

    def tables(pos_row):
        ang = inv * pos_row.astype(F32)
        return jnp.cos(ang), jnp.sin(ang)

    def rope_k(k_nat, cos, sin):
        kt = _rope_t(k_nat.astype(F32).T, cos, sin)
        return kt.T.astype(BF16)

    cos_p, sin_p = tables(posp_ref[...])
    k_prev = rope_k(kp_ref[...], cos_p, sin_p)
    v_prev = vp_ref[...]

    @pl.when((pl.program_id(0) == 0) & (n == 0))
    def _():
        key = lax.broadcasted_iota(jnp.int32, band_ref.shape, 0)
        qry = lax.broadcasted_iota(jnp.int32, band_ref.shape, 1)
        band_ref[...] = jnp.where((key <= qry + w) & (key > qry), 0.0, NEG_BIG).astype(BF16)
        src = lax.broadcasted_iota(jnp.int32, eye_ref.shape, 0)
        dst = lax.broadcasted_iota(jnp.int32, eye_ref.shape, 1) % w
        eye_ref[...] = jnp.where(src == dst, 1.0, 0.0).astype(BF16)

    no_prev = jnp.where(n > 0, 0.0, NEG_BIG)
    q_scale = SWA_HEAD_DIM ** -0.5 * LOG2E
    zeros_half = jnp.zeros((SWA_HEAD_DIM, SWA_GROUP * w), F32)
    ones_rows = jnp.ones((2 * SUBLANES, 2 * w), BF16)

    sinks = [jnp.concatenate(
        [jnp.full((1, w), sink_ref[hd] * LOG2E, F32)
         for hd in range(g * SWA_GROUP, (g + 1) * SWA_GROUP)], axis=1)
        for g in range(SWA_KV_HEADS)]

    def logits(wi, k_prev):
        tok = slice(wi * w, (wi + 1) * w)
        cos, sin = tables(posc_ref[:, tok])
        k_cur = rope_k(kc_ref[tok, :], cos, sin)
        k_all = jnp.concatenate([k_prev, k_cur], axis=0)
        k_aug = jnp.concatenate([k_all, band_ref[...]], axis=1)
        qf = q_ref[tok, :].astype(F32)
        qt = jnp.concatenate(
            [qf[:, c * LANES:(c + 1) * LANES].T for c in range(SWA_WIDTH // LANES)], axis=0)
        qt = _rope_t(qt, cos * q_scale, sin * q_scale)
        out = []
        for g in range(SWA_KV_HEADS):
            heads = range(g * SWA_GROUP, (g + 1) * SWA_GROUP)
            q_g = jnp.concatenate(
                [qt[hd * SWA_HEAD_DIM:(hd + 1) * SWA_HEAD_DIM] for hd in heads], axis=1)
            parts = [zeros_half] * SWA_KV_HEADS
            parts[g] = q_g
            q_z = jnp.concatenate(parts, axis=0).astype(BF16)
            q_aug = jnp.concatenate([q_z, eye_ref[...]], axis=0)
            s = jnp.dot(k_aug, q_aug, preferred_element_type=F32)
            if wi == 0:
                s = jnp.concatenate([s[:w] + no_prev, s[w:]], axis=0)
            out.append(s)
        return out, k_cur

    def finish(wi, scores, v_prev):
        tok = slice(wi * w, (wi + 1) * w)
        v_cur = vc_ref[tok, :]
        v_all_t = jnp.concatenate([v_prev, v_cur], axis=0).astype(F32).T.astype(BF16)
        for g in range(SWA_KV_HEADS):
            s, sink = scores[g], sinks[g]
            m = jnp.maximum(jnp.max(s, axis=0, keepdims=True), sink)
            e = jnp.exp2(s - m).astype(BF16)
            v_aug = jnp.concatenate(
                [v_all_t[g * SWA_HEAD_DIM:(g + 1) * SWA_HEAD_DIM], ones_rows], axis=0)
            pv = jnp.dot(v_aug, e, preferred_element_type=F32)
            denom = pv[SWA_HEAD_DIM:SWA_HEAD_DIM + 1] + jnp.exp2(sink - m)
            out_t = pv[:SWA_HEAD_DIM] / denom
            for jj in range(SWA_GROUP // 2):
                pair = jnp.concatenate(
                    [out_t[:, (2 * jj) * w:(2 * jj + 1) * w],
                     out_t[:, (2 * jj + 1) * w:(2 * jj + 2) * w]], axis=0)
                col = (g * (SWA_GROUP // 2) + jj) * LANES
                o_ref[tok, col:col + LANES] = pair.T.astype(o_ref.dtype)
        return v_cur

    pending, k_prev = logits(0, k_prev)
    for wi in range(SWA_NW):
        if wi + 1 < SWA_NW:
            upcoming, k_prev = logits(wi + 1, k_prev)
        v_prev = finish(wi, pending, v_prev)
        if wi + 1 < SWA_NW:
            pending = upcoming


def _swa_attention(sinks, proj, pos_row, inv_tab, batch, seq, weights_f32):
    nb = seq // SWA_QB
    per = SWA_QB // WINDOW
    qcol = _SWA_OFF["q_b"] // SWA_WIDTH
    kcol = _SWA_OFF["k_b"] // LANES
    vcol = _SWA_OFF["v_b"] // LANES
    steps = batch * nb

    def cur(b, n):
        return b * nb + n

    def prev(b, n):
        return jnp.maximum((b * nb + n) * per - 1, 0)

    def chunk_spec(wgt):
        rows, cols = wgt.shape
        assert rows % (steps * 2 * SUBLANES) == 0
        return pl.BlockSpec((rows // steps, cols), lambda b, n: (cur(b, n), 0))

    cast_specs = [chunk_spec(wgt) for wgt in weights_f32]
    return pl.pallas_call(
        _swa_kernel,
        grid=(batch, nb),
        in_specs=[
            pl.BlockSpec(memory_space=pltpu.SMEM),
            pl.BlockSpec((SWA_QB, SWA_WIDTH), lambda b, n: (cur(b, n), qcol)),
            pl.BlockSpec((SWA_QB, LANES), lambda b, n: (cur(b, n), kcol)),
            pl.BlockSpec((WINDOW, LANES), lambda b, n: (prev(b, n), kcol)),
            pl.BlockSpec((SWA_QB, LANES), lambda b, n: (cur(b, n), vcol)),
            pl.BlockSpec((WINDOW, LANES), lambda b, n: (prev(b, n), vcol)),
            pl.BlockSpec((1, SWA_QB), lambda b, n: (0, cur(b, n))),
            pl.BlockSpec((1, WINDOW), lambda b, n: (0, prev(b, n))),
            pl.BlockSpec((ROPE_HALF, LANES), lambda b, n: (0, 0)),
            *cast_specs,
        ],
        out_specs=[pl.BlockSpec((SWA_QB, SWA_WIDTH), lambda b, n: (cur(b, n), 0)), *cast_specs],
        out_shape=[jax.ShapeDtypeStruct((batch * seq, SWA_WIDTH), BF16),
                   *[jax.ShapeDtypeStruct(wgt.shape, BF16) for wgt in weights_f32]],
        scratch_shapes=[
            pltpu.VMEM((2 * WINDOW, WINDOW), BF16),
            pltpu.VMEM((WINDOW, SWA_GROUP * WINDOW), BF16),
        ],
        compiler_params=pltpu.CompilerParams(
            dimension_semantics=("arbitrary", "arbitrary")),
        name="swa_attention",
    )(sinks, proj, proj, proj, proj, proj, pos_row, pos_row, inv_tab, *weights_f32)


def _silu(z):
    return z * jax.nn.sigmoid(z)


def _epilogue_kernel(ya_ref, yb_ref, gz_ref, x_ref, p_ref,
                     wof_hbm, wos_hbm, wout_hbm, gpost_ref, wple_hbm, wgate_hbm, o_ref,
                     wof_ref, wos_ref, wout_ref, wple_ref, wgate_ref, sem):
    def gz(name, width):
        return gz_ref[:, _MAIN_OFF[name]:_MAIN_OFF[name] + width].astype(F32)

    fetch = [pltpu.make_async_copy(src, dst, sem.at[k]) for k, (src, dst) in enumerate((
        (wof_hbm, wof_ref), (wos_hbm, wos_ref), (wout_hbm, wout_ref),
        (wple_hbm, wple_ref), (wgate_hbm, wgate_ref)))]

    def body(first):
        if first:
            fetch[0].start()
            fetch[1].start()
        ua = (ya_ref[...].astype(F32) * _silu(gz("z_a", FOX_WIDTH))).astype(BF16)
        ub = (yb_ref[...].astype(F32) * _silu(gz("z_b", SWA_WIDTH))).astype(BF16)
        if first:
            fetch[0].wait()
            fetch[2].start()
        oa = jnp.dot(ua, wof_ref[...], preferred_element_type=F32)
        if first:
            fetch[1].wait()
            fetch[3].start()
            fetch[4].start()
        ob = jnp.dot(ub, wos_ref[...], preferred_element_type=F32)
        merged = (jax.nn.sigmoid(gz("g_a", D_MODEL)) * oa
                  + jax.nn.sigmoid(gz("g_b", D_MODEL)) * ob)
        if first:
            fetch[2].wait()
        out = jnp.dot(merged.astype(BF16), wout_ref[...], preferred_element_type=F32)
        ms = jnp.mean(out * out, axis=-1, keepdims=True)
        x1 = x_ref[...] + (out * lax.rsqrt(ms + NORM_EPS)) * gpost_ref[...]
        if first:
            fetch[3].wait()
            fetch[4].wait()
        e = jnp.dot(p_ref[...].astype(BF16), wple_ref[...], preferred_element_type=F32)
        gate = jax.nn.sigmoid(jnp.dot(x1.astype(BF16), wgate_ref[...],
                                      preferred_element_type=F32))
        o_ref[...] = x1 + gate * e

    @pl.when(pl.program_id(0) == 0)
    def _():
        body(True)

    @pl.when(pl.program_id(0) > 0)
    def _():
        body(False)


def _epilogue(ya, yb, proj, x2, p2, wof, wos, wout, gpost, wple, wgate):
    t = x2.shape[0]
    gz_cols = _MAIN_OFF["z_b"] + SWA_WIDTH
    assert _MAIN_OFF["g_a"] == 0 and gz_cols == 2 * D_MODEL + FOX_WIDTH + SWA_WIDTH
    once = pl.Buffered(1)
    in_hbm = pl.BlockSpec(memory_space=pl.ANY)

    def const(shape):
        return pl.BlockSpec(shape, lambda i: (0, 0), pipeline_mode=once)

    return pl.pallas_call(
        _epilogue_kernel,
        grid=(t // EPI_TM,),
        in_specs=[
            pl.BlockSpec((EPI_TM, FOX_WIDTH), lambda i: (i, 0)),
            pl.BlockSpec((EPI_TM, SWA_WIDTH), lambda i: (i, 0)),
            pl.BlockSpec((EPI_TM, gz_cols), lambda i: (i, 0)),
            pl.BlockSpec((EPI_TM, D_MODEL), lambda i: (i, 0)),
            pl.BlockSpec((EPI_TM, PLE_DIM), lambda i: (i, 0)),
            in_hbm, in_hbm, in_hbm,
            const((1, D_MODEL)),
            in_hbm, in_hbm,
        ],
        out_specs=pl.BlockSpec((EPI_TM, D_MODEL), lambda i: (i, 0)),
        out_shape=jax.ShapeDtypeStruct((t, D_MODEL), F32),
        scratch_shapes=[
            pltpu.VMEM((FOX_WIDTH, D_MODEL), BF16),
            pltpu.VMEM((SWA_WIDTH, D_MODEL), BF16),
            pltpu.VMEM((D_MODEL, D_MODEL), BF16),
            pltpu.VMEM((PLE_DIM, D_MODEL), BF16),
            pltpu.VMEM((D_MODEL, D_MODEL), BF16),
            pltpu.SemaphoreType.DMA((5,)),
        ],
        compiler_params=pltpu.CompilerParams(
            dimension_semantics=("arbitrary",),
            vmem_limit_bytes=VMEM_LIMIT_BYTES),
        name="epilogue",
    )(ya, yb, proj, x2, p2, wof, wos, wout, gpost, wple, wgate)


def _layer(x2, p2, pos_row, batch, seq, pre_g, w_in, b_forget, sinks, w_o_fox, w_o_swa,
           w_out, post_g, w_ple, w_ple_gate):
    w_t = w_in.T
    f0 = _REF_OFF["f_a"]
    wf_t = jnp.pad(w_t[f0:f0 + FOX_HEADS], ((0, F_ROWS - FOX_HEADS), (0, 0))).astype(BF16)
    col = jnp.arange(MAIN_COLS)
    in_q_a = (col >= _MAIN_OFF["q_a"]) & (col < _MAIN_OFF["q_a"] + FOX_WIDTH)
    col_scale = jnp.where(in_q_a, Q_A_SCALE, 1.0).astype(F32).reshape(1, MAIN_COLS)

    h, kb_tok, proj_swa = _prenorm(x2, pre_g.reshape(1, D_MODEL), wf_t,
                                   b_forget.reshape(FOX_HEADS, 1).astype(F32), w_t, seq)
    proj = _inproj(h, w_t, col_scale)
    ya = _fox_attention(proj, kb_tok, batch, seq)

    inv = ROPE_THETA ** (-jnp.arange(ROPE_HALF, dtype=F32) / ROPE_HALF)
    inv_tab = jnp.broadcast_to(inv[:, None], (ROPE_HALF, LANES))
    yb, wof, wos, wout, wple, wgate = _swa_attention(
        sinks.astype(F32), proj_swa, pos_row, inv_tab, batch, seq,
        (w_o_fox, w_o_swa, w_out, w_ple, w_ple_gate))

    return _epilogue(ya, yb, proj, x2, p2, wof, wos, wout,
                     post_g.reshape(1, D_MODEL), wple, wgate)


def kernel(x, p, positions, pre_norm_g, w_in, b_forget, sinks, w_o_fox, w_o_swa, w_out,
           post_norm_g, w_ple, w_ple_gate):
    batch, seq, _ = x.shape
    depth = p.shape[0]
    x2 = x.reshape(batch * seq, D_MODEL)
    pos_row = positions.reshape(1, batch * seq)
    for i in range(depth):
        x2 = _layer(x2, p[i].reshape(batch * seq, PLE_DIM), pos_row, batch, seq,
                    pre_norm_g[i], w_in[i], b_forget[i], sinks[i], w_o_fox[i], w_o_swa[i],
                    w_out[i], post_norm_g[i], w_ple[i], w_ple_gate[i])
    return x2.reshape(batch, seq, D_MODEL)
```

```python
import functools
import math

import jax
import jax.numpy as jnp
from jax import lax
from jax.experimental import pallas as pl
from jax.experimental.pallas import tpu as pltpu

F32 = jnp.float32
BF16 = jnp.bfloat16

D_MODEL = 2048
FOX_HEADS = 8
FOX_HEAD_DIM = 128
FOX_WIDTH = FOX_HEADS * FOX_HEAD_DIM
SWA_Q_HEADS = 16
SWA_KV_HEADS = 2
SWA_HEAD_DIM = 64
SWA_WIDTH = SWA_Q_HEADS * SWA_HEAD_DIM
SWA_KV_WIDTH = SWA_KV_HEADS * SWA_HEAD_DIM
SWA_GROUP = SWA_Q_HEADS // SWA_KV_HEADS
ROPE_HALF = SWA_HEAD_DIM // 2
WINDOW = 128
ROPE_THETA = 10000.0
PLE_DIM = 256
NORM_EPS = 1e-6
LANES = 128
V7X_VMEM_BYTES = 64 * 1024 * 1024
VMEM_LIMIT_BYTES = V7X_VMEM_BYTES * 7 // 8
LOG2E = math.log2(math.e)
NEG_BIG = -1e30

_REF_SPLITS = (FOX_WIDTH, FOX_WIDTH, FOX_WIDTH, FOX_WIDTH, FOX_HEADS, SWA_WIDTH,
               SWA_KV_WIDTH, SWA_KV_WIDTH, SWA_WIDTH, D_MODEL, D_MODEL)
_REF_NAMES = ("q_a", "k_a", "v_a", "z_a", "f_a", "q_b", "k_b", "v_b", "z_b", "g_a", "g_b")
_REF_OFF = {}
_o = 0
for _n, _s in zip(_REF_NAMES, _REF_SPLITS):
    _REF_OFF[_n] = _o
    _o += _s

IN_TM = 1024
IN_TN = 2304
NORM_ROWS = 1024
F_ROWS = 16

PREP_TN = 256
_SWA_GROUPS = (
    (_REF_OFF["q_b"], SWA_WIDTH),
    (_REF_OFF["k_b"], 2 * SWA_KV_WIDTH),
)
_MAIN_GROUPS = (
    (_REF_OFF["g_a"], 2 * D_MODEL),
    (_REF_OFF["z_a"], FOX_WIDTH),
    (_REF_OFF["z_b"], SWA_WIDTH),
    (_REF_OFF["q_a"], 3 * FOX_WIDTH),
)
SUBLANES = 8


def _group_table(groups):
    assert all(c0 % SUBLANES == 0 for c0, _ in groups)
    counts = [-(-w // PREP_TN) for _, w in groups]
    return tuple(sum(counts[:i]) for i in range(len(groups))), sum(counts)


_SWA_START, _SWA_WINDOWS = _group_table(_SWA_GROUPS)
_MAIN_START, _MAIN_WINDOWS = _group_table(_MAIN_GROUPS)
W_PER_TILE = IN_TN // PREP_TN
assert _MAIN_WINDOWS % W_PER_TILE == 0
SWA_COLS = _SWA_WINDOWS * PREP_TN
MAIN_COLS = _MAIN_WINDOWS * PREP_TN
_SWA_OFF = {"q_b": 0, "k_b": _SWA_START[1] * PREP_TN,
            "v_b": _SWA_START[1] * PREP_TN + SWA_KV_WIDTH}
_MAIN_OFF = {
    "g_a": 0, "g_b": D_MODEL,
    "z_a": _MAIN_START[1] * PREP_TN,
    "z_b": _MAIN_START[2] * PREP_TN,
    "q_a": _MAIN_START[3] * PREP_TN,
    "k_a": _MAIN_START[3] * PREP_TN + FOX_WIDTH,
    "v_a": _MAIN_START[3] * PREP_TN + 2 * FOX_WIDTH,
}
_NT = (((1,), (1,)), ((), ()))
Q_A_SCALE = FOX_HEAD_DIM ** -0.5 * LOG2E

FOX_T = 512
FOX_CHUNK = 512
FOX_PAIR = 4
BIAS_TERMS = 3
ONES_ROWS = 16

SWA_NW = 8
SWA_QB = SWA_NW * WINDOW

EPI_TM = 256


def _cast_windows(w_refs, wb_ref):
    for k, w_ref in enumerate(w_refs):
        for r in range(PREP_TN // LANES):
            rows = slice(r * LANES, (r + 1) * LANES)
            wb_ref[k * PREP_TN + r * LANES:k * PREP_TN + (r + 1) * LANES, :] = (
                w_ref[rows, :].astype(BF16))


def _w_src_row(window, groups, starts):
    tile_row = jnp.int32(0)
    for (c0, _), start in zip(groups, starts):
        tile_row = jnp.where(window >= start,
                             c0 // SUBLANES + (window - start) * (PREP_TN // SUBLANES), tile_row)
    return tile_row * SUBLANES


def _prenorm_kernel(steps_per_seq, x_ref, g_ref, wf_ref, b_ref, *refs):
    w_refs = refs[:_SWA_WINDOWS]
    h_ref, kb_ref, proj_ref, wb_ref, carry_ref = refs[_SWA_WINDOWS:]
    i = pl.program_id(0)

    @pl.when(i == 0)
    def _():
        _cast_windows(w_refs, wb_ref)

    @pl.when(i % steps_per_seq == 0)
    def _():
        carry_ref[...] = jnp.zeros(carry_ref.shape, F32)

    x = x_ref[...]
    ms = jnp.mean(x * x, axis=-1, keepdims=True)
    h = ((x * lax.rsqrt(ms + NORM_EPS)) * g_ref[...]).astype(BF16)
    h_ref[...] = h
    ft = lax.dot_general(wf_ref[...], h, _NT, preferred_element_type=F32)
    proj_ref[...] = lax.dot_general(h, wb_ref[...], _NT,
                                    preferred_element_type=F32).astype(BF16)

    f = ft[0:FOX_HEADS, :] + b_ref[...]
    c = jnp.minimum(f, 0.0) - jnp.log1p(jnp.exp(-jnp.abs(f)))
    rows = c.shape[1]
    lane = lax.broadcasted_iota(jnp.int32, c.shape, 1)
    shift = 1
    while shift < rows:
        c = c + jnp.where(lane >= shift, pltpu.roll(c, shift, axis=1), 0.0)
        shift *= 2
    c = c + carry_ref[:, 0:1]
    carry_ref[...] = jnp.broadcast_to(c[:, rows - 1:rows], carry_ref.shape)
    rest = c * (-LOG2E)
    pieces = []
    for _ in range(BIAS_TERMS):
        piece = rest.astype(BF16).astype(F32)
        pieces.append(piece)
        rest = rest - piece
    pieces.append(jnp.zeros((LANES - BIAS_TERMS * FOX_HEADS, rows), F32))
    kb_ref[...] = jnp.concatenate(pieces, axis=0).T.astype(BF16)


def _prenorm(x2, g, wf_t, b_col, w_t, seq):
    t = x2.shape[0]
    once = pl.Buffered(1)

    def w_spec(k):
        return pl.BlockSpec((pl.Element(PREP_TN), pl.Element(D_MODEL)),
                            lambda i: (_w_src_row(k, _SWA_GROUPS, _SWA_START), 0),
                            pipeline_mode=once)

    return pl.pallas_call(
        functools.partial(_prenorm_kernel, seq // NORM_ROWS),
        grid=(t // NORM_ROWS,),
        in_specs=[
            pl.BlockSpec((NORM_ROWS, D_MODEL), lambda i: (i, 0)),
            pl.BlockSpec((1, D_MODEL), lambda i: (0, 0)),
            pl.BlockSpec((F_ROWS, D_MODEL), lambda i: (0, 0)),
            pl.BlockSpec((FOX_HEADS, 1), lambda i: (0, 0)),
            *[w_spec(k) for k in range(_SWA_WINDOWS)],
        ],
        out_specs=[
            pl.BlockSpec((NORM_ROWS, D_MODEL), lambda i: (i, 0)),
            pl.BlockSpec((NORM_ROWS, LANES), lambda i: (i, 0)),
            pl.BlockSpec((NORM_ROWS, SWA_COLS), lambda i: (i, 0)),
        ],
        out_shape=[
            jax.ShapeDtypeStruct((t, D_MODEL), BF16),
            jax.ShapeDtypeStruct((t, LANES), BF16),
            jax.ShapeDtypeStruct((t, SWA_COLS), BF16),
        ],
        scratch_shapes=[
            pltpu.VMEM((SWA_COLS, D_MODEL), BF16),
            pltpu.VMEM((FOX_HEADS, LANES), F32),
        ],
        compiler_params=pltpu.CompilerParams(
            dimension_semantics=("arbitrary",),
            vmem_limit_bytes=VMEM_LIMIT_BYTES),
        name="prenorm",
    )(x2, g, wf_t, b_col, *([w_t] * _SWA_WINDOWS))


def _inproj_kernel(h_ref, w_hbm, scale_ref, proj_ref, wf_ref, wb_ref, sem):
    j = pl.program_id(0)

    def window_copy(tile, k):
        row = pl.multiple_of(
            _w_src_row(tile * W_PER_TILE + k, _MAIN_GROUPS, _MAIN_START), SUBLANES)
        return pltpu.make_async_copy(
            w_hbm.at[pl.ds(row, PREP_TN), :],
            wf_ref.at[pl.ds(k * PREP_TN, PREP_TN), :],
            sem.at[k])

    @pl.when(pl.program_id(1) == 0)
    def _():
        @pl.when(j == 0)
        def _():
            for k in range(W_PER_TILE):
                window_copy(j, k).start()

        for k in range(W_PER_TILE):
            window_copy(j, k).wait()
            for r in range(PREP_TN // LANES):
                rows = pl.ds(k * PREP_TN + r * LANES, LANES)
                wb_ref[rows, :] = wf_ref[rows, :].astype(BF16)

        @pl.when(j + 1 < pl.num_programs(0))
        def _():
            for k in range(W_PER_TILE):
                window_copy(j + 1, k).start()

    acc = lax.dot_general(h_ref[...], wb_ref[...], _NT, preferred_element_type=F32)
    proj_ref[...] = (acc * scale_ref[...]).astype(BF16)


def _inproj(h, w_t, col_scale):
    t = h.shape[0]
    return pl.pallas_call(
        _inproj_kernel,
        grid=(MAIN_COLS // IN_TN, t // IN_TM),
        in_specs=[
            pl.BlockSpec((IN_TM, D_MODEL), lambda j, i: (i, 0)),
            pl.BlockSpec(memory_space=pl.ANY),
            pl.BlockSpec((1, IN_TN), lambda j, i: (0, j)),
        ],
        out_specs=pl.BlockSpec((IN_TM, IN_TN), lambda j, i: (i, j)),
        out_shape=jax.ShapeDtypeStruct((t, MAIN_COLS), BF16),
        scratch_shapes=[
            pltpu.VMEM((IN_TN, D_MODEL), F32),
            pltpu.VMEM((IN_TN, D_MODEL), BF16),
            pltpu.SemaphoreType.DMA((W_PER_TILE,)),
        ],
        compiler_params=pltpu.CompilerParams(
            dimension_semantics=("arbitrary", "arbitrary"),
            vmem_limit_bytes=VMEM_LIMIT_BYTES),
        name="inproj",
    )(h, w_t, col_scale)


def _fox_kernel(q_ref, k_ref, v_ref, kb_ref, o_ref, vt_ref, qt_ref, acc_ref,
                sa_ref, ma_ref, sb_ref, mb_ref):
    s0_ref, s1_ref = (sa_ref, ma_ref), (sb_ref, mb_ref)
    hp = pl.program_id(1)
    seq = k_ref.shape[0]
    t = FOX_T
    d = FOX_HEAD_DIM

    def transposed(x):
        return x.astype(F32).T.astype(BF16)

    row = lax.broadcasted_iota(jnp.int32, (d, t), 0)
    for hh in range(FOX_PAIR):
        mine = (row < BIAS_TERMS * FOX_HEADS) & (row % FOX_HEADS == hp * FOX_PAIR + hh)
        qt_ref[hh, d:2 * d, :] = jnp.where(mine, 1.0, 0.0).astype(BF16)
        vt_ref[hh, d:, :] = jnp.ones((ONES_ROWS, seq), BF16)
        for r in range(seq // FOX_CHUNK):
            rows = slice(r * FOX_CHUNK, (r + 1) * FOX_CHUNK)
            vt_ref[hh, 0:d, rows] = transposed(v_ref[rows, hh * d:(hh + 1) * d])

    def scores(i, buf):
        s_ref, max_ref = buf
        rows = pl.ds(pl.multiple_of(i * t, t), t)
        for hh in range(FOX_PAIR):
            k_aug = jnp.concatenate([k_ref[rows, hh * d:(hh + 1) * d], kb_ref[rows, :]], axis=1)
            s = jnp.dot(k_aug, qt_ref[hh], preferred_element_type=F32)
            s_ref[hh] = s
            max_ref[hh] = jnp.max(s, axis=0, keepdims=True)

    def absorb(i, buf, carry, masked):
        s_ref, max_ref = buf
        rows = pl.ds(pl.multiple_of(i * t, t), t)
        new = []
        for hh in range(FOX_PAIR):
            m = carry[hh]
            s = s_ref[hh]
            if masked:
                key = lax.broadcasted_iota(jnp.int32, s.shape, 0)
                qry = lax.broadcasted_iota(jnp.int32, s.shape, 1)
                s = jnp.where(key <= qry, s, NEG_BIG)
                tile_max = jnp.max(s, axis=0, keepdims=True)
            else:
                tile_max = max_ref[hh]
            m_new = jnp.maximum(m, tile_max)
            alpha = jnp.exp2(m - m_new)
            p = jnp.exp2(s - m_new)
            acc_ref[hh] = alpha * acc_ref[hh] + jnp.dot(
                vt_ref[hh, :, rows], p.astype(BF16), preferred_element_type=F32)
            new.append(m_new)
        return tuple(new)

    def step(i, s_cur, s_next, carry):
        scores(i + 1, s_next)
        return absorb(i, s_cur, carry, False)

    def pair(j, carry):
        carry = step(2 * j, s0_ref, s1_ref, carry)
        return step(2 * j + 1, s1_ref, s0_ref, carry)

    n_q = seq // t

    def load_q(qi):
        q_rows = pl.ds(pl.multiple_of(qi * t, t), t)
        for hh in range(FOX_PAIR):
            qt_ref[hh, 0:d, :] = transposed(q_ref[q_rows, hh * d:(hh + 1) * d])

    def q_tile(qi, _):
        for hh in range(FOX_PAIR):
            acc_ref[hh] = jnp.zeros((d + ONES_ROWS, t), F32)

        def finish(s_last, carry):
            load_q(jnp.minimum(qi + 1, n_q - 1))
            absorb(qi, s_last, carry, True)
            q_rows = pl.ds(pl.multiple_of(qi * t, t), t)
            for hh in range(FOX_PAIR):
                o_ref[q_rows, hh * d:(hh + 1) * d] = (
                    acc_ref[hh, 0:d, :] / acc_ref[hh, d:d + 1, :]).T.astype(o_ref.dtype)

        def odd_tail(carry):
            finish(s1_ref, step(qi - 1, s0_ref, s1_ref, carry))

        def even_tail(carry):
            finish(s0_ref, carry)

        init = tuple(jnp.full((1, t), NEG_BIG, F32) for _ in range(FOX_PAIR))
        scores(0, s0_ref)
        carry = lax.fori_loop(0, qi // 2, pair, init)
        lax.cond(qi % 2 == 1, odd_tail, even_tail, carry)
        return 0

    load_q(0)
    lax.fori_loop(0, n_q, q_tile, 0)


def _fox_attention(proj, kb_tok, batch, seq):
    width = FOX_PAIR * FOX_HEAD_DIM
    qcol = _MAIN_OFF["q_a"] // width
    kcol = _MAIN_OFF["k_a"] // width
    vcol = _MAIN_OFF["v_a"] // width
    return pl.pallas_call(
        _fox_kernel,
        grid=(batch, FOX_HEADS // FOX_PAIR),
        in_specs=[
            pl.BlockSpec((seq, width), lambda b, h: (b, qcol + h)),
            pl.BlockSpec((seq, width), lambda b, h: (b, kcol + h)),
            pl.BlockSpec((seq, width), lambda b, h: (b, vcol + h)),
            pl.BlockSpec((seq, LANES), lambda b, h: (b, 0)),
        ],
        out_specs=pl.BlockSpec((seq, width), lambda b, h: (b, h)),
        out_shape=jax.ShapeDtypeStruct((batch * seq, FOX_WIDTH), BF16),
        scratch_shapes=[
            pltpu.VMEM((FOX_PAIR, FOX_HEAD_DIM + ONES_ROWS, seq), BF16),
            pltpu.VMEM((FOX_PAIR, 2 * FOX_HEAD_DIM, FOX_T), BF16),
            pltpu.VMEM((FOX_PAIR, FOX_HEAD_DIM + ONES_ROWS, FOX_T), F32),
            pltpu.VMEM((FOX_PAIR, FOX_T, FOX_T), F32),
            pltpu.VMEM((FOX_PAIR, 1, FOX_T), F32),
            pltpu.VMEM((FOX_PAIR, FOX_T, FOX_T), F32),
            pltpu.VMEM((FOX_PAIR, 1, FOX_T), F32),
        ],
        compiler_params=pltpu.CompilerParams(
            dimension_semantics=("arbitrary", "arbitrary"),
            vmem_limit_bytes=VMEM_LIMIT_BYTES),
        name="fox_attention",
    )(proj, proj, proj, kb_tok)


def _rope_t(xt, cos, sin):
    out = []
    for hd in range(xt.shape[0] // SWA_HEAD_DIM):
        x1 = xt[hd * SWA_HEAD_DIM: hd * SWA_HEAD_DIM + ROPE_HALF]
        x2 = xt[hd * SWA_HEAD_DIM + ROPE_HALF: (hd + 1) * SWA_HEAD_DIM]
        out.append(x1 * cos - x2 * sin)
        out.append(x2 * cos + x1 * sin)
    return jnp.concatenate(out, axis=0)


def _swa_kernel(sink_ref, q_ref, kc_ref, kp_ref, vc_ref, vp_ref, posc_ref, posp_ref,
                inv_ref, *rest):
    n_cast = (len(rest) - 4) // 3
    cast_in, o_ref = rest[:n_cast], rest[n_cast]
    cast_out = rest[n_cast + 1:2 * n_cast + 1]
    band_ref, eye_ref = rest[2 * n_cast + 1:2 * n_cast + 3]
    stage, sem = rest[2 * n_cast + 3:3 * n_cast + 3], rest[-1]
    step = pl.program_id(0) * pl.num_programs(1) + pl.program_id(1)
    n_steps = pl.num_programs(0) * pl.num_programs(1)

    def chunk_copy(k, s):
        rows = stage[k].shape[1]
        return pltpu.make_async_copy(
            cast_in[k].at[pl.ds(pl.multiple_of(s * rows, SUBLANES), rows), :],
            stage[k].at[s % 3], sem.at[k, s % 3])

    @pl.when(step == 0)
    def _():
        for k in range(n_cast):
            chunk_copy(k, step).start()
            chunk_copy(k, step + 1).start()

    @pl.when(step + 2 < n_steps)
    def _():
        for k in range(n_cast):
            chunk_copy(k, step + 2).start()

    for k, dst_ref in enumerate(cast_out):
        chunk_copy(k, step).wait()
        dst_ref[...] = stage[k][step % 3].astype(dst_ref.dtype)

    n = pl.program_id(1)
    w = WINDOW
    inv = inv_ref[...]

    def tables(pos_row):
        ang = inv * pos_row.astype(F32)
        return jnp.cos(ang), jnp.sin(ang)

    def rope_k(k_nat, cos, sin):
        kt = _rope_t(k_nat.astype(F32).T, cos, sin)
        return kt.T.astype(BF16)

    cos_p, sin_p = tables(posp_ref[...])
    k_prev = rope_k(kp_ref[...], cos_p, sin_p)
    v_prev = vp_ref[...]

    @pl.when((pl.program_id(0) == 0) & (n == 0))
    def _():
        key = lax.broadcasted_iota(jnp.int32, band_ref.shape, 0)
        qry = lax.broadcasted_iota(jnp.int32, band_ref.shape, 1)
        band_ref[...] = jnp.where((key <= qry + w) & (key > qry), 0.0, NEG_BIG).astype(BF16)
        src = lax.broadcasted_iota(jnp.int32, eye_ref.shape, 0)
        dst = lax.broadcasted_iota(jnp.int32, eye_ref.shape, 1) % w
        eye_ref[...] = jnp.where(src == dst, 1.0, 0.0).astype(BF16)

    no_prev = jnp.where(n > 0, 0.0, NEG_BIG)
    q_scale = SWA_HEAD_DIM ** -0.5 * LOG2E
    zeros_half = jnp.zeros((SWA_HEAD_DIM, SWA_GROUP * w), F32)
    ones_rows = jnp.ones((2 * SUBLANES, 2 * w), BF16)

    sinks = [jnp.concatenate(
        [jnp.full((1, w), sink_ref[hd] * LOG2E, F32)
         for hd in range(g * SWA_GROUP, (g + 1) * SWA_GROUP)], axis=1)
        for g in range(SWA_KV_HEADS)]

    def logits(wi, k_prev):
        tok = slice(wi * w, (wi + 1) * w)
        cos, sin = tables(posc_ref[:, tok])
        k_cur = rope_k(kc_ref[tok, :], cos, sin)
        k_all = jnp.concatenate([k_prev, k_cur], axis=0)
        k_aug = jnp.concatenate([k_all, band_ref[...]], axis=1)
        qf = q_ref[tok, :].astype(F32)
        qt = jnp.concatenate(
            [qf[:, c * LANES:(c + 1) * LANES].T for c in range(SWA_WIDTH // LANES)], axis=0)
        qt = _rope_t(qt, cos * q_scale, sin * q_scale)
        out = []
        for g in range(SWA_KV_HEADS):
            heads = range(g * SWA_GROUP, (g + 1) * SWA_GROUP)
            q_g = jnp.concatenate(
                [qt[hd * SWA_HEAD_DIM:(hd + 1) * SWA_HEAD_DIM] for hd in heads], axis=1)
            parts = [zeros_half] * SWA_KV_HEADS
            parts[g] = q_g
            q_z = jnp.concatenate(parts, axis=0).astype(BF16)
            q_aug = jnp.concatenate([q_z, eye_ref[...]], axis=0)
            s = jnp.dot(k_aug, q_aug, preferred_element_type=F32)
            if wi == 0:
                s = jnp.concatenate([s[:w] + no_prev, s[w:]], axis=0)
            out.append(s)
        return out, k_cur

    def finish(wi, scores, v_prev):
        tok = slice(wi * w, (wi + 1) * w)
        v_cur = vc_ref[tok, :]
        v_all_t = jnp.concatenate([v_prev, v_cur], axis=0).astype(F32).T.astype(BF16)
        for g in range(SWA_KV_HEADS):
            s, sink = scores[g], sinks[g]
            m = jnp.maximum(jnp.max(s, axis=0, keepdims=True), sink)
            e = jnp.exp2(s - m).astype(BF16)
            v_aug = jnp.concatenate(
                [v_all_t[g * SWA_HEAD_DIM:(g + 1) * SWA_HEAD_DIM], ones_rows], axis=0)
            pv = jnp.dot(v_aug, e, preferred_element_type=F32)
            denom = pv[SWA_HEAD_DIM:SWA_HEAD_DIM + 1] + jnp.exp2(sink - m)
            out_t = pv[:SWA_HEAD_DIM] / denom
            for jj in range(SWA_GROUP // 2):
                pair = jnp.concatenate(
                    [out_t[:, (2 * jj) * w:(2 * jj + 1) * w],
                     out_t[:, (2 * jj + 1) * w:(2 * jj + 2) * w]], axis=0)
                col = (g * (SWA_GROUP // 2) + jj) * LANES
                o_ref[tok, col:col + LANES] = pair.T.astype(o_ref.dtype)
        return v_cur

    pending, k_prev = logits(0, k_prev)
    for wi in range(SWA_NW):
        if wi + 1 < SWA_NW:
            upcoming, k_prev = logits(wi + 1, k_prev)
        v_prev = finish(wi, pending, v_prev)
        if wi + 1 < SWA_NW:
            pending = upcoming


def _swa_attention(sinks, proj, pos_row, inv_tab, batch, seq, weights_f32):
    nb = seq // SWA_QB
    per = SWA_QB // WINDOW
    qcol = _SWA_OFF["q_b"] // SWA_WIDTH
    kcol = _SWA_OFF["k_b"] // LANES
    vcol = _SWA_OFF["v_b"] // LANES
    steps = batch * nb

    def cur(b, n):
        return b * nb + n

    def prev(b, n):
        return jnp.maximum((b * nb + n) * per - 1, 0)

    def chunk_spec(wgt, **mode):
        rows, cols = wgt.shape
        assert rows % (steps * 2 * SUBLANES) == 0
        return pl.BlockSpec((rows // steps, cols), lambda b, n: (cur(b, n), 0), **mode)

    cast_specs = [chunk_spec(wgt) for wgt in weights_f32]
    assert steps >= 2
    cast_in_specs = [pl.BlockSpec(memory_space=pl.ANY) for _ in weights_f32]
    return pl.pallas_call(
        _swa_kernel,
        grid=(batch, nb),
        in_specs=[
            pl.BlockSpec(memory_space=pltpu.SMEM),
            pl.BlockSpec((SWA_QB, SWA_WIDTH), lambda b, n: (cur(b, n), qcol)),
            pl.BlockSpec((SWA_QB, LANES), lambda b, n: (cur(b, n), kcol)),
            pl.BlockSpec((WINDOW, LANES), lambda b, n: (prev(b, n), kcol)),
            pl.BlockSpec((SWA_QB, LANES), lambda b, n: (cur(b, n), vcol)),
            pl.BlockSpec((WINDOW, LANES), lambda b, n: (prev(b, n), vcol)),
            pl.BlockSpec((1, SWA_QB), lambda b, n: (0, cur(b, n))),
            pl.BlockSpec((1, WINDOW), lambda b, n: (0, prev(b, n))),
            pl.BlockSpec((ROPE_HALF, LANES), lambda b, n: (0, 0)),
            *cast_in_specs,
        ],
        out_specs=[pl.BlockSpec((SWA_QB, SWA_WIDTH), lambda b, n: (cur(b, n), 0)), *cast_specs],
        out_shape=[jax.ShapeDtypeStruct((batch * seq, SWA_WIDTH), BF16),
                   *[jax.ShapeDtypeStruct(wgt.shape, BF16) for wgt in weights_f32]],
        scratch_shapes=[
            pltpu.VMEM((2 * WINDOW, WINDOW), BF16),
            pltpu.VMEM((WINDOW, SWA_GROUP * WINDOW), BF16),
            *[pltpu.VMEM((3, wgt.shape[0] // steps, wgt.shape[1]), F32) for wgt in weights_f32],
            pltpu.SemaphoreType.DMA((len(weights_f32), 3)),
        ],
        compiler_params=pltpu.CompilerParams(
            dimension_semantics=("arbitrary", "arbitrary"),
            vmem_limit_bytes=VMEM_LIMIT_BYTES),
        name="swa_attention",
    )(sinks, proj, proj, proj, proj, proj, pos_row, pos_row, inv_tab, *weights_f32)


def _silu(z):
    return z * jax.nn.sigmoid(z)


def _epilogue_kernel(ya_ref, yb_ref, gz_ref, x_ref, p_ref,
                     wof_hbm, wos_hbm, wout_hbm, gpost_ref, wple_hbm, wgate_hbm, o_ref,
                     wof_ref, wos_ref, wout_ref, wple_ref, wgate_ref, sem):
    def gz(name, width):
        return gz_ref[:, _MAIN_OFF[name]:_MAIN_OFF[name] + width].astype(F32)

    fetch = [pltpu.make_async_copy(src, dst, sem.at[k]) for k, (src, dst) in enumerate((
        (wof_hbm, wof_ref), (wos_hbm, wos_ref), (wout_hbm, wout_ref),
        (wple_hbm, wple_ref), (wgate_hbm, wgate_ref)))]

    def body(first):
        if first:
            fetch[0].start()
            fetch[1].start()
        ua = (ya_ref[...].astype(F32) * _silu(gz("z_a", FOX_WIDTH))).astype(BF16)
        ub = (yb_ref[...].astype(F32) * _silu(gz("z_b", SWA_WIDTH))).astype(BF16)
        if first:
            fetch[0].wait()
            fetch[2].start()
        oa = jnp.dot(ua, wof_ref[...], preferred_element_type=F32)
        if first:
            fetch[1].wait()
            fetch[3].start()
            fetch[4].start()
        ob = jnp.dot(ub, wos_ref[...], preferred_element_type=F32)
        merged = (jax.nn.sigmoid(gz("g_a", D_MODEL)) * oa
                  + jax.nn.sigmoid(gz("g_b", D_MODEL)) * ob)
        if first:
            fetch[2].wait()
        out = jnp.dot(merged.astype(BF16), wout_ref[...], preferred_element_type=F32)
        ms = jnp.mean(out * out, axis=-1, keepdims=True)
        x1 = x_ref[...] + (out * lax.rsqrt(ms + NORM_EPS)) * gpost_ref[...]
        if first:
            fetch[3].wait()
            fetch[4].wait()
        e = jnp.dot(p_ref[...].astype(BF16), wple_ref[...], preferred_element_type=F32)
        gate = jax.nn.sigmoid(jnp.dot(x1.astype(BF16), wgate_ref[...],
                                      preferred_element_type=F32))
        o_ref[...] = x1 + gate * e

    @pl.when(pl.program_id(0) == 0)
    def _():
        body(True)

    @pl.when(pl.program_id(0) > 0)
    def _():
        body(False)


def _epilogue(ya, yb, proj, x2, p2, wof, wos, wout, gpost, wple, wgate):
    t = x2.shape[0]
    gz_cols = _MAIN_OFF["z_b"] + SWA_WIDTH
    assert _MAIN_OFF["g_a"] == 0 and gz_cols == 2 * D_MODEL + FOX_WIDTH + SWA_WIDTH
    once = pl.Buffered(1)
    in_hbm = pl.BlockSpec(memory_space=pl.ANY)

    def const(shape):
        return pl.BlockSpec(shape, lambda i: (0, 0), pipeline_mode=once)

    return pl.pallas_call(
        _epilogue_kernel,
        grid=(t // EPI_TM,),
        in_specs=[
            pl.BlockSpec((EPI_TM, FOX_WIDTH), lambda i: (i, 0)),
            pl.BlockSpec((EPI_TM, SWA_WIDTH), lambda i: (i, 0)),
            pl.BlockSpec((EPI_TM, gz_cols), lambda i: (i, 0)),
            pl.BlockSpec((EPI_TM, D_MODEL), lambda i: (i, 0)),
            pl.BlockSpec((EPI_TM, PLE_DIM), lambda i: (i, 0)),
            in_hbm, in_hbm, in_hbm,
            const((1, D_MODEL)),
            in_hbm, in_hbm,
        ],
        out_specs=pl.BlockSpec((EPI_TM, D_MODEL), lambda i: (i, 0)),
        out_shape=jax.ShapeDtypeStruct((t, D_MODEL), F32),
        scratch_shapes=[
            pltpu.VMEM((FOX_WIDTH, D_MODEL), BF16),
            pltpu.VMEM((SWA_WIDTH, D_MODEL), BF16),
            pltpu.VMEM((D_MODEL, D_MODEL), BF16),
            pltpu.VMEM((PLE_DIM, D_MODEL), BF16),
            pltpu.VMEM((D_MODEL, D_MODEL), BF16),
            pltpu.SemaphoreType.DMA((5,)),
        ],
        compiler_params=pltpu.CompilerParams(
            dimension_semantics=("arbitrary",),
            vmem_limit_bytes=VMEM_LIMIT_BYTES),
        name="epilogue",
    )(ya, yb, proj, x2, p2, wof, wos, wout, gpost, wple, wgate)


def _layer(x2, p2, pos_row, batch, seq, pre_g, w_in, b_forget, sinks, w_o_fox, w_o_swa,
           w_out, post_g, w_ple, w_ple_gate):
    w_t = w_in.T
    f0 = _REF_OFF["f_a"]
    wf_t = jnp.pad(w_t[f0:f0 + FOX_HEADS], ((0, F_ROWS - FOX_HEADS), (0, 0))).astype(BF16)
    col = jnp.arange(MAIN_COLS)
    in_q_a = (col >= _MAIN_OFF["q_a"]) & (col < _MAIN_OFF["q_a"] + FOX_WIDTH)
    col_scale = jnp.where(in_q_a, Q_A_SCALE, 1.0).astype(F32).reshape(1, MAIN_COLS)

    h, kb_tok, proj_swa = _prenorm(x2, pre_g.reshape(1, D_MODEL), wf_t,
                                   b_forget.reshape(FOX_HEADS, 1).astype(F32), w_t, seq)
    proj = _inproj(h, w_t, col_scale)
    ya = _fox_attention(proj, kb_tok, batch, seq)

    inv = ROPE_THETA ** (-jnp.arange(ROPE_HALF, dtype=F32) / ROPE_HALF)
    inv_tab = jnp.broadcast_to(inv[:, None], (ROPE_HALF, LANES))
    yb, wof, wos, wout, wple, wgate = _swa_attention(
        sinks.astype(F32), proj_swa, pos_row, inv_tab, batch, seq,
        (w_o_fox, w_o_swa, w_out, w_ple, w_ple_gate))

    return _epilogue(ya, yb, proj, x2, p2, wof, wos, wout,
                     post_g.reshape(1, D_MODEL), wple, wgate)


def kernel(x, p, positions, pre_norm_g, w_in, b_forget, sinks, w_o_fox, w_o_swa, w_out,
           post_norm_g, w_ple, w_ple_gate):
    batch, seq, _ = x.shape
    depth = p.shape[0]
    x2 = x.reshape(batch * seq, D_MODEL)
    pos_row = positions.reshape(1, batch * seq)
    for i in range(depth):
        x2 = _layer(x2, p[i].reshape(batch * seq, PLE_DIM), pos_row, batch, seq,
                    pre_norm_g[i], w_in[i], b_forget[i], sinks[i], w_o_fox[i], w_o_swa[i],
                    w_out[i], post_norm_g[i], w_ple[i], w_ple_gate[i])
    return x2.reshape(batch, seq, D_MODEL)
```

```python
import functools
import math

import jax
import jax.numpy as jnp
from jax import lax
from jax.experimental import pallas as pl
from jax.experimental.pallas import tpu as pltpu

F32 = jnp.float32
BF16 = jnp.bfloat16

D_MODEL = 2048
FOX_HEADS = 8
FOX_HEAD_DIM = 128
FOX_WIDTH = FOX_HEADS * FOX_HEAD_DIM
SWA_Q_HEADS = 16
SWA_KV_HEADS = 2
SWA_HEAD_DIM = 64
SWA_WIDTH = SWA_Q_HEADS * SWA_HEAD_DIM
SWA_KV_WIDTH = SWA_KV_HEADS * SWA_HEAD_DIM
SWA_GROUP = SWA_Q_HEADS // SWA_KV_HEADS
ROPE_HALF = SWA_HEAD_DIM // 2
WINDOW = 128
ROPE_THETA = 10000.0
PLE_DIM = 256
NORM_EPS = 1e-6
LANES = 128
V7X_VMEM_BYTES = 64 * 1024 * 1024
VMEM_LIMIT_BYTES = V7X_VMEM_BYTES * 7 // 8
LOG2E = math.log2(math.e)
NEG_BIG = -1e30

_REF_SPLITS = (FOX_WIDTH, FOX_WIDTH, FOX_WIDTH, FOX_WIDTH, FOX_HEADS, SWA_WIDTH,
               SWA_KV_WIDTH, SWA_KV_WIDTH, SWA_WIDTH, D_MODEL, D_MODEL)
_REF_NAMES = ("q_a", "k_a", "v_a", "z_a", "f_a", "q_b", "k_b", "v_b", "z_b", "g_a", "g_b")
_REF_OFF = {}
_o = 0
for _n, _s in zip(_REF_NAMES, _REF_SPLITS):
    _REF_OFF[_n] = _o
    _o += _s

IN_TM = 1024
IN_TN = 2304
NORM_ROWS = 1024
F_ROWS = 16

PREP_TN = 256
_SWA_GROUPS = (
    (_REF_OFF["q_b"], SWA_WIDTH),
    (_REF_OFF["k_b"], 2 * SWA_KV_WIDTH),
)
_MAIN_GROUPS = (
    (_REF_OFF["g_a"], 2 * D_MODEL),
    (_REF_OFF["z_a"], FOX_WIDTH),
    (_REF_OFF["z_b"], SWA_WIDTH),
    (_REF_OFF["q_a"], 3 * FOX_WIDTH),
)
SUBLANES = 8


def _group_table(groups):
    assert all(c0 % SUBLANES == 0 for c0, _ in groups)
    counts = [-(-w // PREP_TN) for _, w in groups]
    return tuple(sum(counts[:i]) for i in range(len(groups))), sum(counts)


_SWA_START, _SWA_WINDOWS = _group_table(_SWA_GROUPS)
_MAIN_START, _MAIN_WINDOWS = _group_table(_MAIN_GROUPS)
W_PER_TILE = IN_TN // PREP_TN
assert _MAIN_WINDOWS % W_PER_TILE == 0
SWA_COLS = _SWA_WINDOWS * PREP_TN
MAIN_COLS = _MAIN_WINDOWS * PREP_TN
_SWA_OFF = {"q_b": 0, "k_b": _SWA_START[1] * PREP_TN,
            "v_b": _SWA_START[1] * PREP_TN + SWA_KV_WIDTH}
_MAIN_OFF = {
    "g_a": 0, "g_b": D_MODEL,
    "z_a": _MAIN_START[1] * PREP_TN,
    "z_b": _MAIN_START[2] * PREP_TN,
    "q_a": _MAIN_START[3] * PREP_TN,
    "k_a": _MAIN_START[3] * PREP_TN + FOX_WIDTH,
    "v_a": _MAIN_START[3] * PREP_TN + 2 * FOX_WIDTH,
}
_NT = (((1,), (1,)), ((), ()))
Q_A_SCALE = FOX_HEAD_DIM ** -0.5 * LOG2E

FOX_T = 512
FOX_CHUNK = 512
FOX_PAIR = 4
BIAS_TERMS = 3
ONES_ROWS = 16

SWA_NW = 8
SWA_QB = SWA_NW * WINDOW

EPI_TM = 256


def _cast_windows(w_refs, wb_ref):
    for k, w_ref in enumerate(w_refs):
        for r in range(PREP_TN // LANES):
            rows = slice(r * LANES, (r + 1) * LANES)
            wb_ref[k * PREP_TN + r * LANES:k * PREP_TN + (r + 1) * LANES, :] = (
                w_ref[rows, :].astype(BF16))


def _w_src_row(window, groups, starts):
    tile_row = jnp.int32(0)
    for (c0, _), start in zip(groups, starts):
        tile_row = jnp.where(window >= start,
                             c0 // SUBLANES + (window - start) * (PREP_TN // SUBLANES), tile_row)
    return tile_row * SUBLANES


def _prenorm_kernel(steps_per_seq, x_ref, g_ref, wf_ref, b_ref, *refs):
    w_refs = refs[:_SWA_WINDOWS]
    h_ref, kb_ref, proj_ref, wb_ref, carry_ref = refs[_SWA_WINDOWS:]
    i = pl.program_id(0)

    @pl.when(i == 0)
    def _():
        _cast_windows(w_refs, wb_ref)

    @pl.when(i % steps_per_seq == 0)
    def _():
        carry_ref[...] = jnp.zeros(carry_ref.shape, F32)

    x = x_ref[...]
    ms = jnp.mean(x * x, axis=-1, keepdims=True)
    h = ((x * lax.rsqrt(ms + NORM_EPS)) * g_ref[...]).astype(BF16)
    h_ref[...] = h
    ft = lax.dot_general(wf_ref[...], h, _NT, preferred_element_type=F32)
    proj_ref[...] = lax.dot_general(h, wb_ref[...], _NT,
                                    preferred_element_type=F32).astype(BF16)

    f = ft[0:FOX_HEADS, :] + b_ref[...]
    c = jnp.minimum(f, 0.0) - jnp.log1p(jnp.exp(-jnp.abs(f)))
    rows = c.shape[1]
    lane = lax.broadcasted_iota(jnp.int32, c.shape, 1)
    shift = 1
    while shift < rows:
        c = c + jnp.where(lane >= shift, pltpu.roll(c, shift, axis=1), 0.0)
        shift *= 2
    c = c + carry_ref[:, 0:1]
    carry_ref[...] = jnp.broadcast_to(c[:, rows - 1:rows], carry_ref.shape)
    rest = c * (-LOG2E)
    pieces = []
    for _ in range(BIAS_TERMS):
        piece = rest.astype(BF16).astype(F32)
        pieces.append(piece)
        rest = rest - piece
    pieces.append(jnp.zeros((LANES - BIAS_TERMS * FOX_HEADS, rows), F32))
    kb_ref[...] = jnp.concatenate(pieces, axis=0).T.astype(BF16)


def _prenorm(x2, g, wf_t, b_col, w_t, seq):
    t = x2.shape[0]
    once = pl.Buffered(1)

    def w_spec(k):
        return pl.BlockSpec((pl.Element(PREP_TN), pl.Element(D_MODEL)),
                            lambda i: (_w_src_row(k, _SWA_GROUPS, _SWA_START), 0),
                            pipeline_mode=once)

    return pl.pallas_call(
        functools.partial(_prenorm_kernel, seq // NORM_ROWS),
        grid=(t // NORM_ROWS,),
        in_specs=[
            pl.BlockSpec((NORM_ROWS, D_MODEL), lambda i: (i, 0)),
            pl.BlockSpec((1, D_MODEL), lambda i: (0, 0)),
            pl.BlockSpec((F_ROWS, D_MODEL), lambda i: (0, 0)),
            pl.BlockSpec((FOX_HEADS, 1), lambda i: (0, 0)),
            *[w_spec(k) for k in range(_SWA_WINDOWS)],
        ],
        out_specs=[
            pl.BlockSpec((NORM_ROWS, D_MODEL), lambda i: (i, 0)),
            pl.BlockSpec((NORM_ROWS, LANES), lambda i: (i, 0)),
            pl.BlockSpec((NORM_ROWS, SWA_COLS), lambda i: (i, 0)),
        ],
        out_shape=[
            jax.ShapeDtypeStruct((t, D_MODEL), BF16),
            jax.ShapeDtypeStruct((t, LANES), BF16),
            jax.ShapeDtypeStruct((t, SWA_COLS), BF16),
        ],
        scratch_shapes=[
            pltpu.VMEM((SWA_COLS, D_MODEL), BF16),
            pltpu.VMEM((FOX_HEADS, LANES), F32),
        ],
        compiler_params=pltpu.CompilerParams(
            dimension_semantics=("arbitrary",),
            vmem_limit_bytes=VMEM_LIMIT_BYTES),
        name="prenorm",
    )(x2, g, wf_t, b_col, *([w_t] * _SWA_WINDOWS))


def _inproj_kernel(h_ref, w_hbm, scale_ref, proj_ref, wf_ref, wb_ref, sem):
    j = pl.program_id(0)

    def window_copy(tile, k):
        row = pl.multiple_of(
            _w_src_row(tile * W_PER_TILE + k, _MAIN_GROUPS, _MAIN_START), SUBLANES)
        return pltpu.make_async_copy(
            w_hbm.at[pl.ds(row, PREP_TN), :],
            wf_ref.at[pl.ds(k * PREP_TN, PREP_TN), :],
            sem.at[k])

    @pl.when(pl.program_id(1) == 0)
    def _():
        @pl.when(j == 0)
        def _():
            for k in range(W_PER_TILE):
                window_copy(j, k).start()

        for k in range(W_PER_TILE):
            window_copy(j, k).wait()
            for r in range(PREP_TN // LANES):
                rows = pl.ds(k * PREP_TN + r * LANES, LANES)
                wb_ref[rows, :] = wf_ref[rows, :].astype(BF16)

        @pl.when(j + 1 < pl.num_programs(0))
        def _():
            for k in range(W_PER_TILE):
                window_copy(j + 1, k).start()

    acc = lax.dot_general(h_ref[...], wb_ref[...], _NT, preferred_element_type=F32)
    proj_ref[...] = (acc * scale_ref[...]).astype(BF16)


def _inproj(h, w_t, col_scale):
    t = h.shape[0]
    return pl.pallas_call(
        _inproj_kernel,
        grid=(MAIN_COLS // IN_TN, t // IN_TM),
        in_specs=[
            pl.BlockSpec((IN_TM, D_MODEL), lambda j, i: (i, 0)),
            pl.BlockSpec(memory_space=pl.ANY),
            pl.BlockSpec((1, IN_TN), lambda j, i: (0, j)),
        ],
        out_specs=pl.BlockSpec((IN_TM, IN_TN), lambda j, i: (i, j)),
        out_shape=jax.ShapeDtypeStruct((t, MAIN_COLS), BF16),
        scratch_shapes=[
            pltpu.VMEM((IN_TN, D_MODEL), F32),
            pltpu.VMEM((IN_TN, D_MODEL), BF16),
            pltpu.SemaphoreType.DMA((W_PER_TILE,)),
        ],
        compiler_params=pltpu.CompilerParams(
            dimension_semantics=("arbitrary", "arbitrary"),
            vmem_limit_bytes=VMEM_LIMIT_BYTES),
        name="inproj",
    )(h, w_t, col_scale)


def _fox_kernel(q_ref, k_ref, v_ref, kb_ref, o_ref, vt_ref, qt_ref, acc_ref,
                sa_ref, ma_ref, sb_ref, mb_ref):
    s0_ref, s1_ref = (sa_ref, ma_ref), (sb_ref, mb_ref)
    hp = pl.program_id(1)
    seq = k_ref.shape[0]
    t = FOX_T
    d = FOX_HEAD_DIM

    def transposed(x):
        return x.astype(F32).T.astype(BF16)

    row = lax.broadcasted_iota(jnp.int32, (d, t), 0)
    for hh in range(FOX_PAIR):
        mine = (row < BIAS_TERMS * FOX_HEADS) & (row % FOX_HEADS == hp * FOX_PAIR + hh)
        qt_ref[hh, d:2 * d, :] = jnp.where(mine, 1.0, 0.0).astype(BF16)
        vt_ref[hh, d:, :] = jnp.ones((ONES_ROWS, seq), BF16)
        for r in range(seq // FOX_CHUNK):
            rows = slice(r * FOX_CHUNK, (r + 1) * FOX_CHUNK)
            vt_ref[hh, 0:d, rows] = transposed(v_ref[rows, hh * d:(hh + 1) * d])

    def scores(i, buf):
        s_ref, max_ref = buf
        rows = pl.ds(pl.multiple_of(i * t, t), t)
        for hh in range(FOX_PAIR):
            k_aug = jnp.concatenate([k_ref[rows, hh * d:(hh + 1) * d], kb_ref[rows, :]], axis=1)
            s = jnp.dot(k_aug, qt_ref[hh], preferred_element_type=F32)
            s_ref[hh] = s
            max_ref[hh] = jnp.max(s, axis=0, keepdims=True)

    def absorb(i, buf, carry, masked):
        s_ref, max_ref = buf
        rows = pl.ds(pl.multiple_of(i * t, t), t)
        new = []
        for hh in range(FOX_PAIR):
            m = carry[hh]
            s = s_ref[hh]
            if masked:
                key = lax.broadcasted_iota(jnp.int32, s.shape, 0)
                qry = lax.broadcasted_iota(jnp.int32, s.shape, 1)
                s = jnp.where(key <= qry, s, NEG_BIG)
                tile_max = jnp.max(s, axis=0, keepdims=True)
            else:
                tile_max = max_ref[hh]
            m_new = jnp.maximum(m, tile_max)
            alpha = jnp.exp2(m - m_new)
            p = jnp.exp2(s - m_new)
            acc_ref[hh] = alpha * acc_ref[hh] + jnp.dot(
                vt_ref[hh, :, rows], p.astype(BF16), preferred_element_type=F32)
            new.append(m_new)
        return tuple(new)

    def step(i, s_cur, s_next, carry):
        scores(i + 1, s_next)
        return absorb(i, s_cur, carry, False)

    def pair(j, carry):
        carry = step(2 * j, s0_ref, s1_ref, carry)
        return step(2 * j + 1, s1_ref, s0_ref, carry)

    n_q = seq // t

    def load_q(qi):
        q_rows = pl.ds(pl.multiple_of(qi * t, t), t)
        for hh in range(FOX_PAIR):
            qt_ref[hh, 0:d, :] = transposed(q_ref[q_rows, hh * d:(hh + 1) * d])

    def q_tile(qi, _):
        for hh in range(FOX_PAIR):
            acc_ref[hh] = jnp.zeros((d + ONES_ROWS, t), F32)

        def finish(s_last, carry):
            load_q(jnp.minimum(qi + 1, n_q - 1))
            absorb(qi, s_last, carry, True)
            q_rows = pl.ds(pl.multiple_of(qi * t, t), t)
            for hh in range(FOX_PAIR):
                o_ref[q_rows, hh * d:(hh + 1) * d] = (
                    acc_ref[hh, 0:d, :] / acc_ref[hh, d:d + 1, :]).T.astype(o_ref.dtype)

        def odd_tail(carry):
            finish(s1_ref, step(qi - 1, s0_ref, s1_ref, carry))

        def even_tail(carry):
            finish(s0_ref, carry)

        init = tuple(jnp.full((1, t), NEG_BIG, F32) for _ in range(FOX_PAIR))
        scores(0, s0_ref)
        carry = lax.fori_loop(0, qi // 2, pair, init)
        lax.cond(qi % 2 == 1, odd_tail, even_tail, carry)
        return 0

    load_q(0)
    lax.fori_loop(0, n_q, q_tile, 0)


def _fox_attention(proj, kb_tok, batch, seq):
    width = FOX_PAIR * FOX_HEAD_DIM
    qcol = _MAIN_OFF["q_a"] // width
    kcol = _MAIN_OFF["k_a"] // width
    vcol = _MAIN_OFF["v_a"] // width
    return pl.pallas_call(
        _fox_kernel,
        grid=(batch, FOX_HEADS // FOX_PAIR),
        in_specs=[
            pl.BlockSpec((seq, width), lambda b, h: (b, qcol + h)),
            pl.BlockSpec((seq, width), lambda b, h: (b, kcol + h)),
            pl.BlockSpec((seq, width), lambda b, h: (b, vcol + h)),
            pl.BlockSpec((seq, LANES), lambda b, h: (b, 0)),
        ],
        out_specs=pl.BlockSpec((seq, width), lambda b, h: (b, h)),
        out_shape=jax.ShapeDtypeStruct((batch * seq, FOX_WIDTH), BF16),
        scratch_shapes=[
            pltpu.VMEM((FOX_PAIR, FOX_HEAD_DIM + ONES_ROWS, seq), BF16),
            pltpu.VMEM((FOX_PAIR, 2 * FOX_HEAD_DIM, FOX_T), BF16),
            pltpu.VMEM((FOX_PAIR, FOX_HEAD_DIM + ONES_ROWS, FOX_T), F32),
            pltpu.VMEM((FOX_PAIR, FOX_T, FOX_T), F32),
            pltpu.VMEM((FOX_PAIR, 1, FOX_T), F32),
            pltpu.VMEM((FOX_PAIR, FOX_T, FOX_T), F32),
            pltpu.VMEM((FOX_PAIR, 1, FOX_T), F32),
        ],
        compiler_params=pltpu.CompilerParams(
            dimension_semantics=("arbitrary", "arbitrary"),
            vmem_limit_bytes=VMEM_LIMIT_BYTES),
        name="fox_attention",
    )(proj, proj, proj, kb_tok)


def _rope_t(xt, cos, sin):
    out = []
    for hd in range(xt.shape[0] // SWA_HEAD_DIM):
        x1 = xt[hd * SWA_HEAD_DIM: hd * SWA_HEAD_DIM + ROPE_HALF]
        x2 = xt[hd * SWA_HEAD_DIM + ROPE_HALF: (hd + 1) * SWA_HEAD_DIM]
        out.append(x1 * cos - x2 * sin)
        out.append(x2 * cos + x1 * sin)
    return jnp.concatenate(out, axis=0)


def _swa_kernel(sink_ref, q_ref, kc_ref, kp_ref, vc_ref, vp_ref, posc_ref, posp_ref,
                inv_ref, *rest):
    n_cast = (len(rest) - 3) // 2
    cast_in, o_ref = rest[:n_cast], rest[n_cast]
    cast_out, (band_ref, eye_ref) = rest[n_cast + 1:2 * n_cast + 1], rest[-2:]
    for src_ref, dst_ref in zip(cast_in, cast_out):
        dst_ref[...] = src_ref[...].astype(dst_ref.dtype)

    n = pl.program_id(1)
    w = WINDOW
    inv = inv_ref[...]

    def tables(pos_row):
        ang = inv * pos_row.astype(F32)
        return jnp.cos(ang), jnp.sin(ang)

    def rope_k(k_nat, cos, sin):
        kt = _rope_t(k_nat.astype(F32).T, cos, sin)
        return kt.T.astype(BF16)

    cos_p, sin_p = tables(posp_ref[...])
    k_prev = rope_k(kp_ref[...], cos_p, sin_p)
    v_prev = vp_ref[...]

    @pl.when((pl.program_id(0) == 0) & (n == 0))
    def _():
        key = lax.broadcasted_iota(jnp.int32, band_ref.shape, 0)
        qry = lax.broadcasted_iota(jnp.int32, band_ref.shape, 1)
        band_ref[...] = jnp.where((key <= qry + w) & (key > qry), 0.0, NEG_BIG).astype(BF16)
        src = lax.broadcasted_iota(jnp.int32, eye_ref.shape, 0)
        dst = lax.broadcasted_iota(jnp.int32, eye_ref.shape, 1) % w
        eye_ref[...] = jnp.where(src == dst, 1.0, 0.0).astype(BF16)

    no_prev = jnp.where(n > 0, 0.0, NEG_BIG)
    q_scale = SWA_HEAD_DIM ** -0.5 * LOG2E
    zeros_half = jnp.zeros((SWA_HEAD_DIM, SWA_GROUP * w), F32)
    ones_rows = jnp.ones((2 * SUBLANES, 2 * w), BF16)

    sinks = [jnp.concatenate(
        [jnp.full((1, w), sink_ref[hd] * LOG2E, F32)
         for hd in range(g * SWA_GROUP, (g + 1) * SWA_GROUP)], axis=1)
        for g in range(SWA_KV_HEADS)]

    def logits(wi, k_prev):
        tok = slice(wi * w, (wi + 1) * w)
        cos, sin = tables(posc_ref[:, tok])
        k_cur = rope_k(kc_ref[tok, :], cos, sin)
        k_all = jnp.concatenate([k_prev, k_cur], axis=0)
        k_aug = jnp.concatenate([k_all, band_ref[...]], axis=1)
        qf = q_ref[tok, :].astype(F32)
        qt = jnp.concatenate(
            [qf[:, c * LANES:(c + 1) * LANES].T for c in range(SWA_WIDTH // LANES)], axis=0)
        qt = _rope_t(qt, cos * q_scale, sin * q_scale)
        out = []
        for g in range(SWA_KV_HEADS):
            heads = range(g * SWA_GROUP, (g + 1) * SWA_GROUP)
            q_g = jnp.concatenate(
                [qt[hd * SWA_HEAD_DIM:(hd + 1) * SWA_HEAD_DIM] for hd in heads], axis=1)
            parts = [zeros_half] * SWA_KV_HEADS
            parts[g] = q_g
            q_z = jnp.concatenate(parts, axis=0).astype(BF16)
            q_aug = jnp.concatenate([q_z, eye_ref[...]], axis=0)
            s = jnp.dot(k_aug, q_aug, preferred_element_type=F32)
            if wi == 0:
                s = jnp.concatenate([s[:w] + no_prev, s[w:]], axis=0)
            out.append(s)
        return out, k_cur

    def finish(wi, scores, v_prev):
        tok = slice(wi * w, (wi + 1) * w)
        v_cur = vc_ref[tok, :]
        v_all_t = jnp.concatenate([v_prev, v_cur], axis=0).astype(F32).T.astype(BF16)
        for g in range(SWA_KV_HEADS):
            s, sink = scores[g], sinks[g]
            m = jnp.maximum(jnp.max(s, axis=0, keepdims=True), sink)
            e = jnp.exp2(s - m).astype(BF16)
            v_aug = jnp.concatenate(
                [v_all_t[g * SWA_HEAD_DIM:(g + 1) * SWA_HEAD_DIM], ones_rows], axis=0)
            pv = jnp.dot(v_aug, e, preferred_element_type=F32)
            denom = pv[SWA_HEAD_DIM:SWA_HEAD_DIM + 1] + jnp.exp2(sink - m)
            out_t = pv[:SWA_HEAD_DIM] / denom
            for jj in range(SWA_GROUP // 2):
                pair = jnp.concatenate(
                    [out_t[:, (2 * jj) * w:(2 * jj + 1) * w],
                     out_t[:, (2 * jj + 1) * w:(2 * jj + 2) * w]], axis=0)
                col = (g * (SWA_GROUP // 2) + jj) * LANES
                o_ref[tok, col:col + LANES] = pair.T.astype(o_ref.dtype)
        return v_cur

    pending, k_prev = logits(0, k_prev)
    for wi in range(SWA_NW):
        if wi + 1 < SWA_NW:
            upcoming, k_prev = logits(wi + 1, k_prev)
        v_prev = finish(wi, pending, v_prev)
        if wi + 1 < SWA_NW:
            pending = upcoming


def _swa_attention(sinks, proj, pos_row, inv_tab, batch, seq, weights_f32):
    nb = seq // SWA_QB
    per = SWA_QB // WINDOW
    qcol = _SWA_OFF["q_b"] // SWA_WIDTH
    kcol = _SWA_OFF["k_b"] // LANES
    vcol = _SWA_OFF["v_b"] // LANES
    steps = batch * nb

    def cur(b, n):
        return b * nb + n

    def prev(b, n):
        return jnp.maximum((b * nb + n) * per - 1, 0)

    def chunk_spec(wgt):
        rows, cols = wgt.shape
        assert rows % (steps * 2 * SUBLANES) == 0
        return pl.BlockSpec((rows // steps, cols), lambda b, n: (cur(b, n), 0))

    cast_specs = [chunk_spec(wgt) for wgt in weights_f32]
    return pl.pallas_call(
        _swa_kernel,
        grid=(batch, nb),
        in_specs=[
            pl.BlockSpec(memory_space=pltpu.SMEM),
            pl.BlockSpec((SWA_QB, SWA_WIDTH), lambda b, n: (cur(b, n), qcol)),
            pl.BlockSpec((SWA_QB, LANES), lambda b, n: (cur(b, n), kcol)),
            pl.BlockSpec((WINDOW, LANES), lambda b, n: (prev(b, n), kcol)),
            pl.BlockSpec((SWA_QB, LANES), lambda b, n: (cur(b, n), vcol)),
            pl.BlockSpec((WINDOW, LANES), lambda b, n: (prev(b, n), vcol)),
            pl.BlockSpec((1, SWA_QB), lambda b, n: (0, cur(b, n))),
            pl.BlockSpec((1, WINDOW), lambda b, n: (0, prev(b, n))),
            pl.BlockSpec((ROPE_HALF, LANES), lambda b, n: (0, 0)),
            *cast_specs,
        ],
        out_specs=[pl.BlockSpec((SWA_QB, SWA_WIDTH), lambda b, n: (cur(b, n), 0)), *cast_specs],
        out_shape=[jax.ShapeDtypeStruct((batch * seq, SWA_WIDTH), BF16),
                   *[jax.ShapeDtypeStruct(wgt.shape, BF16) for wgt in weights_f32]],
        scratch_shapes=[
            pltpu.VMEM((2 * WINDOW, WINDOW), BF16),
            pltpu.VMEM((WINDOW, SWA_GROUP * WINDOW), BF16),
        ],
        compiler_params=pltpu.CompilerParams(
            dimension_semantics=("arbitrary", "arbitrary")),
        name="swa_attention",
    )(sinks, proj, proj, proj, proj, proj, pos_row, pos_row, inv_tab, *weights_f32)


def _silu(z):
    return z * jax.nn.sigmoid(z)


def _epilogue_kernel(ya_ref, yb_ref, gz_ref, x_ref, p_ref,
                     wof_hbm, wos_hbm, wout_hbm, gpost_ref, wple_hbm, wgate_hbm, o_ref,
                     wof_ref, wos_ref, wout_ref, wple_ref, wgate_ref, sem):
    def gz(name, width):
        return gz_ref[:, _MAIN_OFF[name]:_MAIN_OFF[name] + width].astype(F32)

    fetch = [pltpu.make_async_copy(src, dst, sem.at[k]) for k, (src, dst) in enumerate((
        (wof_hbm, wof_ref), (wos_hbm, wos_ref), (wout_hbm, wout_ref),
        (wple_hbm, wple_ref), (wgate_hbm, wgate_ref)))]

    def body(first):
        if first:
            for copy in fetch:
                copy.start()
        ua = (ya_ref[...].astype(F32) * _silu(gz("z_a", FOX_WIDTH))).astype(BF16)
        ub = (yb_ref[...].astype(F32) * _silu(gz("z_b", SWA_WIDTH))).astype(BF16)
        if first:
            fetch[0].wait()
        oa = jnp.dot(ua, wof_ref[...], preferred_element_type=F32)
        if first:
            fetch[1].wait()
        ob = jnp.dot(ub, wos_ref[...], preferred_element_type=F32)
        merged = (jax.nn.sigmoid(gz("g_a", D_MODEL)) * oa
                  + jax.nn.sigmoid(gz("g_b", D_MODEL)) * ob)
        if first:
            fetch[2].wait()
        out = jnp.dot(merged.astype(BF16), wout_ref[...], preferred_element_type=F32)
        ms = jnp.mean(out * out, axis=-1, keepdims=True)
        x1 = x_ref[...] + (out * lax.rsqrt(ms + NORM_EPS)) * gpost_ref[...]
        if first:
            fetch[3].wait()
            fetch[4].wait()
        e = jnp.dot(p_ref[...].astype(BF16), wple_ref[...], preferred_element_type=F32)
        gate = jax.nn.sigmoid(jnp.dot(x1.astype(BF16), wgate_ref[...],
                                      preferred_element_type=F32))
        o_ref[...] = x1 + gate * e

    @pl.when(pl.program_id(0) == 0)
    def _():
        body(True)

    @pl.when(pl.program_id(0) > 0)
    def _():
        body(False)


def _epilogue(ya, yb, proj, x2, p2, wof, wos, wout, gpost, wple, wgate):
    t = x2.shape[0]
    gz_cols = _MAIN_OFF["z_b"] + SWA_WIDTH
    assert _MAIN_OFF["g_a"] == 0 and gz_cols == 2 * D_MODEL + FOX_WIDTH + SWA_WIDTH
    once = pl.Buffered(1)
    in_hbm = pl.BlockSpec(memory_space=pl.ANY)

    def const(shape):
        return pl.BlockSpec(shape, lambda i: (0, 0), pipeline_mode=once)

    return pl.pallas_call(
        _epilogue_kernel,
        grid=(t // EPI_TM,),
        in_specs=[
            pl.BlockSpec((EPI_TM, FOX_WIDTH), lambda i: (i, 0)),
            pl.BlockSpec((EPI_TM, SWA_WIDTH), lambda i: (i, 0)),
            pl.BlockSpec((EPI_TM, gz_cols), lambda i: (i, 0)),
            pl.BlockSpec((EPI_TM, D_MODEL), lambda i: (i, 0)),
            pl.BlockSpec((EPI_TM, PLE_DIM), lambda i: (i, 0)),
            in_hbm, in_hbm, in_hbm,
            const((1, D_MODEL)),
            in_hbm, in_hbm,
        ],
        out_specs=pl.BlockSpec((EPI_TM, D_MODEL), lambda i: (i, 0)),
        out_shape=jax.ShapeDtypeStruct((t, D_MODEL), F32),
        scratch_shapes=[
            pltpu.VMEM((FOX_WIDTH, D_MODEL), BF16),
            pltpu.VMEM((SWA_WIDTH, D_MODEL), BF16),
            pltpu.VMEM((D_MODEL, D_MODEL), BF16),
            pltpu.VMEM((PLE_DIM, D_MODEL), BF16),
            pltpu.VMEM((D_MODEL, D_MODEL), BF16),
            pltpu.SemaphoreType.DMA((5,)),
        ],
        compiler_params=pltpu.CompilerParams(
            dimension_semantics=("arbitrary",),
            vmem_limit_bytes=VMEM_LIMIT_BYTES),
        name="epilogue",
    )(ya, yb, proj, x2, p2, wof, wos, wout, gpost, wple, wgate)


def _layer(x2, p2, pos_row, batch, seq, pre_g, w_in, b_forget, sinks, w_o_fox, w_o_swa,
           w_out, post_g, w_ple, w_ple_gate):
    w_t = w_in.T
    f0 = _REF_OFF["f_a"]
    wf_t = jnp.pad(w_t[f0:f0 + FOX_HEADS], ((0, F_ROWS - FOX_HEADS), (0, 0))).astype(BF16)
    col = jnp.arange(MAIN_COLS)
    in_q_a = (col >= _MAIN_OFF["q_a"]) & (col < _MAIN_OFF["q_a"] + FOX_WIDTH)
    col_scale = jnp.where(in_q_a, Q_A_SCALE, 1.0).astype(F32).reshape(1, MAIN_COLS)

    h, kb_tok, proj_swa = _prenorm(x2, pre_g.reshape(1, D_MODEL), wf_t,
                                   b_forget.reshape(FOX_HEADS, 1).astype(F32), w_t, seq)
    proj = _inproj(h, w_t, col_scale)
    ya = _fox_attention(proj, kb_tok, batch, seq)

    inv = ROPE_THETA ** (-jnp.arange(ROPE_HALF, dtype=F32) / ROPE_HALF)
    inv_tab = jnp.broadcast_to(inv[:, None], (ROPE_HALF, LANES))
    yb, wof, wos, wout, wple, wgate = _swa_attention(
        sinks.astype(F32), proj_swa, pos_row, inv_tab, batch, seq,
        (w_o_fox, w_o_swa, w_out, w_ple, w_ple_gate))

    return _epilogue(ya, yb, proj, x2, p2, wof, wos, wout,
                     post_g.reshape(1, D_MODEL), wple, wgate)


def kernel(x, p, positions, pre_norm_g, w_in, b_forget, sinks, w_o_fox, w_o_swa, w_out,
           post_norm_g, w_ple, w_ple_gate):
    batch, seq, _ = x.shape
    depth = p.shape[0]
    x2 = x.reshape(batch * seq, D_MODEL)
    pos_row = positions.reshape(1, batch * seq)
    for i in range(depth):
        x2 = _layer(x2, p[i].reshape(batch * seq, PLE_DIM), pos_row, batch, seq,
                    pre_norm_g[i], w_in[i], b_forget[i], sinks[i], w_o_fox[i], w_o_swa[i],
                    w_out[i], post_norm_g[i], w_ple[i], w_ple_gate[i])
    return x2.reshape(batch, seq, D_MODEL)
```
